```python
import jax, jax.numpy as jnp
from jax import lax
import numpy as np

D_MODEL = 1024
BATCH = 16
SEQ = 256
DEPTH = 2
DEC_BATCH = 2
DEC_SEQ = 2048
PAST_LEN = 256

GRID_W = 64
N_HEADS = 8
QK_NOPE = 64
QK_ROPE = 32
V_HEAD = 64
Q_LORA = 384
KV_LORA = 256
AXIS_ROPE = QK_ROPE // 2
ROPE_THETA = 10000.0
Q_BLOCK = 128
W_CONF = D_MODEL // 4
CONF_K = 31
W_SC = D_MODEL // 4
SC_K = 3
W_FN = D_MODEL // 4
FN_GROUPS = 4
FN_GROUP_W = W_FN // FN_GROUPS
N_BRANCH = 4
FF_HIDDEN = ((8 * D_MODEL // 3 + 255) // 256) * 256
EPS = 1e-6
OFF_QA = 0
OFF_KVA = OFF_QA + Q_LORA
OFF_CONF = OFF_KVA + KV_LORA + QK_ROPE
OFF_SC = OFF_CONF + 2 * W_CONF
OFF_FN = OFF_SC + 3 * W_SC
OFF_GATE = OFF_FN + W_FN
IN_COLS = OFF_GATE + N_BRANCH * D_MODEL

kernel_name = 'hybrid_mla_conv_fourier_diffusion_step'


def _rmsnorm(x, g):
    xf = x.astype(jnp.float32)
    y = xf * lax.rsqrt(jnp.mean(xf * xf, axis=-1, keepdims=True) + EPS)
    return (y * g.astype(jnp.float32)).astype(x.dtype)


def _layernorm(x, g, b):
    xf = x.astype(jnp.float32)
    mu = jnp.mean(xf, axis=-1, keepdims=True)
    var = jnp.mean(jnp.square(xf - mu), axis=-1, keepdims=True)
    y = (xf - mu) * lax.rsqrt(var + EPS)
    return (y * g.astype(jnp.float32) + b.astype(jnp.float32)).astype(x.dtype)


def _dwconv(x, w):
    k = w.shape[0]
    return lax.conv_general_dilated(
        x, w[:, None, :].astype(x.dtype), window_strides=(1,),
        padding=[(k // 2, k // 2)], dimension_numbers=('NWC', 'WIO', 'NWC'),
        feature_group_count=x.shape[-1])


def _axial_rope_tables(n_tokens):
    rows = n_tokens // GRID_W
    row_pos = jnp.repeat(jnp.arange(rows, dtype=jnp.float32), GRID_W)
    col_pos = jnp.tile(jnp.arange(GRID_W, dtype=jnp.float32), rows)
    inv = ROPE_THETA ** (-jnp.arange(0, AXIS_ROPE, 2, dtype=jnp.float32) / AXIS_ROPE)
    ang_r = row_pos[:, None] * inv
    ang_c = col_pos[:, None] * inv
    return (jnp.cos(ang_r), jnp.sin(ang_r), jnp.cos(ang_c), jnp.sin(ang_c))


def _rotate(x, cos, sin):
    half = x.shape[-1] // 2
    x1, x2 = x[..., :half], x[..., half:]
    cos = cos.astype(x.dtype)
    sin = sin.astype(x.dtype)
    return jnp.concatenate([x1 * cos - x2 * sin, x2 * cos + x1 * sin], axis=-1)


def _apply_axial_rope(x, tabs):
    cr, sr, cc, sc = tabs
    shape = (x.shape[1],) + (1,) * (x.ndim - 3) + (AXIS_ROPE // 2,)
    xr = _rotate(x[..., :AXIS_ROPE], cr.reshape(shape), sr.reshape(shape))
    xc = _rotate(x[..., AXIS_ROPE:], cc.reshape(shape), sc.reshape(shape))
    return jnp.concatenate([xr, xc], axis=-1)


def _attend(q, k, v):
    b, lq, h, dqk = q.shape
    scale = dqk ** -0.5
    nb = lq // Q_BLOCK
    qb = q.reshape(b, nb, Q_BLOCK, h, dqk).transpose(1, 0, 2, 3, 4)

    def one_block(qi):
        s = jnp.einsum('bqhd,bkhd->bhqk', qi, k, preferred_element_type=jnp.float32) * scale
        p = jax.nn.softmax(s, axis=-1).astype(v.dtype)
        return jnp.einsum('bhqk,bkhd->bqhd', p, v)

    o = lax.map(one_block, qb)
    return o.transpose(1, 0, 2, 3, 4).reshape(b, lq, h, v.shape[-1])


def _mla_keys(c_kv, k_rope, w_kvb):
    b, l, _ = c_kv.shape
    kv = (c_kv @ w_kvb).reshape(b, l, N_HEADS, QK_NOPE + V_HEAD)
    k = jnp.concatenate(
        [kv[..., :QK_NOPE], jnp.broadcast_to(k_rope[:, :, None, :], (b, l, N_HEADS, QK_ROPE))], axis=-1)
    return k, kv[..., QK_NOPE:]


def _fourier(u):
    b, l, _ = u.shape
    ug = u.reshape(b, l, FN_GROUPS, FN_GROUP_W).astype(jnp.float32)
    f = jnp.fft.fft2(ug, axes=(1, 3), norm='ortho').real
    return f.reshape(b, l, W_FN).astype(u.dtype)


def _token_mixers(h, lp, rope, ctx):
    b, l, _ = h.shape
    proj = h @ lp['w_in']
    q_a = proj[..., OFF_QA:OFF_KVA]
    c_kv = _rmsnorm(proj[..., OFF_KVA:OFF_KVA + KV_LORA], lp['g_kva'])
    k_rope = proj[..., OFF_KVA + KV_LORA:OFF_CONF]
    conf_in = proj[..., OFF_CONF:OFF_SC]
    sc_in = proj[..., OFF_SC:OFF_FN]
    fn_in = proj[..., OFF_FN:OFF_GATE]
    gates = jax.nn.sigmoid(proj[..., OFF_GATE:].reshape(b, l, N_BRANCH, D_MODEL))

    q = (_rmsnorm(q_a, lp['g_qa']) @ lp['w_qb']).reshape(b, l, N_HEADS, QK_NOPE + QK_ROPE)
    q_nope, q_rope = q[..., :QK_NOPE], q[..., QK_NOPE:]
    k_rope_pos = k_rope
    if rope is not None:
        q_rope = _apply_axial_rope(q_rope, rope)
        k_rope_pos = _apply_axial_rope(k_rope, rope)
    k, v = _mla_keys(c_kv, k_rope_pos, lp['w_kvb'])
    if ctx is not None:
        k_ctx, v_ctx = _mla_keys(ctx[0], ctx[1], lp['w_kvb'])
        k = jnp.concatenate([k_ctx, k], axis=1)
        v = jnp.concatenate([v_ctx, v], axis=1)
    o = _attend(jnp.concatenate([q_nope, q_rope], axis=-1), k, v)
    y_a = o.reshape(b, l, N_HEADS * V_HEAD) @ lp['w_o_mla']

    u = conf_in[..., :W_CONF] * jax.nn.sigmoid(conf_in[..., W_CONF:])
    u = _dwconv(u, lp['w_conf_dw']) + lp['b_conf_dw']
    u = jax.nn.silu(_layernorm(u, lp['g_conf_ln'], lp['b_conf_ln']))
    y_b = u @ lp['w_conf_pw']

    gb, gc, xs = jnp.split(sc_in, 3, axis=-1)
    y_c = (gb * _dwconv(gc * xs, lp['w_sc_conv'])) @ lp['w_sc_out']

    y_d = _fourier(fn_in) @ lp['w_fn']

    merged = (gates[..., 0, :] * y_a + gates[..., 1, :] * y_b
              + gates[..., 2, :] * y_c + gates[..., 3, :] * y_d)
    return merged @ lp['w_out'], (c_kv, k_rope)


def _swiglu(h, lp):
    return (jax.nn.silu(h @ lp['w_ffn_gate']) * (h @ lp['w_ffn_up'])) @ lp['w_ffn_down']


def _block(x, mod, lp, rope, ctx):
    sh1, sc1, g1, sh2, sc2, g2 = jnp.split(mod, 6, axis=-1)
    h = _rmsnorm(x, lp['g_norm1']) * (1 + sc1) + sh1
    y, ctx_kv = _token_mixers(h, lp, rope, ctx)
    x = x + g1 * y
    h = _rmsnorm(x, lp['g_norm2']) * (1 + sc2) + sh2
    x = x + g2 * _swiglu(h, lp)
    return x, ctx_kv


def setup_inputs(seed: int = 0) -> dict:
    key = jax.random.key(seed)
    ks = iter(jax.random.split(key, 40))

    def nrm(shape, scale):
        return jax.random.normal(next(ks), shape, jnp.float32) * scale

    def gain(shape):
        return 1.0 + nrm(shape, 0.02)

    d = D_MODEL
    return {
        'x_prompt': nrm((BATCH, SEQ, d), 1.0),
        'x_sample': nrm((DEC_BATCH, DEC_SEQ, d), 1.0),
        'cache_ckv': nrm((DEC_BATCH, DEPTH, PAST_LEN, KV_LORA), 1.0),
        'cache_krope': nrm((DEC_BATCH, DEPTH, PAST_LEN, QK_ROPE), 1.0),
        'c': nrm((DEC_BATCH, d), 1.0),
        'c_ctx': nrm((d,), 1.0),
        'w_ada': nrm((DEPTH, d, 6 * d), 0.5 * d ** -0.5),
        'b_ada': nrm((DEPTH, 6 * d), 0.01),
        'g_norm1': gain((DEPTH, d)),
        'g_norm2': gain((DEPTH, d)),
        'w_in': nrm((DEPTH, d, IN_COLS), d ** -0.5),
        'g_qa': gain((DEPTH, Q_LORA)),
        'w_qb': nrm((DEPTH, Q_LORA, N_HEADS * (QK_NOPE + QK_ROPE)), Q_LORA ** -0.5),
        'g_kva': gain((DEPTH, KV_LORA)),
        'w_kvb': nrm((DEPTH, KV_LORA, N_HEADS * (QK_NOPE + V_HEAD)), KV_LORA ** -0.5),
        'w_o_mla': nrm((DEPTH, N_HEADS * V_HEAD, d), (N_HEADS * V_HEAD) ** -0.5),
        'w_conf_dw': nrm((DEPTH, CONF_K, W_CONF), CONF_K ** -0.5),
        'b_conf_dw': nrm((DEPTH, W_CONF), 0.01),
        'g_conf_ln': gain((DEPTH, W_CONF)),
        'b_conf_ln': nrm((DEPTH, W_CONF), 0.01),
        'w_conf_pw': nrm((DEPTH, W_CONF, d), W_CONF ** -0.5),
        'w_sc_conv': nrm((DEPTH, SC_K, W_SC), SC_K ** -0.5),
        'w_sc_out': nrm((DEPTH, W_SC, d), W_SC ** -0.5),
        'w_fn': nrm((DEPTH, W_FN, d), W_FN ** -0.5),
        'w_out': nrm((DEPTH, d, d), d ** -0.5),
        'w_ffn_gate': nrm((DEPTH, d, FF_HIDDEN), d ** -0.5),
        'w_ffn_up': nrm((DEPTH, d, FF_HIDDEN), d ** -0.5),
        'w_ffn_down': nrm((DEPTH, FF_HIDDEN, d), FF_HIDDEN ** -0.5),
        'g_final': gain((d,)),
    }


def reference(x_prompt, x_sample, cache_ckv, cache_krope, c, c_ctx, w_ada, b_ada,
              g_norm1, g_norm2, w_in, g_qa, w_qb, g_kva, w_kvb, w_o_mla,
              w_conf_dw, b_conf_dw, g_conf_ln, b_conf_ln, w_conf_pw,
              w_sc_conv, w_sc_out, w_fn, w_out, w_ffn_gate, w_ffn_up, w_ffn_down, g_final):
    rope = _axial_rope_tables(x_sample.shape[1])
    xp = x_prompt
    xs = x_sample
    new_ckv = []
    new_krope = []
    for l in range(DEPTH):
        lp = {
            'g_norm1': g_norm1[l], 'g_norm2': g_norm2[l], 'w_in': w_in[l],
            'g_qa': g_qa[l], 'w_qb': w_qb[l], 'g_kva': g_kva[l], 'w_kvb': w_kvb[l],
            'w_o_mla': w_o_mla[l], 'w_conf_dw': w_conf_dw[l], 'b_conf_dw': b_conf_dw[l],
            'g_conf_ln': g_conf_ln[l], 'b_conf_ln': b_conf_ln[l], 'w_conf_pw': w_conf_pw[l],
            'w_sc_conv': w_sc_conv[l], 'w_sc_out': w_sc_out[l], 'w_fn': w_fn[l],
            'w_out': w_out[l], 'w_ffn_gate': w_ffn_gate[l], 'w_ffn_up': w_ffn_up[l],
            'w_ffn_down': w_ffn_down[l],
        }
        mod_ctx = (jax.nn.silu(c_ctx) @ w_ada[l] + b_ada[l])[None, None, :]
        mod_lat = (jax.nn.silu(c) @ w_ada[l] + b_ada[l])[:, None, :]
        xp, (ckv_l, krope_l) = _block(xp, mod_ctx, lp, None, None)
        new_ckv.append(ckv_l)
        new_krope.append(krope_l)
        xs, _ = _block(xs, mod_lat, lp, rope, (cache_ckv[:, l], cache_krope[:, l]))
    y_prompt = _rmsnorm(xp, g_final)
    y_sample = _rmsnorm(xs, g_final)
    ckv_out = jnp.stack(new_ckv, axis=1)
    krope_out = jnp.stack(new_krope, axis=1)
    return (y_prompt, y_sample, ckv_out, krope_out)
```

```python
import functools

import numpy as np
import jax
import jax.numpy as jnp
from jax import lax
from jax.experimental import pallas as pl
from jax.experimental.pallas import tpu as pltpu

BF = jnp.bfloat16
F32 = jnp.float32

D = 1024
BATCH = 16
SEQ = 256
DEPTH = 2
DEC_BATCH = 2
DEC_SEQ = 2048
PAST_LEN = 256
GRID_W = 64
N_HEADS = 8
QK_NOPE = 64
QK_ROPE = 32
V_HEAD = 64
Q_LORA = 384
KV_LORA = 256
AXIS_ROPE = QK_ROPE // 2
ROPE_THETA = 10000.0
W_CONF = D // 4
CONF_K = 31
W_SC = D // 4
SC_K = 3
W_FN = D // 4
FN_GROUPS = 4
FN_GROUP_W = W_FN // FN_GROUPS
N_BRANCH = 4
FF_HIDDEN = ((8 * D // 3 + 255) // 256) * 256
EPS = 1e-6
OFF_QA = 0
OFF_KVA = OFF_QA + Q_LORA
OFF_CONF = OFF_KVA + KV_LORA + QK_ROPE
OFF_SC = OFF_CONF + 2 * W_CONF
OFF_FN = OFF_SC + 3 * W_SC
OFF_GATE = OFF_FN + W_FN
IN_COLS = OFF_GATE + N_BRANCH * D

N_PROMPT = BATCH * SEQ
N_SAMPLE = DEC_BATCH * DEC_SEQ
N_TOK = N_PROMPT + N_SAMPLE
LANE = 128
HALO = 16
SM_SCALE = float((QK_NOPE + QK_ROPE) ** -0.5)
VMEM_LIMIT = 52 * 1024 * 1024

P_QA = 0
P_KV = P_QA + Q_LORA
P_CONF = P_KV + 3 * LANE
P_SC = P_CONF + 2 * W_CONF
P_FN = P_SC + 3 * W_SC
P_GATE = P_FN + W_FN
P_COLS = P_GATE + N_BRANCH * D

ROPE_PERM = np.array(list(range(0, 8)) + list(range(16, 24)) + list(range(8, 16)) + list(range(24, 32)))
ROPE_LANE0 = QK_NOPE


def _dot(a, b):
    return jnp.dot(a, b, preferred_element_type=F32)


def _dot_nt(a, b):
    return lax.dot_general(a, b, (((1,), (1,)), ((), ())), preferred_element_type=F32)


def _sigmoid(x):
    return jax.nn.sigmoid(x)


def _rms(x, g):
    return x * lax.rsqrt(jnp.mean(x * x, axis=-1, keepdims=True) + EPS) * g


def _full(shape):
    return pl.BlockSpec(shape, lambda *_: (0,) * len(shape))


def _params(n_axes):
    return pltpu.CompilerParams(
        dimension_semantics=("arbitrary",) * n_axes, vmem_limit_bytes=VMEM_LIMIT)


def _mod_row(i, n_prompt_tiles, tiles_per_seq):
    return jnp.where(i >= n_prompt_tiles, 1 + (i - n_prompt_tiles) // tiles_per_seq, 0)


def _ada_body(c_ref, w_ref, b_ref, out_ref):
    cv = c_ref[...]
    sc = (cv * _sigmoid(cv)).astype(BF)
    out_ref[0] = _dot(sc, w_ref[0].astype(BF)) + b_ref[0]


def _ada_call(cvec, w_ada, b_ada):
    n_col = 6 * D // D
    return pl.pallas_call(
        _ada_body,
        grid=(DEPTH, n_col),
        in_specs=[
            _full((8, D)),
            pl.BlockSpec((1, D, D), lambda l, j: (l, 0, j)),
            pl.BlockSpec((1, 1, D), lambda l, j: (l, 0, j)),
        ],
        out_specs=pl.BlockSpec((1, 8, D), lambda l, j: (l, 0, j)),
        out_shape=jax.ShapeDtypeStruct((DEPTH, 8, 6 * D), F32),
        compiler_params=_params(2),
        name="ada_mod",
    )(cvec, w_ada, b_ada.reshape(DEPTH, 1, 6 * D))


def _ctx_body(ckv_ref, kr_ref, wkvb_ref, place_ref, k_ref, v_ref):
    kv = _dot(ckv_ref[0, 0].astype(BF), wkvb_ref[0])
    krp = _dot(kr_ref[0, 0].astype(BF), place_ref[...])
    for h in range(N_HEADS):
        sl = slice(h * LANE, (h + 1) * LANE)
        k_ref[0, 0, :, sl] = (kv[:, sl] + krp).astype(BF)
    v_ref[0, 0] = kv[:, N_HEADS * LANE:].astype(BF)


def _ctx_call(cache_ckv, cache_krope, wkvb_all, place):
    return pl.pallas_call(
        _ctx_body,
        grid=(DEPTH, DEC_BATCH),
        in_specs=[
            pl.BlockSpec((1, 1, PAST_LEN, KV_LORA), lambda l, b: (b, l, 0, 0)),
            pl.BlockSpec((1, 1, PAST_LEN, QK_ROPE), lambda l, b: (b, l, 0, 0)),
            pl.BlockSpec((1, KV_LORA, N_HEADS * (LANE + V_HEAD)), lambda l, b: (l, 0, 0)),
            _full((QK_ROPE, LANE)),
        ],
        out_specs=[
            pl.BlockSpec((1, 1, PAST_LEN, N_HEADS * LANE), lambda l, b: (l, b, 0, 0)),
            pl.BlockSpec((1, 1, PAST_LEN, N_HEADS * V_HEAD), lambda l, b: (l, b, 0, 0)),
        ],
        out_shape=[
            jax.ShapeDtypeStruct((DEPTH, DEC_BATCH, PAST_LEN, N_HEADS * LANE), BF),
            jax.ShapeDtypeStruct((DEPTH, DEC_BATCH, PAST_LEN, N_HEADS * V_HEAD), BF),
        ],
        compiler_params=_params(2),
        name="ctx_keys",
    )(cache_ckv, cache_krope, wkvb_all, place)


def _stage1_body(x_ref, mod_ref, g1_ref, w1_ref, gqa_ref, wqb_ref, gkva_ref, wkvb_ref, rope_ref,
                 q_ref, k_ref, v_ref, ckv_ref, kr_ref, u0_ref, gcx_ref, gb_ref, fn_ref, gate_ref,
                 *, n_prompt_tiles, tiles_per_seq):
    i = pl.program_id(0)
    is_sample = i >= n_prompt_tiles
    row = _mod_row(i, n_prompt_tiles, tiles_per_seq)
    sh1 = mod_ref[pl.ds(row, 1), 0:D]
    sc1 = mod_ref[pl.ds(row, 1), D:2 * D]
    hb = (_rms(x_ref[...], g1_ref[...]) * (1.0 + sc1) + sh1).astype(BF)

    qa = _dot(hb, w1_ref[:, P_QA:P_QA + Q_LORA])
    q = _dot(_rms(qa, gqa_ref[...]).astype(BF), wqb_ref[...])

    pkv = _dot(hb, w1_ref[:, P_KV:P_KV + 3 * LANE])
    ckv = _rms(pkv[:, 0:KV_LORA], gkva_ref[...])
    ckv_ref[...] = ckv
    kr3 = pkv[:, KV_LORA:KV_LORA + LANE]
    kr_ref[...] = kr3[:, 0:QK_ROPE]
    lane = lax.broadcasted_iota(jnp.int32, (1, LANE), 1)
    krm = jnp.where((lane >= ROPE_LANE0) & (lane < ROPE_LANE0 + QK_ROPE), kr3, 0.0)
    kv = _dot(ckv.astype(BF), wkvb_ref[...])
    v_ref[...] = kv[:, N_HEADS * LANE:].astype(BF)

    def rope(t):
        return (t * rope_ref[:, 0:LANE]
                + pltpu.roll(t, LANE - 16, 1) * rope_ref[:, LANE:2 * LANE]
                + pltpu.roll(t, 16, 1) * rope_ref[:, 2 * LANE:3 * LANE])

    @pl.when(is_sample)
    def _():
        krr = rope(krm)
        for h in range(N_HEADS):
            sl = slice(h * LANE, (h + 1) * LANE)
            q_ref[:, sl] = (rope(q[:, sl]) * SM_SCALE).astype(BF)
            k_ref[:, sl] = (kv[:, sl] + krr).astype(BF)

    @pl.when(jnp.logical_not(is_sample))
    def _():
        for h in range(N_HEADS):
            sl = slice(h * LANE, (h + 1) * LANE)
            q_ref[:, sl] = (q[:, sl] * SM_SCALE).astype(BF)
            k_ref[:, sl] = (kv[:, sl] + krm).astype(BF)

    pc = _dot(hb, w1_ref[:, P_CONF:P_CONF + 2 * W_CONF])
    u0_ref[...] = pc[:, 0:W_CONF] * _sigmoid(pc[:, W_CONF:])
    ps = _dot(hb, w1_ref[:, P_SC:P_SC + 3 * W_SC])
    gb_ref[...] = ps[:, 0:W_SC]
    gcx_ref[...] = ps[:, W_SC:2 * W_SC] * ps[:, 2 * W_SC:]
    fn_ref[...] = _dot(hb, w1_ref[:, P_FN:P_FN + W_FN]).astype(BF)
    for j in range(N_BRANCH):
        gate_ref[:, j * D:(j + 1) * D] = _sigmoid(
            _dot(hb, w1_ref[:, P_GATE + j * D:P_GATE + (j + 1) * D])).astype(BF)


def _stage1_call(x, mod_l, g1, w1, gqa, wqb, gkva, wkvb, rope_tab, tm):
    n_prompt_tiles = N_PROMPT // tm
    tiles_per_seq = DEC_SEQ // tm
    row_blk = lambda w: pl.BlockSpec((tm, w), lambda i: (i, 0))
    body = functools.partial(_stage1_body, n_prompt_tiles=n_prompt_tiles, tiles_per_seq=tiles_per_seq)
    return pl.pallas_call(
        body,
        grid=(N_TOK // tm,),
        in_specs=[
            row_blk(D),
            _full((8, 6 * D)),
            _full((1, D)),
            _full((D, P_COLS)),
            _full((1, Q_LORA)),
            _full((Q_LORA, N_HEADS * LANE)),
            _full((1, KV_LORA)),
            _full((KV_LORA, N_HEADS * (LANE + V_HEAD))),
            pl.BlockSpec((tm, 3 * LANE),
                         lambda i: (jnp.maximum(i - n_prompt_tiles, 0) % tiles_per_seq, 0)),
        ],
        out_specs=[
            row_blk(N_HEADS * LANE), row_blk(N_HEADS * LANE), row_blk(N_HEADS * V_HEAD),
            row_blk(KV_LORA), row_blk(QK_ROPE), row_blk(W_CONF), row_blk(W_SC), row_blk(W_SC),
            row_blk(W_FN), row_blk(N_BRANCH * D),
        ],
        out_shape=[
            jax.ShapeDtypeStruct((N_TOK, N_HEADS * LANE), BF),
            jax.ShapeDtypeStruct((N_TOK, N_HEADS * LANE), BF),
            jax.ShapeDtypeStruct((N_TOK, N_HEADS * V_HEAD), BF),
            jax.ShapeDtypeStruct((N_TOK, KV_LORA), F32),
            jax.ShapeDtypeStruct((N_TOK, QK_ROPE), F32),
            jax.ShapeDtypeStruct((N_TOK, W_CONF), F32),
            jax.ShapeDtypeStruct((N_TOK, W_SC), F32),
            jax.ShapeDtypeStruct((N_TOK, W_SC), F32),
            jax.ShapeDtypeStruct((N_TOK, W_FN), BF),
            jax.ShapeDtypeStruct((N_TOK, N_BRANCH * D), BF),
        ],
        compiler_params=_params(1),
        name="stage1",
    )(x, mod_l, g1, w1, gqa, wqb, gkva, wkvb, rope_tab)


def _attend_heads(q_ref, kv_refs, o_ref):
    lane = lax.broadcasted_iota(jnp.int32, (1, LANE), 1)
    for hp in range(N_HEADS // 2):
        vsl = slice(hp * LANE, (hp + 1) * LANE)
        acc = None
        for e in range(2):
            h = 2 * hp + e
            sl = slice(h * LANE, (h + 1) * LANE)
            qh = q_ref[:, sl]
            ss = [_dot_nt(qh, k_ref[:, sl]) for k_ref, _ in kv_refs]
            m = functools.reduce(jnp.maximum, [jnp.max(s, axis=-1, keepdims=True) for s in ss])
            ps = [jnp.exp(s - m) for s in ss]
            l = functools.reduce(jnp.add, [jnp.sum(p, axis=-1, keepdims=True) for p in ps])
            in_half = (lane >= e * V_HEAD) & (lane < (e + 1) * V_HEAD)
            o = None
            for p, (_, v_ref) in zip(ps, kv_refs):
                vm = jnp.where(in_half, v_ref[:, vsl], jnp.zeros((), BF))
                t = _dot(p.astype(BF), vm)
                o = t if o is None else o + t
            o = o * (1.0 / l)
            acc = o if acc is None else acc + o
        o_ref[:, vsl] = acc.astype(BF)


def _attn_prompt_body(q_ref, k_ref, v_ref, o_ref):
    _attend_heads(q_ref, [(k_ref, v_ref)], o_ref)


def _attn_prompt_call(q, k, v):
    blk = lambda w: pl.BlockSpec((SEQ, w), lambda b: (b, 0))
    return pl.pallas_call(
        _attn_prompt_body,
        grid=(BATCH,),
        in_specs=[blk(N_HEADS * LANE), blk(N_HEADS * LANE), blk(N_HEADS * V_HEAD)],
        out_specs=blk(N_HEADS * V_HEAD),
        out_shape=jax.ShapeDtypeStruct((N_PROMPT, N_HEADS * V_HEAD), BF),
        compiler_params=_params(1),
        name="attn_prompt",
    )(q, k, v)


def _attn_sample_body(q_ref, kc_ref, vc_ref, k_ref, v_ref, o_ref):
    _attend_heads(q_ref, [(kc_ref.at[0], vc_ref.at[0]), (k_ref, v_ref)], o_ref)


def _attn_sample_call(q, k, v, kc_l, vc_l, tq):
    n_q = DEC_SEQ // tq
    q0 = N_PROMPT // tq
    s0 = N_PROMPT // DEC_SEQ
    return pl.pallas_call(
        _attn_sample_body,
        grid=(DEC_BATCH, n_q),
        in_specs=[
            pl.BlockSpec((tq, N_HEADS * LANE), lambda b, j: (q0 + b * n_q + j, 0)),
            pl.BlockSpec((1, PAST_LEN, N_HEADS * LANE), lambda b, j: (b, 0, 0)),
            pl.BlockSpec((1, PAST_LEN, N_HEADS * V_HEAD), lambda b, j: (b, 0, 0)),
            pl.BlockSpec((DEC_SEQ, N_HEADS * LANE), lambda b, j: (s0 + b, 0)),
            pl.BlockSpec((DEC_SEQ, N_HEADS * V_HEAD), lambda b, j: (s0 + b, 0)),
        ],
        out_specs=pl.BlockSpec((tq, N_HEADS * V_HEAD), lambda b, j: (b * n_q + j, 0)),
        out_shape=jax.ShapeDtypeStruct((N_SAMPLE, N_HEADS * V_HEAD), BF),
        compiler_params=_params(2),
        name="attn_sample",
    )(q, kc_l, vc_l, k, v)


def _conv_body(u_ref, up_ref, un_ref, c_ref, cp_ref, cn_ref, gb_ref,
               wdw_ref, bdw_ref, gln_ref, bln_ref, wsc_ref,
               uo_ref, so_ref, ubuf, cbuf, *, tl, n_prompt_tiles, tiles_per_seq):
    i = pl.program_id(0)
    is_sample = i >= n_prompt_tiles
    j = jnp.maximum(i - n_prompt_tiles, 0) % tiles_per_seq
    has_prev = is_sample & (j > 0)
    has_next = is_sample & (j < tiles_per_seq - 1)
    for src, prv, nxt, buf in ((u_ref, up_ref, un_ref, ubuf), (c_ref, cp_ref, cn_ref, cbuf)):
        buf[0:HALO, :] = jnp.where(has_prev, prv[...], 0.0)
        buf[HALO:HALO + tl, :] = src[...]
        buf[HALO + tl:2 * HALO + tl, :] = jnp.where(has_next, nxt[...], 0.0)

    rc = 32
    for r in range(0, tl, rc):
        off = r + HALO - CONF_K // 2
        acc = ubuf[pl.ds(off, rc), :] * wdw_ref[0:1, :]
        for t in range(1, CONF_K):
            acc = acc + ubuf[pl.ds(off + t, rc), :] * wdw_ref[t:t + 1, :]
        acc = acc + bdw_ref[...]
        mu = jnp.mean(acc, axis=-1, keepdims=True)
        cen = acc - mu
        var = jnp.mean(cen * cen, axis=-1, keepdims=True)
        y = cen * lax.rsqrt(var + EPS) * gln_ref[...] + bln_ref[...]
        uo_ref[pl.ds(r, rc), :] = (y * _sigmoid(y)).astype(BF)

        off = r + HALO - SC_K // 2
        cv = cbuf[pl.ds(off, rc), :] * wsc_ref[0:1, :]
        for t in range(1, SC_K):
            cv = cv + cbuf[pl.ds(off + t, rc), :] * wsc_ref[t:t + 1, :]
        so_ref[pl.ds(r, rc), :] = (gb_ref[pl.ds(r, rc), :] * cv).astype(BF)


def _conv_call(u0, gcx, gb, wdw, bdw, gln, bln, wsc, tl):
    n_tiles = N_TOK // tl
    hb = tl // HALO
    n_halo = N_TOK // HALO
    cur = pl.BlockSpec((tl, W_CONF), lambda i: (i, 0))
    prv = pl.BlockSpec((HALO, W_CONF), lambda i: (jnp.maximum(i * hb - 1, 0), 0))
    nxt = pl.BlockSpec((HALO, W_CONF), lambda i: (jnp.minimum((i + 1) * hb, n_halo - 1), 0))
    body = functools.partial(_conv_body, tl=tl, n_prompt_tiles=N_PROMPT // tl,
                             tiles_per_seq=DEC_SEQ // tl)
    return pl.pallas_call(
        body,
        grid=(n_tiles,),
        in_specs=[cur, prv, nxt, cur, prv, nxt, cur,
                  _full((CONF_K, W_CONF)), _full((1, W_CONF)), _full((1, W_CONF)), _full((1, W_CONF)),
                  _full((SC_K, W_SC))],
        out_specs=[cur, cur],
        out_shape=[jax.ShapeDtypeStruct((N_TOK, W_CONF), BF), jax.ShapeDtypeStruct((N_TOK, W_SC), BF)],
        scratch_shapes=[pltpu.VMEM((tl + 2 * HALO, W_CONF), F32), pltpu.VMEM((tl + 2 * HALO, W_SC), F32)],
        compiler_params=_params(1),
        name="dwconv",
    )(u0, u0, u0, gcx, gcx, gcx, gb, wdw, bdw, gln, bln, wsc)


def _fourier_body(fn_ref, gd_ref, cs_ref, out_ref, rhs, *, seq, scale):
    @pl.when(pl.program_id(1) == 0)
    def _():
        v = _dot(fn_ref[...], gd_ref[...])
        rhs[0:seq, :] = v[:, 0:W_FN].astype(BF)
        rhs[seq:2 * seq, :] = (-v[:, W_FN:]).astype(BF)

    out_ref[...] = (_dot(cs_ref[...], rhs[...]) * scale).astype(BF)


def _fourier_call(fn, gd, cs, seq, n_seq, row0, tl):
    n_t = seq // tl
    s0 = row0 // seq
    body = functools.partial(_fourier_body, seq=seq, scale=float((seq * FN_GROUP_W) ** -0.5))
    return pl.pallas_call(
        body,
        grid=(n_seq, n_t),
        in_specs=[
            pl.BlockSpec((seq, W_FN), lambda b, j: (s0 + b, 0)),
            _full((W_FN, 2 * W_FN)),
            pl.BlockSpec((tl, 2 * seq), lambda b, j: (j, 0)),
        ],
        out_specs=pl.BlockSpec((tl, W_FN), lambda b, j: (b * n_t + j, 0)),
        out_shape=jax.ShapeDtypeStruct((n_seq * seq, W_FN), BF),
        scratch_shapes=[pltpu.VMEM((2 * seq, W_FN), BF)],
        compiler_params=_params(2),
        name=f"fourier_{seq}",
    )(fn, gd, cs)


def _dft_cos_sin(n):
    r = lax.iota(jnp.int32, n)
    m = (r[:, None] * r[None, :]) % n
    ang = m.astype(F32) * F32(2.0 * np.pi / n)
    return jnp.cos(ang), jnp.sin(ang)


def _stage3_body(x_ref, o_ref, u_ref, s_ref, f_ref, gate_ref, mod_ref, g2_ref, gfin_ref,
                 wo_ref, wpw_ref, wsco_ref, wfn_ref, wout_ref, wg_ref, wu_ref, wd_ref,
                 out_ref, *, n_prompt_tiles, tiles_per_seq, final, chunk):
    row = _mod_row(pl.program_id(0), n_prompt_tiles, tiles_per_seq)
    mod = lambda k: mod_ref[pl.ds(row, 1), k * D:(k + 1) * D]
    merged = gate_ref[:, 0:D].astype(F32) * _dot(o_ref[...], wo_ref[...])
    merged = merged + gate_ref[:, D:2 * D].astype(F32) * _dot(u_ref[...], wpw_ref[...])
    merged = merged + gate_ref[:, 2 * D:3 * D].astype(F32) * _dot(s_ref[...], wsco_ref[...])
    merged = merged + gate_ref[:, 3 * D:4 * D].astype(F32) * _dot(f_ref[...], wfn_ref[...])
    x1 = x_ref[...] + mod(2) * _dot(merged.astype(BF), wout_ref[...])

    h2 = (_rms(x1, g2_ref[...]) * (1.0 + mod(4)) + mod(3)).astype(BF)
    acc = None
    for c0 in range(0, FF_HIDDEN, chunk):
        a = _dot(h2, wg_ref[:, c0:c0 + chunk])
        b = _dot(h2, wu_ref[:, c0:c0 + chunk])
        t = _dot((a * _sigmoid(a) * b).astype(BF), wd_ref[c0:c0 + chunk, :])
        acc = t if acc is None else acc + t
    x2 = x1 + mod(5) * acc
    out_ref[...] = _rms(x2, gfin_ref[...]) if final else x2


def _stage3_call(x, o, u, s, f, gates, mod_l, g2, gfin, wo, wpw, wsco, wfn, wout, wg, wu, wd, tm, final):
    row_blk = lambda w: pl.BlockSpec((tm, w), lambda i: (i, 0))
    body = functools.partial(_stage3_body, n_prompt_tiles=N_PROMPT // tm, tiles_per_seq=DEC_SEQ // tm,
                             final=final, chunk=256)
    return pl.pallas_call(
        body,
        grid=(N_TOK // tm,),
        in_specs=[
            row_blk(D), row_blk(N_HEADS * V_HEAD), row_blk(W_CONF), row_blk(W_SC), row_blk(W_FN),
            row_blk(N_BRANCH * D), _full((8, 6 * D)), _full((1, D)), _full((1, D)),
            _full((N_HEADS * V_HEAD, D)), _full((W_CONF, D)), _full((W_SC, D)), _full((W_FN, D)),
            _full((D, D)), _full((D, FF_HIDDEN)), _full((D, FF_HIDDEN)), _full((FF_HIDDEN, D)),
        ],
        out_specs=row_blk(D),
        out_shape=jax.ShapeDtypeStruct((N_TOK, D), F32),
        compiler_params=_params(1),
        name="stage3",
    )(x, o, u, s, f, gates, mod_l, g2, gfin, wo, wpw, wsco, wfn, wout, wg, wu, wd)


def _rope_table():
    rows = DEC_SEQ // GRID_W
    row_pos = jnp.repeat(jnp.arange(rows, dtype=F32), GRID_W)
    col_pos = jnp.tile(jnp.arange(GRID_W, dtype=F32), rows)
    inv = ROPE_THETA ** (-jnp.arange(0, AXIS_ROPE, 2, dtype=F32) / AXIS_ROPE)
    ang = jnp.concatenate([row_pos[:, None] * inv, col_pos[:, None] * inv], axis=1)
    cos, sin = jnp.cos(ang), jnp.sin(ang)
    ones = jnp.ones((DEC_SEQ, ROPE_LANE0), F32)
    zeros = jnp.zeros((DEC_SEQ, ROPE_LANE0), F32)
    z16 = jnp.zeros((DEC_SEQ, 16), F32)
    tail1 = jnp.ones((DEC_SEQ, LANE - ROPE_LANE0 - QK_ROPE), F32)
    tail0 = jnp.zeros((DEC_SEQ, LANE - ROPE_LANE0 - QK_ROPE), F32)
    cos_t = jnp.concatenate([ones, cos, cos, tail1], axis=1)
    sinm_t = jnp.concatenate([zeros, -sin, z16, tail0], axis=1)
    sinp_t = jnp.concatenate([zeros, z16, sin, tail0], axis=1)
    return jnp.concatenate([cos_t, sinm_t, sinp_t], axis=1)


def _layer_weights(l, w_in, w_qb, w_kvb):
    wi = w_in[l]
    w_kr = wi[:, OFF_KVA + KV_LORA:OFF_CONF]
    z32 = jnp.zeros((D, 32), F32)
    w1 = jnp.concatenate([
        wi[:, OFF_QA:OFF_KVA],
        wi[:, OFF_KVA:OFF_KVA + KV_LORA], w_kr, z32, w_kr[:, ROPE_PERM], z32,
        wi[:, OFF_CONF:],
    ], axis=1).astype(BF)
    qb = w_qb[l].reshape(Q_LORA, N_HEADS, QK_NOPE + QK_ROPE)
    wqb = jnp.concatenate([
        qb[..., :QK_NOPE], qb[..., QK_NOPE:][..., ROPE_PERM],
        jnp.zeros((Q_LORA, N_HEADS, LANE - QK_NOPE - QK_ROPE), F32),
    ], axis=-1).reshape(Q_LORA, N_HEADS * LANE).astype(BF)
    return w1, wqb


def _kvb_layout(w_kvb):
    kvb = w_kvb.reshape(DEPTH, KV_LORA, N_HEADS, QK_NOPE + V_HEAD)
    kpart = jnp.concatenate([kvb[..., :QK_NOPE], jnp.zeros((DEPTH, KV_LORA, N_HEADS, LANE - QK_NOPE), F32)],
                            axis=-1).reshape(DEPTH, KV_LORA, N_HEADS * LANE)
    vpart = kvb[..., QK_NOPE:].reshape(DEPTH, KV_LORA, N_HEADS * V_HEAD)
    return jnp.concatenate([kpart, vpart], axis=-1).astype(BF)


def kernel(x_prompt, x_sample, cache_ckv, cache_krope, c, c_ctx, w_ada, b_ada, g_norm1, g_norm2, w_in, g_qa, w_qb, g_kva, w_kvb, w_o_mla, w_conf_dw, b_conf_dw, g_conf_ln, b_conf_ln, w_conf_pw, w_sc_conv, w_sc_out, w_fn, w_out, w_ffn_gate, w_ffn_up, w_ffn_down, g_final):
    tm = 256
    x = jnp.concatenate([x_prompt.reshape(N_PROMPT, D), x_sample.reshape(N_SAMPLE, D)], axis=0)
    cvec = jnp.concatenate([c_ctx[None, :], c, jnp.zeros((8 - 1 - DEC_BATCH, D), F32)], axis=0)
    mod = _ada_call(cvec, w_ada, b_ada)

    rope_tab = _rope_table()
    place = np.zeros((QK_ROPE, LANE), np.float32)
    place[ROPE_PERM, ROPE_LANE0 + np.arange(QK_ROPE)] = 1.0
    wkvb_all = _kvb_layout(w_kvb)
    kc, vc = _ctx_call(cache_ckv, cache_krope, wkvb_all, jnp.asarray(place, BF))

    cg, sg = _dft_cos_sin(FN_GROUP_W)
    eye = jnp.eye(FN_GROUPS, dtype=F32)
    gd = jnp.concatenate([jnp.kron(eye, cg), jnp.kron(eye, sg)], axis=1).astype(BF)
    cs_p = jnp.concatenate(_dft_cos_sin(SEQ), axis=1).astype(BF)
    cs_s = jnp.concatenate(_dft_cos_sin(DEC_SEQ), axis=1).astype(BF)

    new_ckv, new_krope = [], []
    for l in range(DEPTH):
        w1, wqb = _layer_weights(l, w_in, w_qb, w_kvb)
        q, k, v, ckv, kr, u0, gcx, gb, fn, gates = _stage1_call(
            x, mod[l], g_norm1[l][None, :], w1, g_qa[l][None, :], wqb, g_kva[l][None, :],
            wkvb_all[l], rope_tab, tm)
        new_ckv.append(ckv[:N_PROMPT].reshape(BATCH, SEQ, KV_LORA))
        new_krope.append(kr[:N_PROMPT].reshape(BATCH, SEQ, QK_ROPE))

        o = jnp.concatenate([_attn_prompt_call(q, k, v),
                             _attn_sample_call(q, k, v, kc[l], vc[l], 256)], axis=0)
        u, s = _conv_call(u0, gcx, gb, w_conf_dw[l], b_conf_dw[l][None, :], g_conf_ln[l][None, :],
                          b_conf_ln[l][None, :], w_sc_conv[l], 256)
        f = jnp.concatenate([_fourier_call(fn, gd, cs_p, SEQ, BATCH, 0, SEQ),
                             _fourier_call(fn, gd, cs_s, DEC_SEQ, DEC_BATCH, N_PROMPT, 256)], axis=0)
        x = _stage3_call(
            x, o, u, s, f, gates, mod[l], g_norm2[l][None, :], g_final[None, :],
            w_o_mla[l].astype(BF), w_conf_pw[l].astype(BF), w_sc_out[l].astype(BF), w_fn[l].astype(BF),
            w_out[l].astype(BF), w_ffn_gate[l].astype(BF), w_ffn_up[l].astype(BF),
            w_ffn_down[l].astype(BF), tm, l == DEPTH - 1)

    y_prompt = x[:N_PROMPT].reshape(BATCH, SEQ, D)
    y_sample = x[N_PROMPT:].reshape(DEC_BATCH, DEC_SEQ, D)
    return (y_prompt, y_sample, jnp.stack(new_ckv, axis=1), jnp.stack(new_krope, axis=1))
```

```python
import functools

import numpy as np
import jax
import jax.numpy as jnp
from jax import lax
from jax.experimental import pallas as pl
from jax.experimental.pallas import tpu as pltpu

BF = jnp.bfloat16
F32 = jnp.float32

D = 1024
BATCH = 16
SEQ = 256
DEPTH = 2
DEC_BATCH = 2
DEC_SEQ = 2048
PAST_LEN = 256
GRID_W = 64
N_HEADS = 8
QK_NOPE = 64
QK_ROPE = 32
V_HEAD = 64
Q_LORA = 384
KV_LORA = 256
AXIS_ROPE = QK_ROPE // 2
ROPE_THETA = 10000.0
W_CONF = D // 4
CONF_K = 31
W_SC = D // 4
SC_K = 3
W_FN = D // 4
FN_GROUPS = 4
FN_GROUP_W = W_FN // FN_GROUPS
N_BRANCH = 4
FF_HIDDEN = ((8 * D // 3 + 255) // 256) * 256
EPS = 1e-6
OFF_QA = 0
OFF_KVA = OFF_QA + Q_LORA
OFF_CONF = OFF_KVA + KV_LORA + QK_ROPE
OFF_SC = OFF_CONF + 2 * W_CONF
OFF_FN = OFF_SC + 3 * W_SC
OFF_GATE = OFF_FN + W_FN
IN_COLS = OFF_GATE + N_BRANCH * D

N_PROMPT = BATCH * SEQ
N_SAMPLE = DEC_BATCH * DEC_SEQ
N_TOK = N_PROMPT + N_SAMPLE
LANE = 128
HALO = 16
SM_SCALE = float((QK_NOPE + QK_ROPE) ** -0.5)
VMEM_LIMIT = 56 * 1024 * 1024

P_QA = 0
P_KV = P_QA + Q_LORA
P_CONF = P_KV + 3 * LANE
P_SC = P_CONF + 2 * W_CONF
P_FN = P_SC + 3 * W_SC
P_GATE = P_FN + W_FN
P_COLS = P_GATE + N_BRANCH * D

ROPE_PERM = np.array(list(range(0, 8)) + list(range(16, 24)) + list(range(8, 16)) + list(range(24, 32)))
ROPE_LANE0 = QK_NOPE


def _dot(a, b):
    return jnp.dot(a, b, preferred_element_type=F32)


def _dot_nt(a, b):
    return lax.dot_general(a, b, (((1,), (1,)), ((), ())), preferred_element_type=F32)


def _sigmoid(x):
    return jax.nn.sigmoid(x)


def _rms(x, g):
    return x * lax.rsqrt(jnp.mean(x * x, axis=-1, keepdims=True) + EPS) * g


def _full(shape):
    return pl.BlockSpec(shape, lambda *_: (0,) * len(shape))


def _params(n_axes):
    return pltpu.CompilerParams(
        dimension_semantics=("arbitrary",) * n_axes, vmem_limit_bytes=VMEM_LIMIT)


def _mod_row(i, n_prompt_tiles, tiles_per_seq):
    return jnp.where(i >= n_prompt_tiles, 1 + (i - n_prompt_tiles) // tiles_per_seq, 0)


def _ada_body(c_ref, w_ref, b_ref, out_ref):
    cv = c_ref[...]
    sc = (cv * _sigmoid(cv)).astype(BF)
    out_ref[0] = _dot(sc, w_ref[0].astype(BF)) + b_ref[0]


def _ada_call(cvec, w_ada, b_ada):
    n_col = 6 * D // D
    return pl.pallas_call(
        _ada_body,
        grid=(DEPTH, n_col),
        in_specs=[
            _full((8, D)),
            pl.BlockSpec((1, D, D), lambda l, j: (l, 0, j)),
            pl.BlockSpec((1, 1, D), lambda l, j: (l, 0, j)),
        ],
        out_specs=pl.BlockSpec((1, 8, D), lambda l, j: (l, 0, j)),
        out_shape=jax.ShapeDtypeStruct((DEPTH, 8, 6 * D), F32),
        compiler_params=_params(2),
        name="ada_mod",
    )(cvec, w_ada, b_ada.reshape(DEPTH, 1, 6 * D))


def _ctx_body(ckv_ref, kr_ref, wkvb_ref, place_ref, k_ref, v_ref):
    kv = _dot(ckv_ref[0, 0].astype(BF), wkvb_ref[0])
    krp = _dot(kr_ref[0, 0].astype(BF), place_ref[...])
    for h in range(N_HEADS):
        sl = slice(h * LANE, (h + 1) * LANE)
        k_ref[0, 0, :, sl] = (kv[:, sl] + krp).astype(BF)
    v_ref[0, 0] = kv[:, N_HEADS * LANE:].astype(BF)


def _ctx_call(cache_ckv, cache_krope, wkvb_all, place):
    return pl.pallas_call(
        _ctx_body,
        grid=(DEPTH, DEC_BATCH),
        in_specs=[
            pl.BlockSpec((1, 1, PAST_LEN, KV_LORA), lambda l, b: (b, l, 0, 0)),
            pl.BlockSpec((1, 1, PAST_LEN, QK_ROPE), lambda l, b: (b, l, 0, 0)),
            pl.BlockSpec((1, KV_LORA, N_HEADS * (LANE + V_HEAD)), lambda l, b: (l, 0, 0)),
            _full((QK_ROPE, LANE)),
        ],
        out_specs=[
            pl.BlockSpec((1, 1, PAST_LEN, N_HEADS * LANE), lambda l, b: (l, b, 0, 0)),
            pl.BlockSpec((1, 1, PAST_LEN, N_HEADS * V_HEAD), lambda l, b: (l, b, 0, 0)),
        ],
        out_shape=[
            jax.ShapeDtypeStruct((DEPTH, DEC_BATCH, PAST_LEN, N_HEADS * LANE), BF),
            jax.ShapeDtypeStruct((DEPTH, DEC_BATCH, PAST_LEN, N_HEADS * V_HEAD), BF),
        ],
        compiler_params=_params(2),
        name="ctx_keys",
    )(cache_ckv, cache_krope, wkvb_all, place)


def _stage1_body(xp_ref, xs_ref, mod_ref, g1_ref, w1_ref, gqa_ref, wqb_ref, gkva_ref, wkvb_ref, rope_ref,
                 q_ref, k_ref, v_ref, ckv_ref, kr_ref, u0_ref, gcx_ref, gb_ref, fn_ref, gate_ref,
                 *, n_prompt_tiles, tiles_per_seq):
    i = pl.program_id(0)
    is_sample = i >= n_prompt_tiles
    row = _mod_row(i, n_prompt_tiles, tiles_per_seq)
    sh1 = mod_ref[pl.ds(row, 1), 0:D]
    sc1 = mod_ref[pl.ds(row, 1), D:2 * D]
    x = jnp.where(is_sample, xs_ref[...], xp_ref[...])
    hb = (_rms(x, g1_ref[...]) * (1.0 + sc1) + sh1).astype(BF)

    qa = _dot(hb, w1_ref[:, P_QA:P_QA + Q_LORA])
    q = _dot(_rms(qa, gqa_ref[...]).astype(BF), wqb_ref[...])

    pkv = _dot(hb, w1_ref[:, P_KV:P_KV + 3 * LANE])
    ckv = _rms(pkv[:, 0:KV_LORA], gkva_ref[...])
    ckv_ref[...] = ckv
    kr3 = pkv[:, KV_LORA:KV_LORA + LANE]
    kr_ref[...] = kr3[:, 0:QK_ROPE]
    lane = lax.broadcasted_iota(jnp.int32, (1, LANE), 1)
    krm = jnp.where((lane >= ROPE_LANE0) & (lane < ROPE_LANE0 + QK_ROPE), kr3, 0.0)
    kv = _dot(ckv.astype(BF), wkvb_ref[...])
    v_ref[...] = kv[:, N_HEADS * LANE:].astype(BF)

    def rope(t):
        return (t * rope_ref[:, 0:LANE]
                + pltpu.roll(t, LANE - 16, 1) * rope_ref[:, LANE:2 * LANE]
                + pltpu.roll(t, 16, 1) * rope_ref[:, 2 * LANE:3 * LANE])

    krr = rope(krm)
    for h in range(N_HEADS):
        sl = slice(h * LANE, (h + 1) * LANE)
        q_ref[:, sl] = (rope(q[:, sl]) * SM_SCALE).astype(BF)
        k_ref[:, sl] = (kv[:, sl] + krr).astype(BF)

    pc = _dot(hb, w1_ref[:, P_CONF:P_CONF + 2 * W_CONF])
    u0_ref[...] = pc[:, 0:W_CONF] * _sigmoid(pc[:, W_CONF:])
    ps = _dot(hb, w1_ref[:, P_SC:P_SC + 3 * W_SC])
    gb_ref[...] = ps[:, 0:W_SC]
    gcx_ref[...] = ps[:, W_SC:2 * W_SC] * ps[:, 2 * W_SC:]
    fn_ref[...] = _dot(hb, w1_ref[:, P_FN:P_FN + W_FN]).astype(BF)
    for j in range(N_BRANCH):
        gate_ref[:, j * D:(j + 1) * D] = _sigmoid(
            _dot(hb, w1_ref[:, P_GATE + j * D:P_GATE + (j + 1) * D])).astype(BF)


def _split_specs(tm, w):
    n_p = N_PROMPT // tm
    return (pl.BlockSpec((tm, w), lambda i: (jnp.minimum(i, n_p - 1), 0)),
            pl.BlockSpec((tm, w), lambda i: (jnp.maximum(i - n_p, 0), 0)))


def _stage1_call(xp, xs, mod_l, g1, w1, gqa, wqb, gkva, wkvb, rope_tab, tm):
    n_prompt_tiles = N_PROMPT // tm
    tiles_per_seq = DEC_SEQ // tm
    row_blk = lambda w: pl.BlockSpec((tm, w), lambda i: (i, 0))
    body = functools.partial(_stage1_body, n_prompt_tiles=n_prompt_tiles, tiles_per_seq=tiles_per_seq)
    return pl.pallas_call(
        body,
        grid=(N_TOK // tm,),
        in_specs=[
            *_split_specs(tm, D),
            _full((8, 6 * D)),
            _full((1, D)),
            _full((D, P_COLS)),
            _full((1, Q_LORA)),
            _full((Q_LORA, N_HEADS * LANE)),
            _full((1, KV_LORA)),
            _full((KV_LORA, N_HEADS * (LANE + V_HEAD))),
            pl.BlockSpec((tm, 3 * LANE),
                         lambda i: (jnp.where(i < n_prompt_tiles, 0,
                                              1 + (i - n_prompt_tiles) % tiles_per_seq), 0)),
        ],
        out_specs=[
            row_blk(N_HEADS * LANE), row_blk(N_HEADS * LANE), row_blk(N_HEADS * V_HEAD),
            row_blk(KV_LORA), row_blk(QK_ROPE), row_blk(W_CONF), row_blk(W_SC), row_blk(W_SC),
            row_blk(W_FN), row_blk(N_BRANCH * D),
        ],
        out_shape=[
            jax.ShapeDtypeStruct((N_TOK, N_HEADS * LANE), BF),
            jax.ShapeDtypeStruct((N_TOK, N_HEADS * LANE), BF),
            jax.ShapeDtypeStruct((N_TOK, N_HEADS * V_HEAD), BF),
            jax.ShapeDtypeStruct((N_TOK, KV_LORA), F32),
            jax.ShapeDtypeStruct((N_TOK, QK_ROPE), F32),
            jax.ShapeDtypeStruct((N_TOK, W_CONF), F32),
            jax.ShapeDtypeStruct((N_TOK, W_SC), F32),
            jax.ShapeDtypeStruct((N_TOK, W_SC), F32),
            jax.ShapeDtypeStruct((N_TOK, W_FN), BF),
            jax.ShapeDtypeStruct((N_TOK, N_BRANCH * D), BF),
        ],
        compiler_params=_params(1),
        name="stage1",
    )(xp, xs, mod_l, g1, w1, gqa, wqb, gkva, wkvb, rope_tab)


def _attend_heads(q_ref, kv_refs, o_ref):
    lane = lax.broadcasted_iota(jnp.int32, (1, LANE), 1)
    for hp in range(N_HEADS // 2):
        vsl = slice(hp * LANE, (hp + 1) * LANE)
        acc = None
        for e in range(2):
            h = 2 * hp + e
            sl = slice(h * LANE, (h + 1) * LANE)
            qh = q_ref[:, sl]
            ss = [_dot_nt(qh, k_ref[:, sl]) for k_ref, _ in kv_refs]
            m = functools.reduce(jnp.maximum, [jnp.max(s, axis=-1, keepdims=True) for s in ss])
            ps = [jnp.exp(s - m) for s in ss]
            l = functools.reduce(jnp.add, [jnp.sum(p, axis=-1, keepdims=True) for p in ps])
            in_half = (lane >= e * V_HEAD) & (lane < (e + 1) * V_HEAD)
            o = None
            for p, (_, v_ref) in zip(ps, kv_refs):
                vm = jnp.where(in_half, v_ref[:, vsl], jnp.zeros((), BF))
                t = _dot(p.astype(BF), vm)
                o = t if o is None else o + t
            o = o * (1.0 / l)
            acc = o if acc is None else acc + o
        o_ref[:, vsl] = acc.astype(BF)


def _attn_prompt_body(q_ref, k_ref, v_ref, o_ref):
    _attend_heads(q_ref, [(k_ref, v_ref)], o_ref)


def _attn_prompt_call(q, k, v):
    blk = lambda w: pl.BlockSpec((SEQ, w), lambda b: (b, 0))
    return pl.pallas_call(
        _attn_prompt_body,
        grid=(BATCH,),
        in_specs=[blk(N_HEADS * LANE), blk(N_HEADS * LANE), blk(N_HEADS * V_HEAD)],
        out_specs=blk(N_HEADS * V_HEAD),
        out_shape=jax.ShapeDtypeStruct((N_PROMPT, N_HEADS * V_HEAD), BF),
        compiler_params=_params(1),
        name="attn_prompt",
    )(q, k, v)


def _attn_sample_body(q_ref, kc_ref, vc_ref, k_ref, v_ref, o_ref):
    _attend_heads(q_ref, [(kc_ref.at[0], vc_ref.at[0]), (k_ref, v_ref)], o_ref)


def _attn_sample_call(q, k, v, kc_l, vc_l, tq):
    n_q = DEC_SEQ // tq
    q0 = N_PROMPT // tq
    s0 = N_PROMPT // DEC_SEQ
    return pl.pallas_call(
        _attn_sample_body,
        grid=(DEC_BATCH, n_q),
        in_specs=[
            pl.BlockSpec((tq, N_HEADS * LANE), lambda b, j: (q0 + b * n_q + j, 0)),
            pl.BlockSpec((1, PAST_LEN, N_HEADS * LANE), lambda b, j: (b, 0, 0)),
            pl.BlockSpec((1, PAST_LEN, N_HEADS * V_HEAD), lambda b, j: (b, 0, 0)),
            pl.BlockSpec((DEC_SEQ, N_HEADS * LANE), lambda b, j: (s0 + b, 0)),
            pl.BlockSpec((DEC_SEQ, N_HEADS * V_HEAD), lambda b, j: (s0 + b, 0)),
        ],
        out_specs=pl.BlockSpec((tq, N_HEADS * V_HEAD), lambda b, j: (b * n_q + j, 0)),
        out_shape=jax.ShapeDtypeStruct((N_SAMPLE, N_HEADS * V_HEAD), BF),
        compiler_params=_params(2),
        name="attn_sample",
    )(q, kc_l, vc_l, k, v)


def _conv_body(u_ref, up_ref, un_ref, c_ref, cp_ref, cn_ref, gb_ref,
               wdw_ref, bdw_ref, gln_ref, bln_ref, wsc_ref,
               uo_ref, so_ref, ubuf, cbuf, *, tl, n_prompt_tiles, tiles_per_seq):
    i = pl.program_id(0)
    is_sample = i >= n_prompt_tiles
    j = jnp.maximum(i - n_prompt_tiles, 0) % tiles_per_seq
    has_prev = is_sample & (j > 0)
    has_next = is_sample & (j < tiles_per_seq - 1)
    for src, prv, nxt, buf in ((u_ref, up_ref, un_ref, ubuf), (c_ref, cp_ref, cn_ref, cbuf)):
        buf[0:HALO, :] = jnp.where(has_prev, prv[...], 0.0)
        buf[HALO:HALO + tl, :] = src[...]
        buf[HALO + tl:2 * HALO + tl, :] = jnp.where(has_next, nxt[...], 0.0)

    rc = 32
    for r in range(0, tl, rc):
        off = r + HALO - CONF_K // 2
        acc = ubuf[pl.ds(off, rc), :] * wdw_ref[0:1, :]
        for t in range(1, CONF_K):
            acc = acc + ubuf[pl.ds(off + t, rc), :] * wdw_ref[t:t + 1, :]
        acc = acc + bdw_ref[...]
        mu = jnp.mean(acc, axis=-1, keepdims=True)
        cen = acc - mu
        var = jnp.mean(cen * cen, axis=-1, keepdims=True)
        y = cen * lax.rsqrt(var + EPS) * gln_ref[...] + bln_ref[...]
        uo_ref[pl.ds(r, rc), :] = (y * _sigmoid(y)).astype(BF)

        off = r + HALO - SC_K // 2
        cv = cbuf[pl.ds(off, rc), :] * wsc_ref[0:1, :]
        for t in range(1, SC_K):
            cv = cv + cbuf[pl.ds(off + t, rc), :] * wsc_ref[t:t + 1, :]
        so_ref[pl.ds(r, rc), :] = (gb_ref[pl.ds(r, rc), :] * cv).astype(BF)


def _conv_call(u0, gcx, gb, wdw, bdw, gln, bln, wsc, tl):
    n_tiles = N_TOK // tl
    hb = tl // HALO
    n_halo = N_TOK // HALO
    cur = pl.BlockSpec((tl, W_CONF), lambda i: (i, 0))
    prv = pl.BlockSpec((HALO, W_CONF), lambda i: (jnp.maximum(i * hb - 1, 0), 0))
    nxt = pl.BlockSpec((HALO, W_CONF), lambda i: (jnp.minimum((i + 1) * hb, n_halo - 1), 0))
    body = functools.partial(_conv_body, tl=tl, n_prompt_tiles=N_PROMPT // tl,
                             tiles_per_seq=DEC_SEQ // tl)
    return pl.pallas_call(
        body,
        grid=(n_tiles,),
        in_specs=[cur, prv, nxt, cur, prv, nxt, cur,
                  _full((CONF_K, W_CONF)), _full((1, W_CONF)), _full((1, W_CONF)), _full((1, W_CONF)),
                  _full((SC_K, W_SC))],
        out_specs=[cur, cur],
        out_shape=[jax.ShapeDtypeStruct((N_TOK, W_CONF), BF), jax.ShapeDtypeStruct((N_TOK, W_SC), BF)],
        scratch_shapes=[pltpu.VMEM((tl + 2 * HALO, W_CONF), F32), pltpu.VMEM((tl + 2 * HALO, W_SC), F32)],
        compiler_params=_params(1),
        name="dwconv",
    )(u0, u0, u0, gcx, gcx, gcx, gb, wdw, bdw, gln, bln, wsc)


def _fourier_body(fn_ref, gd_ref, cs_ref, out_ref, rhs, *, seq, scale):
    @pl.when(pl.program_id(1) == 0)
    def _():
        v = _dot(fn_ref[...], gd_ref[...])
        rhs[0:seq, :] = v[:, 0:W_FN].astype(BF)
        rhs[seq:2 * seq, :] = (-v[:, W_FN:]).astype(BF)

    out_ref[...] = (_dot(cs_ref[...], rhs[...]) * scale).astype(BF)


def _fourier_call(fn, gd, cs, seq, n_seq, row0, tl):
    n_t = seq // tl
    s0 = row0 // seq
    body = functools.partial(_fourier_body, seq=seq, scale=float((seq * FN_GROUP_W) ** -0.5))
    return pl.pallas_call(
        body,
        grid=(n_seq, n_t),
        in_specs=[
            pl.BlockSpec((seq, W_FN), lambda b, j: (s0 + b, 0)),
            _full((W_FN, 2 * W_FN)),
            pl.BlockSpec((tl, 2 * seq), lambda b, j: (j, 0)),
        ],
        out_specs=pl.BlockSpec((tl, W_FN), lambda b, j: (b * n_t + j, 0)),
        out_shape=jax.ShapeDtypeStruct((n_seq * seq, W_FN), BF),
        scratch_shapes=[pltpu.VMEM((2 * seq, W_FN), BF)],
        compiler_params=_params(2),
        name=f"fourier_{seq}",
    )(fn, gd, cs)


def _dft_cos_sin(n):
    r = np.arange(n, dtype=np.int64)
    ang = ((r[:, None] * r[None, :]) % n).astype(np.float64) * (2.0 * np.pi / n)
    return np.cos(ang).astype(np.float32), np.sin(ang).astype(np.float32)


def _stage3_body(xp_ref, xs_ref, op_ref, os_ref, u_ref, s_ref, fp_ref, fs_ref, gate_ref,
                 mod_ref, g2_ref, gfin_ref,
                 wo_ref, wpw_ref, wsco_ref, wfn_ref, wout_ref, wg_ref, wu_ref, wd_ref,
                 outp_ref, outs_ref, *, n_prompt_tiles, tiles_per_seq, final, chunk):
    i = pl.program_id(0)
    is_sample = i >= n_prompt_tiles
    row = _mod_row(i, n_prompt_tiles, tiles_per_seq)
    mod = lambda k: mod_ref[pl.ds(row, 1), k * D:(k + 1) * D]
    pick = lambda p_ref, s_ref: jnp.where(is_sample, s_ref[...], p_ref[...])
    merged = gate_ref[:, 0:D].astype(F32) * _dot(pick(op_ref, os_ref), wo_ref[...])
    merged = merged + gate_ref[:, D:2 * D].astype(F32) * _dot(u_ref[...], wpw_ref[...])
    merged = merged + gate_ref[:, 2 * D:3 * D].astype(F32) * _dot(s_ref[...], wsco_ref[...])
    merged = merged + gate_ref[:, 3 * D:4 * D].astype(F32) * _dot(pick(fp_ref, fs_ref), wfn_ref[...])
    x1 = pick(xp_ref, xs_ref) + mod(2) * _dot(merged.astype(BF), wout_ref[...])

    h2 = (_rms(x1, g2_ref[...]) * (1.0 + mod(4)) + mod(3)).astype(BF)
    acc = None
    for c0 in range(0, FF_HIDDEN, chunk):
        a = _dot(h2, wg_ref[:, c0:c0 + chunk])
        b = _dot(h2, wu_ref[:, c0:c0 + chunk])
        t = _dot((a * _sigmoid(a) * b).astype(BF), wd_ref[c0:c0 + chunk, :])
        acc = t if acc is None else acc + t
    x2 = x1 + mod(5) * acc
    y = _rms(x2, gfin_ref[...]) if final else x2

    @pl.when(is_sample)
    def _():
        outs_ref[...] = y

    @pl.when(jnp.logical_not(is_sample))
    def _():
        outp_ref[...] = y


def _stage3_call(xp, xs, o_p, o_s, u, s, f_p, f_s, gates, mod_l, g2, gfin,
                 wo, wpw, wsco, wfn, wout, wg, wu, wd, tm, final):
    row_blk = lambda w: pl.BlockSpec((tm, w), lambda i: (i, 0))
    body = functools.partial(_stage3_body, n_prompt_tiles=N_PROMPT // tm, tiles_per_seq=DEC_SEQ // tm,
                             final=final, chunk=256)
    return pl.pallas_call(
        body,
        grid=(N_TOK // tm,),
        in_specs=[
            *_split_specs(tm, D), *_split_specs(tm, N_HEADS * V_HEAD), row_blk(W_CONF), row_blk(W_SC),
            *_split_specs(tm, W_FN), row_blk(N_BRANCH * D), _full((8, 6 * D)), _full((1, D)), _full((1, D)),
            _full((N_HEADS * V_HEAD, D)), _full((W_CONF, D)), _full((W_SC, D)), _full((W_FN, D)),
            _full((D, D)), _full((D, FF_HIDDEN)), _full((D, FF_HIDDEN)), _full((FF_HIDDEN, D)),
        ],
        out_specs=list(_split_specs(tm, D)),
        out_shape=[jax.ShapeDtypeStruct((N_PROMPT, D), F32), jax.ShapeDtypeStruct((N_SAMPLE, D), F32)],
        compiler_params=_params(1),
        name="stage3",
    )(xp, xs, o_p, o_s, u, s, f_p, f_s, gates, mod_l, g2, gfin, wo, wpw, wsco, wfn, wout, wg, wu, wd)


def _rope_table(tm):
    rows = DEC_SEQ // GRID_W
    row_pos = np.repeat(np.arange(rows, dtype=np.float64), GRID_W)
    col_pos = np.tile(np.arange(GRID_W, dtype=np.float64), rows)
    inv = ROPE_THETA ** (-np.arange(0, AXIS_ROPE, 2, dtype=np.float64) / AXIS_ROPE)
    ang = np.concatenate([row_pos[:, None] * inv, col_pos[:, None] * inv], axis=1)
    half = QK_ROPE // 2
    tab = np.zeros((tm + DEC_SEQ, 3 * LANE), np.float64)
    tab[:, 0:LANE] = 1.0
    tab[tm:, ROPE_LANE0:ROPE_LANE0 + half] = np.cos(ang)
    tab[tm:, ROPE_LANE0 + half:ROPE_LANE0 + QK_ROPE] = np.cos(ang)
    tab[tm:, LANE + ROPE_LANE0:LANE + ROPE_LANE0 + half] = -np.sin(ang)
    tab[tm:, 2 * LANE + ROPE_LANE0 + half:2 * LANE + ROPE_LANE0 + QK_ROPE] = np.sin(ang)
    return jnp.asarray(tab.astype(np.float32))


def _layer_weights(l, w_in, w_qb, w_kvb):
    wi = w_in[l]
    w_kr = wi[:, OFF_KVA + KV_LORA:OFF_CONF]
    z32 = jnp.zeros((D, 32), F32)
    w1 = jnp.concatenate([
        wi[:, OFF_QA:OFF_KVA],
        wi[:, OFF_KVA:OFF_KVA + KV_LORA], w_kr, z32, w_kr[:, ROPE_PERM], z32,
        wi[:, OFF_CONF:],
    ], axis=1).astype(BF)
    qb = w_qb[l].reshape(Q_LORA, N_HEADS, QK_NOPE + QK_ROPE)
    wqb = jnp.concatenate([
        qb[..., :QK_NOPE], qb[..., QK_NOPE:][..., ROPE_PERM],
        jnp.zeros((Q_LORA, N_HEADS, LANE - QK_NOPE - QK_ROPE), F32),
    ], axis=-1).reshape(Q_LORA, N_HEADS * LANE).astype(BF)
    return w1, wqb


def _kvb_layout(w_kvb):
    kvb = w_kvb.reshape(DEPTH, KV_LORA, N_HEADS, QK_NOPE + V_HEAD)
    kpart = jnp.concatenate([kvb[..., :QK_NOPE], jnp.zeros((DEPTH, KV_LORA, N_HEADS, LANE - QK_NOPE), F32)],
                            axis=-1).reshape(DEPTH, KV_LORA, N_HEADS * LANE)
    vpart = kvb[..., QK_NOPE:].reshape(DEPTH, KV_LORA, N_HEADS * V_HEAD)
    return jnp.concatenate([kpart, vpart], axis=-1).astype(BF)


def kernel(x_prompt, x_sample, cache_ckv, cache_krope, c, c_ctx, w_ada, b_ada, g_norm1, g_norm2, w_in, g_qa, w_qb, g_kva, w_kvb, w_o_mla, w_conf_dw, b_conf_dw, g_conf_ln, b_conf_ln, w_conf_pw, w_sc_conv, w_sc_out, w_fn, w_out, w_ffn_gate, w_ffn_up, w_ffn_down, g_final):
    tm = 512
    xp = x_prompt.reshape(N_PROMPT, D)
    xs = x_sample.reshape(N_SAMPLE, D)
    cvec = jnp.concatenate([c_ctx[None, :], c, jnp.zeros((8 - 1 - DEC_BATCH, D), F32)], axis=0)
    mod = _ada_call(cvec, w_ada, b_ada)

    rope_tab = _rope_table(tm)
    place = np.zeros((QK_ROPE, LANE), np.float32)
    place[ROPE_PERM, ROPE_LANE0 + np.arange(QK_ROPE)] = 1.0
    wkvb_all = _kvb_layout(w_kvb)
    kc, vc = _ctx_call(cache_ckv, cache_krope, wkvb_all, jnp.asarray(place, BF))

    cg, sg = _dft_cos_sin(FN_GROUP_W)
    eye = np.eye(FN_GROUPS, dtype=np.float32)
    gd = jnp.asarray(np.concatenate([np.kron(eye, cg), np.kron(eye, sg)], axis=1)).astype(BF)
    cs_p = jnp.asarray(np.concatenate(_dft_cos_sin(SEQ), axis=1)).astype(BF)
    cs_s = jnp.asarray(np.concatenate(_dft_cos_sin(DEC_SEQ), axis=1)).astype(BF)

    new_ckv, new_krope = [], []
    for l in range(DEPTH):
        w1, wqb = _layer_weights(l, w_in, w_qb, w_kvb)
        q, k, v, ckv, kr, u0, gcx, gb, fn, gates = _stage1_call(
            xp, xs, mod[l], g_norm1[l][None, :], w1, g_qa[l][None, :], wqb, g_kva[l][None, :],
            wkvb_all[l], rope_tab, tm)
        new_ckv.append(ckv[:N_PROMPT].reshape(BATCH, SEQ, KV_LORA))
        new_krope.append(kr[:N_PROMPT].reshape(BATCH, SEQ, QK_ROPE))

        o_p = _attn_prompt_call(q, k, v)
        o_s = _attn_sample_call(q, k, v, kc[l], vc[l], 256)
        u, s = _conv_call(u0, gcx, gb, w_conf_dw[l], b_conf_dw[l][None, :], g_conf_ln[l][None, :],
                          b_conf_ln[l][None, :], w_sc_conv[l], 256)
        f_p = _fourier_call(fn, gd, cs_p, SEQ, BATCH, 0, SEQ)
        f_s = _fourier_call(fn, gd, cs_s, DEC_SEQ, DEC_BATCH, N_PROMPT, 256)
        xp, xs = _stage3_call(
            xp, xs, o_p, o_s, u, s, f_p, f_s, gates, mod[l], g_norm2[l][None, :], g_final[None, :],
            w_o_mla[l].astype(BF), w_conf_pw[l].astype(BF), w_sc_out[l].astype(BF), w_fn[l].astype(BF),
            w_out[l].astype(BF), w_ffn_gate[l].astype(BF), w_ffn_up[l].astype(BF),
            w_ffn_down[l].astype(BF), tm, l == DEPTH - 1)

    return (xp.reshape(BATCH, SEQ, D), xs.reshape(DEC_BATCH, DEC_SEQ, D),
            jnp.stack(new_ckv, axis=1), jnp.stack(new_krope, axis=1))
```

```python
import functools

import numpy as np
import jax
import jax.numpy as jnp
from jax import lax
from jax.experimental import pallas as pl
from jax.experimental.pallas import tpu as pltpu

BF = jnp.bfloat16
F32 = jnp.float32

D = 1024
BATCH = 16
SEQ = 256
DEPTH = 2
DEC_BATCH = 2
DEC_SEQ = 2048
PAST_LEN = 256
GRID_W = 64
N_HEADS = 8
QK_NOPE = 64
QK_ROPE = 32
V_HEAD = 64
Q_LORA = 384
KV_LORA = 256
AXIS_ROPE = QK_ROPE // 2
ROPE_THETA = 10000.0
W_CONF = D // 4
CONF_K = 31
W_SC = D // 4
SC_K = 3
W_FN = D // 4
FN_GROUPS = 4
FN_GROUP_W = W_FN // FN_GROUPS
N_BRANCH = 4
FF_HIDDEN = ((8 * D // 3 + 255) // 256) * 256
EPS = 1e-6
OFF_QA = 0
OFF_KVA = OFF_QA + Q_LORA
OFF_CONF = OFF_KVA + KV_LORA + QK_ROPE
OFF_SC = OFF_CONF + 2 * W_CONF
OFF_FN = OFF_SC + 3 * W_SC
OFF_GATE = OFF_FN + W_FN
IN_COLS = OFF_GATE + N_BRANCH * D

N_PROMPT = BATCH * SEQ
N_SAMPLE = DEC_BATCH * DEC_SEQ
N_TOK = N_PROMPT + N_SAMPLE
LANE = 128
SUBLANES = 8
HALO = 16
SM_SCALE = float((QK_NOPE + QK_ROPE) ** -0.5)
Q_SCALE = SM_SCALE * float(np.log2(np.e))
VMEM_LIMIT = 56 * 1024 * 1024

QK_COLS = Q_LORA + 3 * LANE
M_CONF = 0
M_SC = M_CONF + 2 * W_CONF
M_FN = M_SC + 3 * W_SC
M_COLS = M_FN + W_FN

ROPE_PERM = np.array(list(range(0, 8)) + list(range(16, 24)) + list(range(8, 16)) + list(range(24, 32)))
ROPE_LANE0 = QK_NOPE


def _dot(a, b):
    return jnp.dot(a, b, preferred_element_type=F32)


def _dot_nt(a, b):
    return lax.dot_general(a, b, (((1,), (1,)), ((), ())), preferred_element_type=F32)


def _sigmoid(x):
    return jax.nn.sigmoid(x)


def _rms(x, g):
    return x * lax.rsqrt(jnp.mean(x * x, axis=-1, keepdims=True) + EPS) * g


def _full(shape):
    return pl.BlockSpec(shape, lambda *_: (0,) * len(shape))


def _layer(l, shape):
    return pl.BlockSpec((1, *shape), lambda *_: (l,) + (0,) * len(shape), pipeline_mode=pl.Buffered(1))


def _params(n_axes):
    return pltpu.CompilerParams(
        dimension_semantics=("arbitrary",) * n_axes, vmem_limit_bytes=VMEM_LIMIT)


def _mod_row(i, n_prompt_tiles, tiles_per_seq):
    return jnp.where(i >= n_prompt_tiles, 1 + (i - n_prompt_tiles) // tiles_per_seq, 0)


def _ada_body(c_ref, w_ref, b_ref, out_ref):
    cv = c_ref[...]
    sc = (cv * _sigmoid(cv)).astype(BF)
    out_ref[0] = _dot(sc, w_ref[0].astype(BF)) + b_ref[0]


def _ada_call(cvec, w_ada, b_ada):
    n_col = 6 * D // D
    return pl.pallas_call(
        _ada_body,
        grid=(DEPTH, n_col),
        in_specs=[
            _full((8, D)),
            pl.BlockSpec((1, D, D), lambda l, j: (l, 0, j)),
            pl.BlockSpec((1, 1, D), lambda l, j: (l, 0, j)),
        ],
        out_specs=pl.BlockSpec((1, 8, D), lambda l, j: (l, 0, j)),
        out_shape=jax.ShapeDtypeStruct((DEPTH, 8, 6 * D), F32),
        compiler_params=_params(2),
        name="ada_mod",
    )(cvec, w_ada, b_ada.reshape(DEPTH, 1, 6 * D))


def _ctx_body(ckv_ref, kr_ref, wkvb_ref, place_ref, k_ref, v_ref):
    kv = _dot(ckv_ref[0, 0].astype(BF), wkvb_ref[0])
    krp = _dot(kr_ref[0, 0].astype(BF), place_ref[...])
    for h in range(N_HEADS):
        sl = slice(h * LANE, (h + 1) * LANE)
        k_ref[0, 0, :, sl] = (kv[:, sl] + krp).astype(BF)
    v_ref[0, 0] = (kv[:, N_HEADS * LANE:] + _value_ones_row()).astype(BF)


def _ctx_call(cache_ckv, cache_krope, wkvb_all, place):
    return pl.pallas_call(
        _ctx_body,
        grid=(DEPTH, DEC_BATCH),
        in_specs=[
            pl.BlockSpec((1, 1, PAST_LEN, KV_LORA), lambda l, b: (b, l, 0, 0)),
            pl.BlockSpec((1, 1, PAST_LEN, QK_ROPE), lambda l, b: (b, l, 0, 0)),
            pl.BlockSpec((1, KV_LORA, 2 * N_HEADS * LANE), lambda l, b: (l, 0, 0)),
            _full((QK_ROPE, LANE)),
        ],
        out_specs=[
            pl.BlockSpec((1, 1, PAST_LEN, N_HEADS * LANE), lambda l, b: (l, b, 0, 0)),
            pl.BlockSpec((1, 1, PAST_LEN, N_HEADS * LANE), lambda l, b: (l, b, 0, 0)),
        ],
        out_shape=[
            jax.ShapeDtypeStruct((DEPTH, DEC_BATCH, PAST_LEN, N_HEADS * LANE), BF),
            jax.ShapeDtypeStruct((DEPTH, DEC_BATCH, PAST_LEN, N_HEADS * LANE), BF),
        ],
        compiler_params=_params(2),
        name="ctx_keys",
    )(cache_ckv, cache_krope, wkvb_all, place)


def _stage1_body(xp_ref, xs_ref, mod_ref, g1_ref, gqa_ref, gkva_ref,
                 wqk_ref, wmid_ref, wgate_ref, wqb_ref, wkvb_ref, rope_ref,
                 q_ref, k_ref, v_ref, ckv_ref, kr_ref, u0_ref, gcx_ref, gb_ref, fn_ref, gate_ref,
                 *, n_prompt_tiles, tiles_per_seq):
    i = pl.program_id(0)
    is_sample = i >= n_prompt_tiles
    row = _mod_row(i, n_prompt_tiles, tiles_per_seq)
    sh1 = mod_ref[pl.ds(row, 1), 0:D]
    sc1 = mod_ref[pl.ds(row, 1), D:2 * D]
    x = jnp.where(is_sample, xs_ref[...], xp_ref[...])
    hb = (_rms(x, g1_ref[...]) * (1.0 + sc1) + sh1).astype(BF)

    pqk = _dot(hb, wqk_ref[0])
    q = _dot(_rms(pqk[:, 0:Q_LORA], gqa_ref[...]).astype(BF), wqb_ref[0])

    pkv = pqk[:, Q_LORA:]
    ckv = _rms(pkv[:, 0:KV_LORA], gkva_ref[...])
    ckv_ref[...] = ckv
    kr3 = pkv[:, KV_LORA:KV_LORA + LANE]
    kr_ref[...] = kr3[:, 0:QK_ROPE]
    lane = lax.broadcasted_iota(jnp.int32, (1, LANE), 1)
    krm = jnp.where((lane >= ROPE_LANE0) & (lane < ROPE_LANE0 + QK_ROPE), kr3, 0.0)
    kv = _dot(ckv.astype(BF), wkvb_ref[0])
    v_ref[...] = (kv[:, N_HEADS * LANE:] + _value_ones_row()).astype(BF)

    def rope(t):
        return (t * rope_ref[:, 0:LANE]
                + pltpu.roll(t, LANE - 16, 1) * rope_ref[:, LANE:2 * LANE]
                + pltpu.roll(t, 16, 1) * rope_ref[:, 2 * LANE:3 * LANE])

    krr = rope(krm)
    for h in range(N_HEADS):
        sl = slice(h * LANE, (h + 1) * LANE)
        q_ref[:, sl] = (rope(q[:, sl]) * Q_SCALE).astype(BF)
        k_ref[:, sl] = (kv[:, sl] + krr).astype(BF)

    pc = _dot(hb, wmid_ref[0, :, M_CONF:M_SC])
    u0_ref[...] = pc[:, 0:W_CONF] * _sigmoid(pc[:, W_CONF:])
    ps = _dot(hb, wmid_ref[0, :, M_SC:M_FN])
    gb_ref[...] = ps[:, 0:W_SC]
    gcx_ref[...] = ps[:, W_SC:2 * W_SC] * ps[:, 2 * W_SC:]
    fn_ref[...] = _dot(hb, wmid_ref[0, :, M_FN:M_COLS]).astype(BF)
    for j in range(N_BRANCH):
        gate_ref[:, j * D:(j + 1) * D] = _sigmoid(
            _dot(hb, wgate_ref[0, :, j * D:(j + 1) * D])).astype(BF)


def _split_specs(tm, w):
    n_p = N_PROMPT // tm
    return (pl.BlockSpec((tm, w), lambda i: (jnp.minimum(i, n_p - 1), 0)),
            pl.BlockSpec((tm, w), lambda i: (jnp.maximum(i - n_p, 0), 0)))


def _stage1_call(l, xp, xs, mod_l, g1, gqa, gkva, wqk, wmid, wgate, wqb, wkvb, rope_tab, tm):
    n_prompt_tiles = N_PROMPT // tm
    tiles_per_seq = DEC_SEQ // tm
    row_blk = lambda w: pl.BlockSpec((tm, w), lambda i: (i, 0))
    body = functools.partial(_stage1_body, n_prompt_tiles=n_prompt_tiles, tiles_per_seq=tiles_per_seq)
    return pl.pallas_call(
        body,
        grid=(N_TOK // tm,),
        in_specs=[
            *_split_specs(tm, D),
            _full((8, 6 * D)),
            _full((1, D)),
            _full((1, Q_LORA)),
            _full((1, KV_LORA)),
            _layer(l, (D, QK_COLS)),
            _layer(l, (D, M_COLS)),
            _layer(l, (D, N_BRANCH * D)),
            _layer(l, (Q_LORA, N_HEADS * LANE)),
            _layer(l, (KV_LORA, 2 * N_HEADS * LANE)),
            pl.BlockSpec((tm, 3 * LANE),
                         lambda i: (jnp.where(i < n_prompt_tiles, 0,
                                              1 + (i - n_prompt_tiles) % tiles_per_seq), 0)),
        ],
        out_specs=[
            row_blk(N_HEADS * LANE), row_blk(N_HEADS * LANE), row_blk(N_HEADS * LANE),
            row_blk(KV_LORA), row_blk(QK_ROPE), row_blk(W_CONF), row_blk(W_SC), row_blk(W_SC),
            row_blk(W_FN), row_blk(N_BRANCH * D),
        ],
        out_shape=[
            jax.ShapeDtypeStruct((N_TOK, N_HEADS * LANE), BF),
            jax.ShapeDtypeStruct((N_TOK, N_HEADS * LANE), BF),
            jax.ShapeDtypeStruct((N_TOK, N_HEADS * LANE), BF),
            jax.ShapeDtypeStruct((N_TOK, KV_LORA), F32),
            jax.ShapeDtypeStruct((N_TOK, QK_ROPE), F32),
            jax.ShapeDtypeStruct((N_TOK, W_CONF), F32),
            jax.ShapeDtypeStruct((N_TOK, W_SC), F32),
            jax.ShapeDtypeStruct((N_TOK, W_SC), F32),
            jax.ShapeDtypeStruct((N_TOK, W_FN), BF),
            jax.ShapeDtypeStruct((N_TOK, N_BRANCH * D), BF),
        ],
        compiler_params=_params(1),
        name="stage1",
    )(xp, xs, mod_l, g1, gqa, gkva, wqk, wmid, wgate, wqb, wkvb, rope_tab)


def _ones_lane(parity):
    return (1 - parity) * V_HEAD


def _value_ones_row():
    col = lax.broadcasted_iota(jnp.int32, (1, N_HEADS * LANE), 1)
    odd = (col // LANE) % 2
    return (col % LANE == jnp.where(odd == 1, _ones_lane(1), _ones_lane(0))).astype(F32)


def _attend_heads(q_ref, kv_refs, o_ref):
    lane = lax.broadcasted_iota(jnp.int32, (1, LANE), 1)

    def scores(h):
        sl = slice(h * LANE, (h + 1) * LANE)
        return [_dot_nt(q_ref[:, sl], k_ref[:, sl]) for k_ref, _ in kv_refs]

    ahead = 2
    pending = [scores(h) for h in range(ahead)]
    for hp in range(N_HEADS // 2):
        vsl = slice(hp * LANE, (hp + 1) * LANE)
        outs = []
        for e in range(2):
            h = 2 * hp + e
            sl = slice(h * LANE, (h + 1) * LANE)
            ss = pending.pop(0)
            if h + ahead < N_HEADS:
                pending.append(scores(h + ahead))
            m = functools.reduce(jnp.maximum, [jnp.max(s, axis=-1, keepdims=True) for s in ss])
            o = None
            for s, (_, v_ref) in zip(ss, kv_refs):
                t = _dot(jnp.exp2(s - m).astype(BF), v_ref[:, sl])
                o = t if o is None else o + t
            ones_lane = _ones_lane(e)
            outs.append(o * (1.0 / o[:, ones_lane:ones_lane + 1]))
        o_ref[:, vsl] = jnp.where(lane < V_HEAD, outs[0], outs[1]).astype(BF)


def _attn_prompt_body(q_ref, k_ref, v_ref, o_ref):
    _attend_heads(q_ref, [(k_ref, v_ref)], o_ref)


def _attn_prompt_call(q, k, v):
    blk = lambda w: pl.BlockSpec((SEQ, w), lambda b: (b, 0))
    return pl.pallas_call(
        _attn_prompt_body,
        grid=(BATCH,),
        in_specs=[blk(N_HEADS * LANE), blk(N_HEADS * LANE), blk(N_HEADS * LANE)],
        out_specs=blk(N_HEADS * V_HEAD),
        out_shape=jax.ShapeDtypeStruct((N_PROMPT, N_HEADS * V_HEAD), BF),
        compiler_params=_params(1),
        name="attn_prompt",
    )(q, k, v)


def _attn_sample_body(q_ref, kc_ref, vc_ref, k_ref, v_ref, o_ref):
    _attend_heads(q_ref, [(kc_ref.at[0], vc_ref.at[0]), (k_ref, v_ref)], o_ref)


def _attn_sample_call(q, k, v, kc_l, vc_l, tq):
    n_q = DEC_SEQ // tq
    q0 = N_PROMPT // tq
    s0 = N_PROMPT // DEC_SEQ
    return pl.pallas_call(
        _attn_sample_body,
        grid=(DEC_BATCH, n_q),
        in_specs=[
            pl.BlockSpec((tq, N_HEADS * LANE), lambda b, j: (q0 + b * n_q + j, 0)),
            pl.BlockSpec((1, PAST_LEN, N_HEADS * LANE), lambda b, j: (b, 0, 0)),
            pl.BlockSpec((1, PAST_LEN, N_HEADS * LANE), lambda b, j: (b, 0, 0)),
            pl.BlockSpec((DEC_SEQ, N_HEADS * LANE), lambda b, j: (s0 + b, 0)),
            pl.BlockSpec((DEC_SEQ, N_HEADS * LANE), lambda b, j: (s0 + b, 0)),
        ],
        out_specs=pl.BlockSpec((tq, N_HEADS * V_HEAD), lambda b, j: (b * n_q + j, 0)),
        out_shape=jax.ShapeDtypeStruct((N_SAMPLE, N_HEADS * V_HEAD), BF),
        compiler_params=_params(2),
        name="attn_sample",
    )(q, kc_l, vc_l, k, v)


def _conv_body(u_ref, up_ref, un_ref, c_ref, cp_ref, cn_ref, gb_ref,
               wdw_ref, bdw_ref, gln_ref, bln_ref, wsc_ref,
               uo_ref, so_ref, ubuf, cbuf, shifted, *, tl, n_prompt_tiles, tiles_per_seq):
    i = pl.program_id(0)
    is_sample = i >= n_prompt_tiles
    j = jnp.maximum(i - n_prompt_tiles, 0) % tiles_per_seq
    has_prev = is_sample & (j > 0)
    has_next = is_sample & (j < tiles_per_seq - 1)
    for src, prv, nxt, buf in ((u_ref, up_ref, un_ref, ubuf), (c_ref, cp_ref, cn_ref, cbuf)):
        buf[0:HALO, :] = jnp.where(has_prev, prv[...], 0.0)
        buf[HALO:HALO + tl, :] = src[...]
        buf[HALO + tl:2 * HALO + tl, :] = jnp.where(has_next, nxt[...], 0.0)

    span = shifted.shape[1]
    for b in range(1, SUBLANES):
        shifted[b] = ubuf[pl.ds(b, span), :]

    def staged(p, r, n):
        a, b = divmod(p, SUBLANES)
        if b == 0:
            return ubuf[pl.ds(r + SUBLANES * a, n), :]
        return shifted[b, pl.ds(r + SUBLANES * a, n), :]

    rc = 32
    p0 = HALO - CONF_K // 2
    for r in range(0, tl, rc):
        acc = staged(p0, r, rc) * wdw_ref[0:1, :]
        for t in range(1, CONF_K):
            acc = acc + staged(p0 + t, r, rc) * wdw_ref[t:t + 1, :]
        acc = acc + bdw_ref[...]
        mu = jnp.mean(acc, axis=-1, keepdims=True)
        cen = acc - mu
        var = jnp.mean(cen * cen, axis=-1, keepdims=True)
        y = cen * lax.rsqrt(var + EPS) * gln_ref[...] + bln_ref[...]
        uo_ref[pl.ds(r, rc), :] = (y * _sigmoid(y)).astype(BF)

        off = r + HALO - SC_K // 2
        cv = cbuf[pl.ds(off, rc), :] * wsc_ref[0:1, :]
        for t in range(1, SC_K):
            cv = cv + cbuf[pl.ds(off + t, rc), :] * wsc_ref[t:t + 1, :]
        so_ref[pl.ds(r, rc), :] = (gb_ref[pl.ds(r, rc), :] * cv).astype(BF)


def _conv_call(u0, gcx, gb, wdw, bdw, gln, bln, wsc, tl):
    n_tiles = N_TOK // tl
    hb = tl // HALO
    n_halo = N_TOK // HALO
    cur = pl.BlockSpec((tl, W_CONF), lambda i: (i, 0))
    prv = pl.BlockSpec((HALO, W_CONF), lambda i: (jnp.maximum(i * hb - 1, 0), 0))
    nxt = pl.BlockSpec((HALO, W_CONF), lambda i: (jnp.minimum((i + 1) * hb, n_halo - 1), 0))
    body = functools.partial(_conv_body, tl=tl, n_prompt_tiles=N_PROMPT // tl,
                             tiles_per_seq=DEC_SEQ // tl)
    return pl.pallas_call(
        body,
        grid=(n_tiles,),
        in_specs=[cur, prv, nxt, cur, prv, nxt, cur,
                  _full((CONF_K, W_CONF)), _full((1, W_CONF)), _full((1, W_CONF)), _full((1, W_CONF)),
                  _full((SC_K, W_SC))],
        out_specs=[cur, cur],
        out_shape=[jax.ShapeDtypeStruct((N_TOK, W_CONF), BF), jax.ShapeDtypeStruct((N_TOK, W_SC), BF)],
        scratch_shapes=[pltpu.VMEM((tl + 2 * HALO, W_CONF), F32), pltpu.VMEM((tl + 2 * HALO, W_SC), F32),
                        pltpu.VMEM((SUBLANES, tl + 2 * HALO - SUBLANES, W_CONF), F32)],
        compiler_params=_params(1),
        name="dwconv",
    )(u0, u0, u0, gcx, gcx, gcx, gb, wdw, bdw, gln, bln, wsc)


def _fourier_body(fn_ref, gd_ref, cs_ref, out_ref, rhs, *, seq, scale):
    @pl.when(pl.program_id(1) == 0)
    def _():
        v = _dot(fn_ref[...], gd_ref[...])
        rhs[0:seq, :] = v[:, 0:W_FN].astype(BF)
        rhs[seq:2 * seq, :] = (-v[:, W_FN:]).astype(BF)

    out_ref[...] = (_dot(cs_ref[...], rhs[...]) * scale).astype(BF)


def _fourier_call(fn, gd, cs, seq, n_seq, row0, tl):
    n_t = seq // tl
    s0 = row0 // seq
    body = functools.partial(_fourier_body, seq=seq, scale=float((seq * FN_GROUP_W) ** -0.5))
    return pl.pallas_call(
        body,
        grid=(n_seq, n_t),
        in_specs=[
            pl.BlockSpec((seq, W_FN), lambda b, j: (s0 + b, 0)),
            _full((W_FN, 2 * W_FN)),
            pl.BlockSpec((tl, 2 * seq), lambda b, j: (j, 0)),
        ],
        out_specs=pl.BlockSpec((tl, W_FN), lambda b, j: (b * n_t + j, 0)),
        out_shape=jax.ShapeDtypeStruct((n_seq * seq, W_FN), BF),
        scratch_shapes=[pltpu.VMEM((2 * seq, W_FN), BF)],
        compiler_params=_params(2),
        name=f"fourier_{seq}",
    )(fn, gd, cs)


def _dft_cos_sin(n):
    r = np.arange(n, dtype=np.int64)
    ang = ((r[:, None] * r[None, :]) % n).astype(np.float64) * (2.0 * np.pi / n)
    return np.cos(ang).astype(np.float32), np.sin(ang).astype(np.float32)


def _stage3_body(xp_ref, xs_ref, op_ref, os_ref, u_ref, s_ref, fp_ref, fs_ref, gate_ref,
                 mod_ref, g2_ref, gfin_ref,
                 wo_ref, wpw_ref, wsco_ref, wfn_ref, wout_ref, wg_ref, wu_ref, wd_ref,
                 outp_ref, outs_ref, *, n_prompt_tiles, tiles_per_seq, final, chunk):
    i = pl.program_id(0)
    is_sample = i >= n_prompt_tiles
    row = _mod_row(i, n_prompt_tiles, tiles_per_seq)
    mod = lambda k: mod_ref[pl.ds(row, 1), k * D:(k + 1) * D]
    pick = lambda p_ref, s_ref: jnp.where(is_sample, s_ref[...], p_ref[...])
    merged = gate_ref[:, 0:D].astype(F32) * _dot(pick(op_ref, os_ref), wo_ref[0])
    merged = merged + gate_ref[:, D:2 * D].astype(F32) * _dot(u_ref[...], wpw_ref[0])
    merged = merged + gate_ref[:, 2 * D:3 * D].astype(F32) * _dot(s_ref[...], wsco_ref[0])
    merged = merged + gate_ref[:, 3 * D:4 * D].astype(F32) * _dot(pick(fp_ref, fs_ref), wfn_ref[0])
    x1 = pick(xp_ref, xs_ref) + mod(2) * _dot(merged.astype(BF), wout_ref[0])

    h2 = (_rms(x1, g2_ref[...]) * (1.0 + mod(4)) + mod(3)).astype(BF)
    acc = None
    for c0 in range(0, FF_HIDDEN, chunk):
        a = _dot(h2, wg_ref[0, :, c0:c0 + chunk])
        b = _dot(h2, wu_ref[0, :, c0:c0 + chunk])
        t = _dot((a * _sigmoid(a) * b).astype(BF), wd_ref[0, c0:c0 + chunk, :])
        acc = t if acc is None else acc + t
    x2 = x1 + mod(5) * acc
    y = _rms(x2, gfin_ref[...]) if final else x2

    @pl.when(is_sample)
    def _():
        outs_ref[...] = y

    @pl.when(jnp.logical_not(is_sample))
    def _():
        outp_ref[...] = y


def _stage3_call(l, xp, xs, o_p, o_s, u, s, f_p, f_s, gates, mod_l, g2, gfin,
                 wo, wpw, wsco, wfn, wout, wg, wu, wd, tm, final):
    row_blk = lambda w: pl.BlockSpec((tm, w), lambda i: (i, 0))
    body = functools.partial(_stage3_body, n_prompt_tiles=N_PROMPT // tm, tiles_per_seq=DEC_SEQ // tm,
                             final=final, chunk=256)
    return pl.pallas_call(
        body,
        grid=(N_TOK // tm,),
        in_specs=[
            *_split_specs(tm, D), *_split_specs(tm, N_HEADS * V_HEAD), row_blk(W_CONF), row_blk(W_SC),
            *_split_specs(tm, W_FN), row_blk(N_BRANCH * D), _full((8, 6 * D)), _full((1, D)), _full((1, D)),
            _layer(l, (N_HEADS * V_HEAD, D)), _layer(l, (W_CONF, D)), _layer(l, (W_SC, D)),
            _layer(l, (W_FN, D)), _layer(l, (D, D)), _layer(l, (D, FF_HIDDEN)), _layer(l, (D, FF_HIDDEN)),
            _layer(l, (FF_HIDDEN, D)),
        ],
        out_specs=list(_split_specs(tm, D)),
        out_shape=[jax.ShapeDtypeStruct((N_PROMPT, D), F32), jax.ShapeDtypeStruct((N_SAMPLE, D), F32)],
        compiler_params=_params(1),
        name="stage3",
    )(xp, xs, o_p, o_s, u, s, f_p, f_s, gates, mod_l, g2, gfin, wo, wpw, wsco, wfn, wout, wg, wu, wd)


def _rope_table(tm):
    rows = DEC_SEQ // GRID_W
    row_pos = np.repeat(np.arange(rows, dtype=np.float64), GRID_W)
    col_pos = np.tile(np.arange(GRID_W, dtype=np.float64), rows)
    inv = ROPE_THETA ** (-np.arange(0, AXIS_ROPE, 2, dtype=np.float64) / AXIS_ROPE)
    ang = np.concatenate([row_pos[:, None] * inv, col_pos[:, None] * inv], axis=1)
    half = QK_ROPE // 2
    tab = np.zeros((tm + DEC_SEQ, 3 * LANE), np.float64)
    tab[:, 0:LANE] = 1.0
    tab[tm:, ROPE_LANE0:ROPE_LANE0 + half] = np.cos(ang)
    tab[tm:, ROPE_LANE0 + half:ROPE_LANE0 + QK_ROPE] = np.cos(ang)
    tab[tm:, LANE + ROPE_LANE0:LANE + ROPE_LANE0 + half] = -np.sin(ang)
    tab[tm:, 2 * LANE + ROPE_LANE0 + half:2 * LANE + ROPE_LANE0 + QK_ROPE] = np.sin(ang)
    return jnp.asarray(tab.astype(np.float32))


def _input_proj_layout(w_in):
    w_kr = w_in[:, :, OFF_KVA + KV_LORA:OFF_CONF]
    z32 = jnp.zeros((DEPTH, D, 32), F32)
    wqk = jnp.concatenate([w_in[:, :, OFF_QA:OFF_KVA + KV_LORA], w_kr, z32, w_kr[:, :, ROPE_PERM], z32],
                          axis=-1).astype(BF)
    wmid = w_in[:, :, OFF_CONF:OFF_GATE].astype(BF)
    wgate = w_in[:, :, OFF_GATE:].astype(BF)
    return wqk, wmid, wgate


def _qb_layout(w_qb):
    qb = w_qb.reshape(DEPTH, Q_LORA, N_HEADS, QK_NOPE + QK_ROPE)
    return jnp.concatenate([
        qb[..., :QK_NOPE], qb[..., QK_NOPE:][..., ROPE_PERM],
        jnp.zeros((DEPTH, Q_LORA, N_HEADS, LANE - QK_NOPE - QK_ROPE), F32),
    ], axis=-1).reshape(DEPTH, Q_LORA, N_HEADS * LANE).astype(BF)


def _kvb_layout(w_kvb):
    kvb = w_kvb.reshape(DEPTH, KV_LORA, N_HEADS // 2, 2, QK_NOPE + V_HEAD)
    z = jnp.zeros((DEPTH, KV_LORA, N_HEADS // 2, 2, LANE - QK_NOPE), F32)
    kpart = jnp.concatenate([kvb[..., :QK_NOPE], z], axis=-1)
    ve = jnp.concatenate([kvb[:, :, :, 0:1, QK_NOPE:], z[:, :, :, 0:1]], axis=-1)
    vo = jnp.concatenate([z[:, :, :, 1:2], kvb[:, :, :, 1:2, QK_NOPE:]], axis=-1)
    vpart = jnp.concatenate([ve, vo], axis=3)
    flat = lambda a: a.reshape(DEPTH, KV_LORA, N_HEADS * LANE)
    return jnp.concatenate([flat(kpart), flat(vpart)], axis=-1).astype(BF)


def kernel(x_prompt, x_sample, cache_ckv, cache_krope, c, c_ctx, w_ada, b_ada, g_norm1, g_norm2, w_in, g_qa, w_qb, g_kva, w_kvb, w_o_mla, w_conf_dw, b_conf_dw, g_conf_ln, b_conf_ln, w_conf_pw, w_sc_conv, w_sc_out, w_fn, w_out, w_ffn_gate, w_ffn_up, w_ffn_down, g_final):
    tm = 512
    xp = x_prompt.reshape(N_PROMPT, D)
    xs = x_sample.reshape(N_SAMPLE, D)
    cvec = jnp.concatenate([c_ctx[None, :], c, jnp.zeros((8 - 1 - DEC_BATCH, D), F32)], axis=0)
    mod = _ada_call(cvec, w_ada, b_ada)

    rope_tab = _rope_table(tm)
    place = np.zeros((QK_ROPE, LANE), np.float32)
    place[ROPE_PERM, ROPE_LANE0 + np.arange(QK_ROPE)] = 1.0
    wkvb_all = _kvb_layout(w_kvb)
    kc, vc = _ctx_call(cache_ckv, cache_krope, wkvb_all, jnp.asarray(place, BF))

    cg, sg = _dft_cos_sin(FN_GROUP_W)
    eye = np.eye(FN_GROUPS, dtype=np.float32)
    gd = jnp.asarray(np.concatenate([np.kron(eye, cg), np.kron(eye, sg)], axis=1)).astype(BF)
    cs_p = jnp.asarray(np.concatenate(_dft_cos_sin(SEQ), axis=1)).astype(BF)
    cs_s = jnp.asarray(np.concatenate(_dft_cos_sin(DEC_SEQ), axis=1)).astype(BF)

    wqk, wmid, wgate = _input_proj_layout(w_in)
    wqb = _qb_layout(w_qb)
    w3 = [w.astype(BF) for w in (w_o_mla, w_conf_pw, w_sc_out, w_fn, w_out, w_ffn_gate, w_ffn_up, w_ffn_down)]

    new_ckv, new_krope = [], []
    for l in range(DEPTH):
        q, k, v, ckv, kr, u0, gcx, gb, fn, gates = _stage1_call(
            l, xp, xs, mod[l], g_norm1[l][None, :], g_qa[l][None, :], g_kva[l][None, :],
            wqk, wmid, wgate, wqb, wkvb_all, rope_tab, tm)
        new_ckv.append(ckv[:N_PROMPT].reshape(BATCH, SEQ, KV_LORA))
        new_krope.append(kr[:N_PROMPT].reshape(BATCH, SEQ, QK_ROPE))

        o_p = _attn_prompt_call(q, k, v)
        o_s = _attn_sample_call(q, k, v, kc[l], vc[l], 256)
        u, s = _conv_call(u0, gcx, gb, w_conf_dw[l], b_conf_dw[l][None, :], g_conf_ln[l][None, :],
                          b_conf_ln[l][None, :], w_sc_conv[l], 256)
        f_p = _fourier_call(fn, gd, cs_p, SEQ, BATCH, 0, SEQ)
        f_s = _fourier_call(fn, gd, cs_s, DEC_SEQ, DEC_BATCH, N_PROMPT, 256)
        xp, xs = _stage3_call(
            l, xp, xs, o_p, o_s, u, s, f_p, f_s, gates, mod[l], g_norm2[l][None, :], g_final[None, :],
            *w3, tm, l == DEPTH - 1)

    return (xp.reshape(BATCH, SEQ, D), xs.reshape(DEC_BATCH, DEC_SEQ, D),
            jnp.stack(new_ckv, axis=1), jnp.stack(new_krope, axis=1))
```

```python
import functools

import numpy as np
import jax
import jax.numpy as jnp
from jax import lax
from jax.experimental import pallas as pl
from jax.experimental.pallas import tpu as pltpu

BF = jnp.bfloat16
F32 = jnp.float32

D = 1024
BATCH = 16
SEQ = 256
DEPTH = 2
DEC_BATCH = 2
DEC_SEQ = 2048
PAST_LEN = 256
GRID_W = 64
N_HEADS = 8
QK_NOPE = 64
QK_ROPE = 32
V_HEAD = 64
Q_LORA = 384
KV_LORA = 256
AXIS_ROPE = QK_ROPE // 2
ROPE_THETA = 10000.0
W_CONF = D // 4
CONF_K = 31
W_SC = D // 4
SC_K = 3
W_FN = D // 4
FN_GROUPS = 4
FN_GROUP_W = W_FN // FN_GROUPS
N_BRANCH = 4
FF_HIDDEN = ((8 * D // 3 + 255) // 256) * 256
EPS = 1e-6
OFF_QA = 0
OFF_KVA = OFF_QA + Q_LORA
OFF_CONF = OFF_KVA + KV_LORA + QK_ROPE
OFF_SC = OFF_CONF + 2 * W_CONF
OFF_FN = OFF_SC + 3 * W_SC
OFF_GATE = OFF_FN + W_FN
IN_COLS = OFF_GATE + N_BRANCH * D

N_PROMPT = BATCH * SEQ
N_SAMPLE = DEC_BATCH * DEC_SEQ
N_TOK = N_PROMPT + N_SAMPLE
LANE = 128
SUBLANES = 8
HALO = 16
SM_SCALE = float((QK_NOPE + QK_ROPE) ** -0.5)
Q_SCALE = SM_SCALE * float(np.log2(np.e))
VMEM_LIMIT = 56 * 1024 * 1024

QK_COLS = Q_LORA + 3 * LANE
M_CONF = 0
M_SC = M_CONF + 2 * W_CONF
M_FN = M_SC + 3 * W_SC
M_COLS = M_FN + W_FN

ROPE_PERM = np.array(list(range(0, 8)) + list(range(16, 24)) + list(range(8, 16)) + list(range(24, 32)))
ROPE_LANE0 = QK_NOPE


def _dot(a, b):
    return jnp.dot(a, b, preferred_element_type=F32)


def _dot_nt(a, b):
    return lax.dot_general(a, b, (((1,), (1,)), ((), ())), preferred_element_type=F32)


def _sigmoid(x):
    return jax.nn.sigmoid(x)


def _rms(x, g):
    return x * lax.rsqrt(jnp.mean(x * x, axis=-1, keepdims=True) + EPS) * g


def _full(shape):
    return pl.BlockSpec(shape, lambda *_: (0,) * len(shape))


def _layer(l, shape):
    return pl.BlockSpec((1, *shape), lambda *_: (l,) + (0,) * len(shape), pipeline_mode=pl.Buffered(1))


def _params(n_axes):
    return pltpu.CompilerParams(
        dimension_semantics=("arbitrary",) * n_axes, vmem_limit_bytes=VMEM_LIMIT)


def _mod_row(i, n_prompt_tiles, tiles_per_seq):
    return jnp.where(i >= n_prompt_tiles, 1 + (i - n_prompt_tiles) // tiles_per_seq, 0)


def _ada_body(c_ref, w_ref, b_ref, out_ref):
    cv = c_ref[...]
    sc = (cv * _sigmoid(cv)).astype(BF)
    out_ref[0] = _dot(sc, w_ref[0].astype(BF)) + b_ref[0]


def _ada_call(cvec, w_ada, b_ada):
    n_col = 6 * D // D
    return pl.pallas_call(
        _ada_body,
        grid=(DEPTH, n_col),
        in_specs=[
            _full((8, D)),
            pl.BlockSpec((1, D, D), lambda l, j: (l, 0, j)),
            pl.BlockSpec((1, 1, D), lambda l, j: (l, 0, j)),
        ],
        out_specs=pl.BlockSpec((1, 8, D), lambda l, j: (l, 0, j)),
        out_shape=jax.ShapeDtypeStruct((DEPTH, 8, 6 * D), F32),
        compiler_params=_params(2),
        name="ada_mod",
    )(cvec, w_ada, b_ada.reshape(DEPTH, 1, 6 * D))


def _ctx_body(ckv_ref, kr_ref, wkvb_ref, place_ref, k_ref, v_ref):
    kv = _dot(ckv_ref[0, 0].astype(BF), wkvb_ref[0])
    krp = _dot(kr_ref[0, 0].astype(BF), place_ref[...])
    for h in range(N_HEADS):
        sl = slice(h * LANE, (h + 1) * LANE)
        k_ref[0, 0, :, sl] = (kv[:, sl] + krp).astype(BF)
    v_ref[0, 0] = (kv[:, N_HEADS * LANE:] + _value_ones_row()).astype(BF)


def _ctx_call(cache_ckv, cache_krope, wkvb_all, place):
    return pl.pallas_call(
        _ctx_body,
        grid=(DEPTH, DEC_BATCH),
        in_specs=[
            pl.BlockSpec((1, 1, PAST_LEN, KV_LORA), lambda l, b: (b, l, 0, 0)),
            pl.BlockSpec((1, 1, PAST_LEN, QK_ROPE), lambda l, b: (b, l, 0, 0)),
            pl.BlockSpec((1, KV_LORA, 2 * N_HEADS * LANE), lambda l, b: (l, 0, 0)),
            _full((QK_ROPE, LANE)),
        ],
        out_specs=[
            pl.BlockSpec((1, 1, PAST_LEN, N_HEADS * LANE), lambda l, b: (l, b, 0, 0)),
            pl.BlockSpec((1, 1, PAST_LEN, N_HEADS * LANE), lambda l, b: (l, b, 0, 0)),
        ],
        out_shape=[
            jax.ShapeDtypeStruct((DEPTH, DEC_BATCH, PAST_LEN, N_HEADS * LANE), BF),
            jax.ShapeDtypeStruct((DEPTH, DEC_BATCH, PAST_LEN, N_HEADS * LANE), BF),
        ],
        compiler_params=_params(2),
        name="ctx_keys",
    )(cache_ckv, cache_krope, wkvb_all, place)


def _stage1_body(xp_ref, xs_ref, mod_ref, g1_ref, gqa_ref, gkva_ref,
                 wqk_ref, wmid_ref, wgate_ref, wqb_ref, wkvb_ref, rope_ref,
                 q_ref, k_ref, v_ref, ckv_ref, kr_ref, u0_ref, gcx_ref, gb_ref, fn_ref, gate_ref,
                 *, n_prompt_tiles, tiles_per_seq):
    i = pl.program_id(0)
    is_sample = i >= n_prompt_tiles
    row = _mod_row(i, n_prompt_tiles, tiles_per_seq)
    sh1 = mod_ref[pl.ds(row, 1), 0:D]
    sc1 = mod_ref[pl.ds(row, 1), D:2 * D]
    x = jnp.where(is_sample, xs_ref[...], xp_ref[...])
    hb = (_rms(x, g1_ref[...]) * (1.0 + sc1) + sh1).astype(BF)

    pqk = _dot(hb, wqk_ref[0])
    q = _dot(_rms(pqk[:, 0:Q_LORA], gqa_ref[...]).astype(BF), wqb_ref[0])

    pkv = pqk[:, Q_LORA:]
    ckv = _rms(pkv[:, 0:KV_LORA], gkva_ref[...])
    ckv_ref[...] = ckv
    kr3 = pkv[:, KV_LORA:KV_LORA + LANE]
    kr_ref[...] = kr3[:, 0:QK_ROPE]
    lane = lax.broadcasted_iota(jnp.int32, (1, LANE), 1)
    krm = jnp.where((lane >= ROPE_LANE0) & (lane < ROPE_LANE0 + QK_ROPE), kr3, 0.0)
    kv = _dot(ckv.astype(BF), wkvb_ref[0])
    v_ref[...] = (kv[:, N_HEADS * LANE:] + _value_ones_row()).astype(BF)

    def rope(t):
        return (t * rope_ref[:, 0:LANE]
                + pltpu.roll(t, LANE - 16, 1) * rope_ref[:, LANE:2 * LANE]
                + pltpu.roll(t, 16, 1) * rope_ref[:, 2 * LANE:3 * LANE])

    krr = rope(krm)
    for h in range(N_HEADS):
        sl = slice(h * LANE, (h + 1) * LANE)
        q_ref[:, sl] = (rope(q[:, sl]) * Q_SCALE).astype(BF)
        k_ref[:, sl] = (kv[:, sl] + krr).astype(BF)

    pc = _dot(hb, wmid_ref[0, :, M_CONF:M_SC])
    u0_ref[...] = pc[:, 0:W_CONF] * _sigmoid(pc[:, W_CONF:])
    ps = _dot(hb, wmid_ref[0, :, M_SC:M_FN])
    gb_ref[...] = ps[:, 0:W_SC]
    gcx_ref[...] = ps[:, W_SC:2 * W_SC] * ps[:, 2 * W_SC:]
    fn_ref[...] = _dot(hb, wmid_ref[0, :, M_FN:M_COLS]).astype(BF)
    for j in range(N_BRANCH):
        gate_ref[:, j * D:(j + 1) * D] = _sigmoid(
            _dot(hb, wgate_ref[0, :, j * D:(j + 1) * D])).astype(BF)


def _split_specs(tm, w):
    n_p = N_PROMPT // tm
    return (pl.BlockSpec((tm, w), lambda i: (jnp.minimum(i, n_p - 1), 0)),
            pl.BlockSpec((tm, w), lambda i: (jnp.maximum(i - n_p, 0), 0)))


def _stage1_call(l, xp, xs, mod_l, g1, gqa, gkva, wqk, wmid, wgate, wqb, wkvb, rope_tab, tm):
    n_prompt_tiles = N_PROMPT // tm
    tiles_per_seq = DEC_SEQ // tm
    row_blk = lambda w: pl.BlockSpec((tm, w), lambda i: (i, 0))
    body = functools.partial(_stage1_body, n_prompt_tiles=n_prompt_tiles, tiles_per_seq=tiles_per_seq)
    return pl.pallas_call(
        body,
        grid=(N_TOK // tm,),
        in_specs=[
            *_split_specs(tm, D),
            _full((8, 6 * D)),
            _full((1, D)),
            _full((1, Q_LORA)),
            _full((1, KV_LORA)),
            _layer(l, (D, QK_COLS)),
            _layer(l, (D, M_COLS)),
            _layer(l, (D, N_BRANCH * D)),
            _layer(l, (Q_LORA, N_HEADS * LANE)),
            _layer(l, (KV_LORA, 2 * N_HEADS * LANE)),
            pl.BlockSpec((tm, 3 * LANE),
                         lambda i: (jnp.where(i < n_prompt_tiles, 0,
                                              1 + (i - n_prompt_tiles) % tiles_per_seq), 0)),
        ],
        out_specs=[
            row_blk(N_HEADS * LANE), row_blk(N_HEADS * LANE), row_blk(N_HEADS * LANE),
            row_blk(KV_LORA), row_blk(QK_ROPE), row_blk(W_CONF), row_blk(W_SC), row_blk(W_SC),
            row_blk(W_FN), row_blk(N_BRANCH * D),
        ],
        out_shape=[
            jax.ShapeDtypeStruct((N_TOK, N_HEADS * LANE), BF),
            jax.ShapeDtypeStruct((N_TOK, N_HEADS * LANE), BF),
            jax.ShapeDtypeStruct((N_TOK, N_HEADS * LANE), BF),
            jax.ShapeDtypeStruct((N_TOK, KV_LORA), F32),
            jax.ShapeDtypeStruct((N_TOK, QK_ROPE), F32),
            jax.ShapeDtypeStruct((N_TOK, W_CONF), F32),
            jax.ShapeDtypeStruct((N_TOK, W_SC), F32),
            jax.ShapeDtypeStruct((N_TOK, W_SC), F32),
            jax.ShapeDtypeStruct((N_TOK, W_FN), BF),
            jax.ShapeDtypeStruct((N_TOK, N_BRANCH * D), BF),
        ],
        compiler_params=_params(1),
        name="stage1",
    )(xp, xs, mod_l, g1, gqa, gkva, wqk, wmid, wgate, wqb, wkvb, rope_tab)


def _ones_lane(parity):
    return (1 - parity) * V_HEAD


def _value_ones_row():
    col = lax.broadcasted_iota(jnp.int32, (1, N_HEADS * LANE), 1)
    odd = (col // LANE) % 2
    return (col % LANE == jnp.where(odd == 1, _ones_lane(1), _ones_lane(0))).astype(F32)


def _attend_heads(q_ref, kv_refs, o_ref, ahead):
    lane = lax.broadcasted_iota(jnp.int32, (1, LANE), 1)

    def scores(h):
        sl = slice(h * LANE, (h + 1) * LANE)
        return [_dot_nt(q_ref[:, sl], k_ref[:, sl]) for k_ref, _ in kv_refs]

    pending = [scores(h) for h in range(ahead)]
    for hp in range(N_HEADS // 2):
        vsl = slice(hp * LANE, (hp + 1) * LANE)
        outs = []
        for e in range(2):
            h = 2 * hp + e
            sl = slice(h * LANE, (h + 1) * LANE)
            ss = pending.pop(0)
            if h + ahead < N_HEADS:
                pending.append(scores(h + ahead))
            m = functools.reduce(jnp.maximum, [jnp.max(s, axis=-1, keepdims=True) for s in ss])
            o = None
            for s, (_, v_ref) in zip(ss, kv_refs):
                t = _dot(jnp.exp2(s - m).astype(BF), v_ref[:, sl])
                o = t if o is None else o + t
            ones_lane = _ones_lane(e)
            outs.append(o * (1.0 / o[:, ones_lane:ones_lane + 1]))
        o_ref[:, vsl] = jnp.where(lane < V_HEAD, outs[0], outs[1]).astype(BF)


def _attn_prompt_body(q_ref, k_ref, v_ref, o_ref):
    _attend_heads(q_ref, [(k_ref, v_ref)], o_ref, ahead=N_HEADS)


def _attn_prompt_call(q, k, v):
    blk = lambda w: pl.BlockSpec((SEQ, w), lambda b: (b, 0))
    return pl.pallas_call(
        _attn_prompt_body,
        grid=(BATCH,),
        in_specs=[blk(N_HEADS * LANE), blk(N_HEADS * LANE), blk(N_HEADS * LANE)],
        out_specs=blk(N_HEADS * V_HEAD),
        out_shape=jax.ShapeDtypeStruct((N_PROMPT, N_HEADS * V_HEAD), BF),
        compiler_params=_params(1),
        name="attn_prompt",
    )(q, k, v)


def _attn_sample_body(q_ref, kc_ref, vc_ref, k_ref, v_ref, o_ref):
    _attend_heads(q_ref, [(kc_ref.at[0], vc_ref.at[0]), (k_ref, v_ref)], o_ref, ahead=2)


def _attn_sample_call(q, k, v, kc_l, vc_l, tq):
    n_q = DEC_SEQ // tq
    q0 = N_PROMPT // tq
    s0 = N_PROMPT // DEC_SEQ
    return pl.pallas_call(
        _attn_sample_body,
        grid=(DEC_BATCH, n_q),
        in_specs=[
            pl.BlockSpec((tq, N_HEADS * LANE), lambda b, j: (q0 + b * n_q + j, 0)),
            pl.BlockSpec((1, PAST_LEN, N_HEADS * LANE), lambda b, j: (b, 0, 0)),
            pl.BlockSpec((1, PAST_LEN, N_HEADS * LANE), lambda b, j: (b, 0, 0)),
            pl.BlockSpec((DEC_SEQ, N_HEADS * LANE), lambda b, j: (s0 + b, 0)),
            pl.BlockSpec((DEC_SEQ, N_HEADS * LANE), lambda b, j: (s0 + b, 0)),
        ],
        out_specs=pl.BlockSpec((tq, N_HEADS * V_HEAD), lambda b, j: (b * n_q + j, 0)),
        out_shape=jax.ShapeDtypeStruct((N_SAMPLE, N_HEADS * V_HEAD), BF),
        compiler_params=_params(2),
        name="attn_sample",
    )(q, kc_l, vc_l, k, v)


def _conv_body(u_ref, up_ref, un_ref, c_ref, cp_ref, cn_ref, gb_ref,
               wdw_ref, bdw_ref, gln_ref, bln_ref, wsc_ref,
               uo_ref, so_ref, ubuf, cbuf, shifted, *, tl, n_prompt_tiles, tiles_per_seq):
    i = pl.program_id(0)
    is_sample = i >= n_prompt_tiles
    j = jnp.maximum(i - n_prompt_tiles, 0) % tiles_per_seq
    has_prev = is_sample & (j > 0)
    has_next = is_sample & (j < tiles_per_seq - 1)
    for src, prv, nxt, buf in ((u_ref, up_ref, un_ref, ubuf), (c_ref, cp_ref, cn_ref, cbuf)):
        buf[0:HALO, :] = jnp.where(has_prev, prv[...], 0.0)
        buf[HALO:HALO + tl, :] = src[...]
        buf[HALO + tl:2 * HALO + tl, :] = jnp.where(has_next, nxt[...], 0.0)

    span = shifted.shape[1]
    for b in range(1, SUBLANES):
        shifted[b] = ubuf[pl.ds(b, span), :]

    def staged(p, r, n):
        a, b = divmod(p, SUBLANES)
        if b == 0:
            return ubuf[pl.ds(r + SUBLANES * a, n), :]
        return shifted[b, pl.ds(r + SUBLANES * a, n), :]

    rc = 32
    p0 = HALO - CONF_K // 2
    for r in range(0, tl, rc):
        acc = staged(p0, r, rc) * wdw_ref[0:1, :]
        for t in range(1, CONF_K):
            acc = acc + staged(p0 + t, r, rc) * wdw_ref[t:t + 1, :]
        acc = acc + bdw_ref[...]
        mu = jnp.mean(acc, axis=-1, keepdims=True)
        cen = acc - mu
        var = jnp.mean(cen * cen, axis=-1, keepdims=True)
        y = cen * lax.rsqrt(var + EPS) * gln_ref[...] + bln_ref[...]
        uo_ref[pl.ds(r, rc), :] = (y * _sigmoid(y)).astype(BF)

        off = r + HALO - SC_K // 2
        cv = cbuf[pl.ds(off, rc), :] * wsc_ref[0:1, :]
        for t in range(1, SC_K):
            cv = cv + cbuf[pl.ds(off + t, rc), :] * wsc_ref[t:t + 1, :]
        so_ref[pl.ds(r, rc), :] = (gb_ref[pl.ds(r, rc), :] * cv).astype(BF)


def _conv_call(u0, gcx, gb, wdw, bdw, gln, bln, wsc, tl):
    n_tiles = N_TOK // tl
    hb = tl // HALO
    n_halo = N_TOK // HALO
    cur = pl.BlockSpec((tl, W_CONF), lambda i: (i, 0))
    prv = pl.BlockSpec((HALO, W_CONF), lambda i: (jnp.maximum(i * hb - 1, 0), 0))
    nxt = pl.BlockSpec((HALO, W_CONF), lambda i: (jnp.minimum((i + 1) * hb, n_halo - 1), 0))
    body = functools.partial(_conv_body, tl=tl, n_prompt_tiles=N_PROMPT // tl,
                             tiles_per_seq=DEC_SEQ // tl)
    return pl.pallas_call(
        body,
        grid=(n_tiles,),
        in_specs=[cur, prv, nxt, cur, prv, nxt, cur,
                  _full((CONF_K, W_CONF)), _full((1, W_CONF)), _full((1, W_CONF)), _full((1, W_CONF)),
                  _full((SC_K, W_SC))],
        out_specs=[cur, cur],
        out_shape=[jax.ShapeDtypeStruct((N_TOK, W_CONF), BF), jax.ShapeDtypeStruct((N_TOK, W_SC), BF)],
        scratch_shapes=[pltpu.VMEM((tl + 2 * HALO, W_CONF), F32), pltpu.VMEM((tl + 2 * HALO, W_SC), F32),
                        pltpu.VMEM((SUBLANES, tl + 2 * HALO - SUBLANES, W_CONF), F32)],
        compiler_params=_params(1),
        name="dwconv",
    )(u0, u0, u0, gcx, gcx, gcx, gb, wdw, bdw, gln, bln, wsc)


def _fourier_body(fn_ref, gd_ref, cs_ref, out_ref, rhs, *, seq, scale):
    @pl.when(pl.program_id(1) == 0)
    def _():
        v = _dot(fn_ref[...], gd_ref[...])
        rhs[0:seq, :] = v[:, 0:W_FN].astype(BF)
        rhs[seq:2 * seq, :] = (-v[:, W_FN:]).astype(BF)

    out_ref[...] = (_dot(cs_ref[...], rhs[...]) * scale).astype(BF)


def _fourier_call(fn, gd, cs, seq, n_seq, row0, tl):
    n_t = seq // tl
    s0 = row0 // seq
    body = functools.partial(_fourier_body, seq=seq, scale=float((seq * FN_GROUP_W) ** -0.5))
    return pl.pallas_call(
        body,
        grid=(n_seq, n_t),
        in_specs=[
            pl.BlockSpec((seq, W_FN), lambda b, j: (s0 + b, 0)),
            _full((W_FN, 2 * W_FN)),
            pl.BlockSpec((tl, 2 * seq), lambda b, j: (j, 0)),
        ],
        out_specs=pl.BlockSpec((tl, W_FN), lambda b, j: (b * n_t + j, 0)),
        out_shape=jax.ShapeDtypeStruct((n_seq * seq, W_FN), BF),
        scratch_shapes=[pltpu.VMEM((2 * seq, W_FN), BF)],
        compiler_params=_params(2),
        name=f"fourier_{seq}",
    )(fn, gd, cs)


def _dft_cos_sin(n):
    r = np.arange(n, dtype=np.int64)
    ang = ((r[:, None] * r[None, :]) % n).astype(np.float64) * (2.0 * np.pi / n)
    return np.cos(ang).astype(np.float32), np.sin(ang).astype(np.float32)


def _stage3_body(xp_ref, xs_ref, op_ref, os_ref, u_ref, s_ref, fp_ref, fs_ref, gate_ref,
                 mod_ref, g2_ref, gfin_ref,
                 wo_ref, wpw_ref, wsco_ref, wfn_ref, wout_ref, wg_ref, wu_ref, wd_ref,
                 outp_ref, outs_ref, *, n_prompt_tiles, tiles_per_seq, final, chunk):
    i = pl.program_id(0)
    is_sample = i >= n_prompt_tiles
    row = _mod_row(i, n_prompt_tiles, tiles_per_seq)
    mod = lambda k: mod_ref[pl.ds(row, 1), k * D:(k + 1) * D]
    pick = lambda p_ref, s_ref: jnp.where(is_sample, s_ref[...], p_ref[...])
    merged = gate_ref[:, 0:D].astype(F32) * _dot(pick(op_ref, os_ref), wo_ref[0])
    merged = merged + gate_ref[:, D:2 * D].astype(F32) * _dot(u_ref[...], wpw_ref[0])
    merged = merged + gate_ref[:, 2 * D:3 * D].astype(F32) * _dot(s_ref[...], wsco_ref[0])
    merged = merged + gate_ref[:, 3 * D:4 * D].astype(F32) * _dot(pick(fp_ref, fs_ref), wfn_ref[0])
    x1 = pick(xp_ref, xs_ref) + mod(2) * _dot(merged.astype(BF), wout_ref[0])

    h2 = (_rms(x1, g2_ref[...]) * (1.0 + mod(4)) + mod(3)).astype(BF)
    acc = None
    for c0 in range(0, FF_HIDDEN, chunk):
        a = _dot(h2, wg_ref[0, :, c0:c0 + chunk])
        b = _dot(h2, wu_ref[0, :, c0:c0 + chunk])
        t = _dot((a * _sigmoid(a) * b).astype(BF), wd_ref[0, c0:c0 + chunk, :])
        acc = t if acc is None else acc + t
    x2 = x1 + mod(5) * acc
    y = _rms(x2, gfin_ref[...]) if final else x2

    @pl.when(is_sample)
    def _():
        outs_ref[...] = y

    @pl.when(jnp.logical_not(is_sample))
    def _():
        outp_ref[...] = y


def _stage3_call(l, xp, xs, o_p, o_s, u, s, f_p, f_s, gates, mod_l, g2, gfin,
                 wo, wpw, wsco, wfn, wout, wg, wu, wd, tm, final):
    row_blk = lambda w: pl.BlockSpec((tm, w), lambda i: (i, 0))
    body = functools.partial(_stage3_body, n_prompt_tiles=N_PROMPT // tm, tiles_per_seq=DEC_SEQ // tm,
                             final=final, chunk=256)
    return pl.pallas_call(
        body,
        grid=(N_TOK // tm,),
        in_specs=[
            *_split_specs(tm, D), *_split_specs(tm, N_HEADS * V_HEAD), row_blk(W_CONF), row_blk(W_SC),
            *_split_specs(tm, W_FN), row_blk(N_BRANCH * D), _full((8, 6 * D)), _full((1, D)), _full((1, D)),
            _layer(l, (N_HEADS * V_HEAD, D)), _layer(l, (W_CONF, D)), _layer(l, (W_SC, D)),
            _layer(l, (W_FN, D)), _layer(l, (D, D)), _layer(l, (D, FF_HIDDEN)), _layer(l, (D, FF_HIDDEN)),
            _layer(l, (FF_HIDDEN, D)),
        ],
        out_specs=list(_split_specs(tm, D)),
        out_shape=[jax.ShapeDtypeStruct((N_PROMPT, D), F32), jax.ShapeDtypeStruct((N_SAMPLE, D), F32)],
        compiler_params=_params(1),
        name="stage3",
    )(xp, xs, o_p, o_s, u, s, f_p, f_s, gates, mod_l, g2, gfin, wo, wpw, wsco, wfn, wout, wg, wu, wd)


def _rope_table(tm):
    rows = DEC_SEQ // GRID_W
    row_pos = np.repeat(np.arange(rows, dtype=np.float64), GRID_W)
    col_pos = np.tile(np.arange(GRID_W, dtype=np.float64), rows)
    inv = ROPE_THETA ** (-np.arange(0, AXIS_ROPE, 2, dtype=np.float64) / AXIS_ROPE)
    ang = np.concatenate([row_pos[:, None] * inv, col_pos[:, None] * inv], axis=1)
    half = QK_ROPE // 2
    tab = np.zeros((tm + DEC_SEQ, 3 * LANE), np.float64)
    tab[:, 0:LANE] = 1.0
    tab[tm:, ROPE_LANE0:ROPE_LANE0 + half] = np.cos(ang)
    tab[tm:, ROPE_LANE0 + half:ROPE_LANE0 + QK_ROPE] = np.cos(ang)
    tab[tm:, LANE + ROPE_LANE0:LANE + ROPE_LANE0 + half] = -np.sin(ang)
    tab[tm:, 2 * LANE + ROPE_LANE0 + half:2 * LANE + ROPE_LANE0 + QK_ROPE] = np.sin(ang)
    return jnp.asarray(tab.astype(np.float32))


def _win_layout_body(w_ref, place_ref, wqk_ref, wmid_ref, wgate_ref):
    head = OFF_KVA + KV_LORA
    wqk_ref[0, :, 0:head] = w_ref[0, :, 0:head].astype(BF)
    kr = w_ref[0, :, head:OFF_CONF].astype(BF)
    wqk_ref[0, :, head:QK_COLS] = _dot(kr, place_ref[...]).astype(BF)
    wmid_ref[0] = w_ref[0, :, OFF_CONF:OFF_GATE].astype(BF)
    wgate_ref[0] = w_ref[0, :, OFF_GATE:IN_COLS].astype(BF)


def _win_layout_call(w_in, place2):
    rb = 128
    return pl.pallas_call(
        _win_layout_body,
        grid=(DEPTH, D // rb),
        in_specs=[pl.BlockSpec((1, rb, IN_COLS), lambda l, r: (l, r, 0)), _full((QK_ROPE, LANE))],
        out_specs=[pl.BlockSpec((1, rb, QK_COLS), lambda l, r: (l, r, 0)),
                   pl.BlockSpec((1, rb, M_COLS), lambda l, r: (l, r, 0)),
                   pl.BlockSpec((1, rb, N_BRANCH * D), lambda l, r: (l, r, 0))],
        out_shape=[jax.ShapeDtypeStruct((DEPTH, D, QK_COLS), BF),
                   jax.ShapeDtypeStruct((DEPTH, D, M_COLS), BF),
                   jax.ShapeDtypeStruct((DEPTH, D, N_BRANCH * D), BF)],
        compiler_params=_params(2),
        name="win_layout",
    )(w_in, place2)


def _qb_layout(w_qb):
    qb = w_qb.reshape(DEPTH, Q_LORA, N_HEADS, QK_NOPE + QK_ROPE)
    return jnp.concatenate([
        qb[..., :QK_NOPE], qb[..., QK_NOPE:][..., ROPE_PERM],
        jnp.zeros((DEPTH, Q_LORA, N_HEADS, LANE - QK_NOPE - QK_ROPE), F32),
    ], axis=-1).reshape(DEPTH, Q_LORA, N_HEADS * LANE).astype(BF)


def _kvb_layout(w_kvb):
    kvb = w_kvb.reshape(DEPTH, KV_LORA, N_HEADS // 2, 2, QK_NOPE + V_HEAD)
    z = jnp.zeros((DEPTH, KV_LORA, N_HEADS // 2, 2, LANE - QK_NOPE), F32)
    kpart = jnp.concatenate([kvb[..., :QK_NOPE], z], axis=-1)
    ve = jnp.concatenate([kvb[:, :, :, 0:1, QK_NOPE:], z[:, :, :, 0:1]], axis=-1)
    vo = jnp.concatenate([z[:, :, :, 1:2], kvb[:, :, :, 1:2, QK_NOPE:]], axis=-1)
    vpart = jnp.concatenate([ve, vo], axis=3)
    flat = lambda a: a.reshape(DEPTH, KV_LORA, N_HEADS * LANE)
    return jnp.concatenate([flat(kpart), flat(vpart)], axis=-1).astype(BF)


def kernel(x_prompt, x_sample, cache_ckv, cache_krope, c, c_ctx, w_ada, b_ada, g_norm1, g_norm2, w_in, g_qa, w_qb, g_kva, w_kvb, w_o_mla, w_conf_dw, b_conf_dw, g_conf_ln, b_conf_ln, w_conf_pw, w_sc_conv, w_sc_out, w_fn, w_out, w_ffn_gate, w_ffn_up, w_ffn_down, g_final):
    tm = 512
    xp = x_prompt.reshape(N_PROMPT, D)
    xs = x_sample.reshape(N_SAMPLE, D)
    cvec = jnp.concatenate([c_ctx[None, :], c, jnp.zeros((8 - 1 - DEC_BATCH, D), F32)], axis=0)
    mod = _ada_call(cvec, w_ada, b_ada)

    rope_tab = _rope_table(tm)
    place = np.zeros((QK_ROPE, LANE), np.float32)
    place[ROPE_PERM, ROPE_LANE0 + np.arange(QK_ROPE)] = 1.0
    wkvb_all = _kvb_layout(w_kvb)
    kc, vc = _ctx_call(cache_ckv, cache_krope, wkvb_all, jnp.asarray(place, BF))

    cg, sg = _dft_cos_sin(FN_GROUP_W)
    eye = np.eye(FN_GROUPS, dtype=np.float32)
    gd = jnp.asarray(np.concatenate([np.kron(eye, cg), np.kron(eye, sg)], axis=1)).astype(BF)
    cs_p = jnp.asarray(np.concatenate(_dft_cos_sin(SEQ), axis=1)).astype(BF)
    cs_s = jnp.asarray(np.concatenate(_dft_cos_sin(DEC_SEQ), axis=1)).astype(BF)

    place2 = place.copy()
    place2[np.arange(QK_ROPE), np.arange(QK_ROPE)] = 1.0
    wqk, wmid, wgate = _win_layout_call(w_in, jnp.asarray(place2, BF))
    wqb = _qb_layout(w_qb)
    w3 = [w.astype(BF) for w in (w_o_mla, w_conf_pw, w_sc_out, w_fn, w_out, w_ffn_gate, w_ffn_up, w_ffn_down)]

    new_ckv, new_krope = [], []
    for l in range(DEPTH):
        q, k, v, ckv, kr, u0, gcx, gb, fn, gates = _stage1_call(
            l, xp, xs, mod[l], g_norm1[l][None, :], g_qa[l][None, :], g_kva[l][None, :],
            wqk, wmid, wgate, wqb, wkvb_all, rope_tab, tm)
        new_ckv.append(ckv[:N_PROMPT].reshape(BATCH, SEQ, KV_LORA))
        new_krope.append(kr[:N_PROMPT].reshape(BATCH, SEQ, QK_ROPE))

        o_p = _attn_prompt_call(q, k, v)
        o_s = _attn_sample_call(q, k, v, kc[l], vc[l], 256)
        u, s = _conv_call(u0, gcx, gb, w_conf_dw[l], b_conf_dw[l][None, :], g_conf_ln[l][None, :],
                          b_conf_ln[l][None, :], w_sc_conv[l], 256)
        f_p = _fourier_call(fn, gd, cs_p, SEQ, BATCH, 0, SEQ)
        f_s = _fourier_call(fn, gd, cs_s, DEC_SEQ, DEC_BATCH, N_PROMPT, 256)
        xp, xs = _stage3_call(
            l, xp, xs, o_p, o_s, u, s, f_p, f_s, gates, mod[l], g_norm2[l][None, :], g_final[None, :],
            *w3, tm, l == DEPTH - 1)

    return (xp.reshape(BATCH, SEQ, D), xs.reshape(DEC_BATCH, DEC_SEQ, D),
            jnp.stack(new_ckv, axis=1), jnp.stack(new_krope, axis=1))
```

```python
import functools

import numpy as np
import jax
import jax.numpy as jnp
from jax import lax
from jax.experimental import pallas as pl
from jax.experimental.pallas import tpu as pltpu

BF = jnp.bfloat16
F32 = jnp.float32

D = 1024
BATCH = 16
SEQ = 256
DEPTH = 2
DEC_BATCH = 2
DEC_SEQ = 2048
PAST_LEN = 256
GRID_W = 64
N_HEADS = 8
QK_NOPE = 64
QK_ROPE = 32
V_HEAD = 64
Q_LORA = 384
KV_LORA = 256
AXIS_ROPE = QK_ROPE // 2
ROPE_THETA = 10000.0
W_CONF = D // 4
CONF_K = 31
W_SC = D // 4
SC_K = 3
W_FN = D // 4
FN_GROUPS = 4
FN_GROUP_W = W_FN // FN_GROUPS
N_BRANCH = 4
FF_HIDDEN = ((8 * D // 3 + 255) // 256) * 256
EPS = 1e-6
OFF_QA = 0
OFF_KVA = OFF_QA + Q_LORA
OFF_CONF = OFF_KVA + KV_LORA + QK_ROPE
OFF_SC = OFF_CONF + 2 * W_CONF
OFF_FN = OFF_SC + 3 * W_SC
OFF_GATE = OFF_FN + W_FN
IN_COLS = OFF_GATE + N_BRANCH * D

N_PROMPT = BATCH * SEQ
N_SAMPLE = DEC_BATCH * DEC_SEQ
N_TOK = N_PROMPT + N_SAMPLE
LANE = 128
SUBLANES = 8
HALO = 16
SM_SCALE = float((QK_NOPE + QK_ROPE) ** -0.5)
Q_SCALE = SM_SCALE * float(np.log2(np.e))
VMEM_LIMIT = 56 * 1024 * 1024

QK_END = OFF_KVA + KV_LORA + LANE

ROPE_PERM = np.array(list(range(0, 8)) + list(range(16, 24)) + list(range(8, 16)) + list(range(24, 32)))
ROPE_LANE0 = QK_NOPE


def _dot(a, b):
    return jnp.dot(a, b, preferred_element_type=F32)


def _dot_nt(a, b):
    return lax.dot_general(a, b, (((1,), (1,)), ((), ())), preferred_element_type=F32)


def _sigmoid(x):
    return jax.nn.sigmoid(x)


def _rms(x, g):
    return x * lax.rsqrt(jnp.mean(x * x, axis=-1, keepdims=True) + EPS) * g


def _full(shape):
    return pl.BlockSpec(shape, lambda *_: (0,) * len(shape))


def _layer(l, shape):
    return pl.BlockSpec((1, *shape), lambda *_: (l,) + (0,) * len(shape), pipeline_mode=pl.Buffered(1))


def _params(n_axes):
    return pltpu.CompilerParams(
        dimension_semantics=("arbitrary",) * n_axes, vmem_limit_bytes=VMEM_LIMIT)


def _mod_row(i, n_prompt_tiles, tiles_per_seq):
    return jnp.where(i >= n_prompt_tiles, 1 + (i - n_prompt_tiles) // tiles_per_seq, 0)


def _ada_body(c_ref, w_ref, b_ref, out_ref):
    cv = c_ref[...]
    sc = (cv * _sigmoid(cv)).astype(BF)
    out_ref[0] = _dot(sc, w_ref[0].astype(BF)) + b_ref[0]


def _ada_call(cvec, w_ada, b_ada):
    n_col = 6 * D // D
    return pl.pallas_call(
        _ada_body,
        grid=(DEPTH, n_col),
        in_specs=[
            _full((8, D)),
            pl.BlockSpec((1, D, D), lambda l, j: (l, 0, j)),
            pl.BlockSpec((1, 1, D), lambda l, j: (l, 0, j)),
        ],
        out_specs=pl.BlockSpec((1, 8, D), lambda l, j: (l, 0, j)),
        out_shape=jax.ShapeDtypeStruct((DEPTH, 8, 6 * D), F32),
        compiler_params=_params(2),
        name="ada_mod",
    )(cvec, w_ada, b_ada.reshape(DEPTH, 1, 6 * D))


def _ctx_body(ckv_ref, kr_ref, wkvb_ref, place_ref, k_ref, v_ref):
    kv = _dot(ckv_ref[0, 0].astype(BF), wkvb_ref[0])
    krp = _dot(kr_ref[0, 0].astype(BF), place_ref[...])
    for h in range(N_HEADS):
        sl = slice(h * LANE, (h + 1) * LANE)
        k_ref[0, 0, :, sl] = (kv[:, sl] + krp).astype(BF)
    v_ref[0, 0] = (kv[:, N_HEADS * LANE:] + _value_ones_row()).astype(BF)


def _ctx_call(cache_ckv, cache_krope, wkvb_all, place):
    return pl.pallas_call(
        _ctx_body,
        grid=(DEPTH, DEC_BATCH),
        in_specs=[
            pl.BlockSpec((1, 1, PAST_LEN, KV_LORA), lambda l, b: (b, l, 0, 0)),
            pl.BlockSpec((1, 1, PAST_LEN, QK_ROPE), lambda l, b: (b, l, 0, 0)),
            pl.BlockSpec((1, KV_LORA, 2 * N_HEADS * LANE), lambda l, b: (l, 0, 0)),
            _full((QK_ROPE, LANE)),
        ],
        out_specs=[
            pl.BlockSpec((1, 1, PAST_LEN, N_HEADS * LANE), lambda l, b: (l, b, 0, 0)),
            pl.BlockSpec((1, 1, PAST_LEN, N_HEADS * LANE), lambda l, b: (l, b, 0, 0)),
        ],
        out_shape=[
            jax.ShapeDtypeStruct((DEPTH, DEC_BATCH, PAST_LEN, N_HEADS * LANE), BF),
            jax.ShapeDtypeStruct((DEPTH, DEC_BATCH, PAST_LEN, N_HEADS * LANE), BF),
        ],
        compiler_params=_params(2),
        name="ctx_keys",
    )(cache_ckv, cache_krope, wkvb_all, place)


def _stage1_body(xp_ref, xs_ref, mod_ref, g1_ref, gqa_ref, gkva_ref,
                 wt_ref, wqb_ref, wkvb_ref, rope_ref,
                 q_ref, k_ref, v_ref, ckv_ref, kr_ref, u0_ref, gcx_ref, gb_ref, fn_ref, gate_ref,
                 *, n_prompt_tiles, tiles_per_seq):
    i = pl.program_id(0)
    is_sample = i >= n_prompt_tiles
    row = _mod_row(i, n_prompt_tiles, tiles_per_seq)
    sh1 = mod_ref[pl.ds(row, 1), 0:D]
    sc1 = mod_ref[pl.ds(row, 1), D:2 * D]
    x = jnp.where(is_sample, xs_ref[...], xp_ref[...])
    hb = (_rms(x, g1_ref[...]) * (1.0 + sc1) + sh1).astype(BF)

    def proj(c0, c1):
        return _dot_nt(hb, wt_ref[0, c0:c1, :])

    pqk = proj(OFF_QA, QK_END)
    q = _dot(_rms(pqk[:, 0:Q_LORA], gqa_ref[...]).astype(BF), wqb_ref[0])

    ckv = _rms(pqk[:, OFF_KVA:OFF_KVA + KV_LORA], gkva_ref[...])
    ckv_ref[...] = ckv
    kr3 = pqk[:, OFF_KVA + KV_LORA:QK_END]
    kr_ref[...] = kr3[:, 0:QK_ROPE]
    lane = lax.broadcasted_iota(jnp.int32, (1, LANE), 1)
    group = lambda k: (lane >= ROPE_LANE0 + 8 * k) & (lane < ROPE_LANE0 + 8 * (k + 1))
    krm = jnp.where(group(0) | group(3), pltpu.roll(kr3, ROPE_LANE0, 1),
                    jnp.where(group(1), pltpu.roll(kr3, ROPE_LANE0 - 8, 1),
                              jnp.where(group(2), pltpu.roll(kr3, ROPE_LANE0 + 8, 1), 0.0)))
    kv = _dot(ckv.astype(BF), wkvb_ref[0])
    v_ref[...] = (kv[:, N_HEADS * LANE:] + _value_ones_row()).astype(BF)

    def rope(t):
        return (t * rope_ref[:, 0:LANE]
                + pltpu.roll(t, LANE - 16, 1) * rope_ref[:, LANE:2 * LANE]
                + pltpu.roll(t, 16, 1) * rope_ref[:, 2 * LANE:3 * LANE])

    krr = rope(krm)
    for h in range(N_HEADS):
        sl = slice(h * LANE, (h + 1) * LANE)
        q_ref[:, sl] = (rope(q[:, sl]) * Q_SCALE).astype(BF)
        k_ref[:, sl] = (kv[:, sl] + krr).astype(BF)

    pc = proj(OFF_CONF, OFF_SC)
    u0_ref[...] = pc[:, 0:W_CONF] * _sigmoid(pc[:, W_CONF:])
    ps = proj(OFF_SC, OFF_FN)
    gb_ref[...] = ps[:, 0:W_SC]
    gcx_ref[...] = ps[:, W_SC:2 * W_SC] * ps[:, 2 * W_SC:]
    fn_ref[...] = proj(OFF_FN, OFF_GATE).astype(BF)
    for j in range(N_BRANCH):
        gate_ref[:, j * D:(j + 1) * D] = _sigmoid(
            proj(OFF_GATE + j * D, OFF_GATE + (j + 1) * D)).astype(BF)


def _split_specs(tm, w):
    n_p = N_PROMPT // tm
    return (pl.BlockSpec((tm, w), lambda i: (jnp.minimum(i, n_p - 1), 0)),
            pl.BlockSpec((tm, w), lambda i: (jnp.maximum(i - n_p, 0), 0)))


def _stage1_call(l, xp, xs, mod_l, g1, gqa, gkva, w_in_t, wqb, wkvb, rope_tab, tm):
    n_prompt_tiles = N_PROMPT // tm
    tiles_per_seq = DEC_SEQ // tm
    row_blk = lambda w: pl.BlockSpec((tm, w), lambda i: (i, 0))
    body = functools.partial(_stage1_body, n_prompt_tiles=n_prompt_tiles, tiles_per_seq=tiles_per_seq)
    return pl.pallas_call(
        body,
        grid=(N_TOK // tm,),
        in_specs=[
            *_split_specs(tm, D),
            _full((8, 6 * D)),
            _full((1, D)),
            _full((1, Q_LORA)),
            _full((1, KV_LORA)),
            _layer(l, (IN_COLS, D)),
            _layer(l, (Q_LORA, N_HEADS * LANE)),
            _layer(l, (KV_LORA, 2 * N_HEADS * LANE)),
            pl.BlockSpec((tm, 3 * LANE),
                         lambda i: (jnp.where(i < n_prompt_tiles, 0,
                                              1 + (i - n_prompt_tiles) % tiles_per_seq), 0)),
        ],
        out_specs=[
            row_blk(N_HEADS * LANE), row_blk(N_HEADS * LANE), row_blk(N_HEADS * LANE),
            row_blk(KV_LORA), row_blk(QK_ROPE), row_blk(W_CONF), row_blk(W_SC), row_blk(W_SC),
            row_blk(W_FN), row_blk(N_BRANCH * D),
        ],
        out_shape=[
            jax.ShapeDtypeStruct((N_TOK, N_HEADS * LANE), BF),
            jax.ShapeDtypeStruct((N_TOK, N_HEADS * LANE), BF),
            jax.ShapeDtypeStruct((N_TOK, N_HEADS * LANE), BF),
            jax.ShapeDtypeStruct((N_TOK, KV_LORA), F32),
            jax.ShapeDtypeStruct((N_TOK, QK_ROPE), F32),
            jax.ShapeDtypeStruct((N_TOK, W_CONF), F32),
            jax.ShapeDtypeStruct((N_TOK, W_SC), F32),
            jax.ShapeDtypeStruct((N_TOK, W_SC), F32),
            jax.ShapeDtypeStruct((N_TOK, W_FN), BF),
            jax.ShapeDtypeStruct((N_TOK, N_BRANCH * D), BF),
        ],
        compiler_params=_params(1),
        name="stage1",
    )(xp, xs, mod_l, g1, gqa, gkva, w_in_t, wqb, wkvb, rope_tab)


def _ones_lane(parity):
    return (1 - parity) * V_HEAD


def _value_ones_row():
    col = lax.broadcasted_iota(jnp.int32, (1, N_HEADS * LANE), 1)
    odd = (col // LANE) % 2
    return (col % LANE == jnp.where(odd == 1, _ones_lane(1), _ones_lane(0))).astype(F32)


def _attend_heads(q_ref, kv_refs, o_ref, ahead):
    lane = lax.broadcasted_iota(jnp.int32, (1, LANE), 1)

    def scores(h):
        sl = slice(h * LANE, (h + 1) * LANE)
        return [_dot_nt(q_ref[:, sl], k_ref[:, sl]) for k_ref, _ in kv_refs]

    pending = [scores(h) for h in range(ahead)]
    for hp in range(N_HEADS // 2):
        vsl = slice(hp * LANE, (hp + 1) * LANE)
        outs = []
        for e in range(2):
            h = 2 * hp + e
            sl = slice(h * LANE, (h + 1) * LANE)
            ss = pending.pop(0)
            if h + ahead < N_HEADS:
                pending.append(scores(h + ahead))
            m = functools.reduce(jnp.maximum, [jnp.max(s, axis=-1, keepdims=True) for s in ss])
            o = None
            for s, (_, v_ref) in zip(ss, kv_refs):
                t = _dot(jnp.exp2(s - m).astype(BF), v_ref[:, sl])
                o = t if o is None else o + t
            ones_lane = _ones_lane(e)
            outs.append(o * (1.0 / o[:, ones_lane:ones_lane + 1]))
        o_ref[:, vsl] = jnp.where(lane < V_HEAD, outs[0], outs[1]).astype(BF)


def _attn_prompt_body(q_ref, k_ref, v_ref, o_ref):
    _attend_heads(q_ref, [(k_ref, v_ref)], o_ref, ahead=N_HEADS)


def _attn_prompt_call(q, k, v):
    blk = lambda w: pl.BlockSpec((SEQ, w), lambda b: (b, 0))
    return pl.pallas_call(
        _attn_prompt_body,
        grid=(BATCH,),
        in_specs=[blk(N_HEADS * LANE), blk(N_HEADS * LANE), blk(N_HEADS * LANE)],
        out_specs=blk(N_HEADS * V_HEAD),
        out_shape=jax.ShapeDtypeStruct((N_PROMPT, N_HEADS * V_HEAD), BF),
        compiler_params=_params(1),
        name="attn_prompt",
    )(q, k, v)


def _attn_sample_body(q_ref, kc_ref, vc_ref, k_ref, v_ref, o_ref):
    _attend_heads(q_ref, [(kc_ref.at[0], vc_ref.at[0]), (k_ref, v_ref)], o_ref, ahead=2)


def _attn_sample_call(q, k, v, kc_l, vc_l, tq):
    n_q = DEC_SEQ // tq
    q0 = N_PROMPT // tq
    s0 = N_PROMPT // DEC_SEQ
    return pl.pallas_call(
        _attn_sample_body,
        grid=(DEC_BATCH, n_q),
        in_specs=[
            pl.BlockSpec((tq, N_HEADS * LANE), lambda b, j: (q0 + b * n_q + j, 0)),
            pl.BlockSpec((1, PAST_LEN, N_HEADS * LANE), lambda b, j: (b, 0, 0)),
            pl.BlockSpec((1, PAST_LEN, N_HEADS * LANE), lambda b, j: (b, 0, 0)),
            pl.BlockSpec((DEC_SEQ, N_HEADS * LANE), lambda b, j: (s0 + b, 0)),
            pl.BlockSpec((DEC_SEQ, N_HEADS * LANE), lambda b, j: (s0 + b, 0)),
        ],
        out_specs=pl.BlockSpec((tq, N_HEADS * V_HEAD), lambda b, j: (b * n_q + j, 0)),
        out_shape=jax.ShapeDtypeStruct((N_SAMPLE, N_HEADS * V_HEAD), BF),
        compiler_params=_params(2),
        name="attn_sample",
    )(q, kc_l, vc_l, k, v)


def _conv_body(u_ref, up_ref, un_ref, c_ref, cp_ref, cn_ref, gb_ref,
               wdw_ref, bdw_ref, gln_ref, bln_ref, wsc_ref,
               uo_ref, so_ref, ubuf, cbuf, shifted, *, tl, n_prompt_tiles, tiles_per_seq):
    i = pl.program_id(0)
    is_sample = i >= n_prompt_tiles
    j = jnp.maximum(i - n_prompt_tiles, 0) % tiles_per_seq
    has_prev = is_sample & (j > 0)
    has_next = is_sample & (j < tiles_per_seq - 1)
    for src, prv, nxt, buf in ((u_ref, up_ref, un_ref, ubuf), (c_ref, cp_ref, cn_ref, cbuf)):
        buf[0:HALO, :] = jnp.where(has_prev, prv[...], 0.0)
        buf[HALO:HALO + tl, :] = src[...]
        buf[HALO + tl:2 * HALO + tl, :] = jnp.where(has_next, nxt[...], 0.0)

    span = shifted.shape[1]
    for b in range(1, SUBLANES):
        shifted[b] = ubuf[pl.ds(b, span), :]

    def staged(p, r, n):
        a, b = divmod(p, SUBLANES)
        if b == 0:
            return ubuf[pl.ds(r + SUBLANES * a, n), :]
        return shifted[b, pl.ds(r + SUBLANES * a, n), :]

    rc = 32
    p0 = HALO - CONF_K // 2
    for r in range(0, tl, rc):
        acc = staged(p0, r, rc) * wdw_ref[0:1, :]
        for t in range(1, CONF_K):
            acc = acc + staged(p0 + t, r, rc) * wdw_ref[t:t + 1, :]
        acc = acc + bdw_ref[...]
        mu = jnp.mean(acc, axis=-1, keepdims=True)
        cen = acc - mu
        var = jnp.mean(cen * cen, axis=-1, keepdims=True)
        y = cen * lax.rsqrt(var + EPS) * gln_ref[...] + bln_ref[...]
        uo_ref[pl.ds(r, rc), :] = (y * _sigmoid(y)).astype(BF)

        off = r + HALO - SC_K // 2
        cv = cbuf[pl.ds(off, rc), :] * wsc_ref[0:1, :]
        for t in range(1, SC_K):
            cv = cv + cbuf[pl.ds(off + t, rc), :] * wsc_ref[t:t + 1, :]
        so_ref[pl.ds(r, rc), :] = (gb_ref[pl.ds(r, rc), :] * cv).astype(BF)


def _conv_call(u0, gcx, gb, wdw, bdw, gln, bln, wsc, tl):
    n_tiles = N_TOK // tl
    hb = tl // HALO
    n_halo = N_TOK // HALO
    cur = pl.BlockSpec((tl, W_CONF), lambda i: (i, 0))
    prv = pl.BlockSpec((HALO, W_CONF), lambda i: (jnp.maximum(i * hb - 1, 0), 0))
    nxt = pl.BlockSpec((HALO, W_CONF), lambda i: (jnp.minimum((i + 1) * hb, n_halo - 1), 0))
    body = functools.partial(_conv_body, tl=tl, n_prompt_tiles=N_PROMPT // tl,
                             tiles_per_seq=DEC_SEQ // tl)
    return pl.pallas_call(
        body,
        grid=(n_tiles,),
        in_specs=[cur, prv, nxt, cur, prv, nxt, cur,
                  _full((CONF_K, W_CONF)), _full((1, W_CONF)), _full((1, W_CONF)), _full((1, W_CONF)),
                  _full((SC_K, W_SC))],
        out_specs=[cur, cur],
        out_shape=[jax.ShapeDtypeStruct((N_TOK, W_CONF), BF), jax.ShapeDtypeStruct((N_TOK, W_SC), BF)],
        scratch_shapes=[pltpu.VMEM((tl + 2 * HALO, W_CONF), F32), pltpu.VMEM((tl + 2 * HALO, W_SC), F32),
                        pltpu.VMEM((SUBLANES, tl + 2 * HALO - SUBLANES, W_CONF), F32)],
        compiler_params=_params(1),
        name="dwconv",
    )(u0, u0, u0, gcx, gcx, gcx, gb, wdw, bdw, gln, bln, wsc)


def _fourier_body(fn_ref, gd_ref, cs_ref, out_ref, rhs, *, seq, scale):
    @pl.when(pl.program_id(1) == 0)
    def _():
        v = _dot(fn_ref[...], gd_ref[...])
        rhs[0:seq, :] = v[:, 0:W_FN].astype(BF)
        rhs[seq:2 * seq, :] = (-v[:, W_FN:]).astype(BF)

    out_ref[...] = (_dot(cs_ref[...], rhs[...]) * scale).astype(BF)


def _fourier_call(fn, gd, cs, seq, n_seq, row0, tl):
    n_t = seq // tl
    s0 = row0 // seq
    body = functools.partial(_fourier_body, seq=seq, scale=float((seq * FN_GROUP_W) ** -0.5))
    return pl.pallas_call(
        body,
        grid=(n_seq, n_t),
        in_specs=[
            pl.BlockSpec((seq, W_FN), lambda b, j: (s0 + b, 0)),
            _full((W_FN, 2 * W_FN)),
            pl.BlockSpec((tl, 2 * seq), lambda b, j: (j, 0)),
        ],
        out_specs=pl.BlockSpec((tl, W_FN), lambda b, j: (b * n_t + j, 0)),
        out_shape=jax.ShapeDtypeStruct((n_seq * seq, W_FN), BF),
        scratch_shapes=[pltpu.VMEM((2 * seq, W_FN), BF)],
        compiler_params=_params(2),
        name=f"fourier_{seq}",
    )(fn, gd, cs)


def _dft_cos_sin(n):
    r = np.arange(n, dtype=np.int64)
    ang = ((r[:, None] * r[None, :]) % n).astype(np.float64) * (2.0 * np.pi / n)
    return np.cos(ang).astype(np.float32), np.sin(ang).astype(np.float32)


def _stage3_body(xp_ref, xs_ref, op_ref, os_ref, u_ref, s_ref, fp_ref, fs_ref, gate_ref,
                 mod_ref, g2_ref, gfin_ref,
                 wo_ref, wpw_ref, wsco_ref, wfn_ref, wout_ref, wg_ref, wu_ref, wd_ref,
                 outp_ref, outs_ref, *, n_prompt_tiles, tiles_per_seq, final, chunk):
    i = pl.program_id(0)
    is_sample = i >= n_prompt_tiles
    row = _mod_row(i, n_prompt_tiles, tiles_per_seq)
    mod = lambda k: mod_ref[pl.ds(row, 1), k * D:(k + 1) * D]
    pick = lambda p_ref, s_ref: jnp.where(is_sample, s_ref[...], p_ref[...])
    merged = gate_ref[:, 0:D].astype(F32) * _dot(pick(op_ref, os_ref), wo_ref[0])
    merged = merged + gate_ref[:, D:2 * D].astype(F32) * _dot(u_ref[...], wpw_ref[0])
    merged = merged + gate_ref[:, 2 * D:3 * D].astype(F32) * _dot(s_ref[...], wsco_ref[0])
    merged = merged + gate_ref[:, 3 * D:4 * D].astype(F32) * _dot(pick(fp_ref, fs_ref), wfn_ref[0])
    x1 = pick(xp_ref, xs_ref) + mod(2) * _dot(merged.astype(BF), wout_ref[0])

    h2 = (_rms(x1, g2_ref[...]) * (1.0 + mod(4)) + mod(3)).astype(BF)
    acc = None
    for c0 in range(0, FF_HIDDEN, chunk):
        a = _dot(h2, wg_ref[0, :, c0:c0 + chunk])
        b = _dot(h2, wu_ref[0, :, c0:c0 + chunk])
        t = _dot((a * _sigmoid(a) * b).astype(BF), wd_ref[0, c0:c0 + chunk, :])
        acc = t if acc is None else acc + t
    x2 = x1 + mod(5) * acc
    y = _rms(x2, gfin_ref[...]) if final else x2

    @pl.when(is_sample)
    def _():
        outs_ref[...] = y

    @pl.when(jnp.logical_not(is_sample))
    def _():
        outp_ref[...] = y


def _stage3_call(l, xp, xs, o_p, o_s, u, s, f_p, f_s, gates, mod_l, g2, gfin,
                 wo, wpw, wsco, wfn, wout, wg, wu, wd, tm, final):
    row_blk = lambda w: pl.BlockSpec((tm, w), lambda i: (i, 0))
    body = functools.partial(_stage3_body, n_prompt_tiles=N_PROMPT // tm, tiles_per_seq=DEC_SEQ // tm,
                             final=final, chunk=256)
    return pl.pallas_call(
        body,
        grid=(N_TOK // tm,),
        in_specs=[
            *_split_specs(tm, D), *_split_specs(tm, N_HEADS * V_HEAD), row_blk(W_CONF), row_blk(W_SC),
            *_split_specs(tm, W_FN), row_blk(N_BRANCH * D), _full((8, 6 * D)), _full((1, D)), _full((1, D)),
            _layer(l, (N_HEADS * V_HEAD, D)), _layer(l, (W_CONF, D)), _layer(l, (W_SC, D)),
            _layer(l, (W_FN, D)), _layer(l, (D, D)), _layer(l, (D, FF_HIDDEN)), _layer(l, (D, FF_HIDDEN)),
            _layer(l, (FF_HIDDEN, D)),
        ],
        out_specs=list(_split_specs(tm, D)),
        out_shape=[jax.ShapeDtypeStruct((N_PROMPT, D), F32), jax.ShapeDtypeStruct((N_SAMPLE, D), F32)],
        compiler_params=_params(1),
        name="stage3",
    )(xp, xs, o_p, o_s, u, s, f_p, f_s, gates, mod_l, g2, gfin, wo, wpw, wsco, wfn, wout, wg, wu, wd)


def _rope_table(tm):
    rows = DEC_SEQ // GRID_W
    row_pos = np.repeat(np.arange(rows, dtype=np.float64), GRID_W)
    col_pos = np.tile(np.arange(GRID_W, dtype=np.float64), rows)
    inv = ROPE_THETA ** (-np.arange(0, AXIS_ROPE, 2, dtype=np.float64) / AXIS_ROPE)
    ang = np.concatenate([row_pos[:, None] * inv, col_pos[:, None] * inv], axis=1)
    half = QK_ROPE // 2
    tab = np.zeros((tm + DEC_SEQ, 3 * LANE), np.float64)
    tab[:, 0:LANE] = 1.0
    tab[tm:, ROPE_LANE0:ROPE_LANE0 + half] = np.cos(ang)
    tab[tm:, ROPE_LANE0 + half:ROPE_LANE0 + QK_ROPE] = np.cos(ang)
    tab[tm:, LANE + ROPE_LANE0:LANE + ROPE_LANE0 + half] = -np.sin(ang)
    tab[tm:, 2 * LANE + ROPE_LANE0 + half:2 * LANE + ROPE_LANE0 + QK_ROPE] = np.sin(ang)
    return jnp.asarray(tab.astype(np.float32))


def _qb_layout(w_qb):
    qb = w_qb.reshape(DEPTH, Q_LORA, N_HEADS, QK_NOPE + QK_ROPE)
    return jnp.concatenate([
        qb[..., :QK_NOPE], qb[..., QK_NOPE:][..., ROPE_PERM],
        jnp.zeros((DEPTH, Q_LORA, N_HEADS, LANE - QK_NOPE - QK_ROPE), F32),
    ], axis=-1).reshape(DEPTH, Q_LORA, N_HEADS * LANE).astype(BF)


def _kvb_layout(w_kvb):
    kvb = w_kvb.reshape(DEPTH, KV_LORA, N_HEADS // 2, 2, QK_NOPE + V_HEAD)
    z = jnp.zeros((DEPTH, KV_LORA, N_HEADS // 2, 2, LANE - QK_NOPE), F32)
    kpart = jnp.concatenate([kvb[..., :QK_NOPE], z], axis=-1)
    ve = jnp.concatenate([kvb[:, :, :, 0:1, QK_NOPE:], z[:, :, :, 0:1]], axis=-1)
    vo = jnp.concatenate([z[:, :, :, 1:2], kvb[:, :, :, 1:2, QK_NOPE:]], axis=-1)
    vpart = jnp.concatenate([ve, vo], axis=3)
    flat = lambda a: a.reshape(DEPTH, KV_LORA, N_HEADS * LANE)
    return jnp.concatenate([flat(kpart), flat(vpart)], axis=-1).astype(BF)


def kernel(x_prompt, x_sample, cache_ckv, cache_krope, c, c_ctx, w_ada, b_ada, g_norm1, g_norm2, w_in, g_qa, w_qb, g_kva, w_kvb, w_o_mla, w_conf_dw, b_conf_dw, g_conf_ln, b_conf_ln, w_conf_pw, w_sc_conv, w_sc_out, w_fn, w_out, w_ffn_gate, w_ffn_up, w_ffn_down, g_final):
    tm = 512
    xp = x_prompt.reshape(N_PROMPT, D)
    xs = x_sample.reshape(N_SAMPLE, D)
    cvec = jnp.concatenate([c_ctx[None, :], c, jnp.zeros((8 - 1 - DEC_BATCH, D), F32)], axis=0)
    mod = _ada_call(cvec, w_ada, b_ada)

    rope_tab = _rope_table(tm)
    place = np.zeros((QK_ROPE, LANE), np.float32)
    place[ROPE_PERM, ROPE_LANE0 + np.arange(QK_ROPE)] = 1.0
    wkvb_all = _kvb_layout(w_kvb)
    kc, vc = _ctx_call(cache_ckv, cache_krope, wkvb_all, jnp.asarray(place, BF))

    cg, sg = _dft_cos_sin(FN_GROUP_W)
    eye = np.eye(FN_GROUPS, dtype=np.float32)
    gd = jnp.asarray(np.concatenate([np.kron(eye, cg), np.kron(eye, sg)], axis=1)).astype(BF)
    cs_p = jnp.asarray(np.concatenate(_dft_cos_sin(SEQ), axis=1)).astype(BF)
    cs_s = jnp.asarray(np.concatenate(_dft_cos_sin(DEC_SEQ), axis=1)).astype(BF)

    w_in_t = jnp.swapaxes(w_in, 1, 2).astype(BF)
    wqb = _qb_layout(w_qb)
    w3 = [w.astype(BF) for w in (w_o_mla, w_conf_pw, w_sc_out, w_fn, w_out, w_ffn_gate, w_ffn_up, w_ffn_down)]

    new_ckv, new_krope = [], []
    for l in range(DEPTH):
        q, k, v, ckv, kr, u0, gcx, gb, fn, gates = _stage1_call(
            l, xp, xs, mod[l], g_norm1[l][None, :], g_qa[l][None, :], g_kva[l][None, :],
            w_in_t, wqb, wkvb_all, rope_tab, tm)
        new_ckv.append(ckv[:N_PROMPT].reshape(BATCH, SEQ, KV_LORA))
        new_krope.append(kr[:N_PROMPT].reshape(BATCH, SEQ, QK_ROPE))

        o_p = _attn_prompt_call(q, k, v)
        o_s = _attn_sample_call(q, k, v, kc[l], vc[l], 256)
        u, s = _conv_call(u0, gcx, gb, w_conf_dw[l], b_conf_dw[l][None, :], g_conf_ln[l][None, :],
                          b_conf_ln[l][None, :], w_sc_conv[l], 256)
        f_p = _fourier_call(fn, gd, cs_p, SEQ, BATCH, 0, SEQ)
        f_s = _fourier_call(fn, gd, cs_s, DEC_SEQ, DEC_BATCH, N_PROMPT, 256)
        xp, xs = _stage3_call(
            l, xp, xs, o_p, o_s, u, s, f_p, f_s, gates, mod[l], g_norm2[l][None, :], g_final[None, :],
            *w3, tm, l == DEPTH - 1)

    return (xp.reshape(BATCH, SEQ, D), xs.reshape(DEC_BATCH, DEC_SEQ, D),
            jnp.stack(new_ckv, axis=1), jnp.stack(new_krope, axis=1))
```

```python
import functools

import numpy as np
import jax
import jax.numpy as jnp
from jax import lax
from jax.experimental import pallas as pl
from jax.experimental.pallas import tpu as pltpu

BF = jnp.bfloat16
F32 = jnp.float32

D = 1024
BATCH = 16
SEQ = 256
DEPTH = 2
DEC_BATCH = 2
DEC_SEQ = 2048
PAST_LEN = 256
GRID_W = 64
N_HEADS = 8
QK_NOPE = 64
QK_ROPE = 32
V_HEAD = 64
Q_LORA = 384
KV_LORA = 256
AXIS_ROPE = QK_ROPE // 2
ROPE_THETA = 10000.0
W_CONF = D // 4
CONF_K = 31
W_SC = D // 4
SC_K = 3
W_FN = D // 4
FN_GROUPS = 4
FN_GROUP_W = W_FN // FN_GROUPS
N_BRANCH = 4
FF_HIDDEN = ((8 * D // 3 + 255) // 256) * 256
EPS = 1e-6
OFF_QA = 0
OFF_KVA = OFF_QA + Q_LORA
OFF_CONF = OFF_KVA + KV_LORA + QK_ROPE
OFF_SC = OFF_CONF + 2 * W_CONF
OFF_FN = OFF_SC + 3 * W_SC
OFF_GATE = OFF_FN + W_FN
IN_COLS = OFF_GATE + N_BRANCH * D

N_PROMPT = BATCH * SEQ
N_SAMPLE = DEC_BATCH * DEC_SEQ
N_TOK = N_PROMPT + N_SAMPLE
LANE = 128
SUBLANES = 8
HALO = 16
SM_SCALE = float((QK_NOPE + QK_ROPE) ** -0.5)
Q_SCALE = SM_SCALE * float(np.log2(np.e))
VMEM_LIMIT = 56 * 1024 * 1024

QK_END = OFF_KVA + KV_LORA + LANE

ROPE_PERM = np.array(list(range(0, 8)) + list(range(16, 24)) + list(range(8, 16)) + list(range(24, 32)))
ROPE_LANE0 = QK_NOPE


def _dot(a, b):
    return jnp.dot(a, b, preferred_element_type=F32)


def _dot_nt(a, b):
    return lax.dot_general(a, b, (((1,), (1,)), ((), ())), preferred_element_type=F32)


def _sigmoid(x):
    return jax.nn.sigmoid(x)


def _rms(x, g):
    return x * lax.rsqrt(jnp.mean(x * x, axis=-1, keepdims=True) + EPS) * g


def _full(shape):
    return pl.BlockSpec(shape, lambda *_: (0,) * len(shape))


def _layer(l, shape):
    return pl.BlockSpec((1, *shape), lambda *_: (l,) + (0,) * len(shape), pipeline_mode=pl.Buffered(1))


def _params(n_axes):
    return pltpu.CompilerParams(
        dimension_semantics=("arbitrary",) * n_axes, vmem_limit_bytes=VMEM_LIMIT)


def _mod_row(i, n_prompt_tiles, tiles_per_seq):
    return jnp.where(i >= n_prompt_tiles, 1 + (i - n_prompt_tiles) // tiles_per_seq, 0)


def _ada_body(c_ref, w_ref, b_ref, out_ref):
    cv = c_ref[...]
    sc = (cv * _sigmoid(cv)).astype(BF)
    out_ref[0] = _dot(sc, w_ref[0].astype(BF)) + b_ref[0]


def _ada_call(cvec, w_ada, b_ada):
    n_col = 6 * D // D
    return pl.pallas_call(
        _ada_body,
        grid=(DEPTH, n_col),
        in_specs=[
            _full((8, D)),
            pl.BlockSpec((1, D, D), lambda l, j: (l, 0, j)),
            pl.BlockSpec((1, 1, D), lambda l, j: (l, 0, j)),
        ],
        out_specs=pl.BlockSpec((1, 8, D), lambda l, j: (l, 0, j)),
        out_shape=jax.ShapeDtypeStruct((DEPTH, 8, 6 * D), F32),
        compiler_params=_params(2),
        name="ada_mod",
    )(cvec, w_ada, b_ada.reshape(DEPTH, 1, 6 * D))


def _ctx_body(ckv_ref, kr_ref, wkvb_ref, place_ref, k_ref, v_ref):
    kv = _dot(ckv_ref[0, 0].astype(BF), wkvb_ref[0])
    krp = _dot(kr_ref[0, 0].astype(BF), place_ref[...])
    for h in range(N_HEADS):
        sl = slice(h * LANE, (h + 1) * LANE)
        k_ref[0, 0, :, sl] = (kv[:, sl] + krp).astype(BF)
    v_ref[0, 0] = (kv[:, N_HEADS * LANE:] + _value_ones_row()).astype(BF)


def _ctx_call(cache_ckv, cache_krope, wkvb_all, place):
    return pl.pallas_call(
        _ctx_body,
        grid=(DEPTH, DEC_BATCH),
        in_specs=[
            pl.BlockSpec((1, 1, PAST_LEN, KV_LORA), lambda l, b: (b, l, 0, 0)),
            pl.BlockSpec((1, 1, PAST_LEN, QK_ROPE), lambda l, b: (b, l, 0, 0)),
            pl.BlockSpec((1, KV_LORA, 2 * N_HEADS * LANE), lambda l, b: (l, 0, 0)),
            _full((QK_ROPE, LANE)),
        ],
        out_specs=[
            pl.BlockSpec((1, 1, PAST_LEN, N_HEADS * LANE), lambda l, b: (l, b, 0, 0)),
            pl.BlockSpec((1, 1, PAST_LEN, N_HEADS * LANE), lambda l, b: (l, b, 0, 0)),
        ],
        out_shape=[
            jax.ShapeDtypeStruct((DEPTH, DEC_BATCH, PAST_LEN, N_HEADS * LANE), BF),
            jax.ShapeDtypeStruct((DEPTH, DEC_BATCH, PAST_LEN, N_HEADS * LANE), BF),
        ],
        compiler_params=_params(2),
        name="ctx_keys",
    )(cache_ckv, cache_krope, wkvb_all, place)


def _stage1_body(xp_ref, xs_ref, mod_ref, g1_ref, gqa_ref, gkva_ref,
                 wt_ref, wqb_ref, wkvb_ref, rope_ref,
                 q_ref, k_ref, v_ref, ckv_ref, kr_ref, u0_ref, gcx_ref, gb_ref, fn_ref, gate_ref,
                 *, n_prompt_tiles, tiles_per_seq):
    i = pl.program_id(0)
    is_sample = i >= n_prompt_tiles
    row = _mod_row(i, n_prompt_tiles, tiles_per_seq)
    sh1 = mod_ref[0, pl.ds(row, 1), 0:D]
    sc1 = mod_ref[0, pl.ds(row, 1), D:2 * D]
    x = jnp.where(is_sample, xs_ref[...], xp_ref[...])
    hb = (_rms(x, g1_ref[0]) * (1.0 + sc1) + sh1).astype(BF)

    def proj(c0, c1):
        return _dot_nt(hb, wt_ref[0, c0:c1, :])

    pqk = proj(OFF_QA, QK_END)
    q = _dot(_rms(pqk[:, 0:Q_LORA], gqa_ref[0]).astype(BF), wqb_ref[0])

    ckv = _rms(pqk[:, OFF_KVA:OFF_KVA + KV_LORA], gkva_ref[0])
    ckv_ref[...] = ckv
    kr3 = pqk[:, OFF_KVA + KV_LORA:QK_END]
    kr_ref[...] = kr3[:, 0:QK_ROPE]
    lane = lax.broadcasted_iota(jnp.int32, (1, LANE), 1)
    group = lambda k: (lane >= ROPE_LANE0 + 8 * k) & (lane < ROPE_LANE0 + 8 * (k + 1))
    krm = jnp.where(group(0) | group(3), pltpu.roll(kr3, ROPE_LANE0, 1),
                    jnp.where(group(1), pltpu.roll(kr3, ROPE_LANE0 - 8, 1),
                              jnp.where(group(2), pltpu.roll(kr3, ROPE_LANE0 + 8, 1), 0.0)))
    kv = _dot(ckv.astype(BF), wkvb_ref[0])
    v_ref[...] = (kv[:, N_HEADS * LANE:] + _value_ones_row()).astype(BF)

    def rope(t):
        return (t * rope_ref[:, 0:LANE]
                + pltpu.roll(t, LANE - 16, 1) * rope_ref[:, LANE:2 * LANE]
                + pltpu.roll(t, 16, 1) * rope_ref[:, 2 * LANE:3 * LANE])

    krr = rope(krm)
    for h in range(N_HEADS):
        sl = slice(h * LANE, (h + 1) * LANE)
        q_ref[:, sl] = (rope(q[:, sl]) * Q_SCALE).astype(BF)
        k_ref[:, sl] = (kv[:, sl] + krr).astype(BF)

    pc = proj(OFF_CONF, OFF_SC)
    u0_ref[...] = pc[:, 0:W_CONF] * _sigmoid(pc[:, W_CONF:])
    ps = proj(OFF_SC, OFF_FN)
    gb_ref[...] = ps[:, 0:W_SC]
    gcx_ref[...] = ps[:, W_SC:2 * W_SC] * ps[:, 2 * W_SC:]
    fn_ref[...] = proj(OFF_FN, OFF_GATE).astype(BF)
    for j in range(N_BRANCH):
        gate_ref[:, j * D:(j + 1) * D] = _sigmoid(
            proj(OFF_GATE + j * D, OFF_GATE + (j + 1) * D)).astype(BF)


def _split_specs(tm, w):
    n_p = N_PROMPT // tm
    return (pl.BlockSpec((tm, w), lambda i: (jnp.minimum(i, n_p - 1), 0)),
            pl.BlockSpec((tm, w), lambda i: (jnp.maximum(i - n_p, 0), 0)))


def _stage1_call(l, xp, xs, mod_l, g1, gqa, gkva, w_in_t, wqb, wkvb, rope_tab, tm):
    n_prompt_tiles = N_PROMPT // tm
    tiles_per_seq = DEC_SEQ // tm
    row_blk = lambda w: pl.BlockSpec((tm, w), lambda i: (i, 0))
    body = functools.partial(_stage1_body, n_prompt_tiles=n_prompt_tiles, tiles_per_seq=tiles_per_seq)
    return pl.pallas_call(
        body,
        grid=(N_TOK // tm,),
        in_specs=[
            *_split_specs(tm, D),
            _layer(l, (8, 6 * D)),
            _layer(l, (1, D)),
            _layer(l, (1, Q_LORA)),
            _layer(l, (1, KV_LORA)),
            _layer(l, (IN_COLS, D)),
            _layer(l, (Q_LORA, N_HEADS * LANE)),
            _layer(l, (KV_LORA, 2 * N_HEADS * LANE)),
            pl.BlockSpec((tm, 3 * LANE),
                         lambda i: (jnp.where(i < n_prompt_tiles, 0,
                                              1 + (i - n_prompt_tiles) % tiles_per_seq), 0)),
        ],
        out_specs=[
            row_blk(N_HEADS * LANE), row_blk(N_HEADS * LANE), row_blk(N_HEADS * LANE),
            row_blk(KV_LORA), row_blk(QK_ROPE), row_blk(W_CONF), row_blk(W_SC), row_blk(W_SC),
            row_blk(W_FN), row_blk(N_BRANCH * D),
        ],
        out_shape=[
            jax.ShapeDtypeStruct((N_TOK, N_HEADS * LANE), BF),
            jax.ShapeDtypeStruct((N_TOK, N_HEADS * LANE), BF),
            jax.ShapeDtypeStruct((N_TOK, N_HEADS * LANE), BF),
            jax.ShapeDtypeStruct((N_TOK, KV_LORA), F32),
            jax.ShapeDtypeStruct((N_TOK, QK_ROPE), F32),
            jax.ShapeDtypeStruct((N_TOK, W_CONF), F32),
            jax.ShapeDtypeStruct((N_TOK, W_SC), F32),
            jax.ShapeDtypeStruct((N_TOK, W_SC), F32),
            jax.ShapeDtypeStruct((N_TOK, W_FN), BF),
            jax.ShapeDtypeStruct((N_TOK, N_BRANCH * D), BF),
        ],
        compiler_params=_params(1),
        name="stage1",
    )(xp, xs, mod_l, g1, gqa, gkva, w_in_t, wqb, wkvb, rope_tab)


def _ones_lane(parity):
    return (1 - parity) * V_HEAD


def _value_ones_row():
    col = lax.broadcasted_iota(jnp.int32, (1, N_HEADS * LANE), 1)
    odd = (col // LANE) % 2
    return (col % LANE == jnp.where(odd == 1, _ones_lane(1), _ones_lane(0))).astype(F32)


def _attend_heads(q_ref, kv_refs, o_ref, ahead):
    lane = lax.broadcasted_iota(jnp.int32, (1, LANE), 1)

    def scores(h):
        sl = slice(h * LANE, (h + 1) * LANE)
        return [_dot_nt(q_ref[:, sl], k_ref[:, sl]) for k_ref, _ in kv_refs]

    pending = [scores(h) for h in range(ahead)]
    for hp in range(N_HEADS // 2):
        vsl = slice(hp * LANE, (hp + 1) * LANE)
        outs = []
        for e in range(2):
            h = 2 * hp + e
            sl = slice(h * LANE, (h + 1) * LANE)
            ss = pending.pop(0)
            if h + ahead < N_HEADS:
                pending.append(scores(h + ahead))
            m = functools.reduce(jnp.maximum, [jnp.max(s, axis=-1, keepdims=True) for s in ss])
            o = None
            for s, (_, v_ref) in zip(ss, kv_refs):
                t = _dot(jnp.exp2(s - m).astype(BF), v_ref[:, sl])
                o = t if o is None else o + t
            ones_lane = _ones_lane(e)
            outs.append(o * (1.0 / o[:, ones_lane:ones_lane + 1]))
        o_ref[:, vsl] = jnp.where(lane < V_HEAD, outs[0], outs[1]).astype(BF)


def _attn_prompt_body(q_ref, k_ref, v_ref, o_ref, *, per_step):
    for b in range(per_step):
        rows = pl.ds(b * SEQ, SEQ)
        _attend_heads(q_ref.at[rows], [(k_ref.at[rows], v_ref.at[rows])], o_ref.at[rows], ahead=N_HEADS)


def _attn_prompt_call(q, k, v):
    per_step = 2
    blk = lambda w: pl.BlockSpec((per_step * SEQ, w), lambda b: (b, 0))
    return pl.pallas_call(
        functools.partial(_attn_prompt_body, per_step=per_step),
        grid=(BATCH // per_step,),
        in_specs=[blk(N_HEADS * LANE), blk(N_HEADS * LANE), blk(N_HEADS * LANE)],
        out_specs=blk(N_HEADS * V_HEAD),
        out_shape=jax.ShapeDtypeStruct((N_PROMPT, N_HEADS * V_HEAD), BF),
        compiler_params=_params(1),
        name="attn_prompt",
    )(q, k, v)


def _attn_sample_body(q_ref, kc_ref, vc_ref, k_ref, v_ref, o_ref):
    _attend_heads(q_ref, [(kc_ref.at[0, 0], vc_ref.at[0, 0]), (k_ref, v_ref)], o_ref, ahead=2)


def _attn_sample_call(l, q, k, v, kc, vc, tq):
    n_q = DEC_SEQ // tq
    q0 = N_PROMPT // tq
    s0 = N_PROMPT // DEC_SEQ
    return pl.pallas_call(
        _attn_sample_body,
        grid=(DEC_BATCH, n_q),
        in_specs=[
            pl.BlockSpec((tq, N_HEADS * LANE), lambda b, j: (q0 + b * n_q + j, 0)),
            pl.BlockSpec((1, 1, PAST_LEN, N_HEADS * LANE), lambda b, j: (l, b, 0, 0)),
            pl.BlockSpec((1, 1, PAST_LEN, N_HEADS * LANE), lambda b, j: (l, b, 0, 0)),
            pl.BlockSpec((DEC_SEQ, N_HEADS * LANE), lambda b, j: (s0 + b, 0)),
            pl.BlockSpec((DEC_SEQ, N_HEADS * LANE), lambda b, j: (s0 + b, 0)),
        ],
        out_specs=pl.BlockSpec((tq, N_HEADS * V_HEAD), lambda b, j: (b * n_q + j, 0)),
        out_shape=jax.ShapeDtypeStruct((N_SAMPLE, N_HEADS * V_HEAD), BF),
        compiler_params=_params(2),
        name="attn_sample",
    )(q, kc, vc, k, v)


def _conv_body(u_ref, up_ref, un_ref, c_ref, cp_ref, cn_ref, gb_ref,
               wdw_ref, bdw_ref, gln_ref, bln_ref, wsc_ref,
               uo_ref, so_ref, ubuf, cbuf, shifted, cshifted, *, tl, n_prompt_tiles, tiles_per_seq):
    i = pl.program_id(0)
    is_sample = i >= n_prompt_tiles
    j = jnp.maximum(i - n_prompt_tiles, 0) % tiles_per_seq
    has_prev = is_sample & (j > 0)
    has_next = is_sample & (j < tiles_per_seq - 1)
    for src, prv, nxt, buf in ((u_ref, up_ref, un_ref, ubuf), (c_ref, cp_ref, cn_ref, cbuf)):
        buf[0:HALO, :] = jnp.where(has_prev, prv[...], 0.0)
        buf[HALO:HALO + tl, :] = src[...]
        buf[HALO + tl:2 * HALO + tl, :] = jnp.where(has_next, nxt[...], 0.0)

    span = shifted.shape[1]
    p0 = HALO - CONF_K // 2
    q0 = HALO - SC_K // 2
    for b in range(1, SUBLANES):
        shifted[b] = ubuf[pl.ds(b, span), :]
    for t in range(SC_K):
        if (q0 + t) % SUBLANES:
            cshifted[t] = cbuf[pl.ds((q0 + t) % SUBLANES, span), :]

    def staged(p, r, n):
        a, b = divmod(p, SUBLANES)
        if b == 0:
            return ubuf[pl.ds(r + SUBLANES * a, n), :]
        return shifted[b, pl.ds(r + SUBLANES * a, n), :]

    def staged_c(t, r, n):
        a, b = divmod(q0 + t, SUBLANES)
        if b == 0:
            return cbuf[pl.ds(r + SUBLANES * a, n), :]
        return cshifted[t, pl.ds(r + SUBLANES * a, n), :]

    rc = 32
    for r in range(0, tl, rc):
        acc = staged(p0, r, rc) * wdw_ref[0, 0:1, :]
        for t in range(1, CONF_K):
            acc = acc + staged(p0 + t, r, rc) * wdw_ref[0, t:t + 1, :]
        acc = acc + bdw_ref[0]
        mu = jnp.mean(acc, axis=-1, keepdims=True)
        cen = acc - mu
        var = jnp.mean(cen * cen, axis=-1, keepdims=True)
        y = cen * lax.rsqrt(var + EPS) * gln_ref[0] + bln_ref[0]
        uo_ref[pl.ds(r, rc), :] = (y * _sigmoid(y)).astype(BF)

        cv = staged_c(0, r, rc) * wsc_ref[0, 0:1, :]
        for t in range(1, SC_K):
            cv = cv + staged_c(t, r, rc) * wsc_ref[0, t:t + 1, :]
        so_ref[pl.ds(r, rc), :] = (gb_ref[pl.ds(r, rc), :] * cv).astype(BF)


def _conv_call(l, u0, gcx, gb, wdw, bdw, gln, bln, wsc, tl):
    n_tiles = N_TOK // tl
    hb = tl // HALO
    n_halo = N_TOK // HALO
    cur = pl.BlockSpec((tl, W_CONF), lambda i: (i, 0))
    prv = pl.BlockSpec((HALO, W_CONF), lambda i: (jnp.maximum(i * hb - 1, 0), 0))
    nxt = pl.BlockSpec((HALO, W_CONF), lambda i: (jnp.minimum((i + 1) * hb, n_halo - 1), 0))
    body = functools.partial(_conv_body, tl=tl, n_prompt_tiles=N_PROMPT // tl,
                             tiles_per_seq=DEC_SEQ // tl)
    return pl.pallas_call(
        body,
        grid=(n_tiles,),
        in_specs=[cur, prv, nxt, cur, prv, nxt, cur,
                  _layer(l, (CONF_K, W_CONF)), _layer(l, (1, W_CONF)), _layer(l, (1, W_CONF)),
                  _layer(l, (1, W_CONF)), _layer(l, (SC_K, W_SC))],
        out_specs=[cur, cur],
        out_shape=[jax.ShapeDtypeStruct((N_TOK, W_CONF), BF), jax.ShapeDtypeStruct((N_TOK, W_SC), BF)],
        scratch_shapes=[pltpu.VMEM((tl + 2 * HALO, W_CONF), F32), pltpu.VMEM((tl + 2 * HALO, W_SC), F32),
                        pltpu.VMEM((SUBLANES, tl + 2 * HALO - SUBLANES, W_CONF), F32),
                        pltpu.VMEM((SC_K, tl + 2 * HALO - SUBLANES, W_SC), F32)],
        compiler_params=_params(1),
        name="dwconv",
    )(u0, u0, u0, gcx, gcx, gcx, gb, wdw, bdw, gln, bln, wsc)


def _fourier_body(fn_ref, gd_ref, cs_ref, out_ref, rhs, *, seq, group, scale):
    @pl.when(pl.program_id(1) == 0)
    def _():
        for b in range(group):
            v = _dot(fn_ref[b * seq:(b + 1) * seq, :], gd_ref[...])
            rhs[0:seq, b * W_FN:(b + 1) * W_FN] = v[:, 0:W_FN].astype(BF)
            rhs[seq:2 * seq, b * W_FN:(b + 1) * W_FN] = (-v[:, W_FN:]).astype(BF)

    res = _dot(cs_ref[...], rhs[...]) * scale
    for b in range(group):
        out_ref[b] = res[:, b * W_FN:(b + 1) * W_FN].astype(BF)


def _fourier_call(fn, gd, cs, seq, n_seq, row0, tl, group):
    n_t = seq // tl
    g0 = row0 // (group * seq)
    body = functools.partial(_fourier_body, seq=seq, group=group, scale=float((seq * FN_GROUP_W) ** -0.5))
    out = pl.pallas_call(
        body,
        grid=(n_seq // group, n_t),
        in_specs=[
            pl.BlockSpec((group * seq, W_FN), lambda g, j: (g0 + g, 0)),
            _full((W_FN, 2 * W_FN)),
            pl.BlockSpec((tl, 2 * seq), lambda g, j: (j, 0)),
        ],
        out_specs=pl.BlockSpec((group, tl, W_FN), lambda g, j: (g, j, 0)),
        out_shape=jax.ShapeDtypeStruct((n_seq, seq, W_FN), BF),
        scratch_shapes=[pltpu.VMEM((2 * seq, group * W_FN), BF)],
        compiler_params=_params(2),
        name=f"fourier_{seq}",
    )(fn, gd, cs)
    return out.reshape(n_seq * seq, W_FN)


def _dft_cos_sin(n):
    r = np.arange(n, dtype=np.int64)
    ang = ((r[:, None] * r[None, :]) % n).astype(np.float64) * (2.0 * np.pi / n)
    return np.cos(ang).astype(np.float32), np.sin(ang).astype(np.float32)


def _stage3_body(xp_ref, xs_ref, op_ref, os_ref, u_ref, s_ref, fp_ref, fs_ref, gate_ref,
                 mod_ref, g2_ref, gfin_ref,
                 wo_ref, wpw_ref, wsco_ref, wfn_ref, wout_ref, wg_ref, wu_ref, wd_ref,
                 outp_ref, outs_ref, *, n_prompt_tiles, tiles_per_seq, final, chunk):
    i = pl.program_id(0)
    is_sample = i >= n_prompt_tiles
    row = _mod_row(i, n_prompt_tiles, tiles_per_seq)
    mod = lambda k: mod_ref[0, pl.ds(row, 1), k * D:(k + 1) * D]
    pick = lambda p_ref, s_ref: jnp.where(is_sample, s_ref[...], p_ref[...])
    merged = gate_ref[:, 0:D].astype(F32) * _dot(pick(op_ref, os_ref), wo_ref[0])
    merged = merged + gate_ref[:, D:2 * D].astype(F32) * _dot(u_ref[...], wpw_ref[0])
    merged = merged + gate_ref[:, 2 * D:3 * D].astype(F32) * _dot(s_ref[...], wsco_ref[0])
    merged = merged + gate_ref[:, 3 * D:4 * D].astype(F32) * _dot(pick(fp_ref, fs_ref), wfn_ref[0])
    x1 = pick(xp_ref, xs_ref) + mod(2) * _dot(merged.astype(BF), wout_ref[0])

    h2 = (_rms(x1, g2_ref[0]) * (1.0 + mod(4)) + mod(3)).astype(BF)
    acc = None
    for c0 in range(0, FF_HIDDEN, chunk):
        a = _dot(h2, wg_ref[0, :, c0:c0 + chunk])
        b = _dot(h2, wu_ref[0, :, c0:c0 + chunk])
        t = _dot((a * _sigmoid(a) * b).astype(BF), wd_ref[0, c0:c0 + chunk, :])
        acc = t if acc is None else acc + t
    x2 = x1 + mod(5) * acc
    y = _rms(x2, gfin_ref[...]) if final else x2

    @pl.when(is_sample)
    def _():
        outs_ref[...] = y

    @pl.when(jnp.logical_not(is_sample))
    def _():
        outp_ref[...] = y


def _stage3_call(l, xp, xs, o_p, o_s, u, s, f_p, f_s, gates, mod_l, g2, gfin,
                 wo, wpw, wsco, wfn, wout, wg, wu, wd, tm, final):
    row_blk = lambda w: pl.BlockSpec((tm, w), lambda i: (i, 0))
    body = functools.partial(_stage3_body, n_prompt_tiles=N_PROMPT // tm, tiles_per_seq=DEC_SEQ // tm,
                             final=final, chunk=256)
    return pl.pallas_call(
        body,
        grid=(N_TOK // tm,),
        in_specs=[
            *_split_specs(tm, D), *_split_specs(tm, N_HEADS * V_HEAD), row_blk(W_CONF), row_blk(W_SC),
            *_split_specs(tm, W_FN), row_blk(N_BRANCH * D), _layer(l, (8, 6 * D)), _layer(l, (1, D)),
            _full((1, D)),
            _layer(l, (N_HEADS * V_HEAD, D)), _layer(l, (W_CONF, D)), _layer(l, (W_SC, D)),
            _layer(l, (W_FN, D)), _layer(l, (D, D)), _layer(l, (D, FF_HIDDEN)), _layer(l, (D, FF_HIDDEN)),
            _layer(l, (FF_HIDDEN, D)),
        ],
        out_specs=list(_split_specs(tm, D)),
        out_shape=[jax.ShapeDtypeStruct((N_PROMPT, D), F32), jax.ShapeDtypeStruct((N_SAMPLE, D), F32)],
        compiler_params=_params(1),
        name="stage3",
    )(xp, xs, o_p, o_s, u, s, f_p, f_s, gates, mod_l, g2, gfin, wo, wpw, wsco, wfn, wout, wg, wu, wd)


def _rope_table(tm):
    rows = DEC_SEQ // GRID_W
    row_pos = np.repeat(np.arange(rows, dtype=np.float64), GRID_W)
    col_pos = np.tile(np.arange(GRID_W, dtype=np.float64), rows)
    inv = ROPE_THETA ** (-np.arange(0, AXIS_ROPE, 2, dtype=np.float64) / AXIS_ROPE)
    ang = np.concatenate([row_pos[:, None] * inv, col_pos[:, None] * inv], axis=1)
    half = QK_ROPE // 2
    tab = np.zeros((tm + DEC_SEQ, 3 * LANE), np.float64)
    tab[:, 0:LANE] = 1.0
    tab[tm:, ROPE_LANE0:ROPE_LANE0 + half] = np.cos(ang)
    tab[tm:, ROPE_LANE0 + half:ROPE_LANE0 + QK_ROPE] = np.cos(ang)
    tab[tm:, LANE + ROPE_LANE0:LANE + ROPE_LANE0 + half] = -np.sin(ang)
    tab[tm:, 2 * LANE + ROPE_LANE0 + half:2 * LANE + ROPE_LANE0 + QK_ROPE] = np.sin(ang)
    return jnp.asarray(tab.astype(np.float32))


def _qb_layout(w_qb):
    qb = w_qb.reshape(DEPTH, Q_LORA, N_HEADS, QK_NOPE + QK_ROPE)
    return jnp.concatenate([
        qb[..., :QK_NOPE], qb[..., QK_NOPE:][..., ROPE_PERM],
        jnp.zeros((DEPTH, Q_LORA, N_HEADS, LANE - QK_NOPE - QK_ROPE), F32),
    ], axis=-1).reshape(DEPTH, Q_LORA, N_HEADS * LANE).astype(BF)


def _kvb_layout(w_kvb):
    kvb = w_kvb.reshape(DEPTH, KV_LORA, N_HEADS // 2, 2, QK_NOPE + V_HEAD)
    z = jnp.zeros((DEPTH, KV_LORA, N_HEADS // 2, 2, LANE - QK_NOPE), F32)
    kpart = jnp.concatenate([kvb[..., :QK_NOPE], z], axis=-1)
    ve = jnp.concatenate([kvb[:, :, :, 0:1, QK_NOPE:], z[:, :, :, 0:1]], axis=-1)
    vo = jnp.concatenate([z[:, :, :, 1:2], kvb[:, :, :, 1:2, QK_NOPE:]], axis=-1)
    vpart = jnp.concatenate([ve, vo], axis=3)
    flat = lambda a: a.reshape(DEPTH, KV_LORA, N_HEADS * LANE)
    return jnp.concatenate([flat(kpart), flat(vpart)], axis=-1).astype(BF)


def kernel(x_prompt, x_sample, cache_ckv, cache_krope, c, c_ctx, w_ada, b_ada, g_norm1, g_norm2, w_in, g_qa, w_qb, g_kva, w_kvb, w_o_mla, w_conf_dw, b_conf_dw, g_conf_ln, b_conf_ln, w_conf_pw, w_sc_conv, w_sc_out, w_fn, w_out, w_ffn_gate, w_ffn_up, w_ffn_down, g_final):
    tm = 512
    xp = x_prompt.reshape(N_PROMPT, D)
    xs = x_sample.reshape(N_SAMPLE, D)
    cvec = jnp.concatenate([c_ctx[None, :], c, jnp.zeros((8 - 1 - DEC_BATCH, D), F32)], axis=0)
    mod = _ada_call(cvec, w_ada, b_ada)

    rope_tab = _rope_table(tm)
    place = np.zeros((QK_ROPE, LANE), np.float32)
    place[ROPE_PERM, ROPE_LANE0 + np.arange(QK_ROPE)] = 1.0
    wkvb_all = _kvb_layout(w_kvb)
    kc, vc = _ctx_call(cache_ckv, cache_krope, wkvb_all, jnp.asarray(place, BF))

    cg, sg = _dft_cos_sin(FN_GROUP_W)
    eye = np.eye(FN_GROUPS, dtype=np.float32)
    gd = jnp.asarray(np.concatenate([np.kron(eye, cg), np.kron(eye, sg)], axis=1)).astype(BF)
    cs_p = jnp.asarray(np.concatenate(_dft_cos_sin(SEQ), axis=1)).astype(BF)
    cs_s = jnp.asarray(np.concatenate(_dft_cos_sin(DEC_SEQ), axis=1)).astype(BF)

    w_in_t = jnp.swapaxes(w_in, 1, 2).astype(BF)
    wqb = _qb_layout(w_qb)
    w3 = [w.astype(BF) for w in (w_o_mla, w_conf_pw, w_sc_out, w_fn, w_out, w_ffn_gate, w_ffn_up, w_ffn_down)]

    rows = lambda a: a[:, None, :]
    new_ckv, new_krope = [], []
    for l in range(DEPTH):
        q, k, v, ckv, kr, u0, gcx, gb, fn, gates = _stage1_call(
            l, xp, xs, mod, rows(g_norm1), rows(g_qa), rows(g_kva), w_in_t, wqb, wkvb_all, rope_tab, tm)
        new_ckv.append(ckv[:N_PROMPT].reshape(BATCH, SEQ, KV_LORA))
        new_krope.append(kr[:N_PROMPT].reshape(BATCH, SEQ, QK_ROPE))

        o_p = _attn_prompt_call(q, k, v)
        o_s = _attn_sample_call(l, q, k, v, kc, vc, 256)
        u, s = _conv_call(l, u0, gcx, gb, w_conf_dw, rows(b_conf_dw), rows(g_conf_ln), rows(b_conf_ln),
                          w_sc_conv, 256)
        f_p = _fourier_call(fn, gd, cs_p, SEQ, BATCH, 0, SEQ, 4)
        f_s = _fourier_call(fn, gd, cs_s, DEC_SEQ, DEC_BATCH, N_PROMPT, 256, DEC_BATCH)
        xp, xs = _stage3_call(
            l, xp, xs, o_p, o_s, u, s, f_p, f_s, gates, mod, rows(g_norm2), g_final[None, :],
            *w3, tm, l == DEPTH - 1)

    return (xp.reshape(BATCH, SEQ, D), xs.reshape(DEC_BATCH, DEC_SEQ, D),
            jnp.stack(new_ckv, axis=1), jnp.stack(new_krope, axis=1))
```

```python
import functools

import numpy as np
import jax
import jax.numpy as jnp
from jax import lax
from jax.experimental import pallas as pl
from jax.experimental.pallas import tpu as pltpu

BF = jnp.bfloat16
F32 = jnp.float32

D = 1024
BATCH = 16
SEQ = 256
DEPTH = 2
DEC_BATCH = 2
DEC_SEQ = 2048
PAST_LEN = 256
GRID_W = 64
N_HEADS = 8
QK_NOPE = 64
QK_ROPE = 32
V_HEAD = 64
Q_LORA = 384
KV_LORA = 256
AXIS_ROPE = QK_ROPE // 2
ROPE_THETA = 10000.0
W_CONF = D // 4
CONF_K = 31
W_SC = D // 4
SC_K = 3
W_FN = D // 4
FN_GROUPS = 4
FN_GROUP_W = W_FN // FN_GROUPS
N_BRANCH = 4
FF_HIDDEN = ((8 * D // 3 + 255) // 256) * 256
EPS = 1e-6
OFF_QA = 0
OFF_KVA = OFF_QA + Q_LORA
OFF_CONF = OFF_KVA + KV_LORA + QK_ROPE
OFF_SC = OFF_CONF + 2 * W_CONF
OFF_FN = OFF_SC + 3 * W_SC
OFF_GATE = OFF_FN + W_FN
IN_COLS = OFF_GATE + N_BRANCH * D

N_PROMPT = BATCH * SEQ
N_SAMPLE = DEC_BATCH * DEC_SEQ
N_TOK = N_PROMPT + N_SAMPLE
LANE = 128
SUBLANES = 8
BF16_ROWS = 2 * SUBLANES
HALO = 16
SM_SCALE = float((QK_NOPE + QK_ROPE) ** -0.5)
Q_SCALE = SM_SCALE * float(np.log2(np.e))
VMEM_LIMIT = 56 * 1024 * 1024

QK_END = OFF_KVA + KV_LORA + LANE

ROPE_PERM = np.array(list(range(0, 8)) + list(range(16, 24)) + list(range(8, 16)) + list(range(24, 32)))
ROPE_LANE0 = QK_NOPE


def _dot(a, b):
    return jnp.dot(a, b, preferred_element_type=F32)


def _dot_nt(a, b):
    return lax.dot_general(a, b, (((1,), (1,)), ((), ())), preferred_element_type=F32)


def _sigmoid(x):
    return jax.nn.sigmoid(x)


def _rms(x, g):
    return x * lax.rsqrt(jnp.mean(x * x, axis=-1, keepdims=True) + EPS) * g


def _full(shape):
    return pl.BlockSpec(shape, lambda *_: (0,) * len(shape))


def _layer(l, shape):
    return pl.BlockSpec((1, *shape), lambda *_: (l,) + (0,) * len(shape), pipeline_mode=pl.Buffered(1))


def _params(n_axes):
    return pltpu.CompilerParams(
        dimension_semantics=("arbitrary",) * n_axes, vmem_limit_bytes=VMEM_LIMIT)


def _mod_row(i, n_prompt_tiles, tiles_per_seq):
    return jnp.where(i >= n_prompt_tiles, 1 + (i - n_prompt_tiles) // tiles_per_seq, 0)


def _ada_body(c_ref, w_ref, b_ref, out_ref):
    cv = c_ref[...]
    sc = (cv * _sigmoid(cv)).astype(BF)
    out_ref[0] = _dot(sc, w_ref[0].astype(BF)) + b_ref[0]


def _ada_call(cvec, w_ada, b_ada):
    n_col = 6 * D // D
    return pl.pallas_call(
        _ada_body,
        grid=(DEPTH, n_col),
        in_specs=[
            _full((8, D)),
            pl.BlockSpec((1, D, D), lambda l, j: (l, 0, j)),
            pl.BlockSpec((1, 1, D), lambda l, j: (l, 0, j)),
        ],
        out_specs=pl.BlockSpec((1, 8, D), lambda l, j: (l, 0, j)),
        out_shape=jax.ShapeDtypeStruct((DEPTH, 8, 6 * D), F32),
        compiler_params=_params(2),
        name="ada_mod",
    )(cvec, w_ada, b_ada.reshape(DEPTH, 1, 6 * D))


def _ctx_body(ckv_ref, kr_ref, wkvb_ref, place_ref, k_ref, v_ref):
    kv = _dot(ckv_ref[0, 0].astype(BF), wkvb_ref[0])
    krp = _dot(kr_ref[0, 0].astype(BF), place_ref[...])
    for h in range(N_HEADS):
        sl = slice(h * LANE, (h + 1) * LANE)
        k_ref[0, 0, :, sl] = (kv[:, sl] + krp).astype(BF)
    v_ref[0, 0] = (kv[:, N_HEADS * LANE:] + _value_ones_row()).astype(BF)


def _ctx_call(cache_ckv, cache_krope, wkvb_all, place):
    return pl.pallas_call(
        _ctx_body,
        grid=(DEPTH, DEC_BATCH),
        in_specs=[
            pl.BlockSpec((1, 1, PAST_LEN, KV_LORA), lambda l, b: (b, l, 0, 0)),
            pl.BlockSpec((1, 1, PAST_LEN, QK_ROPE), lambda l, b: (b, l, 0, 0)),
            pl.BlockSpec((1, KV_LORA, 2 * N_HEADS * LANE), lambda l, b: (l, 0, 0)),
            _full((QK_ROPE, LANE)),
        ],
        out_specs=[
            pl.BlockSpec((1, 1, PAST_LEN, N_HEADS * LANE), lambda l, b: (l, b, 0, 0)),
            pl.BlockSpec((1, 1, PAST_LEN, N_HEADS * LANE), lambda l, b: (l, b, 0, 0)),
        ],
        out_shape=[
            jax.ShapeDtypeStruct((DEPTH, DEC_BATCH, PAST_LEN, N_HEADS * LANE), BF),
            jax.ShapeDtypeStruct((DEPTH, DEC_BATCH, PAST_LEN, N_HEADS * LANE), BF),
        ],
        compiler_params=_params(2),
        name="ctx_keys",
    )(cache_ckv, cache_krope, wkvb_all, place)


def _stage1_body(*refs, n_prompt_tiles, tiles_per_seq, n_cast):
    (xp_ref, xs_ref, mod_ref, g1_ref, gqa_ref, gkva_ref, wt_ref, wqb_ref, wkvb_ref, rope_ref) = refs[:10]
    cast_in = refs[10:10 + n_cast]
    (q_ref, k_ref, v_ref, ckv_ref, kr_ref, u0_ref, gcx_ref, gb_ref, fn_ref,
     gate_ref) = refs[10 + n_cast:20 + n_cast]
    cast_out = refs[20 + n_cast:]

    for src, dst in zip(cast_in, cast_out, strict=True):
        dst[...] = src[0].astype(BF)

    i = pl.program_id(0)
    is_sample = i >= n_prompt_tiles
    row = _mod_row(i, n_prompt_tiles, tiles_per_seq)
    sh1 = mod_ref[0, pl.ds(row, 1), 0:D]
    sc1 = mod_ref[0, pl.ds(row, 1), D:2 * D]
    x = jnp.where(is_sample, xs_ref[...], xp_ref[...])
    hb = (_rms(x, g1_ref[0]) * (1.0 + sc1) + sh1).astype(BF)

    def proj(c0, c1):
        return _dot_nt(hb, wt_ref[0, c0:c1, :])

    pqk = proj(OFF_QA, QK_END)
    q = _dot(_rms(pqk[:, 0:Q_LORA], gqa_ref[0]).astype(BF), wqb_ref[0])

    ckv = _rms(pqk[:, OFF_KVA:OFF_KVA + KV_LORA], gkva_ref[0])
    ckv_ref[...] = ckv
    kr3 = pqk[:, OFF_KVA + KV_LORA:QK_END]
    kr_ref[...] = kr3[:, 0:QK_ROPE]
    lane = lax.broadcasted_iota(jnp.int32, (1, LANE), 1)
    group = lambda k: (lane >= ROPE_LANE0 + 8 * k) & (lane < ROPE_LANE0 + 8 * (k + 1))
    krm = jnp.where(group(0) | group(3), pltpu.roll(kr3, ROPE_LANE0, 1),
                    jnp.where(group(1), pltpu.roll(kr3, ROPE_LANE0 - 8, 1),
                              jnp.where(group(2), pltpu.roll(kr3, ROPE_LANE0 + 8, 1), 0.0)))
    kv = _dot(ckv.astype(BF), wkvb_ref[0])
    v_ref[...] = (kv[:, N_HEADS * LANE:] + _value_ones_row()).astype(BF)

    def rope(t):
        return (t * rope_ref[:, 0:LANE]
                + pltpu.roll(t, LANE - 16, 1) * rope_ref[:, LANE:2 * LANE]
                + pltpu.roll(t, 16, 1) * rope_ref[:, 2 * LANE:3 * LANE])

    krr = rope(krm)
    for h in range(N_HEADS):
        sl = slice(h * LANE, (h + 1) * LANE)
        q_ref[:, sl] = (rope(q[:, sl]) * Q_SCALE).astype(BF)
        k_ref[:, sl] = (kv[:, sl] + krr).astype(BF)

    pc = proj(OFF_CONF, OFF_SC)
    u0_ref[...] = pc[:, 0:W_CONF] * _sigmoid(pc[:, W_CONF:])
    ps = proj(OFF_SC, OFF_FN)
    gb_ref[...] = ps[:, 0:W_SC]
    gcx_ref[...] = ps[:, W_SC:2 * W_SC] * ps[:, 2 * W_SC:]
    fn_ref[...] = proj(OFF_FN, OFF_GATE).astype(BF)
    for j in range(N_BRANCH):
        gate_ref[:, j * D:(j + 1) * D] = _sigmoid(
            proj(OFF_GATE + j * D, OFF_GATE + (j + 1) * D)).astype(BF)


def _split_specs(tm, w):
    n_p = N_PROMPT // tm
    return (pl.BlockSpec((tm, w), lambda i: (jnp.minimum(i, n_p - 1), 0)),
            pl.BlockSpec((tm, w), lambda i: (jnp.maximum(i - n_p, 0), 0)))


def _stage1_call(l, xp, xs, mod_l, g1, gqa, gkva, w_in_t, wqb, wkvb, rope_tab, to_cast, tm):
    n_prompt_tiles = N_PROMPT // tm
    tiles_per_seq = DEC_SEQ // tm
    row_blk = lambda w: pl.BlockSpec((tm, w), lambda i: (i, 0))
    n_steps = N_TOK // tm
    body = functools.partial(_stage1_body, n_prompt_tiles=n_prompt_tiles, tiles_per_seq=tiles_per_seq,
                             n_cast=len(to_cast))
    slab = lambda w: w.shape[1] // n_steps
    assert all(w.shape[1] % (BF16_ROWS * n_steps) == 0 for w in to_cast)
    outs = pl.pallas_call(
        body,
        grid=(n_steps,),
        in_specs=[
            *_split_specs(tm, D),
            _layer(l, (8, 6 * D)),
            _layer(l, (1, D)),
            _layer(l, (1, Q_LORA)),
            _layer(l, (1, KV_LORA)),
            _layer(l, (IN_COLS, D)),
            _layer(l, (Q_LORA, N_HEADS * LANE)),
            _layer(l, (KV_LORA, 2 * N_HEADS * LANE)),
            pl.BlockSpec((tm, 3 * LANE),
                         lambda i: (jnp.where(i < n_prompt_tiles, 0,
                                              1 + (i - n_prompt_tiles) % tiles_per_seq), 0)),
            *[pl.BlockSpec((1, slab(w), w.shape[2]), lambda i: (l, i, 0)) for w in to_cast],
        ],
        out_specs=[
            row_blk(N_HEADS * LANE), row_blk(N_HEADS * LANE), row_blk(N_HEADS * LANE),
            row_blk(KV_LORA), row_blk(QK_ROPE), row_blk(W_CONF), row_blk(W_SC), row_blk(W_SC),
            row_blk(W_FN), row_blk(N_BRANCH * D),
            *[pl.BlockSpec((slab(w), w.shape[2]), lambda i: (i, 0)) for w in to_cast],
        ],
        out_shape=[
            jax.ShapeDtypeStruct((N_TOK, N_HEADS * LANE), BF),
            jax.ShapeDtypeStruct((N_TOK, N_HEADS * LANE), BF),
            jax.ShapeDtypeStruct((N_TOK, N_HEADS * LANE), BF),
            jax.ShapeDtypeStruct((N_TOK, KV_LORA), F32),
            jax.ShapeDtypeStruct((N_TOK, QK_ROPE), F32),
            jax.ShapeDtypeStruct((N_TOK, W_CONF), F32),
            jax.ShapeDtypeStruct((N_TOK, W_SC), F32),
            jax.ShapeDtypeStruct((N_TOK, W_SC), F32),
            jax.ShapeDtypeStruct((N_TOK, W_FN), BF),
            jax.ShapeDtypeStruct((N_TOK, N_BRANCH * D), BF),
            *[jax.ShapeDtypeStruct(w.shape[1:], BF) for w in to_cast],
        ],
        compiler_params=_params(1),
        name="stage1",
    )(xp, xs, mod_l, g1, gqa, gkva, w_in_t, wqb, wkvb, rope_tab, *to_cast)
    return outs[:10], outs[10:]


def _ones_lane(parity):
    return (1 - parity) * V_HEAD


def _value_ones_row():
    col = lax.broadcasted_iota(jnp.int32, (1, N_HEADS * LANE), 1)
    odd = (col // LANE) % 2
    return (col % LANE == jnp.where(odd == 1, _ones_lane(1), _ones_lane(0))).astype(F32)


def _attend_heads(q_ref, kv_refs, o_ref, ahead):
    lane = lax.broadcasted_iota(jnp.int32, (1, LANE), 1)

    def scores(h):
        sl = slice(h * LANE, (h + 1) * LANE)
        return [_dot_nt(q_ref[:, sl], k_ref[:, sl]) for k_ref, _ in kv_refs]

    pending = [scores(h) for h in range(ahead)]
    for hp in range(N_HEADS // 2):
        vsl = slice(hp * LANE, (hp + 1) * LANE)
        outs = []
        for e in range(2):
            h = 2 * hp + e
            sl = slice(h * LANE, (h + 1) * LANE)
            ss = pending.pop(0)
            if h + ahead < N_HEADS:
                pending.append(scores(h + ahead))
            m = functools.reduce(jnp.maximum, [jnp.max(s, axis=-1, keepdims=True) for s in ss])
            o = None
            for s, (_, v_ref) in zip(ss, kv_refs):
                t = _dot(jnp.exp2(s - m).astype(BF), v_ref[:, sl])
                o = t if o is None else o + t
            ones_lane = _ones_lane(e)
            outs.append(o * (1.0 / o[:, ones_lane:ones_lane + 1]))
        o_ref[:, vsl] = jnp.where(lane < V_HEAD, outs[0], outs[1]).astype(BF)


def _attn_prompt_body(q_ref, k_ref, v_ref, o_ref, *, per_step):
    for b in range(per_step):
        rows = pl.ds(b * SEQ, SEQ)
        _attend_heads(q_ref.at[rows], [(k_ref.at[rows], v_ref.at[rows])], o_ref.at[rows], ahead=N_HEADS)


def _attn_prompt_call(q, k, v):
    per_step = 2
    blk = lambda w: pl.BlockSpec((per_step * SEQ, w), lambda b: (b, 0))
    return pl.pallas_call(
        functools.partial(_attn_prompt_body, per_step=per_step),
        grid=(BATCH // per_step,),
        in_specs=[blk(N_HEADS * LANE), blk(N_HEADS * LANE), blk(N_HEADS * LANE)],
        out_specs=blk(N_HEADS * V_HEAD),
        out_shape=jax.ShapeDtypeStruct((N_PROMPT, N_HEADS * V_HEAD), BF),
        compiler_params=_params(1),
        name="attn_prompt",
    )(q, k, v)


def _attn_sample_body(q_ref, kc_ref, vc_ref, k_ref, v_ref, o_ref):
    _attend_heads(q_ref, [(kc_ref.at[0, 0], vc_ref.at[0, 0]), (k_ref, v_ref)], o_ref, ahead=2)


def _attn_sample_call(l, q, k, v, kc, vc, tq):
    n_q = DEC_SEQ // tq
    q0 = N_PROMPT // tq
    s0 = N_PROMPT // DEC_SEQ
    return pl.pallas_call(
        _attn_sample_body,
        grid=(DEC_BATCH, n_q),
        in_specs=[
            pl.BlockSpec((tq, N_HEADS * LANE), lambda b, j: (q0 + b * n_q + j, 0)),
            pl.BlockSpec((1, 1, PAST_LEN, N_HEADS * LANE), lambda b, j: (l, b, 0, 0)),
            pl.BlockSpec((1, 1, PAST_LEN, N_HEADS * LANE), lambda b, j: (l, b, 0, 0)),
            pl.BlockSpec((DEC_SEQ, N_HEADS * LANE), lambda b, j: (s0 + b, 0)),
            pl.BlockSpec((DEC_SEQ, N_HEADS * LANE), lambda b, j: (s0 + b, 0)),
        ],
        out_specs=pl.BlockSpec((tq, N_HEADS * V_HEAD), lambda b, j: (b * n_q + j, 0)),
        out_shape=jax.ShapeDtypeStruct((N_SAMPLE, N_HEADS * V_HEAD), BF),
        compiler_params=_params(2),
        name="attn_sample",
    )(q, kc, vc, k, v)


def _conv_body(u_ref, up_ref, un_ref, c_ref, cp_ref, cn_ref, gb_ref,
               wdw_ref, bdw_ref, gln_ref, bln_ref, wsc_ref,
               uo_ref, so_ref, ubuf, cbuf, shifted, cshifted, *, tl, n_prompt_tiles, tiles_per_seq):
    i = pl.program_id(0)
    is_sample = i >= n_prompt_tiles
    j = jnp.maximum(i - n_prompt_tiles, 0) % tiles_per_seq
    has_prev = is_sample & (j > 0)
    has_next = is_sample & (j < tiles_per_seq - 1)
    for src, prv, nxt, buf in ((u_ref, up_ref, un_ref, ubuf), (c_ref, cp_ref, cn_ref, cbuf)):
        buf[0:HALO, :] = jnp.where(has_prev, prv[...], 0.0)
        buf[HALO:HALO + tl, :] = src[...]
        buf[HALO + tl:2 * HALO + tl, :] = jnp.where(has_next, nxt[...], 0.0)

    span = shifted.shape[1]
    p0 = HALO - CONF_K // 2
    q0 = HALO - SC_K // 2
    for b in range(1, SUBLANES):
        shifted[b] = ubuf[pl.ds(b, span), :]
    for t in range(SC_K):
        if (q0 + t) % SUBLANES:
            cshifted[t] = cbuf[pl.ds((q0 + t) % SUBLANES, span), :]

    def staged(p, r, n):
        a, b = divmod(p, SUBLANES)
        if b == 0:
            return ubuf[pl.ds(r + SUBLANES * a, n), :]
        return shifted[b, pl.ds(r + SUBLANES * a, n), :]

    def staged_c(t, r, n):
        a, b = divmod(q0 + t, SUBLANES)
        if b == 0:
            return cbuf[pl.ds(r + SUBLANES * a, n), :]
        return cshifted[t, pl.ds(r + SUBLANES * a, n), :]

    rc = 32
    for r in range(0, tl, rc):
        acc = staged(p0, r, rc) * wdw_ref[0, 0:1, :]
        for t in range(1, CONF_K):
            acc = acc + staged(p0 + t, r, rc) * wdw_ref[0, t:t + 1, :]
        acc = acc + bdw_ref[0]
        mu = jnp.mean(acc, axis=-1, keepdims=True)
        cen = acc - mu
        var = jnp.mean(cen * cen, axis=-1, keepdims=True)
        y = cen * lax.rsqrt(var + EPS) * gln_ref[0] + bln_ref[0]
        uo_ref[pl.ds(r, rc), :] = (y * _sigmoid(y)).astype(BF)

        cv = staged_c(0, r, rc) * wsc_ref[0, 0:1, :]
        for t in range(1, SC_K):
            cv = cv + staged_c(t, r, rc) * wsc_ref[0, t:t + 1, :]
        so_ref[pl.ds(r, rc), :] = (gb_ref[pl.ds(r, rc), :] * cv).astype(BF)


def _conv_call(l, u0, gcx, gb, wdw, bdw, gln, bln, wsc, tl):
    n_tiles = N_TOK // tl
    hb = tl // HALO
    n_halo = N_TOK // HALO
    cur = pl.BlockSpec((tl, W_CONF), lambda i: (i, 0))
    prv = pl.BlockSpec((HALO, W_CONF), lambda i: (jnp.maximum(i * hb - 1, 0), 0))
    nxt = pl.BlockSpec((HALO, W_CONF), lambda i: (jnp.minimum((i + 1) * hb, n_halo - 1), 0))
    body = functools.partial(_conv_body, tl=tl, n_prompt_tiles=N_PROMPT // tl,
                             tiles_per_seq=DEC_SEQ // tl)
    return pl.pallas_call(
        body,
        grid=(n_tiles,),
        in_specs=[cur, prv, nxt, cur, prv, nxt, cur,
                  _layer(l, (CONF_K, W_CONF)), _layer(l, (1, W_CONF)), _layer(l, (1, W_CONF)),
                  _layer(l, (1, W_CONF)), _layer(l, (SC_K, W_SC))],
        out_specs=[cur, cur],
        out_shape=[jax.ShapeDtypeStruct((N_TOK, W_CONF), BF), jax.ShapeDtypeStruct((N_TOK, W_SC), BF)],
        scratch_shapes=[pltpu.VMEM((tl + 2 * HALO, W_CONF), F32), pltpu.VMEM((tl + 2 * HALO, W_SC), F32),
                        pltpu.VMEM((SUBLANES, tl + 2 * HALO - SUBLANES, W_CONF), F32),
                        pltpu.VMEM((SC_K, tl + 2 * HALO - SUBLANES, W_SC), F32)],
        compiler_params=_params(1),
        name="dwconv",
    )(u0, u0, u0, gcx, gcx, gcx, gb, wdw, bdw, gln, bln, wsc)


def _fourier_body(fn_ref, gd_ref, cs_ref, out_ref, rhs, *, seq, group, scale):
    @pl.when(pl.program_id(1) == 0)
    def _():
        for b in range(group):
            v = _dot(fn_ref[b * seq:(b + 1) * seq, :], gd_ref[...])
            rhs[0:seq, b * W_FN:(b + 1) * W_FN] = v[:, 0:W_FN].astype(BF)
            rhs[seq:2 * seq, b * W_FN:(b + 1) * W_FN] = (-v[:, W_FN:]).astype(BF)

    res = _dot(cs_ref[...], rhs[...]) * scale
    for b in range(group):
        out_ref[b] = res[:, b * W_FN:(b + 1) * W_FN].astype(BF)


def _fourier_call(fn, gd, cs, seq, n_seq, row0, tl, group):
    n_t = seq // tl
    g0 = row0 // (group * seq)
    body = functools.partial(_fourier_body, seq=seq, group=group, scale=float((seq * FN_GROUP_W) ** -0.5))
    out = pl.pallas_call(
        body,
        grid=(n_seq // group, n_t),
        in_specs=[
            pl.BlockSpec((group * seq, W_FN), lambda g, j: (g0 + g, 0)),
            _full((W_FN, 2 * W_FN)),
            pl.BlockSpec((tl, 2 * seq), lambda g, j: (j, 0)),
        ],
        out_specs=pl.BlockSpec((group, tl, W_FN), lambda g, j: (g, j, 0)),
        out_shape=jax.ShapeDtypeStruct((n_seq, seq, W_FN), BF),
        scratch_shapes=[pltpu.VMEM((2 * seq, group * W_FN), BF)],
        compiler_params=_params(2),
        name=f"fourier_{seq}",
    )(fn, gd, cs)
    return out.reshape(n_seq * seq, W_FN)


def _dft_cos_sin(n):
    r = np.arange(n, dtype=np.int64)
    ang = ((r[:, None] * r[None, :]) % n).astype(np.float64) * (2.0 * np.pi / n)
    return np.cos(ang).astype(np.float32), np.sin(ang).astype(np.float32)


def _stage3_body(xp_ref, xs_ref, op_ref, os_ref, u_ref, s_ref, fp_ref, fs_ref, gate_ref,
                 mod_ref, g2_ref, gfin_ref,
                 wo_ref, wpw_ref, wsco_ref, wfn_ref, wout_ref, wg_ref, wu_ref, wd_ref,
                 outp_ref, outs_ref, *, n_prompt_tiles, tiles_per_seq, final, chunk):
    i = pl.program_id(0)
    is_sample = i >= n_prompt_tiles
    row = _mod_row(i, n_prompt_tiles, tiles_per_seq)
    mod = lambda k: mod_ref[0, pl.ds(row, 1), k * D:(k + 1) * D]
    pick = lambda p_ref, s_ref: jnp.where(is_sample, s_ref[...], p_ref[...])
    merged = gate_ref[:, 0:D].astype(F32) * _dot(pick(op_ref, os_ref), wo_ref[...])
    merged = merged + gate_ref[:, D:2 * D].astype(F32) * _dot(u_ref[...], wpw_ref[...])
    merged = merged + gate_ref[:, 2 * D:3 * D].astype(F32) * _dot(s_ref[...], wsco_ref[...])
    merged = merged + gate_ref[:, 3 * D:4 * D].astype(F32) * _dot(pick(fp_ref, fs_ref), wfn_ref[...])
    x1 = pick(xp_ref, xs_ref) + mod(2) * _dot(merged.astype(BF), wout_ref[...])

    h2 = (_rms(x1, g2_ref[0]) * (1.0 + mod(4)) + mod(3)).astype(BF)
    acc = None
    for c0 in range(0, FF_HIDDEN, chunk):
        a = _dot(h2, wg_ref[:, c0:c0 + chunk])
        b = _dot(h2, wu_ref[:, c0:c0 + chunk])
        t = _dot((a * _sigmoid(a) * b).astype(BF), wd_ref[c0:c0 + chunk, :])
        acc = t if acc is None else acc + t
    x2 = x1 + mod(5) * acc
    y = _rms(x2, gfin_ref[...]) if final else x2

    @pl.when(is_sample)
    def _():
        outs_ref[...] = y

    @pl.when(jnp.logical_not(is_sample))
    def _():
        outp_ref[...] = y


def _stage3_call(l, xp, xs, o_p, o_s, u, s, f_p, f_s, gates, mod_l, g2, gfin,
                 wo, wpw, wsco, wfn, wout, wg, wu, wd, tm, final):
    row_blk = lambda w: pl.BlockSpec((tm, w), lambda i: (i, 0))
    body = functools.partial(_stage3_body, n_prompt_tiles=N_PROMPT // tm, tiles_per_seq=DEC_SEQ // tm,
                             final=final, chunk=256)
    return pl.pallas_call(
        body,
        grid=(N_TOK // tm,),
        in_specs=[
            *_split_specs(tm, D), *_split_specs(tm, N_HEADS * V_HEAD), row_blk(W_CONF), row_blk(W_SC),
            *_split_specs(tm, W_FN), row_blk(N_BRANCH * D), _layer(l, (8, 6 * D)), _layer(l, (1, D)),
            _full((1, D)),
            _full((N_HEADS * V_HEAD, D)), _full((W_CONF, D)), _full((W_SC, D)), _full((W_FN, D)),
            _full((D, D)), _full((D, FF_HIDDEN)), _full((D, FF_HIDDEN)), _full((FF_HIDDEN, D)),
        ],
        out_specs=list(_split_specs(tm, D)),
        out_shape=[jax.ShapeDtypeStruct((N_PROMPT, D), F32), jax.ShapeDtypeStruct((N_SAMPLE, D), F32)],
        compiler_params=_params(1),
        name="stage3",
    )(xp, xs, o_p, o_s, u, s, f_p, f_s, gates, mod_l, g2, gfin, wo, wpw, wsco, wfn, wout, wg, wu, wd)


def _rope_table(tm):
    rows = DEC_SEQ // GRID_W
    row_pos = np.repeat(np.arange(rows, dtype=np.float64), GRID_W)
    col_pos = np.tile(np.arange(GRID_W, dtype=np.float64), rows)
    inv = ROPE_THETA ** (-np.arange(0, AXIS_ROPE, 2, dtype=np.float64) / AXIS_ROPE)
    ang = np.concatenate([row_pos[:, None] * inv, col_pos[:, None] * inv], axis=1)
    half = QK_ROPE // 2
    tab = np.zeros((tm + DEC_SEQ, 3 * LANE), np.float64)
    tab[:, 0:LANE] = 1.0
    tab[tm:, ROPE_LANE0:ROPE_LANE0 + half] = np.cos(ang)
    tab[tm:, ROPE_LANE0 + half:ROPE_LANE0 + QK_ROPE] = np.cos(ang)
    tab[tm:, LANE + ROPE_LANE0:LANE + ROPE_LANE0 + half] = -np.sin(ang)
    tab[tm:, 2 * LANE + ROPE_LANE0 + half:2 * LANE + ROPE_LANE0 + QK_ROPE] = np.sin(ang)
    return jnp.asarray(tab.astype(np.float32))


def _qb_layout(w_qb):
    qb = w_qb.reshape(DEPTH, Q_LORA, N_HEADS, QK_NOPE + QK_ROPE)
    return jnp.concatenate([
        qb[..., :QK_NOPE], qb[..., QK_NOPE:][..., ROPE_PERM],
        jnp.zeros((DEPTH, Q_LORA, N_HEADS, LANE - QK_NOPE - QK_ROPE), F32),
    ], axis=-1).reshape(DEPTH, Q_LORA, N_HEADS * LANE).astype(BF)


def _kvb_layout(w_kvb):
    kvb = w_kvb.reshape(DEPTH, KV_LORA, N_HEADS // 2, 2, QK_NOPE + V_HEAD)
    z = jnp.zeros((DEPTH, KV_LORA, N_HEADS // 2, 2, LANE - QK_NOPE), F32)
    kpart = jnp.concatenate([kvb[..., :QK_NOPE], z], axis=-1)
    ve = jnp.concatenate([kvb[:, :, :, 0:1, QK_NOPE:], z[:, :, :, 0:1]], axis=-1)
    vo = jnp.concatenate([z[:, :, :, 1:2], kvb[:, :, :, 1:2, QK_NOPE:]], axis=-1)
    vpart = jnp.concatenate([ve, vo], axis=3)
    flat = lambda a: a.reshape(DEPTH, KV_LORA, N_HEADS * LANE)
    return jnp.concatenate([flat(kpart), flat(vpart)], axis=-1).astype(BF)


def kernel(x_prompt, x_sample, cache_ckv, cache_krope, c, c_ctx, w_ada, b_ada, g_norm1, g_norm2, w_in, g_qa, w_qb, g_kva, w_kvb, w_o_mla, w_conf_dw, b_conf_dw, g_conf_ln, b_conf_ln, w_conf_pw, w_sc_conv, w_sc_out, w_fn, w_out, w_ffn_gate, w_ffn_up, w_ffn_down, g_final):
    tm = 512
    xp = x_prompt.reshape(N_PROMPT, D)
    xs = x_sample.reshape(N_SAMPLE, D)
    cvec = jnp.concatenate([c_ctx[None, :], c, jnp.zeros((8 - 1 - DEC_BATCH, D), F32)], axis=0)
    mod = _ada_call(cvec, w_ada, b_ada)

    rope_tab = _rope_table(tm)
    place = np.zeros((QK_ROPE, LANE), np.float32)
    place[ROPE_PERM, ROPE_LANE0 + np.arange(QK_ROPE)] = 1.0
    wkvb_all = _kvb_layout(w_kvb)
    kc, vc = _ctx_call(cache_ckv, cache_krope, wkvb_all, jnp.asarray(place, BF))

    cg, sg = _dft_cos_sin(FN_GROUP_W)
    eye = np.eye(FN_GROUPS, dtype=np.float32)
    gd = jnp.asarray(np.concatenate([np.kron(eye, cg), np.kron(eye, sg)], axis=1)).astype(BF)
    cs_p = jnp.asarray(np.concatenate(_dft_cos_sin(SEQ), axis=1)).astype(BF)
    cs_s = jnp.asarray(np.concatenate(_dft_cos_sin(DEC_SEQ), axis=1)).astype(BF)

    w_in_t = jnp.swapaxes(w_in, 1, 2).astype(BF)
    wqb = _qb_layout(w_qb)
    w3_f32 = (w_o_mla, w_conf_pw, w_sc_out, w_fn, w_out, w_ffn_gate, w_ffn_up, w_ffn_down)

    rows = lambda a: a[:, None, :]
    new_ckv, new_krope = [], []
    for l in range(DEPTH):
        (q, k, v, ckv, kr, u0, gcx, gb, fn, gates), w3 = _stage1_call(
            l, xp, xs, mod, rows(g_norm1), rows(g_qa), rows(g_kva), w_in_t, wqb, wkvb_all, rope_tab,
            w3_f32, tm)
        new_ckv.append(ckv[:N_PROMPT].reshape(BATCH, SEQ, KV_LORA))
        new_krope.append(kr[:N_PROMPT].reshape(BATCH, SEQ, QK_ROPE))

        o_p = _attn_prompt_call(q, k, v)
        o_s = _attn_sample_call(l, q, k, v, kc, vc, 256)
        u, s = _conv_call(l, u0, gcx, gb, w_conf_dw, rows(b_conf_dw), rows(g_conf_ln), rows(b_conf_ln),
                          w_sc_conv, 256)
        f_p = _fourier_call(fn, gd, cs_p, SEQ, BATCH, 0, SEQ, 4)
        f_s = _fourier_call(fn, gd, cs_s, DEC_SEQ, DEC_BATCH, N_PROMPT, 256, DEC_BATCH)
        xp, xs = _stage3_call(
            l, xp, xs, o_p, o_s, u, s, f_p, f_s, gates, mod, rows(g_norm2), g_final[None, :],
            *w3, tm, l == DEPTH - 1)

    return (xp.reshape(BATCH, SEQ, D), xs.reshape(DEC_BATCH, DEC_SEQ, D),
            jnp.stack(new_ckv, axis=1), jnp.stack(new_krope, axis=1))
```

```python
import functools

import numpy as np
import jax
import jax.numpy as jnp
from jax import lax
from jax.experimental import pallas as pl
from jax.experimental.pallas import tpu as pltpu

BF = jnp.bfloat16
F32 = jnp.float32

D = 1024
BATCH = 16
SEQ = 256
DEPTH = 2
DEC_BATCH = 2
DEC_SEQ = 2048
PAST_LEN = 256
GRID_W = 64
N_HEADS = 8
QK_NOPE = 64
QK_ROPE = 32
V_HEAD = 64
Q_LORA = 384
KV_LORA = 256
AXIS_ROPE = QK_ROPE // 2
ROPE_THETA = 10000.0
W_CONF = D // 4
CONF_K = 31
W_SC = D // 4
SC_K = 3
W_FN = D // 4
FN_GROUPS = 4
FN_GROUP_W = W_FN // FN_GROUPS
N_BRANCH = 4
FF_HIDDEN = ((8 * D // 3 + 255) // 256) * 256
EPS = 1e-6
OFF_QA = 0
OFF_KVA = OFF_QA + Q_LORA
OFF_CONF = OFF_KVA + KV_LORA + QK_ROPE
OFF_SC = OFF_CONF + 2 * W_CONF
OFF_FN = OFF_SC + 3 * W_SC
OFF_GATE = OFF_FN + W_FN
IN_COLS = OFF_GATE + N_BRANCH * D

N_PROMPT = BATCH * SEQ
N_SAMPLE = DEC_BATCH * DEC_SEQ
N_TOK = N_PROMPT + N_SAMPLE
LANE = 128
SUBLANES = 8
BF16_ROWS = 2 * SUBLANES
HALO = 16
SM_SCALE = float((QK_NOPE + QK_ROPE) ** -0.5)
Q_SCALE = SM_SCALE * float(np.log2(np.e))
VMEM_LIMIT = 56 * 1024 * 1024

QK_END = OFF_KVA + KV_LORA + LANE

ROPE_PERM = np.array(list(range(0, 8)) + list(range(16, 24)) + list(range(8, 16)) + list(range(24, 32)))
ROPE_LANE0 = QK_NOPE


def _dot(a, b):
    return jnp.dot(a, b, preferred_element_type=F32)


def _dot_nt(a, b):
    return lax.dot_general(a, b, (((1,), (1,)), ((), ())), preferred_element_type=F32)


def _sigmoid(x):
    return jax.nn.sigmoid(x)


def _rms(x, g):
    return x * lax.rsqrt(jnp.mean(x * x, axis=-1, keepdims=True) + EPS) * g


def _full(shape):
    return pl.BlockSpec(shape, lambda *_: (0,) * len(shape))


def _layer(l, shape):
    return pl.BlockSpec((1, *shape), lambda *_: (l,) + (0,) * len(shape), pipeline_mode=pl.Buffered(1))


def _params(n_axes):
    return pltpu.CompilerParams(
        dimension_semantics=("arbitrary",) * n_axes, vmem_limit_bytes=VMEM_LIMIT)


def _mod_row(i, n_prompt_tiles, tiles_per_seq):
    return jnp.where(i >= n_prompt_tiles, 1 + (i - n_prompt_tiles) // tiles_per_seq, 0)


def _ada_body(c_ref, w_ref, b_ref, out_ref):
    cv = c_ref[...]
    sc = (cv * _sigmoid(cv)).astype(BF)
    out_ref[0] = _dot(sc, w_ref[0].astype(BF)) + b_ref[0]


def _ada_call(cvec, w_ada, b_ada):
    n_col = 6 * D // D
    return pl.pallas_call(
        _ada_body,
        grid=(DEPTH, n_col),
        in_specs=[
            _full((8, D)),
            pl.BlockSpec((1, D, D), lambda l, j: (l, 0, j)),
            pl.BlockSpec((1, 1, D), lambda l, j: (l, 0, j)),
        ],
        out_specs=pl.BlockSpec((1, 8, D), lambda l, j: (l, 0, j)),
        out_shape=jax.ShapeDtypeStruct((DEPTH, 8, 6 * D), F32),
        compiler_params=_params(2),
        name="ada_mod",
    )(cvec, w_ada, b_ada.reshape(DEPTH, 1, 6 * D))


def _ctx_body(ckv_ref, kr_ref, wkvb_ref, place_ref, k_ref, v_ref):
    kv = _dot(ckv_ref[0, 0].astype(BF), wkvb_ref[0])
    krp = _dot(kr_ref[0, 0].astype(BF), place_ref[...])
    for h in range(N_HEADS):
        sl = slice(h * LANE, (h + 1) * LANE)
        k_ref[0, 0, :, sl] = (kv[:, sl] + krp).astype(BF)
    v_ref[0, 0] = (kv[:, N_HEADS * LANE:] + _value_ones_row()).astype(BF)


def _ctx_call(cache_ckv, cache_krope, wkvb_all, place):
    return pl.pallas_call(
        _ctx_body,
        grid=(DEPTH, DEC_BATCH),
        in_specs=[
            pl.BlockSpec((1, 1, PAST_LEN, KV_LORA), lambda l, b: (b, l, 0, 0)),
            pl.BlockSpec((1, 1, PAST_LEN, QK_ROPE), lambda l, b: (b, l, 0, 0)),
            pl.BlockSpec((1, KV_LORA, 2 * N_HEADS * LANE), lambda l, b: (l, 0, 0)),
            _full((QK_ROPE, LANE)),
        ],
        out_specs=[
            pl.BlockSpec((1, 1, PAST_LEN, N_HEADS * LANE), lambda l, b: (l, b, 0, 0)),
            pl.BlockSpec((1, 1, PAST_LEN, N_HEADS * LANE), lambda l, b: (l, b, 0, 0)),
        ],
        out_shape=[
            jax.ShapeDtypeStruct((DEPTH, DEC_BATCH, PAST_LEN, N_HEADS * LANE), BF),
            jax.ShapeDtypeStruct((DEPTH, DEC_BATCH, PAST_LEN, N_HEADS * LANE), BF),
        ],
        compiler_params=_params(2),
        name="ctx_keys",
    )(cache_ckv, cache_krope, wkvb_all, place)


def _take(refs, n):
    return refs[:n], refs[n:]


def _read_x(x_refs, is_sample):
    if len(x_refs) == 1:
        return x_refs[0][...]
    return jnp.where(is_sample, x_refs[1][...], x_refs[0][...])


def _stage1_body(*refs, n_prompt_tiles, tiles_per_seq, n_x, n_cast):
    x_refs, refs = _take(refs, n_x)
    (mod_ref, g1_ref, gqa_ref, gkva_ref, wt_ref, wqb_ref, wkvb_ref, rope_ref), refs = _take(refs, 8)
    cast_in, refs = _take(refs, n_cast)
    (q_ref, k_ref, v_ref, ckv_ref, kr_ref, u0_ref, gcx_ref, gb_ref, fn_ref, gate_ref), cast_out = _take(refs, 10)

    for src, dst in zip(cast_in, cast_out, strict=True):
        dst[...] = src[0].astype(BF)

    i = pl.program_id(0)
    is_sample = i >= n_prompt_tiles
    row = _mod_row(i, n_prompt_tiles, tiles_per_seq)
    sh1 = mod_ref[0, pl.ds(row, 1), 0:D]
    sc1 = mod_ref[0, pl.ds(row, 1), D:2 * D]
    hb = (_rms(_read_x(x_refs, is_sample), g1_ref[0]) * (1.0 + sc1) + sh1).astype(BF)

    def proj(c0, c1):
        return _dot_nt(hb, wt_ref[0, c0:c1, :])

    pqk = proj(OFF_QA, QK_END)
    q = _dot(_rms(pqk[:, 0:Q_LORA], gqa_ref[0]).astype(BF), wqb_ref[0])

    ckv = _rms(pqk[:, OFF_KVA:OFF_KVA + KV_LORA], gkva_ref[0])
    ckv_ref[...] = ckv
    kr3 = pqk[:, OFF_KVA + KV_LORA:QK_END]
    kr_ref[...] = kr3[:, 0:QK_ROPE]
    lane = lax.broadcasted_iota(jnp.int32, (1, LANE), 1)
    group = lambda k: (lane >= ROPE_LANE0 + 8 * k) & (lane < ROPE_LANE0 + 8 * (k + 1))
    krm = jnp.where(group(0) | group(3), pltpu.roll(kr3, ROPE_LANE0, 1),
                    jnp.where(group(1), pltpu.roll(kr3, ROPE_LANE0 - 8, 1),
                              jnp.where(group(2), pltpu.roll(kr3, ROPE_LANE0 + 8, 1), 0.0)))
    kv = _dot(ckv.astype(BF), wkvb_ref[0])
    v_ref[...] = (kv[:, N_HEADS * LANE:] + _value_ones_row()).astype(BF)

    def rope(t):
        return (t * rope_ref[:, 0:LANE]
                + pltpu.roll(t, LANE - 16, 1) * rope_ref[:, LANE:2 * LANE]
                + pltpu.roll(t, 16, 1) * rope_ref[:, 2 * LANE:3 * LANE])

    krr = rope(krm)
    for h in range(N_HEADS):
        sl = slice(h * LANE, (h + 1) * LANE)
        q_ref[:, sl] = (rope(q[:, sl]) * Q_SCALE).astype(BF)
        k_ref[:, sl] = (kv[:, sl] + krr).astype(BF)

    pc = proj(OFF_CONF, OFF_SC)
    u0_ref[...] = pc[:, 0:W_CONF] * _sigmoid(pc[:, W_CONF:])
    ps = proj(OFF_SC, OFF_FN)
    gb_ref[...] = ps[:, 0:W_SC]
    gcx_ref[...] = ps[:, W_SC:2 * W_SC] * ps[:, 2 * W_SC:]
    fn_ref[...] = proj(OFF_FN, OFF_GATE).astype(BF)
    for j in range(N_BRANCH):
        gate_ref[:, j * D:(j + 1) * D] = _sigmoid(
            proj(OFF_GATE + j * D, OFF_GATE + (j + 1) * D)).astype(BF)


def _split_specs(tm, w):
    n_p = N_PROMPT // tm
    return (pl.BlockSpec((tm, w), lambda i: (jnp.minimum(i, n_p - 1), 0)),
            pl.BlockSpec((tm, w), lambda i: (jnp.maximum(i - n_p, 0), 0)))


def _x_specs(x, tm):
    return list(_split_specs(tm, D)) if len(x) == 2 else [pl.BlockSpec((tm, D), lambda i: (i, 0))]


def _stage1_call(l, x, mod_l, g1, gqa, gkva, w_in_t, wqb, wkvb, rope_tab, to_cast, tm):
    n_prompt_tiles = N_PROMPT // tm
    tiles_per_seq = DEC_SEQ // tm
    row_blk = lambda w: pl.BlockSpec((tm, w), lambda i: (i, 0))
    n_steps = N_TOK // tm
    body = functools.partial(_stage1_body, n_prompt_tiles=n_prompt_tiles, tiles_per_seq=tiles_per_seq,
                             n_x=len(x), n_cast=len(to_cast))
    slab = lambda w: w.shape[1] // n_steps
    assert all(w.shape[1] % (BF16_ROWS * n_steps) == 0 for w in to_cast)
    outs = pl.pallas_call(
        body,
        grid=(n_steps,),
        in_specs=[
            *_x_specs(x, tm),
            _layer(l, (8, 6 * D)),
            _layer(l, (1, D)),
            _layer(l, (1, Q_LORA)),
            _layer(l, (1, KV_LORA)),
            _layer(l, (IN_COLS, D)),
            _layer(l, (Q_LORA, N_HEADS * LANE)),
            _layer(l, (KV_LORA, 2 * N_HEADS * LANE)),
            pl.BlockSpec((tm, 3 * LANE),
                         lambda i: (jnp.where(i < n_prompt_tiles, 0,
                                              1 + (i - n_prompt_tiles) % tiles_per_seq), 0)),
            *[pl.BlockSpec((1, slab(w), w.shape[2]), lambda i: (l, i, 0)) for w in to_cast],
        ],
        out_specs=[
            row_blk(N_HEADS * LANE), row_blk(N_HEADS * LANE), row_blk(N_HEADS * LANE),
            row_blk(KV_LORA), row_blk(QK_ROPE), row_blk(W_CONF), row_blk(W_SC), row_blk(W_SC),
            row_blk(W_FN), row_blk(N_BRANCH * D),
            *[pl.BlockSpec((slab(w), w.shape[2]), lambda i: (i, 0)) for w in to_cast],
        ],
        out_shape=[
            jax.ShapeDtypeStruct((N_TOK, N_HEADS * LANE), BF),
            jax.ShapeDtypeStruct((N_TOK, N_HEADS * LANE), BF),
            jax.ShapeDtypeStruct((N_TOK, N_HEADS * LANE), BF),
            jax.ShapeDtypeStruct((N_TOK, KV_LORA), F32),
            jax.ShapeDtypeStruct((N_TOK, QK_ROPE), F32),
            jax.ShapeDtypeStruct((N_TOK, W_CONF), F32),
            jax.ShapeDtypeStruct((N_TOK, W_SC), F32),
            jax.ShapeDtypeStruct((N_TOK, W_SC), F32),
            jax.ShapeDtypeStruct((N_TOK, W_FN), BF),
            jax.ShapeDtypeStruct((N_TOK, N_BRANCH * D), BF),
            *[jax.ShapeDtypeStruct(w.shape[1:], BF) for w in to_cast],
        ],
        compiler_params=_params(1),
        name="stage1",
    )(*x, mod_l, g1, gqa, gkva, w_in_t, wqb, wkvb, rope_tab, *to_cast)
    return outs[:10], outs[10:]


def _ones_lane(parity):
    return (1 - parity) * V_HEAD


def _value_ones_row():
    col = lax.broadcasted_iota(jnp.int32, (1, N_HEADS * LANE), 1)
    odd = (col // LANE) % 2
    return (col % LANE == jnp.where(odd == 1, _ones_lane(1), _ones_lane(0))).astype(F32)


def _attend_heads(q_ref, kv_refs, o_ref, ahead):
    lane = lax.broadcasted_iota(jnp.int32, (1, LANE), 1)

    def scores(h):
        sl = slice(h * LANE, (h + 1) * LANE)
        return [_dot_nt(q_ref[:, sl], k_ref[:, sl]) for k_ref, _ in kv_refs]

    pending = [scores(h) for h in range(ahead)]
    for hp in range(N_HEADS // 2):
        vsl = slice(hp * LANE, (hp + 1) * LANE)
        outs = []
        for e in range(2):
            h = 2 * hp + e
            sl = slice(h * LANE, (h + 1) * LANE)
            ss = pending.pop(0)
            if h + ahead < N_HEADS:
                pending.append(scores(h + ahead))
            m = functools.reduce(jnp.maximum, [jnp.max(s, axis=-1, keepdims=True) for s in ss])
            o = None
            for s, (_, v_ref) in zip(ss, kv_refs):
                t = _dot(jnp.exp2(s - m).astype(BF), v_ref[:, sl])
                o = t if o is None else o + t
            ones_lane = _ones_lane(e)
            outs.append(o * (1.0 / o[:, ones_lane:ones_lane + 1]))
        o_ref[:, vsl] = jnp.where(lane < V_HEAD, outs[0], outs[1]).astype(BF)


def _attn_prompt_body(q_ref, k_ref, v_ref, o_ref, *, per_step):
    for b in range(per_step):
        rows = pl.ds(b * SEQ, SEQ)
        _attend_heads(q_ref.at[rows], [(k_ref.at[rows], v_ref.at[rows])], o_ref.at[rows], ahead=N_HEADS)


def _attn_prompt_call(q, k, v):
    per_step = 2
    blk = lambda w: pl.BlockSpec((per_step * SEQ, w), lambda b: (b, 0))
    return pl.pallas_call(
        functools.partial(_attn_prompt_body, per_step=per_step),
        grid=(BATCH // per_step,),
        in_specs=[blk(N_HEADS * LANE), blk(N_HEADS * LANE), blk(N_HEADS * LANE)],
        out_specs=blk(N_HEADS * V_HEAD),
        out_shape=jax.ShapeDtypeStruct((N_PROMPT, N_HEADS * V_HEAD), BF),
        compiler_params=_params(1),
        name="attn_prompt",
    )(q, k, v)


def _attn_sample_body(q_ref, kc_ref, vc_ref, k_ref, v_ref, o_ref):
    _attend_heads(q_ref, [(kc_ref.at[0, 0], vc_ref.at[0, 0]), (k_ref, v_ref)], o_ref, ahead=2)


def _attn_sample_call(l, q, k, v, kc, vc, tq):
    n_q = DEC_SEQ // tq
    q0 = N_PROMPT // tq
    s0 = N_PROMPT // DEC_SEQ
    return pl.pallas_call(
        _attn_sample_body,
        grid=(DEC_BATCH, n_q),
        in_specs=[
            pl.BlockSpec((tq, N_HEADS * LANE), lambda b, j: (q0 + b * n_q + j, 0)),
            pl.BlockSpec((1, 1, PAST_LEN, N_HEADS * LANE), lambda b, j: (l, b, 0, 0)),
            pl.BlockSpec((1, 1, PAST_LEN, N_HEADS * LANE), lambda b, j: (l, b, 0, 0)),
            pl.BlockSpec((DEC_SEQ, N_HEADS * LANE), lambda b, j: (s0 + b, 0)),
            pl.BlockSpec((DEC_SEQ, N_HEADS * LANE), lambda b, j: (s0 + b, 0)),
        ],
        out_specs=pl.BlockSpec((tq, N_HEADS * V_HEAD), lambda b, j: (b * n_q + j, 0)),
        out_shape=jax.ShapeDtypeStruct((N_SAMPLE, N_HEADS * V_HEAD), BF),
        compiler_params=_params(2),
        name="attn_sample",
    )(q, kc, vc, k, v)


CONV_ROWS = 256


def _conv_pass(ubuf, cbuf, gb_ref, wdw_ref, bdw_ref, gln_ref, bln_ref, wsc_ref, uo_ref, so_ref,
               shifted, cshifted):
    tl = CONV_ROWS
    span = shifted.shape[1]
    p0 = HALO - CONF_K // 2
    q0 = HALO - SC_K // 2
    for b in range(1, SUBLANES):
        shifted[b] = ubuf[pl.ds(b, span), :]
    for t in range(SC_K):
        if (q0 + t) % SUBLANES:
            cshifted[t] = cbuf[pl.ds((q0 + t) % SUBLANES, span), :]

    def staged(p, r, n):
        a, b = divmod(p, SUBLANES)
        if b == 0:
            return ubuf[pl.ds(r + SUBLANES * a, n), :]
        return shifted[b, pl.ds(r + SUBLANES * a, n), :]

    def staged_c(t, r, n):
        a, b = divmod(q0 + t, SUBLANES)
        if b == 0:
            return cbuf[pl.ds(r + SUBLANES * a, n), :]
        return cshifted[t, pl.ds(r + SUBLANES * a, n), :]

    rc = 32
    for r in range(0, tl, rc):
        acc = staged(p0, r, rc) * wdw_ref[0, 0:1, :]
        for t in range(1, CONF_K):
            acc = acc + staged(p0 + t, r, rc) * wdw_ref[0, t:t + 1, :]
        acc = acc + bdw_ref[0]
        mu = jnp.mean(acc, axis=-1, keepdims=True)
        cen = acc - mu
        var = jnp.mean(cen * cen, axis=-1, keepdims=True)
        y = cen * lax.rsqrt(var + EPS) * gln_ref[0] + bln_ref[0]
        uo_ref[pl.ds(r, rc), :] = (y * _sigmoid(y)).astype(BF)

        cv = staged_c(0, r, rc) * wsc_ref[0, 0:1, :]
        for t in range(1, SC_K):
            cv = cv + staged_c(t, r, rc) * wsc_ref[0, t:t + 1, :]
        so_ref[pl.ds(r, rc), :] = (gb_ref[pl.ds(r, rc), :] * cv).astype(BF)


def _conv_tile(tm, u_ref, c_ref, gb_ref, halos, is_latent, has_prev, has_next, conv_w, uo_ref, so_ref, scratch):
    ubuf, cbuf, shifted, cshifted = scratch
    n_pass = tm // CONV_ROWS
    zeros = jnp.zeros((HALO, W_CONF), F32)
    for p in range(n_pass):
        r0 = p * CONV_ROWS
        for src, buf, k in ((u_ref, ubuf, 0), (c_ref, cbuf, 2)):
            if p > 0:
                head = jnp.where(is_latent, src[r0 - HALO:r0, :], 0.0)
            else:
                head = zeros if halos is None else jnp.where(has_prev, halos[k][...], 0.0)
            if p < n_pass - 1:
                tail = jnp.where(is_latent, src[r0 + CONV_ROWS:r0 + CONV_ROWS + HALO, :], 0.0)
            else:
                tail = zeros if halos is None else jnp.where(has_next, halos[k + 1][...], 0.0)
            buf[0:HALO, :] = head
            buf[HALO:HALO + CONV_ROWS, :] = src[r0:r0 + CONV_ROWS, :]
            buf[HALO + CONV_ROWS:2 * HALO + CONV_ROWS, :] = tail
        rows = pl.ds(r0, CONV_ROWS)
        _conv_pass(ubuf, cbuf, gb_ref.at[rows], *conv_w, uo_ref.at[rows], so_ref.at[rows], shifted, cshifted)


def _conv_scratch():
    staged = CONV_ROWS + 2 * HALO
    return [pltpu.VMEM((staged, W_CONF), F32), pltpu.VMEM((staged, W_SC), F32),
            pltpu.VMEM((SUBLANES, staged - SUBLANES, W_CONF), F32),
            pltpu.VMEM((SC_K, staged - SUBLANES, W_SC), F32)]


def _conv_body(u_ref, c_ref, gb_ref, up_ref, un_ref, cp_ref, cn_ref, *rest, tl, n_prompt_tiles, tiles_per_seq):
    conv_w, (uo_ref, so_ref, *scratch) = rest[:5], rest[5:]
    i = pl.program_id(0)
    is_latent = i >= n_prompt_tiles
    j = jnp.maximum(i - n_prompt_tiles, 0) % tiles_per_seq
    _conv_tile(tl, u_ref, c_ref, gb_ref, (up_ref, un_ref, cp_ref, cn_ref), is_latent,
               is_latent & (j > 0), is_latent & (j < tiles_per_seq - 1), conv_w, uo_ref, so_ref, scratch)


def _conv_call(l, u0, gcx, gb, conv_w, tl):
    n_halo = N_TOK // HALO
    per_tile = tl // HALO
    cur = pl.BlockSpec((tl, W_CONF), lambda i: (i, 0))
    prv = pl.BlockSpec((HALO, W_CONF), lambda i: (jnp.maximum(i * per_tile - 1, 0), 0))
    nxt = pl.BlockSpec((HALO, W_CONF), lambda i: (jnp.minimum((i + 1) * per_tile, n_halo - 1), 0))
    body = functools.partial(_conv_body, tl=tl, n_prompt_tiles=N_PROMPT // tl, tiles_per_seq=DEC_SEQ // tl)
    return pl.pallas_call(
        body,
        grid=(N_TOK // tl,),
        in_specs=[cur, cur, cur, prv, nxt, prv, nxt,
                  _layer(l, (CONF_K, W_CONF)), _layer(l, (1, W_CONF)), _layer(l, (1, W_CONF)),
                  _layer(l, (1, W_CONF)), _layer(l, (SC_K, W_SC))],
        out_specs=[cur, cur],
        out_shape=[jax.ShapeDtypeStruct((N_TOK, W_CONF), BF), jax.ShapeDtypeStruct((N_TOK, W_SC), BF)],
        scratch_shapes=_conv_scratch(),
        compiler_params=_params(1),
        name="dwconv",
    )(u0, gcx, gb, u0, u0, gcx, gcx, *conv_w)


def _fourier_body(fn_ref, gd_ref, cs_ref, out_ref, rhs, *, seq, group, scale):
    @pl.when(pl.program_id(1) == 0)
    def _():
        for b in range(group):
            v = _dot(fn_ref[b * seq:(b + 1) * seq, :], gd_ref[...])
            rhs[0:seq, b * W_FN:(b + 1) * W_FN] = v[:, 0:W_FN].astype(BF)
            rhs[seq:2 * seq, b * W_FN:(b + 1) * W_FN] = (-v[:, W_FN:]).astype(BF)

    res = _dot(cs_ref[...], rhs[...]) * scale
    for b in range(group):
        out_ref[b] = res[:, b * W_FN:(b + 1) * W_FN].astype(BF)


def _fourier_call(fn, gd, cs, seq, n_seq, row0, tl, group):
    n_t = seq // tl
    g0 = row0 // (group * seq)
    body = functools.partial(_fourier_body, seq=seq, group=group, scale=float((seq * FN_GROUP_W) ** -0.5))
    out = pl.pallas_call(
        body,
        grid=(n_seq // group, n_t),
        in_specs=[
            pl.BlockSpec((group * seq, W_FN), lambda g, j: (g0 + g, 0)),
            _full((W_FN, 2 * W_FN)),
            pl.BlockSpec((tl, 2 * seq), lambda g, j: (j, 0)),
        ],
        out_specs=pl.BlockSpec((group, tl, W_FN), lambda g, j: (g, j, 0)),
        out_shape=jax.ShapeDtypeStruct((n_seq, seq, W_FN), BF),
        scratch_shapes=[pltpu.VMEM((2 * seq, group * W_FN), BF)],
        compiler_params=_params(2),
        name=f"fourier_{seq}",
    )(fn, gd, cs)
    return out.reshape(n_seq * seq, W_FN)


def _dft_cos_sin(n):
    r = np.arange(n, dtype=np.int64)
    ang = ((r[:, None] * r[None, :]) % n).astype(np.float64) * (2.0 * np.pi / n)
    return np.cos(ang).astype(np.float32), np.sin(ang).astype(np.float32)


def _stage3_body(*refs, n_prompt_tiles, tiles_per_seq, n_x, final, chunk):
    x_refs, refs = _take(refs, n_x)
    (op_ref, os_ref, u_ref, s_ref, fp_ref, fs_ref, gate_ref, mod_ref, g2_ref, gfin_ref), refs = _take(refs, 10)
    (wo_ref, wpw_ref, wsco_ref, wfn_ref, wout_ref, wg_ref, wu_ref, wd_ref), out_refs = _take(refs, 8)

    i = pl.program_id(0)
    is_sample = i >= n_prompt_tiles
    row = _mod_row(i, n_prompt_tiles, tiles_per_seq)
    mod = lambda k: mod_ref[0, pl.ds(row, 1), k * D:(k + 1) * D]
    pick = lambda p_ref, s_ref: jnp.where(is_sample, s_ref[...], p_ref[...])
    merged = gate_ref[:, 0:D].astype(F32) * _dot(pick(op_ref, os_ref), wo_ref[...])
    merged = merged + gate_ref[:, D:2 * D].astype(F32) * _dot(u_ref[...], wpw_ref[...])
    merged = merged + gate_ref[:, 2 * D:3 * D].astype(F32) * _dot(s_ref[...], wsco_ref[...])
    merged = merged + gate_ref[:, 3 * D:4 * D].astype(F32) * _dot(pick(fp_ref, fs_ref), wfn_ref[...])
    x1 = _read_x(x_refs, is_sample) + mod(2) * _dot(merged.astype(BF), wout_ref[...])

    h2 = (_rms(x1, g2_ref[0]) * (1.0 + mod(4)) + mod(3)).astype(BF)
    def gate_up(c0):
        return _dot(h2, wg_ref[:, c0:c0 + chunk]), _dot(h2, wu_ref[:, c0:c0 + chunk])

    acc = None
    ab_next = gate_up(0)
    for c0 in range(0, FF_HIDDEN, chunk):
        a, b = ab_next
        if c0 + chunk < FF_HIDDEN:
            ab_next = gate_up(c0 + chunk)
        t = _dot((a * _sigmoid(a) * b).astype(BF), wd_ref[c0:c0 + chunk, :])
        acc = t if acc is None else acc + t
    x2 = x1 + mod(5) * acc
    if not final:
        out_refs[0][...] = x2
        return
    y = _rms(x2, gfin_ref[...])

    @pl.when(is_sample)
    def _():
        out_refs[1][...] = y

    @pl.when(jnp.logical_not(is_sample))
    def _():
        out_refs[0][...] = y


def _stage3_call(l, x, o_p, o_s, u, s, f_p, f_s, gates, mod_l, g2, gfin, w3, tm, final):
    n_tiles = N_TOK // tm
    row_blk = lambda w: pl.BlockSpec((tm, w), lambda i: (i, 0))
    body = functools.partial(_stage3_body, n_prompt_tiles=N_PROMPT // tm, tiles_per_seq=DEC_SEQ // tm,
                             n_x=len(x), final=final, chunk=256)
    if final:
        out_specs = list(_split_specs(tm, D))
        out_shape = [jax.ShapeDtypeStruct((N_PROMPT, D), F32), jax.ShapeDtypeStruct((N_SAMPLE, D), F32)]
    else:
        out_specs = [row_blk(D)]
        out_shape = [jax.ShapeDtypeStruct((N_TOK, D), F32)]
    return pl.pallas_call(
        body,
        grid=(n_tiles,),
        in_specs=[
            *_x_specs(x, tm), *_split_specs(tm, N_HEADS * V_HEAD), row_blk(W_CONF), row_blk(W_SC),
            *_split_specs(tm, W_FN),
            row_blk(N_BRANCH * D), _layer(l, (8, 6 * D)), _layer(l, (1, D)), _full((1, D)),
            _full((N_HEADS * V_HEAD, D)), _full((W_CONF, D)), _full((W_SC, D)), _full((W_FN, D)),
            _full((D, D)), _full((D, FF_HIDDEN)), _full((D, FF_HIDDEN)), _full((FF_HIDDEN, D)),
        ],
        out_specs=out_specs,
        out_shape=out_shape,
        compiler_params=_params(1),
        name="stage3",
    )(*x, o_p, o_s, u, s, f_p, f_s, gates, mod_l, g2, gfin, *w3)


def _rope_table(tm):
    rows = DEC_SEQ // GRID_W
    row_pos = np.repeat(np.arange(rows, dtype=np.float64), GRID_W)
    col_pos = np.tile(np.arange(GRID_W, dtype=np.float64), rows)
    inv = ROPE_THETA ** (-np.arange(0, AXIS_ROPE, 2, dtype=np.float64) / AXIS_ROPE)
    ang = np.concatenate([row_pos[:, None] * inv, col_pos[:, None] * inv], axis=1)
    half = QK_ROPE // 2
    tab = np.zeros((tm + DEC_SEQ, 3 * LANE), np.float64)
    tab[:, 0:LANE] = 1.0
    tab[tm:, ROPE_LANE0:ROPE_LANE0 + half] = np.cos(ang)
    tab[tm:, ROPE_LANE0 + half:ROPE_LANE0 + QK_ROPE] = np.cos(ang)
    tab[tm:, LANE + ROPE_LANE0:LANE + ROPE_LANE0 + half] = -np.sin(ang)
    tab[tm:, 2 * LANE + ROPE_LANE0 + half:2 * LANE + ROPE_LANE0 + QK_ROPE] = np.sin(ang)
    return jnp.asarray(tab.astype(np.float32))


def _qb_layout(w_qb):
    qb = w_qb.reshape(DEPTH, Q_LORA, N_HEADS, QK_NOPE + QK_ROPE)
    return jnp.concatenate([
        qb[..., :QK_NOPE], qb[..., QK_NOPE:][..., ROPE_PERM],
        jnp.zeros((DEPTH, Q_LORA, N_HEADS, LANE - QK_NOPE - QK_ROPE), F32),
    ], axis=-1).reshape(DEPTH, Q_LORA, N_HEADS * LANE).astype(BF)


def _kvb_layout(w_kvb):
    kvb = w_kvb.reshape(DEPTH, KV_LORA, N_HEADS // 2, 2, QK_NOPE + V_HEAD)
    z = jnp.zeros((DEPTH, KV_LORA, N_HEADS // 2, 2, LANE - QK_NOPE), F32)
    kpart = jnp.concatenate([kvb[..., :QK_NOPE], z], axis=-1)
    ve = jnp.concatenate([kvb[:, :, :, 0:1, QK_NOPE:], z[:, :, :, 0:1]], axis=-1)
    vo = jnp.concatenate([z[:, :, :, 1:2], kvb[:, :, :, 1:2, QK_NOPE:]], axis=-1)
    vpart = jnp.concatenate([ve, vo], axis=3)
    flat = lambda a: a.reshape(DEPTH, KV_LORA, N_HEADS * LANE)
    return jnp.concatenate([flat(kpart), flat(vpart)], axis=-1).astype(BF)


def kernel(x_prompt, x_sample, cache_ckv, cache_krope, c, c_ctx, w_ada, b_ada, g_norm1, g_norm2, w_in, g_qa, w_qb, g_kva, w_kvb, w_o_mla, w_conf_dw, b_conf_dw, g_conf_ln, b_conf_ln, w_conf_pw, w_sc_conv, w_sc_out, w_fn, w_out, w_ffn_gate, w_ffn_up, w_ffn_down, g_final):
    tm = 512
    xp = x_prompt.reshape(N_PROMPT, D)
    xs = x_sample.reshape(N_SAMPLE, D)
    cvec = jnp.concatenate([c_ctx[None, :], c, jnp.zeros((8 - 1 - DEC_BATCH, D), F32)], axis=0)
    mod = _ada_call(cvec, w_ada, b_ada)

    rope_tab = _rope_table(tm)
    place = np.zeros((QK_ROPE, LANE), np.float32)
    place[ROPE_PERM, ROPE_LANE0 + np.arange(QK_ROPE)] = 1.0
    wkvb_all = _kvb_layout(w_kvb)
    kc, vc = _ctx_call(cache_ckv, cache_krope, wkvb_all, jnp.asarray(place, BF))

    cg, sg = _dft_cos_sin(FN_GROUP_W)
    eye = np.eye(FN_GROUPS, dtype=np.float32)
    gd = jnp.asarray(np.concatenate([np.kron(eye, cg), np.kron(eye, sg)], axis=1)).astype(BF)
    cs_p = jnp.asarray(np.concatenate(_dft_cos_sin(SEQ), axis=1)).astype(BF)
    cs_s = jnp.asarray(np.concatenate(_dft_cos_sin(DEC_SEQ), axis=1)).astype(BF)

    w_in_t = jnp.swapaxes(w_in, 1, 2).astype(BF)
    wqb = _qb_layout(w_qb)
    w3_f32 = (w_o_mla, w_conf_pw, w_sc_out, w_fn, w_out, w_ffn_gate, w_ffn_up, w_ffn_down)

    rows = lambda a: a[:, None, :]
    conv_w = (w_conf_dw, rows(b_conf_dw), rows(g_conf_ln), rows(b_conf_ln), w_sc_conv)
    x = (xp, xs)
    new_ckv, new_krope = [], []
    for l in range(DEPTH):
        (q, k, v, ckv, kr, u0, gcx, gb, fn, gates), w3 = _stage1_call(
            l, x, mod, rows(g_norm1), rows(g_qa), rows(g_kva), w_in_t, wqb, wkvb_all, rope_tab,
            w3_f32, tm)
        new_ckv.append(ckv[:N_PROMPT].reshape(BATCH, SEQ, KV_LORA))
        new_krope.append(kr[:N_PROMPT].reshape(BATCH, SEQ, QK_ROPE))

        o_p = _attn_prompt_call(q, k, v)
        o_s = _attn_sample_call(l, q, k, v, kc, vc, 256)
        u, s = _conv_call(l, u0, gcx, gb, conv_w, CONV_ROWS)
        f_p = _fourier_call(fn, gd, cs_p, SEQ, BATCH, 0, SEQ, 4)
        f_s = _fourier_call(fn, gd, cs_s, DEC_SEQ, DEC_BATCH, N_PROMPT, 256, DEC_BATCH)
        x = _stage3_call(l, x, o_p, o_s, u, s, f_p, f_s, gates, mod, rows(g_norm2), g_final[None, :], w3,
                         tm, l == DEPTH - 1)

    y_prompt, y_sample = x
    return (y_prompt.reshape(BATCH, SEQ, D), y_sample.reshape(DEC_BATCH, DEC_SEQ, D),
            jnp.stack(new_ckv, axis=1), jnp.stack(new_krope, axis=1))
```

```python
import functools

import numpy as np
import jax
import jax.numpy as jnp
from jax import lax
from jax.experimental import pallas as pl
from jax.experimental.pallas import tpu as pltpu

BF = jnp.bfloat16
F32 = jnp.float32

D = 1024
BATCH = 16
SEQ = 256
DEPTH = 2
DEC_BATCH = 2
DEC_SEQ = 2048
PAST_LEN = 256
GRID_W = 64
N_HEADS = 8
QK_NOPE = 64
QK_ROPE = 32
V_HEAD = 64
Q_LORA = 384
KV_LORA = 256
AXIS_ROPE = QK_ROPE // 2
ROPE_THETA = 10000.0
W_CONF = D // 4
CONF_K = 31
W_SC = D // 4
SC_K = 3
W_FN = D // 4
FN_GROUPS = 4
FN_GROUP_W = W_FN // FN_GROUPS
N_BRANCH = 4
FF_HIDDEN = ((8 * D // 3 + 255) // 256) * 256
EPS = 1e-6
OFF_QA = 0
OFF_KVA = OFF_QA + Q_LORA
OFF_CONF = OFF_KVA + KV_LORA + QK_ROPE
OFF_SC = OFF_CONF + 2 * W_CONF
OFF_FN = OFF_SC + 3 * W_SC
OFF_GATE = OFF_FN + W_FN
IN_COLS = OFF_GATE + N_BRANCH * D

N_PROMPT = BATCH * SEQ
N_SAMPLE = DEC_BATCH * DEC_SEQ
N_TOK = N_PROMPT + N_SAMPLE
LANE = 128
SUBLANES = 8
BF16_ROWS = 2 * SUBLANES
HALO = 16
SM_SCALE = float((QK_NOPE + QK_ROPE) ** -0.5)
Q_SCALE = SM_SCALE * float(np.log2(np.e))
VMEM_LIMIT = 56 * 1024 * 1024

QK_END = OFF_KVA + KV_LORA + LANE

ROPE_PERM = np.array(list(range(0, 8)) + list(range(16, 24)) + list(range(8, 16)) + list(range(24, 32)))
ROPE_LANE0 = QK_NOPE


def _dot(a, b):
    return jnp.dot(a, b, preferred_element_type=F32)


def _dot_nt(a, b):
    return lax.dot_general(a, b, (((1,), (1,)), ((), ())), preferred_element_type=F32)


def _sigmoid(x):
    return jax.nn.sigmoid(x)


def _rms(x, g):
    return x * lax.rsqrt(jnp.mean(x * x, axis=-1, keepdims=True) + EPS) * g


def _full(shape):
    return pl.BlockSpec(shape, lambda *_: (0,) * len(shape))


def _layer(l, shape):
    return pl.BlockSpec((1, *shape), lambda *_: (l,) + (0,) * len(shape), pipeline_mode=pl.Buffered(1))


def _params(n_axes):
    return pltpu.CompilerParams(
        dimension_semantics=("arbitrary",) * n_axes, vmem_limit_bytes=VMEM_LIMIT)


def _mod_row(i, n_prompt_tiles, tiles_per_seq):
    return jnp.where(i >= n_prompt_tiles, 1 + (i - n_prompt_tiles) // tiles_per_seq, 0)


def _ada_body(c_ref, w_ref, b_ref, out_ref):
    cv = c_ref[...]
    sc = (cv * _sigmoid(cv)).astype(BF)
    out_ref[0] = _dot(sc, w_ref[0].astype(BF)) + b_ref[0]


def _ada_call(cvec, w_ada, b_ada):
    n_col = 6 * D // D
    return pl.pallas_call(
        _ada_body,
        grid=(DEPTH, n_col),
        in_specs=[
            _full((8, D)),
            pl.BlockSpec((1, D, D), lambda l, j: (l, 0, j)),
            pl.BlockSpec((1, 1, D), lambda l, j: (l, 0, j)),
        ],
        out_specs=pl.BlockSpec((1, 8, D), lambda l, j: (l, 0, j)),
        out_shape=jax.ShapeDtypeStruct((DEPTH, 8, 6 * D), F32),
        compiler_params=_params(2),
        name="ada_mod",
    )(cvec, w_ada, b_ada.reshape(DEPTH, 1, 6 * D))


def _ctx_body(ckv_ref, kr_ref, wkvb_ref, place_ref, k_ref, v_ref):
    kv = _dot(ckv_ref[0, 0].astype(BF), wkvb_ref[0])
    krp = _dot(kr_ref[0, 0].astype(BF), place_ref[...])
    for h in range(N_HEADS):
        sl = slice(h * LANE, (h + 1) * LANE)
        k_ref[0, 0, :, sl] = (kv[:, sl] + krp).astype(BF)
    v_ref[0, 0] = (kv[:, N_HEADS * LANE:] + _value_ones_row()).astype(BF)


def _ctx_call(cache_ckv, cache_krope, wkvb_all, place):
    return pl.pallas_call(
        _ctx_body,
        grid=(DEPTH, DEC_BATCH),
        in_specs=[
            pl.BlockSpec((1, 1, PAST_LEN, KV_LORA), lambda l, b: (b, l, 0, 0)),
            pl.BlockSpec((1, 1, PAST_LEN, QK_ROPE), lambda l, b: (b, l, 0, 0)),
            pl.BlockSpec((1, KV_LORA, 2 * N_HEADS * LANE), lambda l, b: (l, 0, 0)),
            _full((QK_ROPE, LANE)),
        ],
        out_specs=[
            pl.BlockSpec((1, 1, PAST_LEN, N_HEADS * LANE), lambda l, b: (l, b, 0, 0)),
            pl.BlockSpec((1, 1, PAST_LEN, N_HEADS * LANE), lambda l, b: (l, b, 0, 0)),
        ],
        out_shape=[
            jax.ShapeDtypeStruct((DEPTH, DEC_BATCH, PAST_LEN, N_HEADS * LANE), BF),
            jax.ShapeDtypeStruct((DEPTH, DEC_BATCH, PAST_LEN, N_HEADS * LANE), BF),
        ],
        compiler_params=_params(2),
        name="ctx_keys",
    )(cache_ckv, cache_krope, wkvb_all, place)


def _take(refs, n):
    return refs[:n], refs[n:]


def _read_x(x_refs, is_sample):
    if len(x_refs) == 1:
        return x_refs[0][...]
    return jnp.where(is_sample, x_refs[1][...], x_refs[0][...])


def _stage1_body(*refs, n_prompt_tiles, tiles_per_seq, n_x, n_cast):
    x_refs, refs = _take(refs, n_x)
    (mod_ref, g1_ref, gqa_ref, gkva_ref, wt_ref, wqb_ref, wkvb_ref, rope_ref), refs = _take(refs, 8)
    cast_in, refs = _take(refs, n_cast)
    (q_ref, k_ref, v_ref, ckv_ref, kr_ref, u0_ref, gcx_ref, gb_ref, fn_ref, gate_ref), cast_out = _take(refs, 10)

    for src, dst in zip(cast_in, cast_out, strict=True):
        dst[...] = src[0].astype(BF)

    i = pl.program_id(0)
    is_sample = i >= n_prompt_tiles
    row = _mod_row(i, n_prompt_tiles, tiles_per_seq)
    sh1 = mod_ref[0, pl.ds(row, 1), 0:D]
    sc1 = mod_ref[0, pl.ds(row, 1), D:2 * D]
    hb = (_rms(_read_x(x_refs, is_sample), g1_ref[0]) * (1.0 + sc1) + sh1).astype(BF)

    def proj(c0, c1):
        return _dot_nt(hb, wt_ref[0, c0:c1, :])

    pqk = proj(OFF_QA, QK_END)
    q = _dot(_rms(pqk[:, 0:Q_LORA], gqa_ref[0]).astype(BF), wqb_ref[0])

    ckv = _rms(pqk[:, OFF_KVA:OFF_KVA + KV_LORA], gkva_ref[0])
    ckv_ref[...] = ckv
    kr3 = pqk[:, OFF_KVA + KV_LORA:QK_END]
    kr_ref[...] = kr3[:, 0:QK_ROPE]
    lane = lax.broadcasted_iota(jnp.int32, (1, LANE), 1)
    group = lambda k: (lane >= ROPE_LANE0 + 8 * k) & (lane < ROPE_LANE0 + 8 * (k + 1))
    krm = jnp.where(group(0) | group(3), pltpu.roll(kr3, ROPE_LANE0, 1),
                    jnp.where(group(1), pltpu.roll(kr3, ROPE_LANE0 - 8, 1),
                              jnp.where(group(2), pltpu.roll(kr3, ROPE_LANE0 + 8, 1), 0.0)))
    kv = _dot(ckv.astype(BF), wkvb_ref[0])
    v_ref[...] = (kv[:, N_HEADS * LANE:] + _value_ones_row()).astype(BF)

    def rope(t):
        return (t * rope_ref[:, 0:LANE]
                + pltpu.roll(t, LANE - 16, 1) * rope_ref[:, LANE:2 * LANE]
                + pltpu.roll(t, 16, 1) * rope_ref[:, 2 * LANE:3 * LANE])

    krr = rope(krm)
    for h in range(N_HEADS):
        sl = slice(h * LANE, (h + 1) * LANE)
        q_ref[:, sl] = (rope(q[:, sl]) * Q_SCALE).astype(BF)
        k_ref[:, sl] = (kv[:, sl] + krr).astype(BF)

    for j in range(N_BRANCH):
        gate_ref[:, j * D:(j + 1) * D] = _sigmoid(
            proj(OFF_GATE + j * D, OFF_GATE + (j + 1) * D)).astype(BF)

    pc = proj(OFF_CONF, OFF_SC)
    u0_ref[...] = pc[:, 0:W_CONF] * _sigmoid(pc[:, W_CONF:])
    ps = proj(OFF_SC, OFF_FN)
    gb_ref[...] = ps[:, 0:W_SC]
    gcx_ref[...] = ps[:, W_SC:2 * W_SC] * ps[:, 2 * W_SC:]
    fn_ref[...] = proj(OFF_FN, OFF_GATE).astype(BF)


def _split_specs(tm, w):
    n_p = N_PROMPT // tm
    return (pl.BlockSpec((tm, w), lambda i: (jnp.minimum(i, n_p - 1), 0)),
            pl.BlockSpec((tm, w), lambda i: (jnp.maximum(i - n_p, 0), 0)))


def _x_specs(x, tm):
    return list(_split_specs(tm, D)) if len(x) == 2 else [pl.BlockSpec((tm, D), lambda i: (i, 0))]


def _stage1_call(l, x, mod_l, g1, gqa, gkva, w_in_t, wqb, wkvb, rope_tab, to_cast, tm):
    n_prompt_tiles = N_PROMPT // tm
    tiles_per_seq = DEC_SEQ // tm
    row_blk = lambda w: pl.BlockSpec((tm, w), lambda i: (i, 0))
    n_steps = N_TOK // tm
    body = functools.partial(_stage1_body, n_prompt_tiles=n_prompt_tiles, tiles_per_seq=tiles_per_seq,
                             n_x=len(x), n_cast=len(to_cast))
    slab = lambda w: w.shape[1] // n_steps
    assert all(w.shape[1] % (BF16_ROWS * n_steps) == 0 for w in to_cast)
    outs = pl.pallas_call(
        body,
        grid=(n_steps,),
        in_specs=[
            *_x_specs(x, tm),
            _layer(l, (8, 6 * D)),
            _layer(l, (1, D)),
            _layer(l, (1, Q_LORA)),
            _layer(l, (1, KV_LORA)),
            _layer(l, (IN_COLS, D)),
            _layer(l, (Q_LORA, N_HEADS * LANE)),
            _layer(l, (KV_LORA, 2 * N_HEADS * LANE)),
            pl.BlockSpec((tm, 3 * LANE),
                         lambda i: (jnp.where(i < n_prompt_tiles, 0,
                                              1 + (i - n_prompt_tiles) % tiles_per_seq), 0)),
            *[pl.BlockSpec((1, slab(w), w.shape[2]), lambda i: (l, i, 0)) for w in to_cast],
        ],
        out_specs=[
            row_blk(N_HEADS * LANE), row_blk(N_HEADS * LANE), row_blk(N_HEADS * LANE),
            row_blk(KV_LORA), row_blk(QK_ROPE), row_blk(W_CONF), row_blk(W_SC), row_blk(W_SC),
            row_blk(W_FN), row_blk(N_BRANCH * D),
            *[pl.BlockSpec((slab(w), w.shape[2]), lambda i: (i, 0)) for w in to_cast],
        ],
        out_shape=[
            jax.ShapeDtypeStruct((N_TOK, N_HEADS * LANE), BF),
            jax.ShapeDtypeStruct((N_TOK, N_HEADS * LANE), BF),
            jax.ShapeDtypeStruct((N_TOK, N_HEADS * LANE), BF),
            jax.ShapeDtypeStruct((N_TOK, KV_LORA), F32),
            jax.ShapeDtypeStruct((N_TOK, QK_ROPE), F32),
            jax.ShapeDtypeStruct((N_TOK, W_CONF), F32),
            jax.ShapeDtypeStruct((N_TOK, W_SC), F32),
            jax.ShapeDtypeStruct((N_TOK, W_SC), F32),
            jax.ShapeDtypeStruct((N_TOK, W_FN), BF),
            jax.ShapeDtypeStruct((N_TOK, N_BRANCH * D), BF),
            *[jax.ShapeDtypeStruct(w.shape[1:], BF) for w in to_cast],
        ],
        compiler_params=_params(1),
        name="stage1",
    )(*x, mod_l, g1, gqa, gkva, w_in_t, wqb, wkvb, rope_tab, *to_cast)
    return outs[:10], outs[10:]


def _ones_lane(parity):
    return (1 - parity) * V_HEAD


def _value_ones_row():
    col = lax.broadcasted_iota(jnp.int32, (1, N_HEADS * LANE), 1)
    odd = (col // LANE) % 2
    return (col % LANE == jnp.where(odd == 1, _ones_lane(1), _ones_lane(0))).astype(F32)


def _attend_heads(q_ref, kv_refs, o_ref, ahead):
    lane = lax.broadcasted_iota(jnp.int32, (1, LANE), 1)

    def scores(h):
        sl = slice(h * LANE, (h + 1) * LANE)
        return [_dot_nt(q_ref[:, sl], k_ref[:, sl]) for k_ref, _ in kv_refs]

    pending = [scores(h) for h in range(ahead)]
    for hp in range(N_HEADS // 2):
        vsl = slice(hp * LANE, (hp + 1) * LANE)
        outs = []
        for e in range(2):
            h = 2 * hp + e
            sl = slice(h * LANE, (h + 1) * LANE)
            ss = pending.pop(0)
            if h + ahead < N_HEADS:
                pending.append(scores(h + ahead))
            m = functools.reduce(jnp.maximum, [jnp.max(s, axis=-1, keepdims=True) for s in ss])
            o = None
            for s, (_, v_ref) in zip(ss, kv_refs):
                t = _dot(jnp.exp2(s - m).astype(BF), v_ref[:, sl])
                o = t if o is None else o + t
            ones_lane = _ones_lane(e)
            outs.append(o * (1.0 / o[:, ones_lane:ones_lane + 1]))
        o_ref[:, vsl] = jnp.where(lane < V_HEAD, outs[0], outs[1]).astype(BF)


def _attn_prompt_body(q_ref, k_ref, v_ref, o_ref, *, per_step):
    for b in range(per_step):
        rows = pl.ds(b * SEQ, SEQ)
        _attend_heads(q_ref.at[rows], [(k_ref.at[rows], v_ref.at[rows])], o_ref.at[rows], ahead=N_HEADS)


def _attn_prompt_call(q, k, v):
    per_step = 2
    blk = lambda w: pl.BlockSpec((per_step * SEQ, w), lambda b: (b, 0))
    return pl.pallas_call(
        functools.partial(_attn_prompt_body, per_step=per_step),
        grid=(BATCH // per_step,),
        in_specs=[blk(N_HEADS * LANE), blk(N_HEADS * LANE), blk(N_HEADS * LANE)],
        out_specs=blk(N_HEADS * V_HEAD),
        out_shape=jax.ShapeDtypeStruct((N_PROMPT, N_HEADS * V_HEAD), BF),
        compiler_params=_params(1),
        name="attn_prompt",
    )(q, k, v)


def _attn_sample_body(q_ref, kc_ref, vc_ref, k_ref, v_ref, o_ref):
    _attend_heads(q_ref, [(kc_ref.at[0, 0], vc_ref.at[0, 0]), (k_ref, v_ref)], o_ref, ahead=2)


def _attn_sample_call(l, q, k, v, kc, vc, tq):
    n_q = DEC_SEQ // tq
    q0 = N_PROMPT // tq
    s0 = N_PROMPT // DEC_SEQ
    return pl.pallas_call(
        _attn_sample_body,
        grid=(DEC_BATCH, n_q),
        in_specs=[
            pl.BlockSpec((tq, N_HEADS * LANE), lambda b, j: (q0 + b * n_q + j, 0)),
            pl.BlockSpec((1, 1, PAST_LEN, N_HEADS * LANE), lambda b, j: (l, b, 0, 0)),
            pl.BlockSpec((1, 1, PAST_LEN, N_HEADS * LANE), lambda b, j: (l, b, 0, 0)),
            pl.BlockSpec((DEC_SEQ, N_HEADS * LANE), lambda b, j: (s0 + b, 0)),
            pl.BlockSpec((DEC_SEQ, N_HEADS * LANE), lambda b, j: (s0 + b, 0)),
        ],
        out_specs=pl.BlockSpec((tq, N_HEADS * V_HEAD), lambda b, j: (b * n_q + j, 0)),
        out_shape=jax.ShapeDtypeStruct((N_SAMPLE, N_HEADS * V_HEAD), BF),
        compiler_params=_params(2),
        name="attn_sample",
    )(q, kc, vc, k, v)


CONV_ROWS = 256


def _conv_pass(ubuf, cbuf, gb_ref, wdw_ref, bdw_ref, gln_ref, bln_ref, wsc_ref, uo_ref, so_ref,
               shifted, cshifted):
    tl = CONV_ROWS
    span = shifted.shape[1]
    p0 = HALO - CONF_K // 2
    q0 = HALO - SC_K // 2
    for b in range(1, SUBLANES):
        shifted[b] = ubuf[pl.ds(b, span), :]
    for t in range(SC_K):
        if (q0 + t) % SUBLANES:
            cshifted[t] = cbuf[pl.ds((q0 + t) % SUBLANES, span), :]

    def staged(p, r, n):
        a, b = divmod(p, SUBLANES)
        if b == 0:
            return ubuf[pl.ds(r + SUBLANES * a, n), :]
        return shifted[b, pl.ds(r + SUBLANES * a, n), :]

    def staged_c(t, r, n):
        a, b = divmod(q0 + t, SUBLANES)
        if b == 0:
            return cbuf[pl.ds(r + SUBLANES * a, n), :]
        return cshifted[t, pl.ds(r + SUBLANES * a, n), :]

    rc = 32
    for r in range(0, tl, rc):
        acc = staged(p0, r, rc) * wdw_ref[0, 0:1, :]
        for t in range(1, CONF_K):
            acc = acc + staged(p0 + t, r, rc) * wdw_ref[0, t:t + 1, :]
        acc = acc + bdw_ref[0]
        mu = jnp.mean(acc, axis=-1, keepdims=True)
        cen = acc - mu
        var = jnp.mean(cen * cen, axis=-1, keepdims=True)
        y = cen * lax.rsqrt(var + EPS) * gln_ref[0] + bln_ref[0]
        uo_ref[pl.ds(r, rc), :] = (y * _sigmoid(y)).astype(BF)

        cv = staged_c(0, r, rc) * wsc_ref[0, 0:1, :]
        for t in range(1, SC_K):
            cv = cv + staged_c(t, r, rc) * wsc_ref[0, t:t + 1, :]
        so_ref[pl.ds(r, rc), :] = (gb_ref[pl.ds(r, rc), :] * cv).astype(BF)


def _conv_tile(tm, u_ref, c_ref, gb_ref, halos, is_latent, has_prev, has_next, conv_w, uo_ref, so_ref, scratch):
    ubuf, cbuf, shifted, cshifted = scratch
    n_pass = tm // CONV_ROWS
    zeros = jnp.zeros((HALO, W_CONF), F32)
    for p in range(n_pass):
        r0 = p * CONV_ROWS
        for src, buf, k in ((u_ref, ubuf, 0), (c_ref, cbuf, 2)):
            if p > 0:
                head = jnp.where(is_latent, src[r0 - HALO:r0, :], 0.0)
            else:
                head = zeros if halos is None else jnp.where(has_prev, halos[k][...], 0.0)
            if p < n_pass - 1:
                tail = jnp.where(is_latent, src[r0 + CONV_ROWS:r0 + CONV_ROWS + HALO, :], 0.0)
            else:
                tail = zeros if halos is None else jnp.where(has_next, halos[k + 1][...], 0.0)
            buf[0:HALO, :] = head
            buf[HALO:HALO + CONV_ROWS, :] = src[r0:r0 + CONV_ROWS, :]
            buf[HALO + CONV_ROWS:2 * HALO + CONV_ROWS, :] = tail
        rows = pl.ds(r0, CONV_ROWS)
        _conv_pass(ubuf, cbuf, gb_ref.at[rows], *conv_w, uo_ref.at[rows], so_ref.at[rows], shifted, cshifted)


def _conv_scratch():
    staged = CONV_ROWS + 2 * HALO
    return [pltpu.VMEM((staged, W_CONF), F32), pltpu.VMEM((staged, W_SC), F32),
            pltpu.VMEM((SUBLANES, staged - SUBLANES, W_CONF), F32),
            pltpu.VMEM((SC_K, staged - SUBLANES, W_SC), F32)]


def _conv_body(u_ref, c_ref, gb_ref, up_ref, un_ref, cp_ref, cn_ref, *rest, tl, n_prompt_tiles, tiles_per_seq):
    conv_w, (uo_ref, so_ref, *scratch) = rest[:5], rest[5:]
    i = pl.program_id(0)
    is_latent = i >= n_prompt_tiles
    j = jnp.maximum(i - n_prompt_tiles, 0) % tiles_per_seq
    _conv_tile(tl, u_ref, c_ref, gb_ref, (up_ref, un_ref, cp_ref, cn_ref), is_latent,
               is_latent & (j > 0), is_latent & (j < tiles_per_seq - 1), conv_w, uo_ref, so_ref, scratch)


def _conv_call(l, u0, gcx, gb, conv_w, tl):
    n_halo = N_TOK // HALO
    per_tile = tl // HALO
    cur = pl.BlockSpec((tl, W_CONF), lambda i: (i, 0))
    prv = pl.BlockSpec((HALO, W_CONF), lambda i: (jnp.maximum(i * per_tile - 1, 0), 0))
    nxt = pl.BlockSpec((HALO, W_CONF), lambda i: (jnp.minimum((i + 1) * per_tile, n_halo - 1), 0))
    body = functools.partial(_conv_body, tl=tl, n_prompt_tiles=N_PROMPT // tl, tiles_per_seq=DEC_SEQ // tl)
    return pl.pallas_call(
        body,
        grid=(N_TOK // tl,),
        in_specs=[cur, cur, cur, prv, nxt, prv, nxt,
                  _layer(l, (CONF_K, W_CONF)), _layer(l, (1, W_CONF)), _layer(l, (1, W_CONF)),
                  _layer(l, (1, W_CONF)), _layer(l, (SC_K, W_SC))],
        out_specs=[cur, cur],
        out_shape=[jax.ShapeDtypeStruct((N_TOK, W_CONF), BF), jax.ShapeDtypeStruct((N_TOK, W_SC), BF)],
        scratch_shapes=_conv_scratch(),
        compiler_params=_params(1),
        name="dwconv",
    )(u0, gcx, gb, u0, u0, gcx, gcx, *conv_w)


def _fourier_body(fn_ref, gd_ref, cs_ref, out_ref, rhs, *, seq, group, scale):
    @pl.when(pl.program_id(1) == 0)
    def _():
        for b in range(group):
            v = _dot(fn_ref[b * seq:(b + 1) * seq, :], gd_ref[...])
            rhs[0:seq, b * W_FN:(b + 1) * W_FN] = v[:, 0:W_FN].astype(BF)
            rhs[seq:2 * seq, b * W_FN:(b + 1) * W_FN] = (-v[:, W_FN:]).astype(BF)

    res = _dot(cs_ref[...], rhs[...]) * scale
    for b in range(group):
        out_ref[b] = res[:, b * W_FN:(b + 1) * W_FN].astype(BF)


def _fourier_call(fn, gd, cs, seq, n_seq, row0, tl, group):
    n_t = seq // tl
    g0 = row0 // (group * seq)
    body = functools.partial(_fourier_body, seq=seq, group=group, scale=float((seq * FN_GROUP_W) ** -0.5))
    out = pl.pallas_call(
        body,
        grid=(n_seq // group, n_t),
        in_specs=[
            pl.BlockSpec((group * seq, W_FN), lambda g, j: (g0 + g, 0)),
            _full((W_FN, 2 * W_FN)),
            pl.BlockSpec((tl, 2 * seq), lambda g, j: (j, 0)),
        ],
        out_specs=pl.BlockSpec((group, tl, W_FN), lambda g, j: (g, j, 0)),
        out_shape=jax.ShapeDtypeStruct((n_seq, seq, W_FN), BF),
        scratch_shapes=[pltpu.VMEM((2 * seq, group * W_FN), BF)],
        compiler_params=_params(2),
        name=f"fourier_{seq}",
    )(fn, gd, cs)
    return out.reshape(n_seq * seq, W_FN)


def _dft_cos_sin(n):
    r = np.arange(n, dtype=np.int64)
    ang = ((r[:, None] * r[None, :]) % n).astype(np.float64) * (2.0 * np.pi / n)
    return np.cos(ang).astype(np.float32), np.sin(ang).astype(np.float32)


def _stage3_body(*refs, n_prompt_tiles, tiles_per_seq, n_x, final, chunk):
    x_refs, refs = _take(refs, n_x)
    (op_ref, os_ref, u_ref, s_ref, fp_ref, fs_ref, gate_ref, mod_ref, g2_ref, gfin_ref), refs = _take(refs, 10)
    (wo_ref, wpw_ref, wsco_ref, wfn_ref, wout_ref, wg_ref, wu_ref, wd_ref), out_refs = _take(refs, 8)

    i = pl.program_id(0)
    is_sample = i >= n_prompt_tiles
    row = _mod_row(i, n_prompt_tiles, tiles_per_seq)
    mod = lambda k: mod_ref[0, pl.ds(row, 1), k * D:(k + 1) * D]
    pick = lambda p_ref, s_ref: jnp.where(is_sample, s_ref[...], p_ref[...])
    merged = gate_ref[:, 0:D].astype(F32) * _dot(pick(op_ref, os_ref), wo_ref[...])
    merged = merged + gate_ref[:, D:2 * D].astype(F32) * _dot(u_ref[...], wpw_ref[...])
    merged = merged + gate_ref[:, 2 * D:3 * D].astype(F32) * _dot(s_ref[...], wsco_ref[...])
    merged = merged + gate_ref[:, 3 * D:4 * D].astype(F32) * _dot(pick(fp_ref, fs_ref), wfn_ref[...])
    x1 = _read_x(x_refs, is_sample) + mod(2) * _dot(merged.astype(BF), wout_ref[...])

    h2 = (_rms(x1, g2_ref[0]) * (1.0 + mod(4)) + mod(3)).astype(BF)
    def gate_up(c0):
        return _dot(h2, wg_ref[:, c0:c0 + chunk]), _dot(h2, wu_ref[:, c0:c0 + chunk])

    acc = None
    ab_next = gate_up(0)
    for c0 in range(0, FF_HIDDEN, chunk):
        a, b = ab_next
        if c0 + chunk < FF_HIDDEN:
            ab_next = gate_up(c0 + chunk)
        t = _dot((a * _sigmoid(a) * b).astype(BF), wd_ref[c0:c0 + chunk, :])
        acc = t if acc is None else acc + t
    x2 = x1 + mod(5) * acc
    if not final:
        out_refs[0][...] = x2
        return
    y = _rms(x2, gfin_ref[...])

    @pl.when(is_sample)
    def _():
        out_refs[1][...] = y

    @pl.when(jnp.logical_not(is_sample))
    def _():
        out_refs[0][...] = y


def _stage3_call(l, x, o_p, o_s, u, s, f_p, f_s, gates, mod_l, g2, gfin, w3, tm, final):
    n_tiles = N_TOK // tm
    row_blk = lambda w: pl.BlockSpec((tm, w), lambda i: (i, 0))
    body = functools.partial(_stage3_body, n_prompt_tiles=N_PROMPT // tm, tiles_per_seq=DEC_SEQ // tm,
                             n_x=len(x), final=final, chunk=256)
    if final:
        out_specs = list(_split_specs(tm, D))
        out_shape = [jax.ShapeDtypeStruct((N_PROMPT, D), F32), jax.ShapeDtypeStruct((N_SAMPLE, D), F32)]
    else:
        out_specs = [row_blk(D)]
        out_shape = [jax.ShapeDtypeStruct((N_TOK, D), F32)]
    return pl.pallas_call(
        body,
        grid=(n_tiles,),
        in_specs=[
            *_x_specs(x, tm), *_split_specs(tm, N_HEADS * V_HEAD), row_blk(W_CONF), row_blk(W_SC),
            *_split_specs(tm, W_FN),
            row_blk(N_BRANCH * D), _layer(l, (8, 6 * D)), _layer(l, (1, D)), _full((1, D)),
            _full((N_HEADS * V_HEAD, D)), _full((W_CONF, D)), _full((W_SC, D)), _full((W_FN, D)),
            _full((D, D)), _full((D, FF_HIDDEN)), _full((D, FF_HIDDEN)), _full((FF_HIDDEN, D)),
        ],
        out_specs=out_specs,
        out_shape=out_shape,
        compiler_params=_params(1),
        name="stage3",
    )(*x, o_p, o_s, u, s, f_p, f_s, gates, mod_l, g2, gfin, *w3)


def _rope_table(tm):
    rows = DEC_SEQ // GRID_W
    row_pos = np.repeat(np.arange(rows, dtype=np.float64), GRID_W)
    col_pos = np.tile(np.arange(GRID_W, dtype=np.float64), rows)
    inv = ROPE_THETA ** (-np.arange(0, AXIS_ROPE, 2, dtype=np.float64) / AXIS_ROPE)
    ang = np.concatenate([row_pos[:, None] * inv, col_pos[:, None] * inv], axis=1)
    half = QK_ROPE // 2
    tab = np.zeros((tm + DEC_SEQ, 3 * LANE), np.float64)
    tab[:, 0:LANE] = 1.0
    tab[tm:, ROPE_LANE0:ROPE_LANE0 + half] = np.cos(ang)
    tab[tm:, ROPE_LANE0 + half:ROPE_LANE0 + QK_ROPE] = np.cos(ang)
    tab[tm:, LANE + ROPE_LANE0:LANE + ROPE_LANE0 + half] = -np.sin(ang)
    tab[tm:, 2 * LANE + ROPE_LANE0 + half:2 * LANE + ROPE_LANE0 + QK_ROPE] = np.sin(ang)
    return jnp.asarray(tab.astype(np.float32))


def _head_layout_body(wqb_ref, wkvb_ref, pq_ref, pkv_ref, qb_ref, kvb_ref):
    qb_ref[0] = _dot(wqb_ref[0].astype(BF), pq_ref[...]).astype(BF)
    kvb_ref[0] = _dot(wkvb_ref[0].astype(BF), pkv_ref[...]).astype(BF)


def _head_layout_call(w_qb, w_kvb):
    dqk = QK_NOPE + QK_ROPE
    pq = np.zeros((N_HEADS * dqk, N_HEADS * LANE), np.float32)
    pkv = np.zeros((N_HEADS * LANE, 2 * N_HEADS * LANE), np.float32)
    for h in range(N_HEADS):
        src = h * dqk + np.concatenate([np.arange(QK_NOPE), QK_NOPE + ROPE_PERM])
        pq[src, h * LANE + np.arange(dqk)] = 1.0
        nope = np.arange(QK_NOPE)
        pkv[h * LANE + nope, h * LANE + nope] = 1.0
        val = np.arange(V_HEAD)
        pkv[h * LANE + QK_NOPE + val, (N_HEADS + h) * LANE + (h % 2) * V_HEAD + val] = 1.0
    return pl.pallas_call(
        _head_layout_body,
        grid=(DEPTH,),
        in_specs=[pl.BlockSpec((1, Q_LORA, N_HEADS * dqk), lambda l: (l, 0, 0)),
                  pl.BlockSpec((1, KV_LORA, N_HEADS * LANE), lambda l: (l, 0, 0)),
                  _full(pq.shape), _full(pkv.shape)],
        out_specs=[pl.BlockSpec((1, Q_LORA, N_HEADS * LANE), lambda l: (l, 0, 0)),
                   pl.BlockSpec((1, KV_LORA, 2 * N_HEADS * LANE), lambda l: (l, 0, 0))],
        out_shape=[jax.ShapeDtypeStruct((DEPTH, Q_LORA, N_HEADS * LANE), BF),
                   jax.ShapeDtypeStruct((DEPTH, KV_LORA, 2 * N_HEADS * LANE), BF)],
        compiler_params=_params(1),
        name="head_layout",
    )(w_qb, w_kvb, jnp.asarray(pq, BF), jnp.asarray(pkv, BF))


def kernel(x_prompt, x_sample, cache_ckv, cache_krope, c, c_ctx, w_ada, b_ada, g_norm1, g_norm2, w_in, g_qa, w_qb, g_kva, w_kvb, w_o_mla, w_conf_dw, b_conf_dw, g_conf_ln, b_conf_ln, w_conf_pw, w_sc_conv, w_sc_out, w_fn, w_out, w_ffn_gate, w_ffn_up, w_ffn_down, g_final):
    tm = 512
    xp = x_prompt.reshape(N_PROMPT, D)
    xs = x_sample.reshape(N_SAMPLE, D)
    cvec = jnp.concatenate([c_ctx[None, :], c, jnp.zeros((8 - 1 - DEC_BATCH, D), F32)], axis=0)
    mod = _ada_call(cvec, w_ada, b_ada)

    rope_tab = _rope_table(tm)
    place = np.zeros((QK_ROPE, LANE), np.float32)
    place[ROPE_PERM, ROPE_LANE0 + np.arange(QK_ROPE)] = 1.0
    wqb, wkvb_all = _head_layout_call(w_qb, w_kvb)
    kc, vc = _ctx_call(cache_ckv, cache_krope, wkvb_all, jnp.asarray(place, BF))

    cg, sg = _dft_cos_sin(FN_GROUP_W)
    eye = np.eye(FN_GROUPS, dtype=np.float32)
    gd = jnp.asarray(np.concatenate([np.kron(eye, cg), np.kron(eye, sg)], axis=1)).astype(BF)
    cs_p = jnp.asarray(np.concatenate(_dft_cos_sin(SEQ), axis=1)).astype(BF)
    cs_s = jnp.asarray(np.concatenate(_dft_cos_sin(DEC_SEQ), axis=1)).astype(BF)

    w_in_t = jnp.swapaxes(w_in, 1, 2).astype(BF)
    w3_f32 = (w_o_mla, w_conf_pw, w_sc_out, w_fn, w_out, w_ffn_gate, w_ffn_up, w_ffn_down)

    rows = lambda a: a[:, None, :]
    conv_w = (w_conf_dw, rows(b_conf_dw), rows(g_conf_ln), rows(b_conf_ln), w_sc_conv)
    x = (xp, xs)
    new_ckv, new_krope = [], []
    for l in range(DEPTH):
        (q, k, v, ckv, kr, u0, gcx, gb, fn, gates), w3 = _stage1_call(
            l, x, mod, rows(g_norm1), rows(g_qa), rows(g_kva), w_in_t, wqb, wkvb_all, rope_tab,
            w3_f32, tm)
        new_ckv.append(ckv[:N_PROMPT].reshape(BATCH, SEQ, KV_LORA))
        new_krope.append(kr[:N_PROMPT].reshape(BATCH, SEQ, QK_ROPE))

        o_p = _attn_prompt_call(q, k, v)
        o_s = _attn_sample_call(l, q, k, v, kc, vc, 256)
        u, s = _conv_call(l, u0, gcx, gb, conv_w, CONV_ROWS)
        f_p = _fourier_call(fn, gd, cs_p, SEQ, BATCH, 0, SEQ, 4)
        f_s = _fourier_call(fn, gd, cs_s, DEC_SEQ, DEC_BATCH, N_PROMPT, 256, DEC_BATCH)
        x = _stage3_call(l, x, o_p, o_s, u, s, f_p, f_s, gates, mod, rows(g_norm2), g_final[None, :], w3,
                         tm, l == DEPTH - 1)

    y_prompt, y_sample = x
    return (y_prompt.reshape(BATCH, SEQ, D), y_sample.reshape(DEC_BATCH, DEC_SEQ, D),
            jnp.stack(new_ckv, axis=1), jnp.stack(new_krope, axis=1))
```

```python
import functools

import numpy as np
import jax
import jax.numpy as jnp
from jax import lax
from jax.experimental import pallas as pl
from jax.experimental.pallas import tpu as pltpu

BF = jnp.bfloat16
F32 = jnp.float32

D = 1024
BATCH = 16
SEQ = 256
DEPTH = 2
DEC_BATCH = 2
DEC_SEQ = 2048
PAST_LEN = 256
GRID_W = 64
N_HEADS = 8
QK_NOPE = 64
QK_ROPE = 32
V_HEAD = 64
Q_LORA = 384
KV_LORA = 256
AXIS_ROPE = QK_ROPE // 2
ROPE_THETA = 10000.0
W_CONF = D // 4
CONF_K = 31
W_SC = D // 4
SC_K = 3
W_FN = D // 4
FN_GROUPS = 4
FN_GROUP_W = W_FN // FN_GROUPS
N_BRANCH = 4
FF_HIDDEN = ((8 * D // 3 + 255) // 256) * 256
EPS = 1e-6
OFF_QA = 0
OFF_KVA = OFF_QA + Q_LORA
OFF_CONF = OFF_KVA + KV_LORA + QK_ROPE
OFF_SC = OFF_CONF + 2 * W_CONF
OFF_FN = OFF_SC + 3 * W_SC
OFF_GATE = OFF_FN + W_FN
IN_COLS = OFF_GATE + N_BRANCH * D

N_PROMPT = BATCH * SEQ
N_SAMPLE = DEC_BATCH * DEC_SEQ
N_TOK = N_PROMPT + N_SAMPLE
LANE = 128
SUBLANES = 8
BF16_ROWS = 2 * SUBLANES
HALO = 16
SM_SCALE = float((QK_NOPE + QK_ROPE) ** -0.5)
Q_SCALE = SM_SCALE * float(np.log2(np.e))
VMEM_LIMIT = 56 * 1024 * 1024

QK_END = OFF_KVA + KV_LORA + LANE

ROPE_PERM = np.array(list(range(0, 8)) + list(range(16, 24)) + list(range(8, 16)) + list(range(24, 32)))
ROPE_LANE0 = QK_NOPE


def _dot(a, b):
    return jnp.dot(a, b, preferred_element_type=F32)


def _dot_nt(a, b):
    return lax.dot_general(a, b, (((1,), (1,)), ((), ())), preferred_element_type=F32)


def _sigmoid(x):
    return jax.nn.sigmoid(x)


def _rms(x, g):
    return x * lax.rsqrt(jnp.mean(x * x, axis=-1, keepdims=True) + EPS) * g


def _full(shape):
    return pl.BlockSpec(shape, lambda *_: (0,) * len(shape))


def _layer(l, shape):
    return pl.BlockSpec((1, *shape), lambda *_: (l,) + (0,) * len(shape), pipeline_mode=pl.Buffered(1))


def _params(n_axes):
    return pltpu.CompilerParams(
        dimension_semantics=("arbitrary",) * n_axes, vmem_limit_bytes=VMEM_LIMIT)


def _mod_row(i, n_prompt_tiles, tiles_per_seq):
    return jnp.where(i >= n_prompt_tiles, 1 + (i - n_prompt_tiles) // tiles_per_seq, 0)


def _ada_body(c_ref, w_ref, b_ref, out_ref):
    cv = c_ref[...]
    sc = (cv * _sigmoid(cv)).astype(BF)
    out_ref[0] = _dot(sc, w_ref[0].astype(BF)) + b_ref[0]


def _ada_call(cvec, w_ada, b_ada):
    n_col = 6 * D // D
    return pl.pallas_call(
        _ada_body,
        grid=(DEPTH, n_col),
        in_specs=[
            _full((8, D)),
            pl.BlockSpec((1, D, D), lambda l, j: (l, 0, j)),
            pl.BlockSpec((1, 1, D), lambda l, j: (l, 0, j)),
        ],
        out_specs=pl.BlockSpec((1, 8, D), lambda l, j: (l, 0, j)),
        out_shape=jax.ShapeDtypeStruct((DEPTH, 8, 6 * D), F32),
        compiler_params=_params(2),
        name="ada_mod",
    )(cvec, w_ada, b_ada.reshape(DEPTH, 1, 6 * D))


def _ctx_body(ckv_ref, kr_ref, wkvb_ref, place_ref, k_ref, v_ref):
    kv = _dot(ckv_ref[0, 0].astype(BF), wkvb_ref[0])
    krp = _dot(kr_ref[0, 0].astype(BF), place_ref[...])
    for h in range(N_HEADS):
        sl = slice(h * LANE, (h + 1) * LANE)
        k_ref[0, 0, :, sl] = (kv[:, sl] + krp).astype(BF)
    v_ref[0, 0] = (kv[:, N_HEADS * LANE:] + _value_ones_row()).astype(BF)


def _ctx_call(cache_ckv, cache_krope, wkvb_all, place):
    return pl.pallas_call(
        _ctx_body,
        grid=(DEPTH, DEC_BATCH),
        in_specs=[
            pl.BlockSpec((1, 1, PAST_LEN, KV_LORA), lambda l, b: (b, l, 0, 0)),
            pl.BlockSpec((1, 1, PAST_LEN, QK_ROPE), lambda l, b: (b, l, 0, 0)),
            pl.BlockSpec((1, KV_LORA, 2 * N_HEADS * LANE), lambda l, b: (l, 0, 0)),
            _full((QK_ROPE, LANE)),
        ],
        out_specs=[
            pl.BlockSpec((1, 1, PAST_LEN, N_HEADS * LANE), lambda l, b: (l, b, 0, 0)),
            pl.BlockSpec((1, 1, PAST_LEN, N_HEADS * LANE), lambda l, b: (l, b, 0, 0)),
        ],
        out_shape=[
            jax.ShapeDtypeStruct((DEPTH, DEC_BATCH, PAST_LEN, N_HEADS * LANE), BF),
            jax.ShapeDtypeStruct((DEPTH, DEC_BATCH, PAST_LEN, N_HEADS * LANE), BF),
        ],
        compiler_params=_params(2),
        name="ctx_keys",
    )(cache_ckv, cache_krope, wkvb_all, place)


def _take(refs, n):
    return refs[:n], refs[n:]


def _read_x(x_refs, is_sample):
    if len(x_refs) == 1:
        return x_refs[0][...]
    return jnp.where(is_sample, x_refs[1][...], x_refs[0][...])


def _stage1_body(*refs, n_prompt_tiles, tiles_per_seq, n_x, n_cast):
    x_refs, refs = _take(refs, n_x)
    (mod_ref, g1_ref, gqa_ref, gkva_ref, wt_ref, wqb_ref, wkvb_ref, rope_ref), refs = _take(refs, 8)
    cast_in, refs = _take(refs, n_cast)
    (q_ref, k_ref, v_ref, ckv_ref, kr_ref, u0_ref, gcx_ref, gb_ref, fn_ref, gate_ref), cast_out = _take(refs, 10)

    for src, dst in zip(cast_in, cast_out, strict=True):
        dst[...] = src[0].astype(BF)

    i = pl.program_id(0)
    is_sample = i >= n_prompt_tiles
    row = _mod_row(i, n_prompt_tiles, tiles_per_seq)
    sh1 = mod_ref[0, pl.ds(row, 1), 0:D]
    sc1 = mod_ref[0, pl.ds(row, 1), D:2 * D]
    hb = (_rms(_read_x(x_refs, is_sample), g1_ref[0]) * (1.0 + sc1) + sh1).astype(BF)

    def proj(c0, c1):
        return _dot_nt(hb, wt_ref[0, c0:c1, :])

    pqk = proj(OFF_QA, QK_END)
    q = _dot(_rms(pqk[:, 0:Q_LORA], gqa_ref[0]).astype(BF), wqb_ref[0])

    ckv = _rms(pqk[:, OFF_KVA:OFF_KVA + KV_LORA], gkva_ref[0])
    ckv_ref[...] = ckv
    kr3 = pqk[:, OFF_KVA + KV_LORA:QK_END]
    kr_ref[...] = kr3[:, 0:QK_ROPE]
    lane = lax.broadcasted_iota(jnp.int32, (1, LANE), 1)
    group = lambda k: (lane >= ROPE_LANE0 + 8 * k) & (lane < ROPE_LANE0 + 8 * (k + 1))
    krm = jnp.where(group(0) | group(3), pltpu.roll(kr3, ROPE_LANE0, 1),
                    jnp.where(group(1), pltpu.roll(kr3, ROPE_LANE0 - 8, 1),
                              jnp.where(group(2), pltpu.roll(kr3, ROPE_LANE0 + 8, 1), 0.0)))
    kv = _dot(ckv.astype(BF), wkvb_ref[0])
    v_ref[...] = (kv[:, N_HEADS * LANE:] + _value_ones_row()).astype(BF)

    def rope(t):
        return (t * rope_ref[:, 0:LANE]
                + pltpu.roll(t, LANE - 16, 1) * rope_ref[:, LANE:2 * LANE]
                + pltpu.roll(t, 16, 1) * rope_ref[:, 2 * LANE:3 * LANE])

    krr = rope(krm)
    for h in range(N_HEADS):
        sl = slice(h * LANE, (h + 1) * LANE)
        q_ref[:, sl] = (rope(q[:, sl]) * Q_SCALE).astype(BF)
        k_ref[:, sl] = (kv[:, sl] + krr).astype(BF)

    for j in range(N_BRANCH):
        gate_ref[:, j * D:(j + 1) * D] = _sigmoid(
            proj(OFF_GATE + j * D, OFF_GATE + (j + 1) * D)).astype(BF)

    pc = proj(OFF_CONF, OFF_SC)
    u0_ref[...] = pc[:, 0:W_CONF] * _sigmoid(pc[:, W_CONF:])
    ps = proj(OFF_SC, OFF_FN)
    gb_ref[...] = ps[:, 0:W_SC]
    gcx_ref[...] = ps[:, W_SC:2 * W_SC] * ps[:, 2 * W_SC:]
    fn_ref[...] = proj(OFF_FN, OFF_GATE).astype(BF)


def _split_specs(tm, w):
    n_p = N_PROMPT // tm
    return (pl.BlockSpec((tm, w), lambda i: (jnp.minimum(i, n_p - 1), 0)),
            pl.BlockSpec((tm, w), lambda i: (jnp.maximum(i - n_p, 0), 0)))


def _x_specs(x, tm):
    return list(_split_specs(tm, D)) if len(x) == 2 else [pl.BlockSpec((tm, D), lambda i: (i, 0))]


def _stage1_call(l, x, mod_l, g1, gqa, gkva, w_in_t, wqb, wkvb, rope_tab, to_cast, tm):
    n_prompt_tiles = N_PROMPT // tm
    tiles_per_seq = DEC_SEQ // tm
    row_blk = lambda w: pl.BlockSpec((tm, w), lambda i: (i, 0))
    n_steps = N_TOK // tm
    body = functools.partial(_stage1_body, n_prompt_tiles=n_prompt_tiles, tiles_per_seq=tiles_per_seq,
                             n_x=len(x), n_cast=len(to_cast))
    slab = lambda w: w.shape[1] // n_steps
    assert all(w.shape[1] % (BF16_ROWS * n_steps) == 0 for w in to_cast)
    outs = pl.pallas_call(
        body,
        grid=(n_steps,),
        in_specs=[
            *_x_specs(x, tm),
            _layer(l, (8, 6 * D)),
            _layer(l, (1, D)),
            _layer(l, (1, Q_LORA)),
            _layer(l, (1, KV_LORA)),
            _layer(l, (IN_COLS, D)),
            _layer(l, (Q_LORA, N_HEADS * LANE)),
            _layer(l, (KV_LORA, 2 * N_HEADS * LANE)),
            pl.BlockSpec((tm, 3 * LANE),
                         lambda i: (jnp.where(i < n_prompt_tiles, 0,
                                              1 + (i - n_prompt_tiles) % tiles_per_seq), 0)),
            *[pl.BlockSpec((1, slab(w), w.shape[2]), lambda i: (l, i, 0)) for w in to_cast],
        ],
        out_specs=[
            row_blk(N_HEADS * LANE), row_blk(N_HEADS * LANE), row_blk(N_HEADS * LANE),
            row_blk(KV_LORA), row_blk(QK_ROPE), row_blk(W_CONF), row_blk(W_SC), row_blk(W_SC),
            row_blk(W_FN), row_blk(N_BRANCH * D),
            *[pl.BlockSpec((slab(w), w.shape[2]), lambda i: (i, 0)) for w in to_cast],
        ],
        out_shape=[
            jax.ShapeDtypeStruct((N_TOK, N_HEADS * LANE), BF),
            jax.ShapeDtypeStruct((N_TOK, N_HEADS * LANE), BF),
            jax.ShapeDtypeStruct((N_TOK, N_HEADS * LANE), BF),
            jax.ShapeDtypeStruct((N_TOK, KV_LORA), F32),
            jax.ShapeDtypeStruct((N_TOK, QK_ROPE), F32),
            jax.ShapeDtypeStruct((N_TOK, W_CONF), F32),
            jax.ShapeDtypeStruct((N_TOK, W_SC), F32),
            jax.ShapeDtypeStruct((N_TOK, W_SC), F32),
            jax.ShapeDtypeStruct((N_TOK, W_FN), BF),
            jax.ShapeDtypeStruct((N_TOK, N_BRANCH * D), BF),
            *[jax.ShapeDtypeStruct(w.shape[1:], BF) for w in to_cast],
        ],
        compiler_params=_params(1),
        name="stage1",
    )(*x, mod_l, g1, gqa, gkva, w_in_t, wqb, wkvb, rope_tab, *to_cast)
    return outs[:10], outs[10:]


def _ones_lane(parity):
    return (1 - parity) * V_HEAD


def _value_ones_row():
    col = lax.broadcasted_iota(jnp.int32, (1, N_HEADS * LANE), 1)
    odd = (col // LANE) % 2
    return (col % LANE == jnp.where(odd == 1, _ones_lane(1), _ones_lane(0))).astype(F32)


def _attend_heads(q_ref, kv_refs, o_ref, ahead):
    lane = lax.broadcasted_iota(jnp.int32, (1, LANE), 1)

    def scores(h):
        sl = slice(h * LANE, (h + 1) * LANE)
        return [_dot_nt(q_ref[:, sl], k_ref[:, sl]) for k_ref, _ in kv_refs]

    pending = [scores(h) for h in range(ahead)]
    for hp in range(N_HEADS // 2):
        vsl = slice(hp * LANE, (hp + 1) * LANE)
        outs = []
        for e in range(2):
            h = 2 * hp + e
            sl = slice(h * LANE, (h + 1) * LANE)
            ss = pending.pop(0)
            if h + ahead < N_HEADS:
                pending.append(scores(h + ahead))
            m = functools.reduce(jnp.maximum, [jnp.max(s, axis=-1, keepdims=True) for s in ss])
            o = None
            for s, (_, v_ref) in zip(ss, kv_refs):
                t = _dot(jnp.exp2(s - m).astype(BF), v_ref[:, sl])
                o = t if o is None else o + t
            ones_lane = _ones_lane(e)
            outs.append(o * (1.0 / o[:, ones_lane:ones_lane + 1]))
        o_ref[:, vsl] = jnp.where(lane < V_HEAD, outs[0], outs[1]).astype(BF)


def _attn_prompt_body(q_ref, k_ref, v_ref, o_ref, *, per_step):
    for b in range(per_step):
        rows = pl.ds(b * SEQ, SEQ)
        _attend_heads(q_ref.at[rows], [(k_ref.at[rows], v_ref.at[rows])], o_ref.at[rows], ahead=N_HEADS)


def _attn_prompt_call(q, k, v):
    per_step = 2
    blk = lambda w: pl.BlockSpec((per_step * SEQ, w), lambda b: (b, 0))
    return pl.pallas_call(
        functools.partial(_attn_prompt_body, per_step=per_step),
        grid=(BATCH // per_step,),
        in_specs=[blk(N_HEADS * LANE), blk(N_HEADS * LANE), blk(N_HEADS * LANE)],
        out_specs=blk(N_HEADS * V_HEAD),
        out_shape=jax.ShapeDtypeStruct((N_PROMPT, N_HEADS * V_HEAD), BF),
        compiler_params=_params(1),
        name="attn_prompt",
    )(q, k, v)


def _attn_sample_body(q_ref, kc_ref, vc_ref, k_ref, v_ref, o_ref):
    _attend_heads(q_ref, [(kc_ref.at[0, 0], vc_ref.at[0, 0]), (k_ref, v_ref)], o_ref, ahead=2)


def _attn_sample_call(l, q, k, v, kc, vc, tq):
    n_q = DEC_SEQ // tq
    q0 = N_PROMPT // tq
    s0 = N_PROMPT // DEC_SEQ
    return pl.pallas_call(
        _attn_sample_body,
        grid=(DEC_BATCH, n_q),
        in_specs=[
            pl.BlockSpec((tq, N_HEADS * LANE), lambda b, j: (q0 + b * n_q + j, 0)),
            pl.BlockSpec((1, 1, PAST_LEN, N_HEADS * LANE), lambda b, j: (l, b, 0, 0)),
            pl.BlockSpec((1, 1, PAST_LEN, N_HEADS * LANE), lambda b, j: (l, b, 0, 0)),
            pl.BlockSpec((DEC_SEQ, N_HEADS * LANE), lambda b, j: (s0 + b, 0)),
            pl.BlockSpec((DEC_SEQ, N_HEADS * LANE), lambda b, j: (s0 + b, 0)),
        ],
        out_specs=pl.BlockSpec((tq, N_HEADS * V_HEAD), lambda b, j: (b * n_q + j, 0)),
        out_shape=jax.ShapeDtypeStruct((N_SAMPLE, N_HEADS * V_HEAD), BF),
        compiler_params=_params(2),
        name="attn_sample",
    )(q, kc, vc, k, v)


CONV_ROWS = 256


def _conv_pass(ubuf, cbuf, gb_ref, wdw_ref, bdw_ref, gln_ref, bln_ref, wsc_ref, uo_ref, so_ref,
               shifted, cshifted):
    tl = CONV_ROWS
    span = shifted.shape[1]
    p0 = HALO - CONF_K // 2
    q0 = HALO - SC_K // 2
    for b in range(1, SUBLANES):
        shifted[b] = ubuf[pl.ds(b, span), :]
    for t in range(SC_K):
        if (q0 + t) % SUBLANES:
            cshifted[t] = cbuf[pl.ds((q0 + t) % SUBLANES, span), :]

    def staged(p, r, n):
        a, b = divmod(p, SUBLANES)
        if b == 0:
            return ubuf[pl.ds(r + SUBLANES * a, n), :]
        return shifted[b, pl.ds(r + SUBLANES * a, n), :]

    def staged_c(t, r, n):
        a, b = divmod(q0 + t, SUBLANES)
        if b == 0:
            return cbuf[pl.ds(r + SUBLANES * a, n), :]
        return cshifted[t, pl.ds(r + SUBLANES * a, n), :]

    rc = 32
    for r in range(0, tl, rc):
        acc = staged(p0, r, rc) * wdw_ref[0, 0:1, :]
        for t in range(1, CONF_K):
            acc = acc + staged(p0 + t, r, rc) * wdw_ref[0, t:t + 1, :]
        acc = acc + bdw_ref[0]
        mu = jnp.mean(acc, axis=-1, keepdims=True)
        cen = acc - mu
        var = jnp.mean(cen * cen, axis=-1, keepdims=True)
        y = cen * lax.rsqrt(var + EPS) * gln_ref[0] + bln_ref[0]
        uo_ref[pl.ds(r, rc), :] = (y * _sigmoid(y)).astype(BF)

        cv = staged_c(0, r, rc) * wsc_ref[0, 0:1, :]
        for t in range(1, SC_K):
            cv = cv + staged_c(t, r, rc) * wsc_ref[0, t:t + 1, :]
        so_ref[pl.ds(r, rc), :] = (gb_ref[pl.ds(r, rc), :] * cv).astype(BF)


def _conv_tile(tm, u_ref, c_ref, gb_ref, halos, is_latent, has_prev, has_next, conv_w, uo_ref, so_ref, scratch):
    ubuf, cbuf, shifted, cshifted = scratch
    n_pass = tm // CONV_ROWS
    zeros = jnp.zeros((HALO, W_CONF), F32)
    for p in range(n_pass):
        r0 = p * CONV_ROWS
        for src, buf, k in ((u_ref, ubuf, 0), (c_ref, cbuf, 2)):
            if p > 0:
                head = jnp.where(is_latent, src[r0 - HALO:r0, :], 0.0)
            else:
                head = zeros if halos is None else jnp.where(has_prev, halos[k][...], 0.0)
            if p < n_pass - 1:
                tail = jnp.where(is_latent, src[r0 + CONV_ROWS:r0 + CONV_ROWS + HALO, :], 0.0)
            else:
                tail = zeros if halos is None else jnp.where(has_next, halos[k + 1][...], 0.0)
            buf[0:HALO, :] = head
            buf[HALO:HALO + CONV_ROWS, :] = src[r0:r0 + CONV_ROWS, :]
            buf[HALO + CONV_ROWS:2 * HALO + CONV_ROWS, :] = tail
        rows = pl.ds(r0, CONV_ROWS)
        _conv_pass(ubuf, cbuf, gb_ref.at[rows], *conv_w, uo_ref.at[rows], so_ref.at[rows], shifted, cshifted)


def _conv_scratch():
    staged = CONV_ROWS + 2 * HALO
    return [pltpu.VMEM((staged, W_CONF), F32), pltpu.VMEM((staged, W_SC), F32),
            pltpu.VMEM((SUBLANES, staged - SUBLANES, W_CONF), F32),
            pltpu.VMEM((SC_K, staged - SUBLANES, W_SC), F32)]


def _conv_body(u_ref, c_ref, gb_ref, up_ref, un_ref, cp_ref, cn_ref, *rest, tl, n_prompt_tiles, tiles_per_seq):
    conv_w, (uo_ref, so_ref, *scratch) = rest[:5], rest[5:]
    i = pl.program_id(0)
    is_latent = i >= n_prompt_tiles
    j = jnp.maximum(i - n_prompt_tiles, 0) % tiles_per_seq
    _conv_tile(tl, u_ref, c_ref, gb_ref, (up_ref, un_ref, cp_ref, cn_ref), is_latent,
               is_latent & (j > 0), is_latent & (j < tiles_per_seq - 1), conv_w, uo_ref, so_ref, scratch)


def _conv_call(l, u0, gcx, gb, conv_w, tl):
    n_halo = N_TOK // HALO
    per_tile = tl // HALO
    cur = pl.BlockSpec((tl, W_CONF), lambda i: (i, 0))
    prv = pl.BlockSpec((HALO, W_CONF), lambda i: (jnp.maximum(i * per_tile - 1, 0), 0))
    nxt = pl.BlockSpec((HALO, W_CONF), lambda i: (jnp.minimum((i + 1) * per_tile, n_halo - 1), 0))
    body = functools.partial(_conv_body, tl=tl, n_prompt_tiles=N_PROMPT // tl, tiles_per_seq=DEC_SEQ // tl)
    return pl.pallas_call(
        body,
        grid=(N_TOK // tl,),
        in_specs=[cur, cur, cur, prv, nxt, prv, nxt,
                  _layer(l, (CONF_K, W_CONF)), _layer(l, (1, W_CONF)), _layer(l, (1, W_CONF)),
                  _layer(l, (1, W_CONF)), _layer(l, (SC_K, W_SC))],
        out_specs=[cur, cur],
        out_shape=[jax.ShapeDtypeStruct((N_TOK, W_CONF), BF), jax.ShapeDtypeStruct((N_TOK, W_SC), BF)],
        scratch_shapes=_conv_scratch(),
        compiler_params=_params(1),
        name="dwconv",
    )(u0, gcx, gb, u0, u0, gcx, gcx, *conv_w)


def _fourier_body(fn_ref, gd_ref, cs_ref, out_ref, rhs, *, seq, group, scale):
    @pl.when(pl.program_id(1) == 0)
    def _():
        for b in range(group):
            v = _dot(fn_ref[b * seq:(b + 1) * seq, :], gd_ref[...])
            rhs[0:seq, b * W_FN:(b + 1) * W_FN] = v[:, 0:W_FN].astype(BF)
            rhs[seq:2 * seq, b * W_FN:(b + 1) * W_FN] = (-v[:, W_FN:]).astype(BF)

    res = _dot(cs_ref[...], rhs[...]) * scale
    for b in range(group):
        out_ref[b] = res[:, b * W_FN:(b + 1) * W_FN].astype(BF)


def _fourier_call(fn, gd, cs, seq, n_seq, row0, tl, group):
    n_t = seq // tl
    g0 = row0 // (group * seq)
    body = functools.partial(_fourier_body, seq=seq, group=group, scale=float((seq * FN_GROUP_W) ** -0.5))
    out = pl.pallas_call(
        body,
        grid=(n_seq // group, n_t),
        in_specs=[
            pl.BlockSpec((group * seq, W_FN), lambda g, j: (g0 + g, 0)),
            _full((W_FN, 2 * W_FN)),
            pl.BlockSpec((tl, 2 * seq), lambda g, j: (j, 0)),
        ],
        out_specs=pl.BlockSpec((group, tl, W_FN), lambda g, j: (g, j, 0)),
        out_shape=jax.ShapeDtypeStruct((n_seq, seq, W_FN), BF),
        scratch_shapes=[pltpu.VMEM((2 * seq, group * W_FN), BF)],
        compiler_params=_params(2),
        name=f"fourier_{seq}",
    )(fn, gd, cs)
    return out.reshape(n_seq * seq, W_FN)


def _dft_cos_sin(n):
    r = np.arange(n, dtype=np.int64)
    ang = ((r[:, None] * r[None, :]) % n).astype(np.float64) * (2.0 * np.pi / n)
    return np.cos(ang).astype(np.float32), np.sin(ang).astype(np.float32)


def _stage3_body(*refs, n_prompt_tiles, tiles_per_seq, n_x, final, chunk):
    x_refs, refs = _take(refs, n_x)
    (op_ref, os_ref, u_ref, s_ref, fp_ref, fs_ref, gate_ref, mod_ref, g2_ref, gfin_ref), refs = _take(refs, 10)
    (wo_ref, wpw_ref, wsco_ref, wfn_ref, wout_ref, wg_ref, wu_ref, wd_ref), out_refs = _take(refs, 8)

    i = pl.program_id(0)
    is_sample = i >= n_prompt_tiles
    row = _mod_row(i, n_prompt_tiles, tiles_per_seq)
    mod = lambda k: mod_ref[0, pl.ds(row, 1), k * D:(k + 1) * D]
    pick = lambda p_ref, s_ref: jnp.where(is_sample, s_ref[...], p_ref[...])
    merged = gate_ref[:, 0:D].astype(F32) * _dot(pick(op_ref, os_ref), wo_ref[...])
    merged = merged + gate_ref[:, D:2 * D].astype(F32) * _dot(u_ref[...], wpw_ref[...])
    merged = merged + gate_ref[:, 2 * D:3 * D].astype(F32) * _dot(s_ref[...], wsco_ref[...])
    merged = merged + gate_ref[:, 3 * D:4 * D].astype(F32) * _dot(pick(fp_ref, fs_ref), wfn_ref[...])
    x1 = _read_x(x_refs, is_sample) + mod(2) * _dot(merged.astype(BF), wout_ref[...])

    h2 = (_rms(x1, g2_ref[0]) * (1.0 + mod(4)) + mod(3)).astype(BF)
    def gate_up(c0):
        return _dot(h2, wg_ref[:, c0:c0 + chunk]), _dot(h2, wu_ref[:, c0:c0 + chunk])

    acc = None
    ab_next = gate_up(0)
    for c0 in range(0, FF_HIDDEN, chunk):
        a, b = ab_next
        if c0 + chunk < FF_HIDDEN:
            ab_next = gate_up(c0 + chunk)
        t = _dot((a * _sigmoid(a) * b).astype(BF), wd_ref[c0:c0 + chunk, :])
        acc = t if acc is None else acc + t
    x2 = x1 + mod(5) * acc
    if not final:
        out_refs[0][...] = x2
        return
    y = _rms(x2, gfin_ref[...])

    @pl.when(is_sample)
    def _():
        out_refs[1][...] = y

    @pl.when(jnp.logical_not(is_sample))
    def _():
        out_refs[0][...] = y


def _stage3_call(l, x, o_p, o_s, u, s, f_p, f_s, gates, mod_l, g2, gfin, w3, tm, final):
    n_tiles = N_TOK // tm
    row_blk = lambda w: pl.BlockSpec((tm, w), lambda i: (i, 0))
    body = functools.partial(_stage3_body, n_prompt_tiles=N_PROMPT // tm, tiles_per_seq=DEC_SEQ // tm,
                             n_x=len(x), final=final, chunk=256)
    if final:
        out_specs = list(_split_specs(tm, D))
        out_shape = [jax.ShapeDtypeStruct((N_PROMPT, D), F32), jax.ShapeDtypeStruct((N_SAMPLE, D), F32)]
    else:
        out_specs = [row_blk(D)]
        out_shape = [jax.ShapeDtypeStruct((N_TOK, D), F32)]
    return pl.pallas_call(
        body,
        grid=(n_tiles,),
        in_specs=[
            *_x_specs(x, tm), *_split_specs(tm, N_HEADS * V_HEAD), row_blk(W_CONF), row_blk(W_SC),
            *_split_specs(tm, W_FN),
            row_blk(N_BRANCH * D), _layer(l, (8, 6 * D)), _layer(l, (1, D)), _full((1, D)),
            _full((N_HEADS * V_HEAD, D)), _full((W_CONF, D)), _full((W_SC, D)), _full((W_FN, D)),
            _full((D, D)), _full((D, FF_HIDDEN)), _full((D, FF_HIDDEN)), _full((FF_HIDDEN, D)),
        ],
        out_specs=out_specs,
        out_shape=out_shape,
        compiler_params=_params(1),
        name="stage3",
    )(*x, o_p, o_s, u, s, f_p, f_s, gates, mod_l, g2, gfin, *w3)


def _rope_table(tm):
    rows = DEC_SEQ // GRID_W
    row_pos = np.repeat(np.arange(rows, dtype=np.float64), GRID_W)
    col_pos = np.tile(np.arange(GRID_W, dtype=np.float64), rows)
    inv = ROPE_THETA ** (-np.arange(0, AXIS_ROPE, 2, dtype=np.float64) / AXIS_ROPE)
    ang = np.concatenate([row_pos[:, None] * inv, col_pos[:, None] * inv], axis=1)
    half = QK_ROPE // 2
    tab = np.zeros((tm + DEC_SEQ, 3 * LANE), np.float64)
    tab[:, 0:LANE] = 1.0
    tab[tm:, ROPE_LANE0:ROPE_LANE0 + half] = np.cos(ang)
    tab[tm:, ROPE_LANE0 + half:ROPE_LANE0 + QK_ROPE] = np.cos(ang)
    tab[tm:, LANE + ROPE_LANE0:LANE + ROPE_LANE0 + half] = -np.sin(ang)
    tab[tm:, 2 * LANE + ROPE_LANE0 + half:2 * LANE + ROPE_LANE0 + QK_ROPE] = np.sin(ang)
    return jnp.asarray(tab.astype(np.float32))


def _head_layout_body(wqb_ref, wkvb_ref, pq_ref, pkv_ref, qb_ref, kvb_ref):
    qb_ref[0] = _dot(wqb_ref[0].astype(BF), pq_ref[...]).astype(BF)
    kvb_ref[0] = _dot(wkvb_ref[0].astype(BF), pkv_ref[...]).astype(BF)


def _head_layout_call(w_qb, w_kvb):
    dqk = QK_NOPE + QK_ROPE
    pq = np.zeros((N_HEADS * dqk, N_HEADS * LANE), np.float32)
    pkv = np.zeros((N_HEADS * LANE, 2 * N_HEADS * LANE), np.float32)
    for h in range(N_HEADS):
        src = h * dqk + np.concatenate([np.arange(QK_NOPE), QK_NOPE + ROPE_PERM])
        pq[src, h * LANE + np.arange(dqk)] = 1.0
        nope = np.arange(QK_NOPE)
        pkv[h * LANE + nope, h * LANE + nope] = 1.0
        val = np.arange(V_HEAD)
        pkv[h * LANE + QK_NOPE + val, (N_HEADS + h) * LANE + (h % 2) * V_HEAD + val] = 1.0
    return pl.pallas_call(
        _head_layout_body,
        grid=(DEPTH,),
        in_specs=[pl.BlockSpec((1, Q_LORA, N_HEADS * dqk), lambda l: (l, 0, 0)),
                  pl.BlockSpec((1, KV_LORA, N_HEADS * LANE), lambda l: (l, 0, 0)),
                  _full(pq.shape), _full(pkv.shape)],
        out_specs=[pl.BlockSpec((1, Q_LORA, N_HEADS * LANE), lambda l: (l, 0, 0)),
                   pl.BlockSpec((1, KV_LORA, 2 * N_HEADS * LANE), lambda l: (l, 0, 0))],
        out_shape=[jax.ShapeDtypeStruct((DEPTH, Q_LORA, N_HEADS * LANE), BF),
                   jax.ShapeDtypeStruct((DEPTH, KV_LORA, 2 * N_HEADS * LANE), BF)],
        compiler_params=_params(1),
        name="head_layout",
    )(w_qb, w_kvb, jnp.asarray(pq, BF), jnp.asarray(pkv, BF))


def kernel(x_prompt, x_sample, cache_ckv, cache_krope, c, c_ctx, w_ada, b_ada, g_norm1, g_norm2, w_in, g_qa, w_qb, g_kva, w_kvb, w_o_mla, w_conf_dw, b_conf_dw, g_conf_ln, b_conf_ln, w_conf_pw, w_sc_conv, w_sc_out, w_fn, w_out, w_ffn_gate, w_ffn_up, w_ffn_down, g_final):
    tm = 512
    xp = x_prompt.reshape(N_PROMPT, D)
    xs = x_sample.reshape(N_SAMPLE, D)
    cvec = jnp.concatenate([c_ctx[None, :], c, jnp.zeros((8 - 1 - DEC_BATCH, D), F32)], axis=0)
    mod = _ada_call(cvec, w_ada, b_ada)

    rope_tab = _rope_table(tm)
    place = np.zeros((QK_ROPE, LANE), np.float32)
    place[ROPE_PERM, ROPE_LANE0 + np.arange(QK_ROPE)] = 1.0
    wqb, wkvb_all = _head_layout_call(w_qb, w_kvb)
    kc, vc = _ctx_call(cache_ckv, cache_krope, wkvb_all, jnp.asarray(place, BF))

    cg, sg = _dft_cos_sin(FN_GROUP_W)
    eye = np.eye(FN_GROUPS, dtype=np.float32)
    gd = jnp.asarray(np.concatenate([np.kron(eye, cg), np.kron(eye, sg)], axis=1)).astype(BF)
    cs_p = jnp.asarray(np.concatenate(_dft_cos_sin(SEQ), axis=1)).astype(BF)
    cs_s = jnp.asarray(np.concatenate(_dft_cos_sin(DEC_SEQ), axis=1)).astype(BF)

    w_in_t = jnp.swapaxes(w_in, 1, 2).astype(BF)
    w3_f32 = (w_o_mla, w_conf_pw, w_sc_out, w_fn, w_out, w_ffn_gate, w_ffn_up, w_ffn_down)

    rows = lambda a: a[:, None, :]
    conv_w = (w_conf_dw, rows(b_conf_dw), rows(g_conf_ln), rows(b_conf_ln), w_sc_conv)
    x = (xp, xs)
    new_ckv, new_krope = [], []
    for l in range(DEPTH):
        (q, k, v, ckv, kr, u0, gcx, gb, fn, gates), w3 = _stage1_call(
            l, x, mod, rows(g_norm1), rows(g_qa), rows(g_kva), w_in_t, wqb, wkvb_all, rope_tab,
            w3_f32, tm)
        new_ckv.append(ckv[:N_PROMPT].reshape(BATCH, SEQ, KV_LORA))
        new_krope.append(kr[:N_PROMPT].reshape(BATCH, SEQ, QK_ROPE))

        o_p = _attn_prompt_call(q, k, v)
        o_s = _attn_sample_call(l, q, k, v, kc, vc, 512)
        u, s = _conv_call(l, u0, gcx, gb, conv_w, CONV_ROWS)
        f_p = _fourier_call(fn, gd, cs_p, SEQ, BATCH, 0, SEQ, 4)
        f_s = _fourier_call(fn, gd, cs_s, DEC_SEQ, DEC_BATCH, N_PROMPT, 512, DEC_BATCH)
        x = _stage3_call(l, x, o_p, o_s, u, s, f_p, f_s, gates, mod, rows(g_norm2), g_final[None, :], w3,
                         tm, l == DEPTH - 1)

    y_prompt, y_sample = x
    return (y_prompt.reshape(BATCH, SEQ, D), y_sample.reshape(DEC_BATCH, DEC_SEQ, D),
            jnp.stack(new_ckv, axis=1), jnp.stack(new_krope, axis=1))
```

```python
import functools

import numpy as np
import jax
import jax.numpy as jnp
from jax import lax
from jax.experimental import pallas as pl
from jax.experimental.pallas import tpu as pltpu

BF = jnp.bfloat16
F32 = jnp.float32

D = 1024
BATCH = 16
SEQ = 256
DEPTH = 2
DEC_BATCH = 2
DEC_SEQ = 2048
PAST_LEN = 256
GRID_W = 64
N_HEADS = 8
QK_NOPE = 64
QK_ROPE = 32
V_HEAD = 64
Q_LORA = 384
KV_LORA = 256
AXIS_ROPE = QK_ROPE // 2
ROPE_THETA = 10000.0
W_CONF = D // 4
CONF_K = 31
W_SC = D // 4
SC_K = 3
W_FN = D // 4
FN_GROUPS = 4
FN_GROUP_W = W_FN // FN_GROUPS
N_BRANCH = 4
FF_HIDDEN = ((8 * D // 3 + 255) // 256) * 256
EPS = 1e-6
OFF_QA = 0
OFF_KVA = OFF_QA + Q_LORA
OFF_CONF = OFF_KVA + KV_LORA + QK_ROPE
OFF_SC = OFF_CONF + 2 * W_CONF
OFF_FN = OFF_SC + 3 * W_SC
OFF_GATE = OFF_FN + W_FN
IN_COLS = OFF_GATE + N_BRANCH * D

N_PROMPT = BATCH * SEQ
N_SAMPLE = DEC_BATCH * DEC_SEQ
N_TOK = N_PROMPT + N_SAMPLE
LANE = 128
SUBLANES = 8
BF16_ROWS = 2 * SUBLANES
HALO = 16
SM_SCALE = float((QK_NOPE + QK_ROPE) ** -0.5)
Q_SCALE = SM_SCALE * float(np.log2(np.e))
VMEM_LIMIT = 56 * 1024 * 1024

QK_END = OFF_KVA + KV_LORA + LANE

ROPE_PERM = np.array(list(range(0, 8)) + list(range(16, 24)) + list(range(8, 16)) + list(range(24, 32)))
ROPE_LANE0 = QK_NOPE


def _dot(a, b):
    return jnp.dot(a, b, preferred_element_type=F32)


def _dot_nt(a, b):
    return lax.dot_general(a, b, (((1,), (1,)), ((), ())), preferred_element_type=F32)


def _sigmoid(x):
    return jax.nn.sigmoid(x)


def _rms(x, g):
    return x * lax.rsqrt(jnp.mean(x * x, axis=-1, keepdims=True) + EPS) * g


def _full(shape):
    return pl.BlockSpec(shape, lambda *_: (0,) * len(shape))


def _layer(l, shape):
    return pl.BlockSpec((1, *shape), lambda *_: (l,) + (0,) * len(shape), pipeline_mode=pl.Buffered(1))


def _params(n_axes):
    return pltpu.CompilerParams(
        dimension_semantics=("arbitrary",) * n_axes, vmem_limit_bytes=VMEM_LIMIT)


def _mod_row(i, n_prompt_tiles, tiles_per_seq):
    return jnp.where(i >= n_prompt_tiles, 1 + (i - n_prompt_tiles) // tiles_per_seq, 0)


def _ada_body(c_ref, w_ref, b_ref, out_ref):
    cv = c_ref[...]
    sc = (cv * _sigmoid(cv)).astype(BF)
    out_ref[0] = _dot(sc, w_ref[0].astype(BF)) + b_ref[0]


def _ada_call(cvec, w_ada, b_ada):
    n_col = 6 * D // D
    return pl.pallas_call(
        _ada_body,
        grid=(DEPTH, n_col),
        in_specs=[
            _full((8, D)),
            pl.BlockSpec((1, D, D), lambda l, j: (l, 0, j)),
            pl.BlockSpec((1, 1, D), lambda l, j: (l, 0, j)),
        ],
        out_specs=pl.BlockSpec((1, 8, D), lambda l, j: (l, 0, j)),
        out_shape=jax.ShapeDtypeStruct((DEPTH, 8, 6 * D), F32),
        compiler_params=_params(2),
        name="ada_mod",
    )(cvec, w_ada, b_ada.reshape(DEPTH, 1, 6 * D))


def _ctx_body(ckv_ref, kr_ref, wkvb_ref, place_ref, k_ref, v_ref):
    kv = _dot(ckv_ref[0, 0].astype(BF), wkvb_ref[0])
    krp = _dot(kr_ref[0, 0].astype(BF), place_ref[...])
    for h in range(N_HEADS):
        sl = slice(h * LANE, (h + 1) * LANE)
        k_ref[0, 0, :, sl] = (kv[:, sl] + krp).astype(BF)
    v_ref[0, 0] = (kv[:, N_HEADS * LANE:] + _value_ones_row()).astype(BF)


def _ctx_call(cache_ckv, cache_krope, wkvb_all, place):
    return pl.pallas_call(
        _ctx_body,
        grid=(DEPTH, DEC_BATCH),
        in_specs=[
            pl.BlockSpec((1, 1, PAST_LEN, KV_LORA), lambda l, b: (b, l, 0, 0)),
            pl.BlockSpec((1, 1, PAST_LEN, QK_ROPE), lambda l, b: (b, l, 0, 0)),
            pl.BlockSpec((1, KV_LORA, 2 * N_HEADS * LANE), lambda l, b: (l, 0, 0)),
            _full((QK_ROPE, LANE)),
        ],
        out_specs=[
            pl.BlockSpec((1, 1, PAST_LEN, N_HEADS * LANE), lambda l, b: (l, b, 0, 0)),
            pl.BlockSpec((1, 1, PAST_LEN, N_HEADS * LANE), lambda l, b: (l, b, 0, 0)),
        ],
        out_shape=[
            jax.ShapeDtypeStruct((DEPTH, DEC_BATCH, PAST_LEN, N_HEADS * LANE), BF),
            jax.ShapeDtypeStruct((DEPTH, DEC_BATCH, PAST_LEN, N_HEADS * LANE), BF),
        ],
        compiler_params=_params(2),
        name="ctx_keys",
    )(cache_ckv, cache_krope, wkvb_all, place)


def _take(refs, n):
    return refs[:n], refs[n:]


def _read_x(x_refs, is_sample):
    if len(x_refs) == 1:
        return x_refs[0][...]
    return jnp.where(is_sample, x_refs[1][...], x_refs[0][...])


def _stage1_body(*refs, n_prompt_tiles, tiles_per_seq, n_x, n_cast):
    x_refs, refs = _take(refs, n_x)
    (mod_ref, g1_ref, gqa_ref, gkva_ref, wt_ref, wqb_ref, wkvb_ref, rope_ref), refs = _take(refs, 8)
    cast_in, refs = _take(refs, n_cast)
    (q_ref, k_ref, v_ref, ckv_ref, kr_ref, u0_ref, gcx_ref, gb_ref, fn_ref, gate_ref), cast_out = _take(refs, 10)

    for src, dst in zip(cast_in, cast_out, strict=True):
        dst[...] = src[0].astype(BF)

    i = pl.program_id(0)
    is_sample = i >= n_prompt_tiles
    row = _mod_row(i, n_prompt_tiles, tiles_per_seq)
    sh1 = mod_ref[0, pl.ds(row, 1), 0:D]
    sc1 = mod_ref[0, pl.ds(row, 1), D:2 * D]
    hb = (_rms(_read_x(x_refs, is_sample), g1_ref[0]) * (1.0 + sc1) + sh1).astype(BF)

    def proj(c0, c1):
        return _dot_nt(hb, wt_ref[0, c0:c1, :])

    pqk = proj(OFF_QA, QK_END)
    q = _dot(_rms(pqk[:, 0:Q_LORA], gqa_ref[0]).astype(BF), wqb_ref[0])

    ckv = _rms(pqk[:, OFF_KVA:OFF_KVA + KV_LORA], gkva_ref[0])
    ckv_ref[...] = ckv
    kr3 = pqk[:, OFF_KVA + KV_LORA:QK_END]
    kr_ref[...] = kr3[:, 0:QK_ROPE]
    lane = lax.broadcasted_iota(jnp.int32, (1, LANE), 1)
    group = lambda k: (lane >= ROPE_LANE0 + 8 * k) & (lane < ROPE_LANE0 + 8 * (k + 1))
    krm = jnp.where(group(0) | group(3), pltpu.roll(kr3, ROPE_LANE0, 1),
                    jnp.where(group(1), pltpu.roll(kr3, ROPE_LANE0 - 8, 1),
                              jnp.where(group(2), pltpu.roll(kr3, ROPE_LANE0 + 8, 1), 0.0)))
    kv = _dot(ckv.astype(BF), wkvb_ref[0])
    v_ref[...] = (kv[:, N_HEADS * LANE:] + _value_ones_row()).astype(BF)

    def rope(t):
        return (t * rope_ref[:, 0:LANE]
                + pltpu.roll(t, LANE - 16, 1) * rope_ref[:, LANE:2 * LANE]
                + pltpu.roll(t, 16, 1) * rope_ref[:, 2 * LANE:3 * LANE])

    krr = rope(krm)
    for h in range(N_HEADS):
        sl = slice(h * LANE, (h + 1) * LANE)
        q_ref[:, sl] = (rope(q[:, sl]) * Q_SCALE).astype(BF)
        k_ref[:, sl] = (kv[:, sl] + krr).astype(BF)

    for j in range(N_BRANCH):
        gate_ref[:, j * D:(j + 1) * D] = _sigmoid(
            proj(OFF_GATE + j * D, OFF_GATE + (j + 1) * D)).astype(BF)

    pc = proj(OFF_CONF, OFF_SC)
    u0_ref[...] = pc[:, 0:W_CONF] * _sigmoid(pc[:, W_CONF:])
    ps = proj(OFF_SC, OFF_FN)
    gb_ref[...] = ps[:, 0:W_SC]
    gcx_ref[...] = ps[:, W_SC:2 * W_SC] * ps[:, 2 * W_SC:]
    fn_ref[...] = proj(OFF_FN, OFF_GATE).astype(BF)


def _split_specs(tm, w):
    n_p = N_PROMPT // tm
    return (pl.BlockSpec((tm, w), lambda i: (jnp.minimum(i, n_p - 1), 0)),
            pl.BlockSpec((tm, w), lambda i: (jnp.maximum(i - n_p, 0), 0)))


def _x_specs(x, tm):
    return list(_split_specs(tm, D)) if len(x) == 2 else [pl.BlockSpec((tm, D), lambda i: (i, 0))]


def _stage1_call(l, x, mod_l, g1, gqa, gkva, w_in_t, wqb, wkvb, rope_tab, to_cast, tm):
    n_prompt_tiles = N_PROMPT // tm
    tiles_per_seq = DEC_SEQ // tm
    row_blk = lambda w: pl.BlockSpec((tm, w), lambda i: (i, 0))
    n_steps = N_TOK // tm
    body = functools.partial(_stage1_body, n_prompt_tiles=n_prompt_tiles, tiles_per_seq=tiles_per_seq,
                             n_x=len(x), n_cast=len(to_cast))
    slab = lambda w: w.shape[1] // n_steps
    assert all(w.shape[1] % (BF16_ROWS * n_steps) == 0 for w in to_cast)
    outs = pl.pallas_call(
        body,
        grid=(n_steps,),
        in_specs=[
            *_x_specs(x, tm),
            _layer(l, (8, 6 * D)),
            _layer(l, (1, D)),
            _layer(l, (1, Q_LORA)),
            _layer(l, (1, KV_LORA)),
            _layer(0, (IN_COLS, D)),
            _layer(l, (Q_LORA, N_HEADS * LANE)),
            _layer(l, (KV_LORA, 2 * N_HEADS * LANE)),
            pl.BlockSpec((tm, 3 * LANE),
                         lambda i: (jnp.where(i < n_prompt_tiles, 0,
                                              1 + (i - n_prompt_tiles) % tiles_per_seq), 0)),
            *[pl.BlockSpec((1, slab(w), w.shape[2]), lambda i: (l, i, 0)) for w in to_cast],
        ],
        out_specs=[
            row_blk(N_HEADS * LANE), row_blk(N_HEADS * LANE), row_blk(N_HEADS * LANE),
            row_blk(KV_LORA), row_blk(QK_ROPE), row_blk(W_CONF), row_blk(W_SC), row_blk(W_SC),
            row_blk(W_FN), row_blk(N_BRANCH * D),
            *[pl.BlockSpec((slab(w), w.shape[2]), lambda i: (i, 0)) for w in to_cast],
        ],
        out_shape=[
            jax.ShapeDtypeStruct((N_TOK, N_HEADS * LANE), BF),
            jax.ShapeDtypeStruct((N_TOK, N_HEADS * LANE), BF),
            jax.ShapeDtypeStruct((N_TOK, N_HEADS * LANE), BF),
            jax.ShapeDtypeStruct((N_TOK, KV_LORA), F32),
            jax.ShapeDtypeStruct((N_TOK, QK_ROPE), F32),
            jax.ShapeDtypeStruct((N_TOK, W_CONF), F32),
            jax.ShapeDtypeStruct((N_TOK, W_SC), F32),
            jax.ShapeDtypeStruct((N_TOK, W_SC), F32),
            jax.ShapeDtypeStruct((N_TOK, W_FN), BF),
            jax.ShapeDtypeStruct((N_TOK, N_BRANCH * D), BF),
            *[jax.ShapeDtypeStruct(w.shape[1:], BF) for w in to_cast],
        ],
        compiler_params=_params(1),
        name="stage1",
    )(*x, mod_l, g1, gqa, gkva, w_in_t, wqb, wkvb, rope_tab, *to_cast)
    return outs[:10], outs[10:]


def _ones_lane(parity):
    return (1 - parity) * V_HEAD


def _value_ones_row():
    col = lax.broadcasted_iota(jnp.int32, (1, N_HEADS * LANE), 1)
    odd = (col // LANE) % 2
    return (col % LANE == jnp.where(odd == 1, _ones_lane(1), _ones_lane(0))).astype(F32)


def _attend_heads(q_ref, kv_refs, o_ref, ahead):
    lane = lax.broadcasted_iota(jnp.int32, (1, LANE), 1)

    def scores(h):
        sl = slice(h * LANE, (h + 1) * LANE)
        return [_dot_nt(q_ref[:, sl], k_ref[:, sl]) for k_ref, _ in kv_refs]

    pending = [scores(h) for h in range(ahead)]
    for hp in range(N_HEADS // 2):
        vsl = slice(hp * LANE, (hp + 1) * LANE)
        outs = []
        for e in range(2):
            h = 2 * hp + e
            sl = slice(h * LANE, (h + 1) * LANE)
            ss = pending.pop(0)
            if h + ahead < N_HEADS:
                pending.append(scores(h + ahead))
            m = functools.reduce(jnp.maximum, [jnp.max(s, axis=-1, keepdims=True) for s in ss])
            o = None
            for s, (_, v_ref) in zip(ss, kv_refs):
                t = _dot(jnp.exp2(s - m).astype(BF), v_ref[:, sl])
                o = t if o is None else o + t
            ones_lane = _ones_lane(e)
            outs.append(o * (1.0 / o[:, ones_lane:ones_lane + 1]))
        o_ref[:, vsl] = jnp.where(lane < V_HEAD, outs[0], outs[1]).astype(BF)


def _attn_prompt_body(q_ref, k_ref, v_ref, o_ref, *, per_step):
    for b in range(per_step):
        rows = pl.ds(b * SEQ, SEQ)
        _attend_heads(q_ref.at[rows], [(k_ref.at[rows], v_ref.at[rows])], o_ref.at[rows], ahead=N_HEADS)


def _attn_prompt_call(q, k, v):
    per_step = 4
    blk = lambda w: pl.BlockSpec((per_step * SEQ, w), lambda b: (b, 0))
    return pl.pallas_call(
        functools.partial(_attn_prompt_body, per_step=per_step),
        grid=(BATCH // per_step,),
        in_specs=[blk(N_HEADS * LANE), blk(N_HEADS * LANE), blk(N_HEADS * LANE)],
        out_specs=blk(N_HEADS * V_HEAD),
        out_shape=jax.ShapeDtypeStruct((N_PROMPT, N_HEADS * V_HEAD), BF),
        compiler_params=_params(1),
        name="attn_prompt",
    )(q, k, v)


def _attn_sample_body(q_ref, kc_ref, vc_ref, k_ref, v_ref, o_ref):
    _attend_heads(q_ref, [(kc_ref.at[0, 0], vc_ref.at[0, 0]), (k_ref, v_ref)], o_ref, ahead=2)


def _attn_sample_call(l, q, k, v, kc, vc, tq):
    n_q = DEC_SEQ // tq
    q0 = N_PROMPT // tq
    s0 = N_PROMPT // DEC_SEQ
    return pl.pallas_call(
        _attn_sample_body,
        grid=(DEC_BATCH, n_q),
        in_specs=[
            pl.BlockSpec((tq, N_HEADS * LANE), lambda b, j: (q0 + b * n_q + j, 0)),
            pl.BlockSpec((1, 1, PAST_LEN, N_HEADS * LANE), lambda b, j: (l, b, 0, 0)),
            pl.BlockSpec((1, 1, PAST_LEN, N_HEADS * LANE), lambda b, j: (l, b, 0, 0)),
            pl.BlockSpec((DEC_SEQ, N_HEADS * LANE), lambda b, j: (s0 + b, 0)),
            pl.BlockSpec((DEC_SEQ, N_HEADS * LANE), lambda b, j: (s0 + b, 0)),
        ],
        out_specs=pl.BlockSpec((tq, N_HEADS * V_HEAD), lambda b, j: (b * n_q + j, 0)),
        out_shape=jax.ShapeDtypeStruct((N_SAMPLE, N_HEADS * V_HEAD), BF),
        compiler_params=_params(2),
        name="attn_sample",
    )(q, kc, vc, k, v)


CONV_ROWS = 256


def _conv_pass(ubuf, cbuf, gb_ref, wdw_ref, bdw_ref, gln_ref, bln_ref, wsc_ref, uo_ref, so_ref,
               shifted, cshifted):
    tl = CONV_ROWS
    span = shifted.shape[1]
    p0 = HALO - CONF_K // 2
    q0 = HALO - SC_K // 2
    for b in range(1, SUBLANES):
        shifted[b] = ubuf[pl.ds(b, span), :]
    for t in range(SC_K):
        if (q0 + t) % SUBLANES:
            cshifted[t] = cbuf[pl.ds((q0 + t) % SUBLANES, span), :]

    def staged(p, r, n):
        a, b = divmod(p, SUBLANES)
        if b == 0:
            return ubuf[pl.ds(r + SUBLANES * a, n), :]
        return shifted[b, pl.ds(r + SUBLANES * a, n), :]

    def staged_c(t, r, n):
        a, b = divmod(q0 + t, SUBLANES)
        if b == 0:
            return cbuf[pl.ds(r + SUBLANES * a, n), :]
        return cshifted[t, pl.ds(r + SUBLANES * a, n), :]

    rc = 32
    for r in range(0, tl, rc):
        acc = staged(p0, r, rc) * wdw_ref[0, 0:1, :]
        for t in range(1, CONF_K):
            acc = acc + staged(p0 + t, r, rc) * wdw_ref[0, t:t + 1, :]
        acc = acc + bdw_ref[0]
        mu = jnp.mean(acc, axis=-1, keepdims=True)
        cen = acc - mu
        var = jnp.mean(cen * cen, axis=-1, keepdims=True)
        y = cen * lax.rsqrt(var + EPS) * gln_ref[0] + bln_ref[0]
        uo_ref[pl.ds(r, rc), :] = (y * _sigmoid(y)).astype(BF)

        cv = staged_c(0, r, rc) * wsc_ref[0, 0:1, :]
        for t in range(1, SC_K):
            cv = cv + staged_c(t, r, rc) * wsc_ref[0, t:t + 1, :]
        so_ref[pl.ds(r, rc), :] = (gb_ref[pl.ds(r, rc), :] * cv).astype(BF)


def _conv_tile(tm, u_ref, c_ref, gb_ref, halos, is_latent, has_prev, has_next, conv_w, uo_ref, so_ref, scratch):
    ubuf, cbuf, shifted, cshifted = scratch
    n_pass = tm // CONV_ROWS
    zeros = jnp.zeros((HALO, W_CONF), F32)
    for p in range(n_pass):
        r0 = p * CONV_ROWS
        for src, buf, k in ((u_ref, ubuf, 0), (c_ref, cbuf, 2)):
            if p > 0:
                head = jnp.where(is_latent, src[r0 - HALO:r0, :], 0.0)
            else:
                head = zeros if halos is None else jnp.where(has_prev, halos[k][...], 0.0)
            if p < n_pass - 1:
                tail = jnp.where(is_latent, src[r0 + CONV_ROWS:r0 + CONV_ROWS + HALO, :], 0.0)
            else:
                tail = zeros if halos is None else jnp.where(has_next, halos[k + 1][...], 0.0)
            buf[0:HALO, :] = head
            buf[HALO:HALO + CONV_ROWS, :] = src[r0:r0 + CONV_ROWS, :]
            buf[HALO + CONV_ROWS:2 * HALO + CONV_ROWS, :] = tail
        rows = pl.ds(r0, CONV_ROWS)
        _conv_pass(ubuf, cbuf, gb_ref.at[rows], *conv_w, uo_ref.at[rows], so_ref.at[rows], shifted, cshifted)


def _conv_scratch():
    staged = CONV_ROWS + 2 * HALO
    return [pltpu.VMEM((staged, W_CONF), F32), pltpu.VMEM((staged, W_SC), F32),
            pltpu.VMEM((SUBLANES, staged - SUBLANES, W_CONF), F32),
            pltpu.VMEM((SC_K, staged - SUBLANES, W_SC), F32)]


def _conv_body(u_ref, c_ref, gb_ref, up_ref, un_ref, cp_ref, cn_ref, *rest, tl, n_prompt_tiles, tiles_per_seq):
    conv_w, (uo_ref, so_ref, *scratch) = rest[:5], rest[5:]
    i = pl.program_id(0)
    is_latent = i >= n_prompt_tiles
    j = jnp.maximum(i - n_prompt_tiles, 0) % tiles_per_seq
    _conv_tile(tl, u_ref, c_ref, gb_ref, (up_ref, un_ref, cp_ref, cn_ref), is_latent,
               is_latent & (j > 0), is_latent & (j < tiles_per_seq - 1), conv_w, uo_ref, so_ref, scratch)


def _conv_call(l, u0, gcx, gb, conv_w, tl):
    n_halo = N_TOK // HALO
    per_tile = tl // HALO
    cur = pl.BlockSpec((tl, W_CONF), lambda i: (i, 0))
    prv = pl.BlockSpec((HALO, W_CONF), lambda i: (jnp.maximum(i * per_tile - 1, 0), 0))
    nxt = pl.BlockSpec((HALO, W_CONF), lambda i: (jnp.minimum((i + 1) * per_tile, n_halo - 1), 0))
    body = functools.partial(_conv_body, tl=tl, n_prompt_tiles=N_PROMPT // tl, tiles_per_seq=DEC_SEQ // tl)
    return pl.pallas_call(
        body,
        grid=(N_TOK // tl,),
        in_specs=[cur, cur, cur, prv, nxt, prv, nxt,
                  _layer(l, (CONF_K, W_CONF)), _layer(l, (1, W_CONF)), _layer(l, (1, W_CONF)),
                  _layer(l, (1, W_CONF)), _layer(l, (SC_K, W_SC))],
        out_specs=[cur, cur],
        out_shape=[jax.ShapeDtypeStruct((N_TOK, W_CONF), BF), jax.ShapeDtypeStruct((N_TOK, W_SC), BF)],
        scratch_shapes=_conv_scratch(),
        compiler_params=_params(1),
        name="dwconv",
    )(u0, gcx, gb, u0, u0, gcx, gcx, *conv_w)


def _fourier_body(fn_ref, gd_ref, cs_ref, out_ref, rhs, *, seq, group, scale):
    @pl.when(pl.program_id(1) == 0)
    def _():
        for b in range(group):
            v = _dot(fn_ref[b * seq:(b + 1) * seq, :], gd_ref[...])
            rhs[0:seq, b * W_FN:(b + 1) * W_FN] = v[:, 0:W_FN].astype(BF)
            rhs[seq:2 * seq, b * W_FN:(b + 1) * W_FN] = (-v[:, W_FN:]).astype(BF)

    res = _dot(cs_ref[...], rhs[...]) * scale
    for b in range(group):
        out_ref[b] = res[:, b * W_FN:(b + 1) * W_FN].astype(BF)


def _fourier_call(fn, gd, cs, seq, n_seq, row0, tl, group):
    n_t = seq // tl
    g0 = row0 // (group * seq)
    body = functools.partial(_fourier_body, seq=seq, group=group, scale=float((seq * FN_GROUP_W) ** -0.5))
    out = pl.pallas_call(
        body,
        grid=(n_seq // group, n_t),
        in_specs=[
            pl.BlockSpec((group * seq, W_FN), lambda g, j: (g0 + g, 0)),
            _full((W_FN, 2 * W_FN)),
            pl.BlockSpec((tl, 2 * seq), lambda g, j: (j, 0)),
        ],
        out_specs=pl.BlockSpec((group, tl, W_FN), lambda g, j: (g, j, 0)),
        out_shape=jax.ShapeDtypeStruct((n_seq, seq, W_FN), BF),
        scratch_shapes=[pltpu.VMEM((2 * seq, group * W_FN), BF)],
        compiler_params=_params(2),
        name=f"fourier_{seq}",
    )(fn, gd, cs)
    return out.reshape(n_seq * seq, W_FN)


def _dft_cos_sin(n):
    r = np.arange(n, dtype=np.int64)
    ang = ((r[:, None] * r[None, :]) % n).astype(np.float64) * (2.0 * np.pi / n)
    return np.cos(ang).astype(np.float32), np.sin(ang).astype(np.float32)


def _stage3_body(*refs, n_prompt_tiles, tiles_per_seq, n_x, final, chunk):
    x_refs, refs = _take(refs, n_x)
    (op_ref, os_ref, u_ref, s_ref, fp_ref, fs_ref, gate_ref, mod_ref, g2_ref, gfin_ref), refs = _take(refs, 10)
    (wo_ref, wpw_ref, wsco_ref, wfn_ref, wout_ref, wg_ref, wu_ref, wd_ref), refs = _take(refs, 8)
    if final:
        out_refs = refs
    else:
        (next_w_ref,), (x_out_ref, next_w_out_ref) = _take(refs, 1)
        next_w_out_ref[...] = next_w_ref[...].astype(BF)
        out_refs = (x_out_ref,)

    i = pl.program_id(0)
    is_sample = i >= n_prompt_tiles
    row = _mod_row(i, n_prompt_tiles, tiles_per_seq)
    mod = lambda k: mod_ref[0, pl.ds(row, 1), k * D:(k + 1) * D]
    pick = lambda p_ref, s_ref: jnp.where(is_sample, s_ref[...], p_ref[...])
    merged = gate_ref[:, 0:D].astype(F32) * _dot(pick(op_ref, os_ref), wo_ref[...])
    merged = merged + gate_ref[:, D:2 * D].astype(F32) * _dot(u_ref[...], wpw_ref[...])
    merged = merged + gate_ref[:, 2 * D:3 * D].astype(F32) * _dot(s_ref[...], wsco_ref[...])
    merged = merged + gate_ref[:, 3 * D:4 * D].astype(F32) * _dot(pick(fp_ref, fs_ref), wfn_ref[...])
    x1 = _read_x(x_refs, is_sample) + mod(2) * _dot(merged.astype(BF), wout_ref[...])

    h2 = (_rms(x1, g2_ref[0]) * (1.0 + mod(4)) + mod(3)).astype(BF)
    def gate_up(c0):
        return _dot(h2, wg_ref[:, c0:c0 + chunk]), _dot(h2, wu_ref[:, c0:c0 + chunk])

    acc = None
    ab_next = gate_up(0)
    for c0 in range(0, FF_HIDDEN, chunk):
        a, b = ab_next
        if c0 + chunk < FF_HIDDEN:
            ab_next = gate_up(c0 + chunk)
        t = _dot((a * _sigmoid(a) * b).astype(BF), wd_ref[c0:c0 + chunk, :])
        acc = t if acc is None else acc + t
    x2 = x1 + mod(5) * acc
    if not final:
        out_refs[0][...] = x2
        return
    y = _rms(x2, gfin_ref[...])

    @pl.when(is_sample)
    def _():
        out_refs[1][...] = y

    @pl.when(jnp.logical_not(is_sample))
    def _():
        out_refs[0][...] = y


def _stage3_call(l, x, o_p, o_s, u, s, f_p, f_s, gates, mod_l, g2, gfin, w3, w_in_t_f32, tm, final):
    n_tiles = N_TOK // tm
    row_blk = lambda w: pl.BlockSpec((tm, w), lambda i: (i, 0))
    body = functools.partial(_stage3_body, n_prompt_tiles=N_PROMPT // tm, tiles_per_seq=DEC_SEQ // tm,
                             n_x=len(x), final=final, chunk=256)
    if final:
        extra_in, extra_args = [], []
        out_specs = list(_split_specs(tm, D))
        out_shape = [jax.ShapeDtypeStruct((N_PROMPT, D), F32), jax.ShapeDtypeStruct((N_SAMPLE, D), F32)]
    else:
        slab = -(-IN_COLS // (n_tiles * BF16_ROWS)) * BF16_ROWS
        extra_in = [pl.BlockSpec((1, slab, D), lambda i: (l + 1, i, 0))]
        extra_args = [w_in_t_f32]
        out_specs = [row_blk(D), pl.BlockSpec((1, slab, D), lambda i: (0, i, 0))]
        out_shape = [jax.ShapeDtypeStruct((N_TOK, D), F32), jax.ShapeDtypeStruct((1, IN_COLS, D), BF)]
    return pl.pallas_call(
        body,
        grid=(n_tiles,),
        in_specs=[
            *_x_specs(x, tm), *_split_specs(tm, N_HEADS * V_HEAD), row_blk(W_CONF), row_blk(W_SC),
            *_split_specs(tm, W_FN),
            row_blk(N_BRANCH * D), _layer(l, (8, 6 * D)), _layer(l, (1, D)), _full((1, D)),
            _full((N_HEADS * V_HEAD, D)), _full((W_CONF, D)), _full((W_SC, D)), _full((W_FN, D)),
            _full((D, D)), _full((D, FF_HIDDEN)), _full((D, FF_HIDDEN)), _full((FF_HIDDEN, D)),
            *extra_in,
        ],
        out_specs=out_specs,
        out_shape=out_shape,
        compiler_params=_params(1),
        name="stage3",
    )(*x, o_p, o_s, u, s, f_p, f_s, gates, mod_l, g2, gfin, *w3, *extra_args)


def _rope_table(tm):
    rows = DEC_SEQ // GRID_W
    row_pos = np.repeat(np.arange(rows, dtype=np.float64), GRID_W)
    col_pos = np.tile(np.arange(GRID_W, dtype=np.float64), rows)
    inv = ROPE_THETA ** (-np.arange(0, AXIS_ROPE, 2, dtype=np.float64) / AXIS_ROPE)
    ang = np.concatenate([row_pos[:, None] * inv, col_pos[:, None] * inv], axis=1)
    half = QK_ROPE // 2
    tab = np.zeros((tm + DEC_SEQ, 3 * LANE), np.float64)
    tab[:, 0:LANE] = 1.0
    tab[tm:, ROPE_LANE0:ROPE_LANE0 + half] = np.cos(ang)
    tab[tm:, ROPE_LANE0 + half:ROPE_LANE0 + QK_ROPE] = np.cos(ang)
    tab[tm:, LANE + ROPE_LANE0:LANE + ROPE_LANE0 + half] = -np.sin(ang)
    tab[tm:, 2 * LANE + ROPE_LANE0 + half:2 * LANE + ROPE_LANE0 + QK_ROPE] = np.sin(ang)
    return jnp.asarray(tab.astype(np.float32))


def _head_layout_body(wqb_ref, wkvb_ref, pq_ref, pkv_ref, qb_ref, kvb_ref):
    qb_ref[0] = _dot(wqb_ref[0].astype(BF), pq_ref[...]).astype(BF)
    kvb_ref[0] = _dot(wkvb_ref[0].astype(BF), pkv_ref[...]).astype(BF)


def _head_layout_call(w_qb, w_kvb):
    dqk = QK_NOPE + QK_ROPE
    pq = np.zeros((N_HEADS * dqk, N_HEADS * LANE), np.float32)
    pkv = np.zeros((N_HEADS * LANE, 2 * N_HEADS * LANE), np.float32)
    for h in range(N_HEADS):
        src = h * dqk + np.concatenate([np.arange(QK_NOPE), QK_NOPE + ROPE_PERM])
        pq[src, h * LANE + np.arange(dqk)] = 1.0
        nope = np.arange(QK_NOPE)
        pkv[h * LANE + nope, h * LANE + nope] = 1.0
        val = np.arange(V_HEAD)
        pkv[h * LANE + QK_NOPE + val, (N_HEADS + h) * LANE + (h % 2) * V_HEAD + val] = 1.0
    return pl.pallas_call(
        _head_layout_body,
        grid=(DEPTH,),
        in_specs=[pl.BlockSpec((1, Q_LORA, N_HEADS * dqk), lambda l: (l, 0, 0)),
                  pl.BlockSpec((1, KV_LORA, N_HEADS * LANE), lambda l: (l, 0, 0)),
                  _full(pq.shape), _full(pkv.shape)],
        out_specs=[pl.BlockSpec((1, Q_LORA, N_HEADS * LANE), lambda l: (l, 0, 0)),
                   pl.BlockSpec((1, KV_LORA, 2 * N_HEADS * LANE), lambda l: (l, 0, 0))],
        out_shape=[jax.ShapeDtypeStruct((DEPTH, Q_LORA, N_HEADS * LANE), BF),
                   jax.ShapeDtypeStruct((DEPTH, KV_LORA, 2 * N_HEADS * LANE), BF)],
        compiler_params=_params(1),
        name="head_layout",
    )(w_qb, w_kvb, jnp.asarray(pq, BF), jnp.asarray(pkv, BF))


def kernel(x_prompt, x_sample, cache_ckv, cache_krope, c, c_ctx, w_ada, b_ada, g_norm1, g_norm2, w_in, g_qa, w_qb, g_kva, w_kvb, w_o_mla, w_conf_dw, b_conf_dw, g_conf_ln, b_conf_ln, w_conf_pw, w_sc_conv, w_sc_out, w_fn, w_out, w_ffn_gate, w_ffn_up, w_ffn_down, g_final):
    tm = 512
    xp = x_prompt.reshape(N_PROMPT, D)
    xs = x_sample.reshape(N_SAMPLE, D)
    cvec = jnp.concatenate([c_ctx[None, :], c, jnp.zeros((8 - 1 - DEC_BATCH, D), F32)], axis=0)
    mod = _ada_call(cvec, w_ada, b_ada)

    rope_tab = _rope_table(tm)
    place = np.zeros((QK_ROPE, LANE), np.float32)
    place[ROPE_PERM, ROPE_LANE0 + np.arange(QK_ROPE)] = 1.0
    wqb, wkvb_all = _head_layout_call(w_qb, w_kvb)
    kc, vc = _ctx_call(cache_ckv, cache_krope, wkvb_all, jnp.asarray(place, BF))

    cg, sg = _dft_cos_sin(FN_GROUP_W)
    eye = np.eye(FN_GROUPS, dtype=np.float32)
    gd = jnp.asarray(np.concatenate([np.kron(eye, cg), np.kron(eye, sg)], axis=1)).astype(BF)
    cs_p = jnp.asarray(np.concatenate(_dft_cos_sin(SEQ), axis=1)).astype(BF)
    cs_s = jnp.asarray(np.concatenate(_dft_cos_sin(DEC_SEQ), axis=1)).astype(BF)

    w_in_t_f32 = jnp.swapaxes(w_in, 1, 2)
    w_in_t = w_in_t_f32[0:1].astype(BF)
    w3_f32 = (w_o_mla, w_conf_pw, w_sc_out, w_fn, w_out, w_ffn_gate, w_ffn_up, w_ffn_down)

    rows = lambda a: a[:, None, :]
    conv_w = (w_conf_dw, rows(b_conf_dw), rows(g_conf_ln), rows(b_conf_ln), w_sc_conv)
    x = (xp, xs)
    new_ckv, new_krope = [], []
    for l in range(DEPTH):
        (q, k, v, ckv, kr, u0, gcx, gb, fn, gates), w3 = _stage1_call(
            l, x, mod, rows(g_norm1), rows(g_qa), rows(g_kva), w_in_t, wqb, wkvb_all, rope_tab,
            w3_f32, tm)
        new_ckv.append(ckv[:N_PROMPT].reshape(BATCH, SEQ, KV_LORA))
        new_krope.append(kr[:N_PROMPT].reshape(BATCH, SEQ, QK_ROPE))

        o_p = _attn_prompt_call(q, k, v)
        o_s = _attn_sample_call(l, q, k, v, kc, vc, 1024)
        u, s = _conv_call(l, u0, gcx, gb, conv_w, CONV_ROWS)
        f_p = _fourier_call(fn, gd, cs_p, SEQ, BATCH, 0, SEQ, 4)
        f_s = _fourier_call(fn, gd, cs_s, DEC_SEQ, DEC_BATCH, N_PROMPT, 512, DEC_BATCH)
        out = _stage3_call(l, x, o_p, o_s, u, s, f_p, f_s, gates, mod, rows(g_norm2), g_final[None, :], w3,
                           w_in_t_f32, tm, l == DEPTH - 1)
        if l < DEPTH - 1:
            x, w_in_t = (out[0],), out[1]

    y_prompt, y_sample = out
    return (y_prompt.reshape(BATCH, SEQ, D), y_sample.reshape(DEC_BATCH, DEC_SEQ, D),
            jnp.stack(new_ckv, axis=1), jnp.stack(new_krope, axis=1))
```

```python
import functools

import numpy as np
import jax
import jax.numpy as jnp
from jax import lax
from jax.experimental import pallas as pl
from jax.experimental.pallas import tpu as pltpu

BF = jnp.bfloat16
F32 = jnp.float32

D = 1024
BATCH = 16
SEQ = 256
DEPTH = 2
DEC_BATCH = 2
DEC_SEQ = 2048
PAST_LEN = 256
GRID_W = 64
N_HEADS = 8
QK_NOPE = 64
QK_ROPE = 32
V_HEAD = 64
Q_LORA = 384
KV_LORA = 256
AXIS_ROPE = QK_ROPE // 2
ROPE_THETA = 10000.0
W_CONF = D // 4
CONF_K = 31
W_SC = D // 4
SC_K = 3
W_FN = D // 4
FN_GROUPS = 4
FN_GROUP_W = W_FN // FN_GROUPS
N_BRANCH = 4
FF_HIDDEN = ((8 * D // 3 + 255) // 256) * 256
EPS = 1e-6
OFF_QA = 0
OFF_KVA = OFF_QA + Q_LORA
OFF_CONF = OFF_KVA + KV_LORA + QK_ROPE
OFF_SC = OFF_CONF + 2 * W_CONF
OFF_FN = OFF_SC + 3 * W_SC
OFF_GATE = OFF_FN + W_FN
IN_COLS = OFF_GATE + N_BRANCH * D

N_PROMPT = BATCH * SEQ
N_SAMPLE = DEC_BATCH * DEC_SEQ
N_TOK = N_PROMPT + N_SAMPLE
LANE = 128
SUBLANES = 8
BF16_ROWS = 2 * SUBLANES
HALO = 16
SM_SCALE = float((QK_NOPE + QK_ROPE) ** -0.5)
Q_SCALE = SM_SCALE * float(np.log2(np.e))
VMEM_LIMIT = 56 * 1024 * 1024

QK_END = OFF_KVA + KV_LORA + LANE

ROPE_PERM = np.array(list(range(0, 8)) + list(range(16, 24)) + list(range(8, 16)) + list(range(24, 32)))
ROPE_LANE0 = QK_NOPE


def _dot(a, b):
    return jnp.dot(a, b, preferred_element_type=F32)


def _dot_nt(a, b):
    return lax.dot_general(a, b, (((1,), (1,)), ((), ())), preferred_element_type=F32)


def _sigmoid(x):
    return jax.nn.sigmoid(x)


def _rms(x, g):
    return x * lax.rsqrt(jnp.mean(x * x, axis=-1, keepdims=True) + EPS) * g


def _full(shape):
    return pl.BlockSpec(shape, lambda *_: (0,) * len(shape))


def _layer(l, shape):
    return pl.BlockSpec((1, *shape), lambda *_: (l,) + (0,) * len(shape), pipeline_mode=pl.Buffered(1))


def _params(n_axes):
    return pltpu.CompilerParams(
        dimension_semantics=("arbitrary",) * n_axes, vmem_limit_bytes=VMEM_LIMIT)


def _mod_row(i, n_prompt_tiles, tiles_per_seq):
    return jnp.where(i >= n_prompt_tiles, 1 + (i - n_prompt_tiles) // tiles_per_seq, 0)


def _ada_body(c_ref, w_ref, b_ref, out_ref):
    cv = c_ref[...]
    sc = (cv * _sigmoid(cv)).astype(BF)
    out_ref[0] = _dot(sc, w_ref[0].astype(BF)) + b_ref[0]


def _ada_call(cvec, w_ada, b_ada):
    n_col = 6 * D // D
    return pl.pallas_call(
        _ada_body,
        grid=(DEPTH, n_col),
        in_specs=[
            _full((8, D)),
            pl.BlockSpec((1, D, D), lambda l, j: (l, 0, j)),
            pl.BlockSpec((1, 1, D), lambda l, j: (l, 0, j)),
        ],
        out_specs=pl.BlockSpec((1, 8, D), lambda l, j: (l, 0, j)),
        out_shape=jax.ShapeDtypeStruct((DEPTH, 8, 6 * D), F32),
        compiler_params=_params(2),
        name="ada_mod",
    )(cvec, w_ada, b_ada.reshape(DEPTH, 1, 6 * D))


def _ctx_body(ckv_ref, kr_ref, wkvb_ref, place_ref, k_ref, v_ref):
    kv = _dot(ckv_ref[0, 0].astype(BF), wkvb_ref[0])
    krp = _dot(kr_ref[0, 0].astype(BF), place_ref[...])
    for h in range(N_HEADS):
        sl = slice(h * LANE, (h + 1) * LANE)
        k_ref[0, 0, :, sl] = (kv[:, sl] + krp).astype(BF)
    v_ref[0, 0] = (kv[:, N_HEADS * LANE:] + _value_ones_row()).astype(BF)


def _ctx_call(cache_ckv, cache_krope, wkvb_all, place):
    return pl.pallas_call(
        _ctx_body,
        grid=(DEPTH, DEC_BATCH),
        in_specs=[
            pl.BlockSpec((1, 1, PAST_LEN, KV_LORA), lambda l, b: (b, l, 0, 0)),
            pl.BlockSpec((1, 1, PAST_LEN, QK_ROPE), lambda l, b: (b, l, 0, 0)),
            pl.BlockSpec((1, KV_LORA, 2 * N_HEADS * LANE), lambda l, b: (l, 0, 0)),
            _full((QK_ROPE, LANE)),
        ],
        out_specs=[
            pl.BlockSpec((1, 1, PAST_LEN, N_HEADS * LANE), lambda l, b: (l, b, 0, 0)),
            pl.BlockSpec((1, 1, PAST_LEN, N_HEADS * LANE), lambda l, b: (l, b, 0, 0)),
        ],
        out_shape=[
            jax.ShapeDtypeStruct((DEPTH, DEC_BATCH, PAST_LEN, N_HEADS * LANE), BF),
            jax.ShapeDtypeStruct((DEPTH, DEC_BATCH, PAST_LEN, N_HEADS * LANE), BF),
        ],
        compiler_params=_params(2),
        name="ctx_keys",
    )(cache_ckv, cache_krope, wkvb_all, place)


def _take(refs, n):
    return refs[:n], refs[n:]


def _read_x(x_refs, is_sample):
    if len(x_refs) == 1:
        return x_refs[0][...]
    return jnp.where(is_sample, x_refs[1][...], x_refs[0][...])


def _stage1_body(*refs, n_prompt_tiles, tiles_per_seq, n_x, n_cast):
    x_refs, refs = _take(refs, n_x)
    (mod_ref, g1_ref, gqa_ref, gkva_ref, wt_ref, wqb_ref, wkvb_ref, rope_ref), refs = _take(refs, 8)
    cast_in, refs = _take(refs, n_cast)
    (q_ref, k_ref, v_ref, ckv_ref, kr_ref, u0_ref, gcx_ref, gb_ref, fn_ref, gate_ref), cast_out = _take(refs, 10)

    for src, dst in zip(cast_in, cast_out, strict=True):
        dst[...] = src[0].astype(BF)

    i = pl.program_id(0)
    is_sample = i >= n_prompt_tiles
    row = _mod_row(i, n_prompt_tiles, tiles_per_seq)
    sh1 = mod_ref[0, pl.ds(row, 1), 0:D]
    sc1 = mod_ref[0, pl.ds(row, 1), D:2 * D]
    hb = (_rms(_read_x(x_refs, is_sample), g1_ref[0]) * (1.0 + sc1) + sh1).astype(BF)

    def proj(c0, c1):
        return _dot_nt(hb, wt_ref[0, c0:c1, :])

    pqk = proj(OFF_QA, QK_END)
    q = _dot(_rms(pqk[:, 0:Q_LORA], gqa_ref[0]).astype(BF), wqb_ref[0])

    ckv = _rms(pqk[:, OFF_KVA:OFF_KVA + KV_LORA], gkva_ref[0])
    ckv_ref[...] = ckv
    kr3 = pqk[:, OFF_KVA + KV_LORA:QK_END]
    kr_ref[...] = kr3[:, 0:QK_ROPE]
    lane = lax.broadcasted_iota(jnp.int32, (1, LANE), 1)
    group = lambda k: (lane >= ROPE_LANE0 + 8 * k) & (lane < ROPE_LANE0 + 8 * (k + 1))
    krm = jnp.where(group(0) | group(3), pltpu.roll(kr3, ROPE_LANE0, 1),
                    jnp.where(group(1), pltpu.roll(kr3, ROPE_LANE0 - 8, 1),
                              jnp.where(group(2), pltpu.roll(kr3, ROPE_LANE0 + 8, 1), 0.0)))
    kv = _dot(ckv.astype(BF), wkvb_ref[0])
    v_ref[...] = (kv[:, N_HEADS * LANE:] + _value_ones_row()).astype(BF)

    def rope(t):
        return (t * rope_ref[:, 0:LANE]
                + pltpu.roll(t, LANE - 16, 1) * rope_ref[:, LANE:2 * LANE]
                + pltpu.roll(t, 16, 1) * rope_ref[:, 2 * LANE:3 * LANE])

    krr = rope(krm)
    for h in range(N_HEADS):
        sl = slice(h * LANE, (h + 1) * LANE)
        q_ref[:, sl] = (rope(q[:, sl]) * Q_SCALE).astype(BF)
        k_ref[:, sl] = (kv[:, sl] + krr).astype(BF)

    for j in range(N_BRANCH):
        gate_ref[:, j * D:(j + 1) * D] = _sigmoid(
            proj(OFF_GATE + j * D, OFF_GATE + (j + 1) * D)).astype(BF)

    pc = proj(OFF_CONF, OFF_SC)
    u0_ref[...] = pc[:, 0:W_CONF] * _sigmoid(pc[:, W_CONF:])
    ps = proj(OFF_SC, OFF_FN)
    gb_ref[...] = ps[:, 0:W_SC]
    gcx_ref[...] = ps[:, W_SC:2 * W_SC] * ps[:, 2 * W_SC:]
    fn_ref[...] = proj(OFF_FN, OFF_GATE).astype(BF)


def _split_specs(tm, w):
    n_p = N_PROMPT // tm
    return (pl.BlockSpec((tm, w), lambda i: (jnp.minimum(i, n_p - 1), 0)),
            pl.BlockSpec((tm, w), lambda i: (jnp.maximum(i - n_p, 0), 0)))


def _x_specs(x, tm):
    return list(_split_specs(tm, D)) if len(x) == 2 else [pl.BlockSpec((tm, D), lambda i: (i, 0))]


def _stage1_call(l, x, mod_l, g1, gqa, gkva, w_in_t, wqb, wkvb, rope_tab, to_cast, tm):
    n_prompt_tiles = N_PROMPT // tm
    tiles_per_seq = DEC_SEQ // tm
    row_blk = lambda w: pl.BlockSpec((tm, w), lambda i: (i, 0))
    n_steps = N_TOK // tm
    body = functools.partial(_stage1_body, n_prompt_tiles=n_prompt_tiles, tiles_per_seq=tiles_per_seq,
                             n_x=len(x), n_cast=len(to_cast))
    slab = lambda w: w.shape[1] // n_steps
    assert all(w.shape[1] % (BF16_ROWS * n_steps) == 0 for w in to_cast)
    outs = pl.pallas_call(
        body,
        grid=(n_steps,),
        in_specs=[
            *_x_specs(x, tm),
            _layer(l, (8, 6 * D)),
            _layer(l, (1, D)),
            _layer(l, (1, Q_LORA)),
            _layer(l, (1, KV_LORA)),
            _layer(0, (IN_COLS, D)),
            _layer(l, (Q_LORA, N_HEADS * LANE)),
            _layer(l, (KV_LORA, 2 * N_HEADS * LANE)),
            pl.BlockSpec((tm, 3 * LANE),
                         lambda i: (jnp.where(i < n_prompt_tiles, 0,
                                              1 + (i - n_prompt_tiles) % tiles_per_seq), 0)),
            *[pl.BlockSpec((1, slab(w), w.shape[2]), lambda i: (l, i, 0)) for w in to_cast],
        ],
        out_specs=[
            row_blk(N_HEADS * LANE), row_blk(N_HEADS * LANE), row_blk(N_HEADS * LANE),
            row_blk(KV_LORA), row_blk(QK_ROPE), row_blk(W_CONF), row_blk(W_SC), row_blk(W_SC),
            row_blk(W_FN), row_blk(N_BRANCH * D),
            *[pl.BlockSpec((slab(w), w.shape[2]), lambda i: (i, 0)) for w in to_cast],
        ],
        out_shape=[
            jax.ShapeDtypeStruct((N_TOK, N_HEADS * LANE), BF),
            jax.ShapeDtypeStruct((N_TOK, N_HEADS * LANE), BF),
            jax.ShapeDtypeStruct((N_TOK, N_HEADS * LANE), BF),
            jax.ShapeDtypeStruct((N_TOK, KV_LORA), F32),
            jax.ShapeDtypeStruct((N_TOK, QK_ROPE), F32),
            jax.ShapeDtypeStruct((N_TOK, W_CONF), F32),
            jax.ShapeDtypeStruct((N_TOK, W_SC), F32),
            jax.ShapeDtypeStruct((N_TOK, W_SC), F32),
            jax.ShapeDtypeStruct((N_TOK, W_FN), BF),
            jax.ShapeDtypeStruct((N_TOK, N_BRANCH * D), BF),
            *[jax.ShapeDtypeStruct(w.shape[1:], BF) for w in to_cast],
        ],
        compiler_params=_params(1),
        name="stage1",
    )(*x, mod_l, g1, gqa, gkva, w_in_t, wqb, wkvb, rope_tab, *to_cast)
    return outs[:10], outs[10:]


def _ones_lane(parity):
    return (1 - parity) * V_HEAD


def _value_ones_row():
    col = lax.broadcasted_iota(jnp.int32, (1, N_HEADS * LANE), 1)
    odd = (col // LANE) % 2
    return (col % LANE == jnp.where(odd == 1, _ones_lane(1), _ones_lane(0))).astype(F32)


def _attend_heads(q_ref, kv_refs, o_ref, ahead):
    lane = lax.broadcasted_iota(jnp.int32, (1, LANE), 1)

    def scores(h):
        sl = slice(h * LANE, (h + 1) * LANE)
        return [_dot_nt(q_ref[:, sl], k_ref[:, sl]) for k_ref, _ in kv_refs]

    pending = [scores(h) for h in range(ahead)]
    for hp in range(N_HEADS // 2):
        vsl = slice(hp * LANE, (hp + 1) * LANE)
        outs = []
        for e in range(2):
            h = 2 * hp + e
            sl = slice(h * LANE, (h + 1) * LANE)
            ss = pending.pop(0)
            if h + ahead < N_HEADS:
                pending.append(scores(h + ahead))
            m = functools.reduce(jnp.maximum, [jnp.max(s, axis=-1, keepdims=True) for s in ss])
            o = None
            for s, (_, v_ref) in zip(ss, kv_refs):
                t = _dot(jnp.exp2(s - m).astype(BF), v_ref[:, sl])
                o = t if o is None else o + t
            ones_lane = _ones_lane(e)
            outs.append(o * (1.0 / o[:, ones_lane:ones_lane + 1]))
        o_ref[:, vsl] = jnp.where(lane < V_HEAD, outs[0], outs[1]).astype(BF)


def _attn_prompt_body(q_ref, k_ref, v_ref, o_ref, *, per_step):
    for b in range(per_step):
        rows = pl.ds(b * SEQ, SEQ)
        _attend_heads(q_ref.at[rows], [(k_ref.at[rows], v_ref.at[rows])], o_ref.at[rows], ahead=N_HEADS)


def _attn_prompt_call(q, k, v):
    per_step = 4
    blk = lambda w: pl.BlockSpec((per_step * SEQ, w), lambda b: (b, 0))
    return pl.pallas_call(
        functools.partial(_attn_prompt_body, per_step=per_step),
        grid=(BATCH // per_step,),
        in_specs=[blk(N_HEADS * LANE), blk(N_HEADS * LANE), blk(N_HEADS * LANE)],
        out_specs=blk(N_HEADS * V_HEAD),
        out_shape=jax.ShapeDtypeStruct((N_PROMPT, N_HEADS * V_HEAD), BF),
        compiler_params=_params(1),
        name="attn_prompt",
    )(q, k, v)


def _attn_sample_body(q_ref, kc_ref, vc_ref, k_ref, v_ref, o_ref):
    _attend_heads(q_ref, [(kc_ref.at[0, 0], vc_ref.at[0, 0]), (k_ref, v_ref)], o_ref, ahead=2)


def _attn_sample_call(l, q, k, v, kc, vc, tq):
    n_q = DEC_SEQ // tq
    q0 = N_PROMPT // tq
    s0 = N_PROMPT // DEC_SEQ
    return pl.pallas_call(
        _attn_sample_body,
        grid=(DEC_BATCH, n_q),
        in_specs=[
            pl.BlockSpec((tq, N_HEADS * LANE), lambda b, j: (q0 + b * n_q + j, 0)),
            pl.BlockSpec((1, 1, PAST_LEN, N_HEADS * LANE), lambda b, j: (l, b, 0, 0)),
            pl.BlockSpec((1, 1, PAST_LEN, N_HEADS * LANE), lambda b, j: (l, b, 0, 0)),
            pl.BlockSpec((DEC_SEQ, N_HEADS * LANE), lambda b, j: (s0 + b, 0)),
            pl.BlockSpec((DEC_SEQ, N_HEADS * LANE), lambda b, j: (s0 + b, 0)),
        ],
        out_specs=pl.BlockSpec((tq, N_HEADS * V_HEAD), lambda b, j: (b * n_q + j, 0)),
        out_shape=jax.ShapeDtypeStruct((N_SAMPLE, N_HEADS * V_HEAD), BF),
        compiler_params=_params(2),
        name="attn_sample",
    )(q, kc, vc, k, v)


CONV_ROWS = 256


def _conv_pass(ubuf, cbuf, gb_ref, wdw_ref, bdw_ref, gln_ref, bln_ref, wsc_ref, uo_ref, so_ref,
               shifted, cshifted):
    tl = CONV_ROWS
    span = shifted.shape[1]
    p0 = HALO - CONF_K // 2
    q0 = HALO - SC_K // 2
    for b in range(1, SUBLANES):
        shifted[b] = ubuf[pl.ds(b, span), :]
    for t in range(SC_K):
        if (q0 + t) % SUBLANES:
            cshifted[t] = cbuf[pl.ds((q0 + t) % SUBLANES, span), :]

    def staged(p, r, n):
        a, b = divmod(p, SUBLANES)
        if b == 0:
            return ubuf[pl.ds(r + SUBLANES * a, n), :]
        return shifted[b, pl.ds(r + SUBLANES * a, n), :]

    def staged_c(t, r, n):
        a, b = divmod(q0 + t, SUBLANES)
        if b == 0:
            return cbuf[pl.ds(r + SUBLANES * a, n), :]
        return cshifted[t, pl.ds(r + SUBLANES * a, n), :]

    rc = 32
    for r in range(0, tl, rc):
        acc = staged(p0, r, rc) * wdw_ref[0, 0:1, :]
        for t in range(1, CONF_K):
            acc = acc + staged(p0 + t, r, rc) * wdw_ref[0, t:t + 1, :]
        acc = acc + bdw_ref[0]
        mu = jnp.mean(acc, axis=-1, keepdims=True)
        cen = acc - mu
        var = jnp.mean(cen * cen, axis=-1, keepdims=True)
        y = cen * lax.rsqrt(var + EPS) * gln_ref[0] + bln_ref[0]
        uo_ref[pl.ds(r, rc), :] = (y * _sigmoid(y)).astype(BF)

        cv = staged_c(0, r, rc) * wsc_ref[0, 0:1, :]
        for t in range(1, SC_K):
            cv = cv + staged_c(t, r, rc) * wsc_ref[0, t:t + 1, :]
        so_ref[pl.ds(r, rc), :] = (gb_ref[pl.ds(r, rc), :] * cv).astype(BF)


def _conv_tile(tm, u_ref, c_ref, gb_ref, halos, is_latent, has_prev, has_next, conv_w, uo_ref, so_ref, scratch):
    ubuf, cbuf, shifted, cshifted = scratch
    n_pass = tm // CONV_ROWS
    zeros = jnp.zeros((HALO, W_CONF), F32)
    for p in range(n_pass):
        r0 = p * CONV_ROWS
        for src, buf, k in ((u_ref, ubuf, 0), (c_ref, cbuf, 2)):
            if p > 0:
                head = jnp.where(is_latent, src[r0 - HALO:r0, :], 0.0)
            else:
                head = zeros if halos is None else jnp.where(has_prev, halos[k][...], 0.0)
            if p < n_pass - 1:
                tail = jnp.where(is_latent, src[r0 + CONV_ROWS:r0 + CONV_ROWS + HALO, :], 0.0)
            else:
                tail = zeros if halos is None else jnp.where(has_next, halos[k + 1][...], 0.0)
            buf[0:HALO, :] = head
            buf[HALO:HALO + CONV_ROWS, :] = src[r0:r0 + CONV_ROWS, :]
            buf[HALO + CONV_ROWS:2 * HALO + CONV_ROWS, :] = tail
        rows = pl.ds(r0, CONV_ROWS)
        _conv_pass(ubuf, cbuf, gb_ref.at[rows], *conv_w, uo_ref.at[rows], so_ref.at[rows], shifted, cshifted)


def _conv_scratch():
    staged = CONV_ROWS + 2 * HALO
    return [pltpu.VMEM((staged, W_CONF), F32), pltpu.VMEM((staged, W_SC), F32),
            pltpu.VMEM((SUBLANES, staged - SUBLANES, W_CONF), F32),
            pltpu.VMEM((SC_K, staged - SUBLANES, W_SC), F32)]


def _conv_body(u_ref, c_ref, gb_ref, up_ref, un_ref, cp_ref, cn_ref, *rest, tl, n_prompt_tiles, tiles_per_seq):
    conv_w, (uo_ref, so_ref, *scratch) = rest[:5], rest[5:]
    i = pl.program_id(0)
    is_latent = i >= n_prompt_tiles
    j = jnp.maximum(i - n_prompt_tiles, 0) % tiles_per_seq
    _conv_tile(tl, u_ref, c_ref, gb_ref, (up_ref, un_ref, cp_ref, cn_ref), is_latent,
               is_latent & (j > 0), is_latent & (j < tiles_per_seq - 1), conv_w, uo_ref, so_ref, scratch)


def _conv_call(l, u0, gcx, gb, conv_w, tl):
    n_halo = N_TOK // HALO
    per_tile = tl // HALO
    cur = pl.BlockSpec((tl, W_CONF), lambda i: (i, 0))
    prv = pl.BlockSpec((HALO, W_CONF), lambda i: (jnp.maximum(i * per_tile - 1, 0), 0))
    nxt = pl.BlockSpec((HALO, W_CONF), lambda i: (jnp.minimum((i + 1) * per_tile, n_halo - 1), 0))
    body = functools.partial(_conv_body, tl=tl, n_prompt_tiles=N_PROMPT // tl, tiles_per_seq=DEC_SEQ // tl)
    return pl.pallas_call(
        body,
        grid=(N_TOK // tl,),
        in_specs=[cur, cur, cur, prv, nxt, prv, nxt,
                  _layer(l, (CONF_K, W_CONF)), _layer(l, (1, W_CONF)), _layer(l, (1, W_CONF)),
                  _layer(l, (1, W_CONF)), _layer(l, (SC_K, W_SC))],
        out_specs=[cur, cur],
        out_shape=[jax.ShapeDtypeStruct((N_TOK, W_CONF), BF), jax.ShapeDtypeStruct((N_TOK, W_SC), BF)],
        scratch_shapes=_conv_scratch(),
        compiler_params=_params(1),
        name="dwconv",
    )(u0, gcx, gb, u0, u0, gcx, gcx, *conv_w)


def _fourier_body(fn_ref, gd_ref, cs_ref, *rest, seq, group, scale, conv_rows):
    if conv_rows:
        (u_ref, c_ref, gb_ref), conv_w, (out_ref, uo_ref, so_ref, rhs, *conv_scratch) = (
            rest[:3], rest[3:8], rest[8:])
    else:
        out_ref, rhs = rest

    @pl.when(pl.program_id(1) == 0)
    def _():
        for b in range(group):
            v = _dot(fn_ref[b * seq:(b + 1) * seq, :], gd_ref[...])
            rhs[0:seq, b * W_FN:(b + 1) * W_FN] = v[:, 0:W_FN].astype(BF)
            rhs[seq:2 * seq, b * W_FN:(b + 1) * W_FN] = (-v[:, W_FN:]).astype(BF)

    res = _dot(cs_ref[...], rhs[...]) * scale
    for b in range(group):
        out_ref[b] = res[:, b * W_FN:(b + 1) * W_FN].astype(BF)

    if conv_rows:
        is_latent = pl.program_id(1) * conv_rows >= N_PROMPT
        _conv_tile(conv_rows, u_ref, c_ref, gb_ref, None, is_latent, False, False, conv_w, uo_ref, so_ref,
                   conv_scratch)


def _fourier_call(fn, gd, cs, seq, n_seq, row0, tl, group, conv=None):
    n_t = seq // tl
    g0 = row0 // (group * seq)
    n_steps = (n_seq // group) * n_t
    conv_rows = 0
    extra_in, extra_out, extra_shape, extra_scratch, extra_args = [], [], [], [], []
    if conv is not None:
        l, u0, gcx, gb, conv_w = conv
        conv_rows = N_TOK // n_steps
        assert n_seq == group and conv_rows % DEC_SEQ == 0
        blk = pl.BlockSpec((conv_rows, W_CONF), lambda g, j: (j, 0))
        extra_in = [blk, blk, blk, _layer(l, (CONF_K, W_CONF)), _layer(l, (1, W_CONF)),
                    _layer(l, (1, W_CONF)), _layer(l, (1, W_CONF)), _layer(l, (SC_K, W_SC))]
        extra_out = [blk, blk]
        extra_shape = [jax.ShapeDtypeStruct((N_TOK, W_CONF), BF), jax.ShapeDtypeStruct((N_TOK, W_SC), BF)]
        extra_scratch = _conv_scratch()
        extra_args = [u0, gcx, gb, *conv_w]
    body = functools.partial(_fourier_body, seq=seq, group=group, scale=float((seq * FN_GROUP_W) ** -0.5),
                             conv_rows=conv_rows)
    outs = pl.pallas_call(
        body,
        grid=(n_seq // group, n_t),
        in_specs=[
            pl.BlockSpec((group * seq, W_FN), lambda g, j: (g0 + g, 0)),
            _full((W_FN, 2 * W_FN)),
            pl.BlockSpec((tl, 2 * seq), lambda g, j: (j, 0)),
            *extra_in,
        ],
        out_specs=[pl.BlockSpec((group, tl, W_FN), lambda g, j: (g, j, 0)), *extra_out],
        out_shape=[jax.ShapeDtypeStruct((n_seq, seq, W_FN), BF), *extra_shape],
        scratch_shapes=[pltpu.VMEM((2 * seq, group * W_FN), BF), *extra_scratch],
        compiler_params=_params(2),
        name=f"fourier_{seq}",
    )(fn, gd, cs, *extra_args)
    f = outs[0].reshape(n_seq * seq, W_FN)
    return f if conv is None else (f, outs[1], outs[2])


def _dft_cos_sin(n):
    r = np.arange(n, dtype=np.int64)
    ang = ((r[:, None] * r[None, :]) % n).astype(np.float64) * (2.0 * np.pi / n)
    return np.cos(ang).astype(np.float32), np.sin(ang).astype(np.float32)


def _stage3_body(*refs, n_prompt_tiles, tiles_per_seq, n_x, final, chunk):
    x_refs, refs = _take(refs, n_x)
    (op_ref, os_ref, u_ref, s_ref, fp_ref, fs_ref, gate_ref, mod_ref, g2_ref, gfin_ref), refs = _take(refs, 10)
    (wo_ref, wpw_ref, wsco_ref, wfn_ref, wout_ref, wg_ref, wu_ref, wd_ref), refs = _take(refs, 8)
    if final:
        out_refs = refs
    else:
        (next_w_ref,), (x_out_ref, next_w_out_ref) = _take(refs, 1)
        next_w_out_ref[...] = next_w_ref[...].astype(BF)
        out_refs = (x_out_ref,)

    i = pl.program_id(0)
    is_sample = i >= n_prompt_tiles
    row = _mod_row(i, n_prompt_tiles, tiles_per_seq)
    mod = lambda k: mod_ref[0, pl.ds(row, 1), k * D:(k + 1) * D]
    pick = lambda p_ref, s_ref: jnp.where(is_sample, s_ref[...], p_ref[...])
    merged = gate_ref[:, 0:D].astype(F32) * _dot(pick(op_ref, os_ref), wo_ref[...])
    merged = merged + gate_ref[:, D:2 * D].astype(F32) * _dot(u_ref[...], wpw_ref[...])
    merged = merged + gate_ref[:, 2 * D:3 * D].astype(F32) * _dot(s_ref[...], wsco_ref[...])
    merged = merged + gate_ref[:, 3 * D:4 * D].astype(F32) * _dot(pick(fp_ref, fs_ref), wfn_ref[...])
    x1 = _read_x(x_refs, is_sample) + mod(2) * _dot(merged.astype(BF), wout_ref[...])

    h2 = (_rms(x1, g2_ref[0]) * (1.0 + mod(4)) + mod(3)).astype(BF)
    bounds = list(range(0, FF_HIDDEN, chunk)) + [FF_HIDDEN]
    spans = list(zip(bounds[:-1], bounds[1:]))

    def gate_up(span):
        c0, c1 = span
        return _dot(h2, wg_ref[:, c0:c1]), _dot(h2, wu_ref[:, c0:c1])

    acc = None
    ab_next = gate_up(spans[0])
    for n, (c0, c1) in enumerate(spans):
        a, b = ab_next
        if n + 1 < len(spans):
            ab_next = gate_up(spans[n + 1])
        t = _dot((a * _sigmoid(a) * b).astype(BF), wd_ref[c0:c1, :])
        acc = t if acc is None else acc + t
    x2 = x1 + mod(5) * acc
    if not final:
        out_refs[0][...] = x2
        return
    y = _rms(x2, gfin_ref[...])

    @pl.when(is_sample)
    def _():
        out_refs[1][...] = y

    @pl.when(jnp.logical_not(is_sample))
    def _():
        out_refs[0][...] = y


def _stage3_call(l, x, o_p, o_s, u, s, f_p, f_s, gates, mod_l, g2, gfin, w3, w_in_t_f32, tm, final):
    n_tiles = N_TOK // tm
    row_blk = lambda w: pl.BlockSpec((tm, w), lambda i: (i, 0))
    body = functools.partial(_stage3_body, n_prompt_tiles=N_PROMPT // tm, tiles_per_seq=DEC_SEQ // tm,
                             n_x=len(x), final=final, chunk=256)
    if final:
        extra_in, extra_args = [], []
        out_specs = list(_split_specs(tm, D))
        out_shape = [jax.ShapeDtypeStruct((N_PROMPT, D), F32), jax.ShapeDtypeStruct((N_SAMPLE, D), F32)]
    else:
        slab = -(-IN_COLS // (n_tiles * BF16_ROWS)) * BF16_ROWS
        extra_in = [pl.BlockSpec((1, slab, D), lambda i: (l + 1, i, 0))]
        extra_args = [w_in_t_f32]
        out_specs = [row_blk(D), pl.BlockSpec((1, slab, D), lambda i: (0, i, 0))]
        out_shape = [jax.ShapeDtypeStruct((N_TOK, D), F32), jax.ShapeDtypeStruct((1, IN_COLS, D), BF)]
    return pl.pallas_call(
        body,
        grid=(n_tiles,),
        in_specs=[
            *_x_specs(x, tm), *_split_specs(tm, N_HEADS * V_HEAD), row_blk(W_CONF), row_blk(W_SC),
            *_split_specs(tm, W_FN),
            row_blk(N_BRANCH * D), _layer(l, (8, 6 * D)), _layer(l, (1, D)), _full((1, D)),
            _full((N_HEADS * V_HEAD, D)), _full((W_CONF, D)), _full((W_SC, D)), _full((W_FN, D)),
            _full((D, D)), _full((D, FF_HIDDEN)), _full((D, FF_HIDDEN)), _full((FF_HIDDEN, D)),
            *extra_in,
        ],
        out_specs=out_specs,
        out_shape=out_shape,
        compiler_params=_params(1),
        name="stage3",
    )(*x, o_p, o_s, u, s, f_p, f_s, gates, mod_l, g2, gfin, *w3, *extra_args)


def _rope_table(tm):
    rows = DEC_SEQ // GRID_W
    row_pos = np.repeat(np.arange(rows, dtype=np.float64), GRID_W)
    col_pos = np.tile(np.arange(GRID_W, dtype=np.float64), rows)
    inv = ROPE_THETA ** (-np.arange(0, AXIS_ROPE, 2, dtype=np.float64) / AXIS_ROPE)
    ang = np.concatenate([row_pos[:, None] * inv, col_pos[:, None] * inv], axis=1)
    half = QK_ROPE // 2
    tab = np.zeros((tm + DEC_SEQ, 3 * LANE), np.float64)
    tab[:, 0:LANE] = 1.0
    tab[tm:, ROPE_LANE0:ROPE_LANE0 + half] = np.cos(ang)
    tab[tm:, ROPE_LANE0 + half:ROPE_LANE0 + QK_ROPE] = np.cos(ang)
    tab[tm:, LANE + ROPE_LANE0:LANE + ROPE_LANE0 + half] = -np.sin(ang)
    tab[tm:, 2 * LANE + ROPE_LANE0 + half:2 * LANE + ROPE_LANE0 + QK_ROPE] = np.sin(ang)
    return jnp.asarray(tab.astype(np.float32))


def _head_layout_body(wqb_ref, wkvb_ref, pq_ref, pkv_ref, qb_ref, kvb_ref):
    qb_ref[0] = _dot(wqb_ref[0].astype(BF), pq_ref[...]).astype(BF)
    kvb_ref[0] = _dot(wkvb_ref[0].astype(BF), pkv_ref[...]).astype(BF)


def _head_layout_call(w_qb, w_kvb):
    dqk = QK_NOPE + QK_ROPE
    pq = np.zeros((N_HEADS * dqk, N_HEADS * LANE), np.float32)
    pkv = np.zeros((N_HEADS * LANE, 2 * N_HEADS * LANE), np.float32)
    for h in range(N_HEADS):
        src = h * dqk + np.concatenate([np.arange(QK_NOPE), QK_NOPE + ROPE_PERM])
        pq[src, h * LANE + np.arange(dqk)] = 1.0
        nope = np.arange(QK_NOPE)
        pkv[h * LANE + nope, h * LANE + nope] = 1.0
        val = np.arange(V_HEAD)
        pkv[h * LANE + QK_NOPE + val, (N_HEADS + h) * LANE + (h % 2) * V_HEAD + val] = 1.0
    return pl.pallas_call(
        _head_layout_body,
        grid=(DEPTH,),
        in_specs=[pl.BlockSpec((1, Q_LORA, N_HEADS * dqk), lambda l: (l, 0, 0)),
                  pl.BlockSpec((1, KV_LORA, N_HEADS * LANE), lambda l: (l, 0, 0)),
                  _full(pq.shape), _full(pkv.shape)],
        out_specs=[pl.BlockSpec((1, Q_LORA, N_HEADS * LANE), lambda l: (l, 0, 0)),
                   pl.BlockSpec((1, KV_LORA, 2 * N_HEADS * LANE), lambda l: (l, 0, 0))],
        out_shape=[jax.ShapeDtypeStruct((DEPTH, Q_LORA, N_HEADS * LANE), BF),
                   jax.ShapeDtypeStruct((DEPTH, KV_LORA, 2 * N_HEADS * LANE), BF)],
        compiler_params=_params(1),
        name="head_layout",
    )(w_qb, w_kvb, jnp.asarray(pq, BF), jnp.asarray(pkv, BF))


def kernel(x_prompt, x_sample, cache_ckv, cache_krope, c, c_ctx, w_ada, b_ada, g_norm1, g_norm2, w_in, g_qa, w_qb, g_kva, w_kvb, w_o_mla, w_conf_dw, b_conf_dw, g_conf_ln, b_conf_ln, w_conf_pw, w_sc_conv, w_sc_out, w_fn, w_out, w_ffn_gate, w_ffn_up, w_ffn_down, g_final):
    tm = 512
    xp = x_prompt.reshape(N_PROMPT, D)
    xs = x_sample.reshape(N_SAMPLE, D)
    cvec = jnp.concatenate([c_ctx[None, :], c, jnp.zeros((8 - 1 - DEC_BATCH, D), F32)], axis=0)
    mod = _ada_call(cvec, w_ada, b_ada)

    rope_tab = _rope_table(tm)
    place = np.zeros((QK_ROPE, LANE), np.float32)
    place[ROPE_PERM, ROPE_LANE0 + np.arange(QK_ROPE)] = 1.0
    wqb, wkvb_all = _head_layout_call(w_qb, w_kvb)
    kc, vc = _ctx_call(cache_ckv, cache_krope, wkvb_all, jnp.asarray(place, BF))

    cg, sg = _dft_cos_sin(FN_GROUP_W)
    eye = np.eye(FN_GROUPS, dtype=np.float32)
    gd = jnp.asarray(np.concatenate([np.kron(eye, cg), np.kron(eye, sg)], axis=1)).astype(BF)
    cs_p = jnp.asarray(np.concatenate(_dft_cos_sin(SEQ), axis=1)).astype(BF)
    cs_s = jnp.asarray(np.concatenate(_dft_cos_sin(DEC_SEQ), axis=1)).astype(BF)

    w_in_t_f32 = jnp.swapaxes(w_in, 1, 2)
    w_in_t = w_in_t_f32[0:1].astype(BF)
    w3_f32 = (w_o_mla, w_conf_pw, w_sc_out, w_fn, w_out, w_ffn_gate, w_ffn_up, w_ffn_down)

    rows = lambda a: a[:, None, :]
    conv_w = (w_conf_dw, rows(b_conf_dw), rows(g_conf_ln), rows(b_conf_ln), w_sc_conv)
    x = (xp, xs)
    new_ckv, new_krope = [], []
    for l in range(DEPTH):
        (q, k, v, ckv, kr, u0, gcx, gb, fn, gates), w3 = _stage1_call(
            l, x, mod, rows(g_norm1), rows(g_qa), rows(g_kva), w_in_t, wqb, wkvb_all, rope_tab,
            w3_f32, tm)
        new_ckv.append(ckv[:N_PROMPT].reshape(BATCH, SEQ, KV_LORA))
        new_krope.append(kr[:N_PROMPT].reshape(BATCH, SEQ, QK_ROPE))

        o_p = _attn_prompt_call(q, k, v)
        o_s = _attn_sample_call(l, q, k, v, kc, vc, 1024)
        f_p = _fourier_call(fn, gd, cs_p, SEQ, BATCH, 0, SEQ, 4)
        f_s, u, s = _fourier_call(fn, gd, cs_s, DEC_SEQ, DEC_BATCH, N_PROMPT, 512, DEC_BATCH,
                                  conv=(l, u0, gcx, gb, conv_w))
        out = _stage3_call(l, x, o_p, o_s, u, s, f_p, f_s, gates, mod, rows(g_norm2), g_final[None, :], w3,
                           w_in_t_f32, tm, l == DEPTH - 1)
        if l < DEPTH - 1:
            x, w_in_t = (out[0],), out[1]

    y_prompt, y_sample = out
    return (y_prompt.reshape(BATCH, SEQ, D), y_sample.reshape(DEC_BATCH, DEC_SEQ, D),
            jnp.stack(new_ckv, axis=1), jnp.stack(new_krope, axis=1))
```

```python
import functools

import numpy as np
import jax
import jax.numpy as jnp
from jax import lax
from jax.experimental import pallas as pl
from jax.experimental.pallas import tpu as pltpu

BF = jnp.bfloat16
F32 = jnp.float32

D = 1024
BATCH = 16
SEQ = 256
DEPTH = 2
DEC_BATCH = 2
DEC_SEQ = 2048
PAST_LEN = 256
GRID_W = 64
N_HEADS = 8
QK_NOPE = 64
QK_ROPE = 32
V_HEAD = 64
Q_LORA = 384
KV_LORA = 256
AXIS_ROPE = QK_ROPE // 2
ROPE_THETA = 10000.0
W_CONF = D // 4
CONF_K = 31
W_SC = D // 4
SC_K = 3
W_FN = D // 4
FN_GROUPS = 4
FN_GROUP_W = W_FN // FN_GROUPS
N_BRANCH = 4
FF_HIDDEN = ((8 * D // 3 + 255) // 256) * 256
EPS = 1e-6
OFF_QA = 0
OFF_KVA = OFF_QA + Q_LORA
OFF_CONF = OFF_KVA + KV_LORA + QK_ROPE
OFF_SC = OFF_CONF + 2 * W_CONF
OFF_FN = OFF_SC + 3 * W_SC
OFF_GATE = OFF_FN + W_FN
IN_COLS = OFF_GATE + N_BRANCH * D

N_PROMPT = BATCH * SEQ
N_SAMPLE = DEC_BATCH * DEC_SEQ
N_TOK = N_PROMPT + N_SAMPLE
LANE = 128
SUBLANES = 8
BF16_ROWS = 2 * SUBLANES
HALO = 16
SM_SCALE = float((QK_NOPE + QK_ROPE) ** -0.5)
Q_SCALE = SM_SCALE * float(np.log2(np.e))
VMEM_LIMIT = 56 * 1024 * 1024

QK_END = OFF_KVA + KV_LORA + LANE

ROPE_PERM = np.array(list(range(0, 8)) + list(range(16, 24)) + list(range(8, 16)) + list(range(24, 32)))
ROPE_LANE0 = QK_NOPE


def _dot(a, b):
    return jnp.dot(a, b, preferred_element_type=F32)


def _dot_nt(a, b):
    return lax.dot_general(a, b, (((1,), (1,)), ((), ())), preferred_element_type=F32)


def _sigmoid(x):
    return jax.nn.sigmoid(x)


def _rms(x, g):
    return x * lax.rsqrt(jnp.mean(x * x, axis=-1, keepdims=True) + EPS) * g


def _full(shape):
    return pl.BlockSpec(shape, lambda *_: (0,) * len(shape))


def _layer(l, shape):
    return pl.BlockSpec((1, *shape), lambda *_: (l,) + (0,) * len(shape), pipeline_mode=pl.Buffered(1))


def _params(n_axes):
    return pltpu.CompilerParams(
        dimension_semantics=("arbitrary",) * n_axes, vmem_limit_bytes=VMEM_LIMIT)


def _mod_row(i, n_prompt_tiles, tiles_per_seq):
    return jnp.where(i >= n_prompt_tiles, 1 + (i - n_prompt_tiles) // tiles_per_seq, 0)


def _ada_body(c_ref, w_ref, b_ref, out_ref):
    cv = c_ref[...]
    sc = (cv * _sigmoid(cv)).astype(BF)
    out_ref[0] = _dot(sc, w_ref[0].astype(BF)) + b_ref[0]


def _ada_call(cvec, w_ada, b_ada):
    n_col = 6 * D // D
    return pl.pallas_call(
        _ada_body,
        grid=(DEPTH, n_col),
        in_specs=[
            _full((8, D)),
            pl.BlockSpec((1, D, D), lambda l, j: (l, 0, j)),
            pl.BlockSpec((1, 1, D), lambda l, j: (l, 0, j)),
        ],
        out_specs=pl.BlockSpec((1, 8, D), lambda l, j: (l, 0, j)),
        out_shape=jax.ShapeDtypeStruct((DEPTH, 8, 6 * D), F32),
        compiler_params=_params(2),
        name="ada_mod",
    )(cvec, w_ada, b_ada.reshape(DEPTH, 1, 6 * D))


def _ctx_body(ckv_ref, kr_ref, wkvb_ref, place_ref, k_ref, v_ref):
    kv = _dot(ckv_ref[0, 0].astype(BF), wkvb_ref[0])
    krp = _dot(kr_ref[0, 0].astype(BF), place_ref[...])
    for h in range(N_HEADS):
        sl = slice(h * LANE, (h + 1) * LANE)
        k_ref[0, 0, :, sl] = (kv[:, sl] + krp).astype(BF)
    v_ref[0, 0] = (kv[:, N_HEADS * LANE:] + _value_ones_row()).astype(BF)


def _ctx_call(cache_ckv, cache_krope, wkvb_all, place):
    return pl.pallas_call(
        _ctx_body,
        grid=(DEPTH, DEC_BATCH),
        in_specs=[
            pl.BlockSpec((1, 1, PAST_LEN, KV_LORA), lambda l, b: (b, l, 0, 0)),
            pl.BlockSpec((1, 1, PAST_LEN, QK_ROPE), lambda l, b: (b, l, 0, 0)),
            pl.BlockSpec((1, KV_LORA, 2 * N_HEADS * LANE), lambda l, b: (l, 0, 0)),
            _full((QK_ROPE, LANE)),
        ],
        out_specs=[
            pl.BlockSpec((1, 1, PAST_LEN, N_HEADS * LANE), lambda l, b: (l, b, 0, 0)),
            pl.BlockSpec((1, 1, PAST_LEN, N_HEADS * LANE), lambda l, b: (l, b, 0, 0)),
        ],
        out_shape=[
            jax.ShapeDtypeStruct((DEPTH, DEC_BATCH, PAST_LEN, N_HEADS * LANE), BF),
            jax.ShapeDtypeStruct((DEPTH, DEC_BATCH, PAST_LEN, N_HEADS * LANE), BF),
        ],
        compiler_params=_params(2),
        name="ctx_keys",
    )(cache_ckv, cache_krope, wkvb_all, place)


def _take(refs, n):
    return refs[:n], refs[n:]


def _read_x(x_refs, is_sample):
    if len(x_refs) == 1:
        return x_refs[0][...]
    return jnp.where(is_sample, x_refs[1][...], x_refs[0][...])


def _stage1_body(*refs, n_prompt_tiles, tiles_per_seq, n_x, n_cast):
    x_refs, refs = _take(refs, n_x)
    (mod_ref, g1_ref, gqa_ref, gkva_ref, wt_ref, wqb_ref, wkvb_ref, rope_ref), refs = _take(refs, 8)
    cast_in, refs = _take(refs, n_cast)
    (q_ref, k_ref, v_ref, ckv_ref, kr_ref, u0_ref, gcx_ref, gb_ref, fn_ref, gate_ref), cast_out = _take(refs, 10)

    for src, dst in zip(cast_in, cast_out, strict=True):
        dst[...] = src[0].astype(BF)

    i = pl.program_id(0)
    is_sample = i >= n_prompt_tiles
    row = _mod_row(i, n_prompt_tiles, tiles_per_seq)
    sh1 = mod_ref[0, pl.ds(row, 1), 0:D]
    sc1 = mod_ref[0, pl.ds(row, 1), D:2 * D]
    hb = (_rms(_read_x(x_refs, is_sample), g1_ref[0]) * (1.0 + sc1) + sh1).astype(BF)

    def proj(c0, c1):
        return _dot_nt(hb, wt_ref[0, c0:c1, :])

    pqk = proj(OFF_QA, QK_END)
    q = _dot(_rms(pqk[:, 0:Q_LORA], gqa_ref[0]).astype(BF), wqb_ref[0])

    ckv = _rms(pqk[:, OFF_KVA:OFF_KVA + KV_LORA], gkva_ref[0])
    ckv_ref[...] = ckv
    kr3 = pqk[:, OFF_KVA + KV_LORA:QK_END]
    kr_ref[...] = kr3[:, 0:QK_ROPE]
    lane = lax.broadcasted_iota(jnp.int32, (1, LANE), 1)
    group = lambda k: (lane >= ROPE_LANE0 + 8 * k) & (lane < ROPE_LANE0 + 8 * (k + 1))
    krm = jnp.where(group(0) | group(3), pltpu.roll(kr3, ROPE_LANE0, 1),
                    jnp.where(group(1), pltpu.roll(kr3, ROPE_LANE0 - 8, 1),
                              jnp.where(group(2), pltpu.roll(kr3, ROPE_LANE0 + 8, 1), 0.0)))
    kv = _dot(ckv.astype(BF), wkvb_ref[0])
    v_ref[...] = (kv[:, N_HEADS * LANE:] + _value_ones_row()).astype(BF)

    def rope(t):
        return (t * rope_ref[:, 0:LANE]
                + pltpu.roll(t, LANE - 16, 1) * rope_ref[:, LANE:2 * LANE]
                + pltpu.roll(t, 16, 1) * rope_ref[:, 2 * LANE:3 * LANE])

    krr = rope(krm)
    for h in range(N_HEADS):
        sl = slice(h * LANE, (h + 1) * LANE)
        q_ref[:, sl] = (rope(q[:, sl]) * Q_SCALE).astype(BF)
        k_ref[:, sl] = (kv[:, sl] + krr).astype(BF)

    for j in range(N_BRANCH):
        gate_ref[:, j * D:(j + 1) * D] = _sigmoid(
            proj(OFF_GATE + j * D, OFF_GATE + (j + 1) * D)).astype(BF)

    pc = proj(OFF_CONF, OFF_SC)
    u0_ref[...] = pc[:, 0:W_CONF] * _sigmoid(pc[:, W_CONF:])
    ps = proj(OFF_SC, OFF_FN)
    gb_ref[...] = ps[:, 0:W_SC]
    gcx_ref[...] = ps[:, W_SC:2 * W_SC] * ps[:, 2 * W_SC:]
    fn_ref[...] = proj(OFF_FN, OFF_GATE).astype(BF)


def _split_specs(tm, w):
    n_p = N_PROMPT // tm
    return (pl.BlockSpec((tm, w), lambda i: (jnp.minimum(i, n_p - 1), 0)),
            pl.BlockSpec((tm, w), lambda i: (jnp.maximum(i - n_p, 0), 0)))


def _x_specs(x, tm):
    return list(_split_specs(tm, D)) if len(x) == 2 else [pl.BlockSpec((tm, D), lambda i: (i, 0))]


def _stage1_call(l, x, mod_l, g1, gqa, gkva, w_in_t, wqb, wkvb, rope_tab, to_cast, tm):
    n_prompt_tiles = N_PROMPT // tm
    tiles_per_seq = DEC_SEQ // tm
    row_blk = lambda w: pl.BlockSpec((tm, w), lambda i: (i, 0))
    n_steps = N_TOK // tm
    body = functools.partial(_stage1_body, n_prompt_tiles=n_prompt_tiles, tiles_per_seq=tiles_per_seq,
                             n_x=len(x), n_cast=len(to_cast))
    slab = lambda w: w.shape[1] // n_steps
    assert all(w.shape[1] % (BF16_ROWS * n_steps) == 0 for w in to_cast)
    outs = pl.pallas_call(
        body,
        grid=(n_steps,),
        in_specs=[
            *_x_specs(x, tm),
            _layer(l, (8, 6 * D)),
            _layer(l, (1, D)),
            _layer(l, (1, Q_LORA)),
            _layer(l, (1, KV_LORA)),
            _layer(0, (IN_COLS, D)),
            _layer(l, (Q_LORA, N_HEADS * LANE)),
            _layer(l, (KV_LORA, 2 * N_HEADS * LANE)),
            pl.BlockSpec((tm, 3 * LANE),
                         lambda i: (jnp.where(i < n_prompt_tiles, 0,
                                              1 + (i - n_prompt_tiles) % tiles_per_seq), 0)),
            *[pl.BlockSpec((1, slab(w), w.shape[2]), lambda i: (l, i, 0)) for w in to_cast],
        ],
        out_specs=[
            row_blk(N_HEADS * LANE), row_blk(N_HEADS * LANE), row_blk(N_HEADS * LANE),
            row_blk(KV_LORA), row_blk(QK_ROPE), row_blk(W_CONF), row_blk(W_SC), row_blk(W_SC),
            row_blk(W_FN), row_blk(N_BRANCH * D),
            *[pl.BlockSpec((slab(w), w.shape[2]), lambda i: (i, 0)) for w in to_cast],
        ],
        out_shape=[
            jax.ShapeDtypeStruct((N_TOK, N_HEADS * LANE), BF),
            jax.ShapeDtypeStruct((N_TOK, N_HEADS * LANE), BF),
            jax.ShapeDtypeStruct((N_TOK, N_HEADS * LANE), BF),
            jax.ShapeDtypeStruct((N_TOK, KV_LORA), F32),
            jax.ShapeDtypeStruct((N_TOK, QK_ROPE), F32),
            jax.ShapeDtypeStruct((N_TOK, W_CONF), F32),
            jax.ShapeDtypeStruct((N_TOK, W_SC), F32),
            jax.ShapeDtypeStruct((N_TOK, W_SC), F32),
            jax.ShapeDtypeStruct((N_TOK, W_FN), BF),
            jax.ShapeDtypeStruct((N_TOK, N_BRANCH * D), BF),
            *[jax.ShapeDtypeStruct(w.shape[1:], BF) for w in to_cast],
        ],
        compiler_params=_params(1),
        name="stage1",
    )(*x, mod_l, g1, gqa, gkva, w_in_t, wqb, wkvb, rope_tab, *to_cast)
    return outs[:10], outs[10:]


def _ones_lane(parity):
    return (1 - parity) * V_HEAD


def _value_ones_row():
    col = lax.broadcasted_iota(jnp.int32, (1, N_HEADS * LANE), 1)
    odd = (col // LANE) % 2
    return (col % LANE == jnp.where(odd == 1, _ones_lane(1), _ones_lane(0))).astype(F32)


def _attend_heads(q_ref, kv_refs, o_ref, ahead):
    lane = lax.broadcasted_iota(jnp.int32, (1, LANE), 1)

    def scores(h):
        sl = slice(h * LANE, (h + 1) * LANE)
        return [_dot_nt(q_ref[:, sl], k_ref[:, sl]) for k_ref, _ in kv_refs]

    pending = [scores(h) for h in range(ahead)]
    for hp in range(N_HEADS // 2):
        vsl = slice(hp * LANE, (hp + 1) * LANE)
        outs = []
        for e in range(2):
            h = 2 * hp + e
            sl = slice(h * LANE, (h + 1) * LANE)
            ss = pending.pop(0)
            if h + ahead < N_HEADS:
                pending.append(scores(h + ahead))
            m = functools.reduce(jnp.maximum, [jnp.max(s, axis=-1, keepdims=True) for s in ss])
            o = None
            for s, (_, v_ref) in zip(ss, kv_refs):
                t = _dot(jnp.exp2(s - m).astype(BF), v_ref[:, sl])
                o = t if o is None else o + t
            ones_lane = _ones_lane(e)
            outs.append(o * (1.0 / o[:, ones_lane:ones_lane + 1]))
        o_ref[:, vsl] = jnp.where(lane < V_HEAD, outs[0], outs[1]).astype(BF)


def _attn_prompt_body(q_ref, k_ref, v_ref, o_ref, *, per_step):
    for b in range(per_step):
        rows = pl.ds(b * SEQ, SEQ)
        _attend_heads(q_ref.at[rows], [(k_ref.at[rows], v_ref.at[rows])], o_ref.at[rows], ahead=N_HEADS)


def _attn_prompt_call(q, k, v):
    per_step = 4
    blk = lambda w: pl.BlockSpec((per_step * SEQ, w), lambda b: (b, 0))
    return pl.pallas_call(
        functools.partial(_attn_prompt_body, per_step=per_step),
        grid=(BATCH // per_step,),
        in_specs=[blk(N_HEADS * LANE), blk(N_HEADS * LANE), blk(N_HEADS * LANE)],
        out_specs=blk(N_HEADS * V_HEAD),
        out_shape=jax.ShapeDtypeStruct((N_PROMPT, N_HEADS * V_HEAD), BF),
        compiler_params=_params(1),
        name="attn_prompt",
    )(q, k, v)


def _attn_sample_body(q_ref, kc_ref, vc_ref, k_ref, v_ref, o_ref):
    _attend_heads(q_ref, [(kc_ref.at[0, 0], vc_ref.at[0, 0]), (k_ref, v_ref)], o_ref, ahead=2)


def _attn_sample_call(l, q, k, v, kc, vc, tq):
    n_q = DEC_SEQ // tq
    q0 = N_PROMPT // tq
    s0 = N_PROMPT // DEC_SEQ
    return pl.pallas_call(
        _attn_sample_body,
        grid=(DEC_BATCH, n_q),
        in_specs=[
            pl.BlockSpec((tq, N_HEADS * LANE), lambda b, j: (q0 + b * n_q + j, 0)),
            pl.BlockSpec((1, 1, PAST_LEN, N_HEADS * LANE), lambda b, j: (l, b, 0, 0)),
            pl.BlockSpec((1, 1, PAST_LEN, N_HEADS * LANE), lambda b, j: (l, b, 0, 0)),
            pl.BlockSpec((DEC_SEQ, N_HEADS * LANE), lambda b, j: (s0 + b, 0)),
            pl.BlockSpec((DEC_SEQ, N_HEADS * LANE), lambda b, j: (s0 + b, 0)),
        ],
        out_specs=pl.BlockSpec((tq, N_HEADS * V_HEAD), lambda b, j: (b * n_q + j, 0)),
        out_shape=jax.ShapeDtypeStruct((N_SAMPLE, N_HEADS * V_HEAD), BF),
        compiler_params=_params(2),
        name="attn_sample",
    )(q, kc, vc, k, v)


CONV_ROWS = 256


def _conv_pass(ubuf, cbuf, gb_ref, wdw_ref, bdw_ref, gln_ref, bln_ref, wsc_ref, uo_ref, so_ref,
               shifted, cshifted):
    tl = CONV_ROWS
    span = shifted.shape[1]
    p0 = HALO - CONF_K // 2
    q0 = HALO - SC_K // 2
    for b in range(1, SUBLANES):
        shifted[b] = ubuf[pl.ds(b, span), :]
    for t in range(SC_K):
        if (q0 + t) % SUBLANES:
            cshifted[t] = cbuf[pl.ds((q0 + t) % SUBLANES, span), :]

    def staged(p, r, n):
        a, b = divmod(p, SUBLANES)
        if b == 0:
            return ubuf[pl.ds(r + SUBLANES * a, n), :]
        return shifted[b, pl.ds(r + SUBLANES * a, n), :]

    def staged_c(t, r, n):
        a, b = divmod(q0 + t, SUBLANES)
        if b == 0:
            return cbuf[pl.ds(r + SUBLANES * a, n), :]
        return cshifted[t, pl.ds(r + SUBLANES * a, n), :]

    rc = 32
    for r in range(0, tl, rc):
        acc = staged(p0, r, rc) * wdw_ref[0, 0:1, :]
        for t in range(1, CONF_K):
            acc = acc + staged(p0 + t, r, rc) * wdw_ref[0, t:t + 1, :]
        acc = acc + bdw_ref[0]
        mu = jnp.mean(acc, axis=-1, keepdims=True)
        cen = acc - mu
        var = jnp.mean(cen * cen, axis=-1, keepdims=True)
        y = cen * lax.rsqrt(var + EPS) * gln_ref[0] + bln_ref[0]
        uo_ref[pl.ds(r, rc), :] = (y * _sigmoid(y)).astype(BF)

        cv = staged_c(0, r, rc) * wsc_ref[0, 0:1, :]
        for t in range(1, SC_K):
            cv = cv + staged_c(t, r, rc) * wsc_ref[0, t:t + 1, :]
        so_ref[pl.ds(r, rc), :] = (gb_ref[pl.ds(r, rc), :] * cv).astype(BF)


def _conv_tile(tm, u_ref, c_ref, gb_ref, halos, is_latent, has_prev, has_next, conv_w, uo_ref, so_ref, scratch):
    ubuf, cbuf, shifted, cshifted = scratch
    n_pass = tm // CONV_ROWS
    zeros = jnp.zeros((HALO, W_CONF), F32)
    for p in range(n_pass):
        r0 = p * CONV_ROWS
        for src, buf, k in ((u_ref, ubuf, 0), (c_ref, cbuf, 2)):
            if p > 0:
                head = jnp.where(is_latent, src[r0 - HALO:r0, :], 0.0)
            else:
                head = zeros if halos is None else jnp.where(has_prev, halos[k][...], 0.0)
            if p < n_pass - 1:
                tail = jnp.where(is_latent, src[r0 + CONV_ROWS:r0 + CONV_ROWS + HALO, :], 0.0)
            else:
                tail = zeros if halos is None else jnp.where(has_next, halos[k + 1][...], 0.0)
            buf[0:HALO, :] = head
            buf[HALO:HALO + CONV_ROWS, :] = src[r0:r0 + CONV_ROWS, :]
            buf[HALO + CONV_ROWS:2 * HALO + CONV_ROWS, :] = tail
        rows = pl.ds(r0, CONV_ROWS)
        _conv_pass(ubuf, cbuf, gb_ref.at[rows], *conv_w, uo_ref.at[rows], so_ref.at[rows], shifted, cshifted)


def _conv_scratch():
    staged = CONV_ROWS + 2 * HALO
    return [pltpu.VMEM((staged, W_CONF), F32), pltpu.VMEM((staged, W_SC), F32),
            pltpu.VMEM((SUBLANES, staged - SUBLANES, W_CONF), F32),
            pltpu.VMEM((SC_K, staged - SUBLANES, W_SC), F32)]


def _conv_body(u_ref, c_ref, gb_ref, up_ref, un_ref, cp_ref, cn_ref, *rest, tl, n_prompt_tiles, tiles_per_seq):
    conv_w, (uo_ref, so_ref, *scratch) = rest[:5], rest[5:]
    i = pl.program_id(0)
    is_latent = i >= n_prompt_tiles
    j = jnp.maximum(i - n_prompt_tiles, 0) % tiles_per_seq
    _conv_tile(tl, u_ref, c_ref, gb_ref, (up_ref, un_ref, cp_ref, cn_ref), is_latent,
               is_latent & (j > 0), is_latent & (j < tiles_per_seq - 1), conv_w, uo_ref, so_ref, scratch)


def _conv_call(l, u0, gcx, gb, conv_w, tl):
    n_halo = N_TOK // HALO
    per_tile = tl // HALO
    cur = pl.BlockSpec((tl, W_CONF), lambda i: (i, 0))
    prv = pl.BlockSpec((HALO, W_CONF), lambda i: (jnp.maximum(i * per_tile - 1, 0), 0))
    nxt = pl.BlockSpec((HALO, W_CONF), lambda i: (jnp.minimum((i + 1) * per_tile, n_halo - 1), 0))
    body = functools.partial(_conv_body, tl=tl, n_prompt_tiles=N_PROMPT // tl, tiles_per_seq=DEC_SEQ // tl)
    return pl.pallas_call(
        body,
        grid=(N_TOK // tl,),
        in_specs=[cur, cur, cur, prv, nxt, prv, nxt,
                  _layer(l, (CONF_K, W_CONF)), _layer(l, (1, W_CONF)), _layer(l, (1, W_CONF)),
                  _layer(l, (1, W_CONF)), _layer(l, (SC_K, W_SC))],
        out_specs=[cur, cur],
        out_shape=[jax.ShapeDtypeStruct((N_TOK, W_CONF), BF), jax.ShapeDtypeStruct((N_TOK, W_SC), BF)],
        scratch_shapes=_conv_scratch(),
        compiler_params=_params(1),
        name="dwconv",
    )(u0, gcx, gb, u0, u0, gcx, gcx, *conv_w)


def _fourier_body(fn_ref, gd_ref, cs_ref, *rest, seq, group, scale, conv_rows):
    if conv_rows:
        (u_ref, c_ref, gb_ref), conv_w, (out_ref, uo_ref, so_ref, rhs, *conv_scratch) = (
            rest[:3], rest[3:8], rest[9:])
    else:
        out_ref, rhs = rest

    @pl.when(pl.program_id(1) == 0)
    def _():
        for b in range(group):
            v = _dot(fn_ref[b * seq:(b + 1) * seq, :], gd_ref[...])
            rhs[0:seq, b * W_FN:(b + 1) * W_FN] = v[:, 0:W_FN].astype(BF)
            rhs[seq:2 * seq, b * W_FN:(b + 1) * W_FN] = (-v[:, W_FN:]).astype(BF)

    res = _dot(cs_ref[...], rhs[...]) * scale
    for b in range(group):
        out_ref[b] = res[:, b * W_FN:(b + 1) * W_FN].astype(BF)

    if conv_rows:
        is_latent = pl.program_id(1) * conv_rows >= N_PROMPT
        _conv_tile(conv_rows, u_ref, c_ref, gb_ref, None, is_latent, False, False, conv_w, uo_ref, so_ref,
                   conv_scratch)


def _fourier_call(fn, gd, cs, seq, n_seq, row0, tl, group, conv=None):
    n_t = seq // tl
    g0 = row0 // (group * seq)
    n_steps = (n_seq // group) * n_t
    conv_rows = 0
    extra_in, extra_out, extra_shape, extra_scratch, extra_args = [], [], [], [], []
    if conv is not None:
        l, u0, gcx, gb, conv_w, run_after = conv
        conv_rows = N_TOK // n_steps
        assert n_seq == group and conv_rows % DEC_SEQ == 0
        blk = pl.BlockSpec((conv_rows, W_CONF), lambda g, j: (j, 0))
        extra_in = [blk, blk, blk, _layer(l, (CONF_K, W_CONF)), _layer(l, (1, W_CONF)),
                    _layer(l, (1, W_CONF)), _layer(l, (1, W_CONF)), _layer(l, (SC_K, W_SC)),
                    pl.BlockSpec(memory_space=pl.ANY)]
        extra_out = [blk, blk]
        extra_shape = [jax.ShapeDtypeStruct((N_TOK, W_CONF), BF), jax.ShapeDtypeStruct((N_TOK, W_SC), BF)]
        extra_scratch = _conv_scratch()
        extra_args = [u0, gcx, gb, *conv_w, run_after]
    body = functools.partial(_fourier_body, seq=seq, group=group, scale=float((seq * FN_GROUP_W) ** -0.5),
                             conv_rows=conv_rows)
    outs = pl.pallas_call(
        body,
        grid=(n_seq // group, n_t),
        in_specs=[
            pl.BlockSpec((group * seq, W_FN), lambda g, j: (g0 + g, 0)),
            _full((W_FN, 2 * W_FN)),
            pl.BlockSpec((tl, 2 * seq), lambda g, j: (j, 0)),
            *extra_in,
        ],
        out_specs=[pl.BlockSpec((group, tl, W_FN), lambda g, j: (g, j, 0)), *extra_out],
        out_shape=[jax.ShapeDtypeStruct((n_seq, seq, W_FN), BF), *extra_shape],
        scratch_shapes=[pltpu.VMEM((2 * seq, group * W_FN), BF), *extra_scratch],
        compiler_params=_params(2),
        name=f"fourier_{seq}",
    )(fn, gd, cs, *extra_args)
    f = outs[0].reshape(n_seq * seq, W_FN)
    return f if conv is None else (f, outs[1], outs[2])


def _dft_cos_sin(n):
    r = np.arange(n, dtype=np.int64)
    ang = ((r[:, None] * r[None, :]) % n).astype(np.float64) * (2.0 * np.pi / n)
    return np.cos(ang).astype(np.float32), np.sin(ang).astype(np.float32)


def _stage3_body(*refs, n_prompt_tiles, tiles_per_seq, n_x, final, chunk):
    x_refs, refs = _take(refs, n_x)
    (op_ref, os_ref, u_ref, s_ref, fp_ref, fs_ref, gate_ref, mod_ref, g2_ref, gfin_ref), refs = _take(refs, 10)
    (wo_ref, wpw_ref, wsco_ref, wfn_ref, wout_ref, wg_ref, wu_ref, wd_ref), refs = _take(refs, 8)
    if final:
        out_refs = refs
    else:
        (next_w_ref,), (x_out_ref, next_w_out_ref) = _take(refs, 1)
        next_w_out_ref[...] = next_w_ref[...].astype(BF)
        out_refs = (x_out_ref,)

    i = pl.program_id(0)
    is_sample = i >= n_prompt_tiles
    row = _mod_row(i, n_prompt_tiles, tiles_per_seq)
    mod = lambda k: mod_ref[0, pl.ds(row, 1), k * D:(k + 1) * D]
    pick = lambda p_ref, s_ref: jnp.where(is_sample, s_ref[...], p_ref[...])
    merged = gate_ref[:, 0:D].astype(F32) * _dot(pick(op_ref, os_ref), wo_ref[...])
    merged = merged + gate_ref[:, D:2 * D].astype(F32) * _dot(u_ref[...], wpw_ref[...])
    merged = merged + gate_ref[:, 2 * D:3 * D].astype(F32) * _dot(s_ref[...], wsco_ref[...])
    merged = merged + gate_ref[:, 3 * D:4 * D].astype(F32) * _dot(pick(fp_ref, fs_ref), wfn_ref[...])
    x1 = _read_x(x_refs, is_sample) + mod(2) * _dot(merged.astype(BF), wout_ref[...])

    h2 = (_rms(x1, g2_ref[0]) * (1.0 + mod(4)) + mod(3)).astype(BF)
    bounds = list(range(0, FF_HIDDEN, chunk)) + [FF_HIDDEN]
    spans = list(zip(bounds[:-1], bounds[1:]))

    def gate_up(span):
        c0, c1 = span
        return _dot(h2, wg_ref[:, c0:c1]), _dot(h2, wu_ref[:, c0:c1])

    acc = None
    ab_next = gate_up(spans[0])
    for n, (c0, c1) in enumerate(spans):
        a, b = ab_next
        if n + 1 < len(spans):
            ab_next = gate_up(spans[n + 1])
        t = _dot((a * _sigmoid(a) * b).astype(BF), wd_ref[c0:c1, :])
        acc = t if acc is None else acc + t
    x2 = x1 + mod(5) * acc
    if not final:
        out_refs[0][...] = x2
        return
    y = _rms(x2, gfin_ref[...])

    @pl.when(is_sample)
    def _():
        out_refs[1][...] = y

    @pl.when(jnp.logical_not(is_sample))
    def _():
        out_refs[0][...] = y


def _stage3_call(l, x, o_p, o_s, u, s, f_p, f_s, gates, mod_l, g2, gfin, w3, w_in_t_f32, tm, final):
    n_tiles = N_TOK // tm
    row_blk = lambda w: pl.BlockSpec((tm, w), lambda i: (i, 0))
    body = functools.partial(_stage3_body, n_prompt_tiles=N_PROMPT // tm, tiles_per_seq=DEC_SEQ // tm,
                             n_x=len(x), final=final, chunk=256)
    if final:
        extra_in, extra_args = [], []
        out_specs = list(_split_specs(tm, D))
        out_shape = [jax.ShapeDtypeStruct((N_PROMPT, D), F32), jax.ShapeDtypeStruct((N_SAMPLE, D), F32)]
    else:
        slab = -(-IN_COLS // (n_tiles * BF16_ROWS)) * BF16_ROWS
        extra_in = [pl.BlockSpec((1, slab, D), lambda i: (l + 1, i, 0))]
        extra_args = [w_in_t_f32]
        out_specs = [row_blk(D), pl.BlockSpec((1, slab, D), lambda i: (0, i, 0))]
        out_shape = [jax.ShapeDtypeStruct((N_TOK, D), F32), jax.ShapeDtypeStruct((1, IN_COLS, D), BF)]
    return pl.pallas_call(
        body,
        grid=(n_tiles,),
        in_specs=[
            *_x_specs(x, tm), *_split_specs(tm, N_HEADS * V_HEAD), row_blk(W_CONF), row_blk(W_SC),
            *_split_specs(tm, W_FN),
            row_blk(N_BRANCH * D), _layer(l, (8, 6 * D)), _layer(l, (1, D)), _full((1, D)),
            _full((N_HEADS * V_HEAD, D)), _full((W_CONF, D)), _full((W_SC, D)), _full((W_FN, D)),
            _full((D, D)), _full((D, FF_HIDDEN)), _full((D, FF_HIDDEN)), _full((FF_HIDDEN, D)),
            *extra_in,
        ],
        out_specs=out_specs,
        out_shape=out_shape,
        compiler_params=_params(1),
        name="stage3",
    )(*x, o_p, o_s, u, s, f_p, f_s, gates, mod_l, g2, gfin, *w3, *extra_args)


def _rope_table(tm):
    rows = DEC_SEQ // GRID_W
    row_pos = np.repeat(np.arange(rows, dtype=np.float64), GRID_W)
    col_pos = np.tile(np.arange(GRID_W, dtype=np.float64), rows)
    inv = ROPE_THETA ** (-np.arange(0, AXIS_ROPE, 2, dtype=np.float64) / AXIS_ROPE)
    ang = np.concatenate([row_pos[:, None] * inv, col_pos[:, None] * inv], axis=1)
    half = QK_ROPE // 2
    tab = np.zeros((tm + DEC_SEQ, 3 * LANE), np.float64)
    tab[:, 0:LANE] = 1.0
    tab[tm:, ROPE_LANE0:ROPE_LANE0 + half] = np.cos(ang)
    tab[tm:, ROPE_LANE0 + half:ROPE_LANE0 + QK_ROPE] = np.cos(ang)
    tab[tm:, LANE + ROPE_LANE0:LANE + ROPE_LANE0 + half] = -np.sin(ang)
    tab[tm:, 2 * LANE + ROPE_LANE0 + half:2 * LANE + ROPE_LANE0 + QK_ROPE] = np.sin(ang)
    return jnp.asarray(tab.astype(np.float32))


def _head_layout_body(wqb_ref, wkvb_ref, pq_ref, pkv_ref, qb_ref, kvb_ref):
    qb_ref[0] = _dot(wqb_ref[0].astype(BF), pq_ref[...]).astype(BF)
    kvb_ref[0] = _dot(wkvb_ref[0].astype(BF), pkv_ref[...]).astype(BF)


def _head_layout_call(w_qb, w_kvb):
    dqk = QK_NOPE + QK_ROPE
    pq = np.zeros((N_HEADS * dqk, N_HEADS * LANE), np.float32)
    pkv = np.zeros((N_HEADS * LANE, 2 * N_HEADS * LANE), np.float32)
    for h in range(N_HEADS):
        src = h * dqk + np.concatenate([np.arange(QK_NOPE), QK_NOPE + ROPE_PERM])
        pq[src, h * LANE + np.arange(dqk)] = 1.0
        nope = np.arange(QK_NOPE)
        pkv[h * LANE + nope, h * LANE + nope] = 1.0
        val = np.arange(V_HEAD)
        pkv[h * LANE + QK_NOPE + val, (N_HEADS + h) * LANE + (h % 2) * V_HEAD + val] = 1.0
    return pl.pallas_call(
        _head_layout_body,
        grid=(DEPTH,),
        in_specs=[pl.BlockSpec((1, Q_LORA, N_HEADS * dqk), lambda l: (l, 0, 0)),
                  pl.BlockSpec((1, KV_LORA, N_HEADS * LANE), lambda l: (l, 0, 0)),
                  _full(pq.shape), _full(pkv.shape)],
        out_specs=[pl.BlockSpec((1, Q_LORA, N_HEADS * LANE), lambda l: (l, 0, 0)),
                   pl.BlockSpec((1, KV_LORA, 2 * N_HEADS * LANE), lambda l: (l, 0, 0))],
        out_shape=[jax.ShapeDtypeStruct((DEPTH, Q_LORA, N_HEADS * LANE), BF),
                   jax.ShapeDtypeStruct((DEPTH, KV_LORA, 2 * N_HEADS * LANE), BF)],
        compiler_params=_params(1),
        name="head_layout",
    )(w_qb, w_kvb, jnp.asarray(pq, BF), jnp.asarray(pkv, BF))


def kernel(x_prompt, x_sample, cache_ckv, cache_krope, c, c_ctx, w_ada, b_ada, g_norm1, g_norm2, w_in, g_qa, w_qb, g_kva, w_kvb, w_o_mla, w_conf_dw, b_conf_dw, g_conf_ln, b_conf_ln, w_conf_pw, w_sc_conv, w_sc_out, w_fn, w_out, w_ffn_gate, w_ffn_up, w_ffn_down, g_final):
    tm = 512
    xp = x_prompt.reshape(N_PROMPT, D)
    xs = x_sample.reshape(N_SAMPLE, D)
    cvec = jnp.concatenate([c_ctx[None, :], c, jnp.zeros((8 - 1 - DEC_BATCH, D), F32)], axis=0)
    mod = _ada_call(cvec, w_ada, b_ada)

    rope_tab = _rope_table(tm)
    place = np.zeros((QK_ROPE, LANE), np.float32)
    place[ROPE_PERM, ROPE_LANE0 + np.arange(QK_ROPE)] = 1.0
    wqb, wkvb_all = _head_layout_call(w_qb, w_kvb)
    kc, vc = _ctx_call(cache_ckv, cache_krope, wkvb_all, jnp.asarray(place, BF))

    cg, sg = _dft_cos_sin(FN_GROUP_W)
    eye = np.eye(FN_GROUPS, dtype=np.float32)
    gd = jnp.asarray(np.concatenate([np.kron(eye, cg), np.kron(eye, sg)], axis=1)).astype(BF)
    cs_p = jnp.asarray(np.concatenate(_dft_cos_sin(SEQ), axis=1)).astype(BF)
    cs_s = jnp.asarray(np.concatenate(_dft_cos_sin(DEC_SEQ), axis=1)).astype(BF)

    w_in_t_f32 = jnp.swapaxes(w_in, 1, 2)
    w_in_t = w_in_t_f32[0:1].astype(BF)
    w3_f32 = (w_o_mla, w_conf_pw, w_sc_out, w_fn, w_out, w_ffn_gate, w_ffn_up, w_ffn_down)

    rows = lambda a: a[:, None, :]
    conv_w = (w_conf_dw, rows(b_conf_dw), rows(g_conf_ln), rows(b_conf_ln), w_sc_conv)
    x = (xp, xs)
    new_ckv, new_krope = [], []
    for l in range(DEPTH):
        (q, k, v, ckv, kr, u0, gcx, gb, fn, gates), w3 = _stage1_call(
            l, x, mod, rows(g_norm1), rows(g_qa), rows(g_kva), w_in_t, wqb, wkvb_all, rope_tab,
            w3_f32, tm)
        new_ckv.append(ckv[:N_PROMPT].reshape(BATCH, SEQ, KV_LORA))
        new_krope.append(kr[:N_PROMPT].reshape(BATCH, SEQ, QK_ROPE))

        o_p = _attn_prompt_call(q, k, v)
        o_s = _attn_sample_call(l, q, k, v, kc, vc, 1024)
        f_p = _fourier_call(fn, gd, cs_p, SEQ, BATCH, 0, SEQ, 4)
        f_s, u, s = _fourier_call(fn, gd, cs_s, DEC_SEQ, DEC_BATCH, N_PROMPT, 512, DEC_BATCH,
                                  conv=(l, u0, gcx, gb, conv_w, o_s))
        out = _stage3_call(l, x, o_p, o_s, u, s, f_p, f_s, gates, mod, rows(g_norm2), g_final[None, :], w3,
                           w_in_t_f32, tm, l == DEPTH - 1)
        if l < DEPTH - 1:
            x, w_in_t = (out[0],), out[1]

    y_prompt, y_sample = out
    return (y_prompt.reshape(BATCH, SEQ, D), y_sample.reshape(DEC_BATCH, DEC_SEQ, D),
            jnp.stack(new_ckv, axis=1), jnp.stack(new_krope, axis=1))
```

```python
import functools

import numpy as np
import jax
import jax.numpy as jnp
from jax import lax
from jax.experimental import pallas as pl
from jax.experimental.pallas import tpu as pltpu

BF = jnp.bfloat16
F32 = jnp.float32

D = 1024
BATCH = 16
SEQ = 256
DEPTH = 2
DEC_BATCH = 2
DEC_SEQ = 2048
PAST_LEN = 256
GRID_W = 64
N_HEADS = 8
QK_NOPE = 64
QK_ROPE = 32
V_HEAD = 64
Q_LORA = 384
KV_LORA = 256
AXIS_ROPE = QK_ROPE // 2
ROPE_THETA = 10000.0
W_CONF = D // 4
CONF_K = 31
W_SC = D // 4
SC_K = 3
W_FN = D // 4
FN_GROUPS = 4
FN_GROUP_W = W_FN // FN_GROUPS
N_BRANCH = 4
FF_HIDDEN = ((8 * D // 3 + 255) // 256) * 256
EPS = 1e-6
OFF_QA = 0
OFF_KVA = OFF_QA + Q_LORA
OFF_CONF = OFF_KVA + KV_LORA + QK_ROPE
OFF_SC = OFF_CONF + 2 * W_CONF
OFF_FN = OFF_SC + 3 * W_SC
OFF_GATE = OFF_FN + W_FN
IN_COLS = OFF_GATE + N_BRANCH * D

N_PROMPT = BATCH * SEQ
N_SAMPLE = DEC_BATCH * DEC_SEQ
N_TOK = N_PROMPT + N_SAMPLE
LANE = 128
SUBLANES = 8
BF16_ROWS = 2 * SUBLANES
HALO = 16
SM_SCALE = float((QK_NOPE + QK_ROPE) ** -0.5)
Q_SCALE = SM_SCALE * float(np.log2(np.e))
VMEM_LIMIT = 56 * 1024 * 1024

QK_END = OFF_KVA + KV_LORA + LANE

ROPE_PERM = np.array(list(range(0, 8)) + list(range(16, 24)) + list(range(8, 16)) + list(range(24, 32)))
ROPE_LANE0 = QK_NOPE


def _dot(a, b):
    return jnp.dot(a, b, preferred_element_type=F32)


def _dot_nt(a, b):
    return lax.dot_general(a, b, (((1,), (1,)), ((), ())), preferred_element_type=F32)


def _sigmoid(x):
    return jax.nn.sigmoid(x)


def _rms(x, g):
    return x * lax.rsqrt(jnp.mean(x * x, axis=-1, keepdims=True) + EPS) * g


def _full(shape):
    return pl.BlockSpec(shape, lambda *_: (0,) * len(shape))


def _layer(l, shape):
    return pl.BlockSpec((1, *shape), lambda *_: (l,) + (0,) * len(shape), pipeline_mode=pl.Buffered(1))


def _params(n_axes):
    return pltpu.CompilerParams(
        dimension_semantics=("arbitrary",) * n_axes, vmem_limit_bytes=VMEM_LIMIT)


def _mod_row(i, n_prompt_tiles, tiles_per_seq):
    return jnp.where(i >= n_prompt_tiles, 1 + (i - n_prompt_tiles) // tiles_per_seq, 0)


def _ada_body(c_ref, w_ref, b_ref, out_ref):
    cv = c_ref[...]
    sc = (cv * _sigmoid(cv)).astype(BF)
    out_ref[0] = _dot(sc, w_ref[0].astype(BF)) + b_ref[0]


def _ada_call(cvec, w_ada, b_ada):
    n_col = 6 * D // D
    return pl.pallas_call(
        _ada_body,
        grid=(DEPTH, n_col),
        in_specs=[
            _full((8, D)),
            pl.BlockSpec((1, D, D), lambda l, j: (l, 0, j)),
            pl.BlockSpec((1, 1, D), lambda l, j: (l, 0, j)),
        ],
        out_specs=pl.BlockSpec((1, 8, D), lambda l, j: (l, 0, j)),
        out_shape=jax.ShapeDtypeStruct((DEPTH, 8, 6 * D), F32),
        compiler_params=_params(2),
        name="ada_mod",
    )(cvec, w_ada, b_ada.reshape(DEPTH, 1, 6 * D))


def _ctx_body(ckv_ref, kr_ref, wkvb_ref, place_ref, k_ref, v_ref):
    kv = _dot(ckv_ref[0, 0].astype(BF), wkvb_ref[0])
    krp = _dot(kr_ref[0, 0].astype(BF), place_ref[...])
    for h in range(N_HEADS):
        sl = slice(h * LANE, (h + 1) * LANE)
        k_ref[0, 0, :, sl] = (kv[:, sl] + krp).astype(BF)
    v_ref[0, 0] = (kv[:, N_HEADS * LANE:] + _value_ones_row()).astype(BF)


def _ctx_call(cache_ckv, cache_krope, wkvb_all, place):
    return pl.pallas_call(
        _ctx_body,
        grid=(DEPTH, DEC_BATCH),
        in_specs=[
            pl.BlockSpec((1, 1, PAST_LEN, KV_LORA), lambda l, b: (b, l, 0, 0)),
            pl.BlockSpec((1, 1, PAST_LEN, QK_ROPE), lambda l, b: (b, l, 0, 0)),
            pl.BlockSpec((1, KV_LORA, 2 * N_HEADS * LANE), lambda l, b: (l, 0, 0)),
            _full((QK_ROPE, LANE)),
        ],
        out_specs=[
            pl.BlockSpec((1, 1, PAST_LEN, N_HEADS * LANE), lambda l, b: (l, b, 0, 0)),
            pl.BlockSpec((1, 1, PAST_LEN, N_HEADS * LANE), lambda l, b: (l, b, 0, 0)),
        ],
        out_shape=[
            jax.ShapeDtypeStruct((DEPTH, DEC_BATCH, PAST_LEN, N_HEADS * LANE), BF),
            jax.ShapeDtypeStruct((DEPTH, DEC_BATCH, PAST_LEN, N_HEADS * LANE), BF),
        ],
        compiler_params=_params(2),
        name="ctx_keys",
    )(cache_ckv, cache_krope, wkvb_all, place)


def _take(refs, n):
    return refs[:n], refs[n:]


def _read_x(x_refs, is_sample):
    if len(x_refs) == 1:
        return x_refs[0][...]
    return jnp.where(is_sample, x_refs[1][...], x_refs[0][...])


def _stage1_body(*refs, n_prompt_tiles, tiles_per_seq, n_x, n_cast):
    x_refs, refs = _take(refs, n_x)
    (mod_ref, g1_ref, gqa_ref, gkva_ref, wt_ref, wqb_ref, wkvb_ref, rope_ref), refs = _take(refs, 8)
    cast_in, refs = _take(refs, n_cast)
    (q_ref, k_ref, v_ref, ckv_ref, kr_ref, u0_ref, gcx_ref, gb_ref, fn_ref, gate_ref), cast_out = _take(refs, 10)

    for src, dst in zip(cast_in, cast_out, strict=True):
        dst[...] = src[0].astype(BF)

    i = pl.program_id(0)
    is_sample = i >= n_prompt_tiles
    row = _mod_row(i, n_prompt_tiles, tiles_per_seq)
    sh1 = mod_ref[0, pl.ds(row, 1), 0:D]
    sc1 = mod_ref[0, pl.ds(row, 1), D:2 * D]
    hb = (_rms(_read_x(x_refs, is_sample), g1_ref[0]) * (1.0 + sc1) + sh1).astype(BF)

    def proj(c0, c1):
        return _dot_nt(hb, wt_ref[0, c0:c1, :])

    pqk = proj(OFF_QA, QK_END)
    q = _dot(_rms(pqk[:, 0:Q_LORA], gqa_ref[0]).astype(BF), wqb_ref[0])

    ckv = _rms(pqk[:, OFF_KVA:OFF_KVA + KV_LORA], gkva_ref[0])
    ckv_ref[...] = ckv
    kr3 = pqk[:, OFF_KVA + KV_LORA:QK_END]
    kr_ref[...] = kr3[:, 0:QK_ROPE]
    lane = lax.broadcasted_iota(jnp.int32, (1, LANE), 1)
    group = lambda k: (lane >= ROPE_LANE0 + 8 * k) & (lane < ROPE_LANE0 + 8 * (k + 1))
    krm = jnp.where(group(0) | group(3), pltpu.roll(kr3, ROPE_LANE0, 1),
                    jnp.where(group(1), pltpu.roll(kr3, ROPE_LANE0 - 8, 1),
                              jnp.where(group(2), pltpu.roll(kr3, ROPE_LANE0 + 8, 1), 0.0)))
    kv = _dot(ckv.astype(BF), wkvb_ref[0])
    v_ref[...] = (kv[:, N_HEADS * LANE:] + _value_ones_row()).astype(BF)

    def rope(t):
        return (t * rope_ref[:, 0:LANE]
                + pltpu.roll(t, LANE - 16, 1) * rope_ref[:, LANE:2 * LANE]
                + pltpu.roll(t, 16, 1) * rope_ref[:, 2 * LANE:3 * LANE])

    krr = rope(krm)
    for h in range(N_HEADS):
        sl = slice(h * LANE, (h + 1) * LANE)
        q_ref[:, sl] = (rope(q[:, sl]) * Q_SCALE).astype(BF)
        k_ref[:, sl] = (kv[:, sl] + krr).astype(BF)

    for j in range(N_BRANCH):
        gate_ref[:, j * D:(j + 1) * D] = _sigmoid(
            proj(OFF_GATE + j * D, OFF_GATE + (j + 1) * D)).astype(BF)

    pc = proj(OFF_CONF, OFF_SC)
    u0_ref[...] = pc[:, 0:W_CONF] * _sigmoid(pc[:, W_CONF:])
    ps = proj(OFF_SC, OFF_FN)
    gb_ref[...] = ps[:, 0:W_SC]
    gcx_ref[...] = ps[:, W_SC:2 * W_SC] * ps[:, 2 * W_SC:]
    fn_ref[...] = proj(OFF_FN, OFF_GATE).astype(BF)


def _split_specs(tm, w):
    n_p = N_PROMPT // tm
    return (pl.BlockSpec((tm, w), lambda i: (jnp.minimum(i, n_p - 1), 0)),
            pl.BlockSpec((tm, w), lambda i: (jnp.maximum(i - n_p, 0), 0)))


def _x_specs(x, tm):
    return list(_split_specs(tm, D)) if len(x) == 2 else [pl.BlockSpec((tm, D), lambda i: (i, 0))]


def _stage1_call(l, x, mod_l, g1, gqa, gkva, w_in_t, wqb, wkvb, rope_tab, to_cast, tm):
    n_prompt_tiles = N_PROMPT // tm
    tiles_per_seq = DEC_SEQ // tm
    row_blk = lambda w: pl.BlockSpec((tm, w), lambda i: (i, 0))
    n_steps = N_TOK // tm
    body = functools.partial(_stage1_body, n_prompt_tiles=n_prompt_tiles, tiles_per_seq=tiles_per_seq,
                             n_x=len(x), n_cast=len(to_cast))
    slab = lambda w: w.shape[1] // n_steps
    assert all(w.shape[1] % (BF16_ROWS * n_steps) == 0 for w in to_cast)
    outs = pl.pallas_call(
        body,
        grid=(n_steps,),
        in_specs=[
            *_x_specs(x, tm),
            _layer(l, (8, 6 * D)),
            _layer(l, (1, D)),
            _layer(l, (1, Q_LORA)),
            _layer(l, (1, KV_LORA)),
            _layer(0, (IN_COLS, D)),
            _layer(l, (Q_LORA, N_HEADS * LANE)),
            _layer(l, (KV_LORA, 2 * N_HEADS * LANE)),
            pl.BlockSpec((tm, 3 * LANE),
                         lambda i: (jnp.where(i < n_prompt_tiles, 0,
                                              1 + (i - n_prompt_tiles) % tiles_per_seq), 0)),
            *[pl.BlockSpec((1, slab(w), w.shape[2]), lambda i: (l, i, 0)) for w in to_cast],
        ],
        out_specs=[
            row_blk(N_HEADS * LANE), row_blk(N_HEADS * LANE), row_blk(N_HEADS * LANE),
            row_blk(KV_LORA), row_blk(QK_ROPE), row_blk(W_CONF), row_blk(W_SC), row_blk(W_SC),
            row_blk(W_FN), row_blk(N_BRANCH * D),
            *[pl.BlockSpec((slab(w), w.shape[2]), lambda i: (i, 0)) for w in to_cast],
        ],
        out_shape=[
            jax.ShapeDtypeStruct((N_TOK, N_HEADS * LANE), BF),
            jax.ShapeDtypeStruct((N_TOK, N_HEADS * LANE), BF),
            jax.ShapeDtypeStruct((N_TOK, N_HEADS * LANE), BF),
            jax.ShapeDtypeStruct((N_TOK, KV_LORA), F32),
            jax.ShapeDtypeStruct((N_TOK, QK_ROPE), F32),
            jax.ShapeDtypeStruct((N_TOK, W_CONF), F32),
            jax.ShapeDtypeStruct((N_TOK, W_SC), F32),
            jax.ShapeDtypeStruct((N_TOK, W_SC), F32),
            jax.ShapeDtypeStruct((N_TOK, W_FN), BF),
            jax.ShapeDtypeStruct((N_TOK, N_BRANCH * D), BF),
            *[jax.ShapeDtypeStruct(w.shape[1:], BF) for w in to_cast],
        ],
        compiler_params=_params(1),
        name="stage1",
    )(*x, mod_l, g1, gqa, gkva, w_in_t, wqb, wkvb, rope_tab, *to_cast)
    return outs[:10], outs[10:]


def _ones_lane(parity):
    return (1 - parity) * V_HEAD


def _value_ones_row():
    col = lax.broadcasted_iota(jnp.int32, (1, N_HEADS * LANE), 1)
    odd = (col // LANE) % 2
    return (col % LANE == jnp.where(odd == 1, _ones_lane(1), _ones_lane(0))).astype(F32)


def _attend_heads(q_ref, kv_refs, o_ref, ahead):
    lane = lax.broadcasted_iota(jnp.int32, (1, LANE), 1)

    def scores(h):
        sl = slice(h * LANE, (h + 1) * LANE)
        return [_dot_nt(q_ref[:, sl], k_ref[:, sl]) for k_ref, _ in kv_refs]

    pending = [scores(h) for h in range(ahead)]
    for hp in range(N_HEADS // 2):
        vsl = slice(hp * LANE, (hp + 1) * LANE)
        outs = []
        for e in range(2):
            h = 2 * hp + e
            sl = slice(h * LANE, (h + 1) * LANE)
            ss = pending.pop(0)
            if h + ahead < N_HEADS:
                pending.append(scores(h + ahead))
            m = functools.reduce(jnp.maximum, [jnp.max(s, axis=-1, keepdims=True) for s in ss])
            o = None
            for s, (_, v_ref) in zip(ss, kv_refs):
                t = _dot(jnp.exp2(s - m).astype(BF), v_ref[:, sl])
                o = t if o is None else o + t
            ones_lane = _ones_lane(e)
            outs.append(o * (1.0 / o[:, ones_lane:ones_lane + 1]))
        o_ref[:, vsl] = jnp.where(lane < V_HEAD, outs[0], outs[1]).astype(BF)


def _attn_prompt_body(q_ref, k_ref, v_ref, o_ref, *, per_step):
    for b in range(per_step):
        rows = pl.ds(b * SEQ, SEQ)
        _attend_heads(q_ref.at[rows], [(k_ref.at[rows], v_ref.at[rows])], o_ref.at[rows], ahead=N_HEADS)


def _attn_prompt_call(q, k, v):
    per_step = 4
    blk = lambda w: pl.BlockSpec((per_step * SEQ, w), lambda b: (b, 0))
    return pl.pallas_call(
        functools.partial(_attn_prompt_body, per_step=per_step),
        grid=(BATCH // per_step,),
        in_specs=[blk(N_HEADS * LANE), blk(N_HEADS * LANE), blk(N_HEADS * LANE)],
        out_specs=blk(N_HEADS * V_HEAD),
        out_shape=jax.ShapeDtypeStruct((N_PROMPT, N_HEADS * V_HEAD), BF),
        compiler_params=_params(1),
        name="attn_prompt",
    )(q, k, v)


def _attn_sample_body(q_ref, kc_ref, vc_ref, k_ref, v_ref, o_ref):
    _attend_heads(q_ref, [(kc_ref.at[0, 0], vc_ref.at[0, 0]), (k_ref, v_ref)], o_ref, ahead=2)


def _attn_sample_call(l, q, k, v, kc, vc, tq):
    n_q = DEC_SEQ // tq
    q0 = N_PROMPT // tq
    s0 = N_PROMPT // DEC_SEQ
    return pl.pallas_call(
        _attn_sample_body,
        grid=(DEC_BATCH, n_q),
        in_specs=[
            pl.BlockSpec((tq, N_HEADS * LANE), lambda b, j: (q0 + b * n_q + j, 0)),
            pl.BlockSpec((1, 1, PAST_LEN, N_HEADS * LANE), lambda b, j: (l, b, 0, 0)),
            pl.BlockSpec((1, 1, PAST_LEN, N_HEADS * LANE), lambda b, j: (l, b, 0, 0)),
            pl.BlockSpec((DEC_SEQ, N_HEADS * LANE), lambda b, j: (s0 + b, 0)),
            pl.BlockSpec((DEC_SEQ, N_HEADS * LANE), lambda b, j: (s0 + b, 0)),
        ],
        out_specs=pl.BlockSpec((tq, N_HEADS * V_HEAD), lambda b, j: (b * n_q + j, 0)),
        out_shape=jax.ShapeDtypeStruct((N_SAMPLE, N_HEADS * V_HEAD), BF),
        compiler_params=_params(2),
        name="attn_sample",
    )(q, kc, vc, k, v)


CONV_ROWS = 256


def _conv_pass(ubuf, cbuf, gb_ref, wdw_ref, bdw_ref, gln_ref, bln_ref, wsc_ref, uo_ref, so_ref,
               shifted, cshifted):
    tl = CONV_ROWS
    span = shifted.shape[1]
    p0 = HALO - CONF_K // 2
    q0 = HALO - SC_K // 2
    for b in range(1, SUBLANES):
        shifted[b] = ubuf[pl.ds(b, span), :]
    for t in range(SC_K):
        if (q0 + t) % SUBLANES:
            cshifted[t] = cbuf[pl.ds((q0 + t) % SUBLANES, span), :]

    def staged(p, r, n):
        a, b = divmod(p, SUBLANES)
        if b == 0:
            return ubuf[pl.ds(r + SUBLANES * a, n), :]
        return shifted[b, pl.ds(r + SUBLANES * a, n), :]

    def staged_c(t, r, n):
        a, b = divmod(q0 + t, SUBLANES)
        if b == 0:
            return cbuf[pl.ds(r + SUBLANES * a, n), :]
        return cshifted[t, pl.ds(r + SUBLANES * a, n), :]

    rc = 32
    for r in range(0, tl, rc):
        acc = staged(p0, r, rc) * wdw_ref[0, 0:1, :]
        for t in range(1, CONF_K):
            acc = acc + staged(p0 + t, r, rc) * wdw_ref[0, t:t + 1, :]
        acc = acc + bdw_ref[0]
        mu = jnp.mean(acc, axis=-1, keepdims=True)
        cen = acc - mu
        var = jnp.mean(cen * cen, axis=-1, keepdims=True)
        y = cen * lax.rsqrt(var + EPS) * gln_ref[0] + bln_ref[0]
        uo_ref[pl.ds(r, rc), :] = (y * _sigmoid(y)).astype(BF)

        cv = staged_c(0, r, rc) * wsc_ref[0, 0:1, :]
        for t in range(1, SC_K):
            cv = cv + staged_c(t, r, rc) * wsc_ref[0, t:t + 1, :]
        so_ref[pl.ds(r, rc), :] = (gb_ref[pl.ds(r, rc), :] * cv).astype(BF)


def _conv_tile(tm, u_ref, c_ref, gb_ref, halos, is_latent, has_prev, has_next, conv_w, uo_ref, so_ref, scratch):
    ubuf, cbuf, shifted, cshifted = scratch
    n_pass = tm // CONV_ROWS
    zeros = jnp.zeros((HALO, W_CONF), F32)
    for p in range(n_pass):
        r0 = p * CONV_ROWS
        for src, buf, k in ((u_ref, ubuf, 0), (c_ref, cbuf, 2)):
            if p > 0:
                head = jnp.where(is_latent, src[r0 - HALO:r0, :], 0.0)
            else:
                head = zeros if halos is None else jnp.where(has_prev, halos[k][...], 0.0)
            if p < n_pass - 1:
                tail = jnp.where(is_latent, src[r0 + CONV_ROWS:r0 + CONV_ROWS + HALO, :], 0.0)
            else:
                tail = zeros if halos is None else jnp.where(has_next, halos[k + 1][...], 0.0)
            buf[0:HALO, :] = head
            buf[HALO:HALO + CONV_ROWS, :] = src[r0:r0 + CONV_ROWS, :]
            buf[HALO + CONV_ROWS:2 * HALO + CONV_ROWS, :] = tail
        rows = pl.ds(r0, CONV_ROWS)
        _conv_pass(ubuf, cbuf, gb_ref.at[rows], *conv_w, uo_ref.at[rows], so_ref.at[rows], shifted, cshifted)


def _conv_scratch():
    staged = CONV_ROWS + 2 * HALO
    return [pltpu.VMEM((staged, W_CONF), F32), pltpu.VMEM((staged, W_SC), F32),
            pltpu.VMEM((SUBLANES, staged - SUBLANES, W_CONF), F32),
            pltpu.VMEM((SC_K, staged - SUBLANES, W_SC), F32)]


def _conv_body(u_ref, c_ref, gb_ref, up_ref, un_ref, cp_ref, cn_ref, *rest, tl, n_prompt_tiles, tiles_per_seq):
    conv_w, (uo_ref, so_ref, *scratch) = rest[:5], rest[5:]
    i = pl.program_id(0)
    is_latent = i >= n_prompt_tiles
    j = jnp.maximum(i - n_prompt_tiles, 0) % tiles_per_seq
    _conv_tile(tl, u_ref, c_ref, gb_ref, (up_ref, un_ref, cp_ref, cn_ref), is_latent,
               is_latent & (j > 0), is_latent & (j < tiles_per_seq - 1), conv_w, uo_ref, so_ref, scratch)


def _conv_call(l, u0, gcx, gb, conv_w, tl):
    n_halo = N_TOK // HALO
    per_tile = tl // HALO
    cur = pl.BlockSpec((tl, W_CONF), lambda i: (i, 0))
    prv = pl.BlockSpec((HALO, W_CONF), lambda i: (jnp.maximum(i * per_tile - 1, 0), 0))
    nxt = pl.BlockSpec((HALO, W_CONF), lambda i: (jnp.minimum((i + 1) * per_tile, n_halo - 1), 0))
    body = functools.partial(_conv_body, tl=tl, n_prompt_tiles=N_PROMPT // tl, tiles_per_seq=DEC_SEQ // tl)
    return pl.pallas_call(
        body,
        grid=(N_TOK // tl,),
        in_specs=[cur, cur, cur, prv, nxt, prv, nxt,
                  _layer(l, (CONF_K, W_CONF)), _layer(l, (1, W_CONF)), _layer(l, (1, W_CONF)),
                  _layer(l, (1, W_CONF)), _layer(l, (SC_K, W_SC))],
        out_specs=[cur, cur],
        out_shape=[jax.ShapeDtypeStruct((N_TOK, W_CONF), BF), jax.ShapeDtypeStruct((N_TOK, W_SC), BF)],
        scratch_shapes=_conv_scratch(),
        compiler_params=_params(1),
        name="dwconv",
    )(u0, gcx, gb, u0, u0, gcx, gcx, *conv_w)


def _fourier_body(fn_ref, gd_ref, cs_ref, *rest, seq, group, scale, conv_rows):
    if conv_rows:
        (u_ref, c_ref, gb_ref), conv_w, (out_ref, uo_ref, so_ref, rhs, *conv_scratch) = (
            rest[:3], rest[3:8], rest[8:])
    else:
        out_ref, rhs = rest

    @pl.when(pl.program_id(1) == 0)
    def _():
        for b in range(group):
            v = _dot(fn_ref[b * seq:(b + 1) * seq, :], gd_ref[...])
            rhs[0:seq, b * W_FN:(b + 1) * W_FN] = v[:, 0:W_FN].astype(BF)
            rhs[seq:2 * seq, b * W_FN:(b + 1) * W_FN] = (-v[:, W_FN:]).astype(BF)

    res = _dot(cs_ref[...], rhs[...]) * scale
    for b in range(group):
        out_ref[b] = res[:, b * W_FN:(b + 1) * W_FN].astype(BF)

    if conv_rows:
        is_latent = pl.program_id(1) * conv_rows >= N_PROMPT
        _conv_tile(conv_rows, u_ref, c_ref, gb_ref, None, is_latent, False, False, conv_w, uo_ref, so_ref,
                   conv_scratch)


def _fourier_call(fn, gd, cs, seq, n_seq, row0, tl, group, conv=None):
    n_t = seq // tl
    g0 = row0 // (group * seq)
    n_steps = (n_seq // group) * n_t
    conv_rows = 0
    extra_in, extra_out, extra_shape, extra_scratch, extra_args = [], [], [], [], []
    if conv is not None:
        l, u0, gcx, gb, conv_w = conv
        conv_rows = N_TOK // n_steps
        assert n_seq == group and conv_rows % DEC_SEQ == 0
        blk = pl.BlockSpec((conv_rows, W_CONF), lambda g, j: (j, 0))
        extra_in = [blk, blk, blk, _layer(l, (CONF_K, W_CONF)), _layer(l, (1, W_CONF)),
                    _layer(l, (1, W_CONF)), _layer(l, (1, W_CONF)), _layer(l, (SC_K, W_SC))]
        extra_out = [blk, blk]
        extra_shape = [jax.ShapeDtypeStruct((N_TOK, W_CONF), BF), jax.ShapeDtypeStruct((N_TOK, W_SC), BF)]
        extra_scratch = _conv_scratch()
        extra_args = [u0, gcx, gb, *conv_w]
    body = functools.partial(_fourier_body, seq=seq, group=group, scale=float((seq * FN_GROUP_W) ** -0.5),
                             conv_rows=conv_rows)
    outs = pl.pallas_call(
        body,
        grid=(n_seq // group, n_t),
        in_specs=[
            pl.BlockSpec((group * seq, W_FN), lambda g, j: (g0 + g, 0)),
            _full((W_FN, 2 * W_FN)),
            pl.BlockSpec((tl, 2 * seq), lambda g, j: (j, 0)),
            *extra_in,
        ],
        out_specs=[pl.BlockSpec((group, tl, W_FN), lambda g, j: (g, j, 0)), *extra_out],
        out_shape=[jax.ShapeDtypeStruct((n_seq, seq, W_FN), BF), *extra_shape],
        scratch_shapes=[pltpu.VMEM((2 * seq, group * W_FN), BF), *extra_scratch],
        compiler_params=_params(2),
        name=f"fourier_{seq}",
    )(fn, gd, cs, *extra_args)
    f = outs[0].reshape(n_seq * seq, W_FN)
    return f if conv is None else (f, outs[1], outs[2])


def _dft_cos_sin(n):
    r = np.arange(n, dtype=np.int64)
    ang = ((r[:, None] * r[None, :]) % n).astype(np.float64) * (2.0 * np.pi / n)
    return np.cos(ang).astype(np.float32), np.sin(ang).astype(np.float32)


def _stage3_body(*refs, n_prompt_tiles, tiles_per_seq, n_x, final, chunk):
    x_refs, refs = _take(refs, n_x)
    (op_ref, os_ref, u_ref, s_ref, fp_ref, fs_ref, gate_ref, mod_ref, g2_ref, gfin_ref), refs = _take(refs, 10)
    (wo_ref, wpw_ref, wsco_ref, wfn_ref, wout_ref, wg_ref, wu_ref, wd_ref), refs = _take(refs, 8)
    if final:
        out_refs = refs
    else:
        (next_w_ref,), (x_out_ref, next_w_out_ref) = _take(refs, 1)
        next_w_out_ref[...] = next_w_ref[...].astype(BF)
        out_refs = (x_out_ref,)

    i = pl.program_id(0)
    is_sample = i >= n_prompt_tiles
    row = _mod_row(i, n_prompt_tiles, tiles_per_seq)
    mod = lambda k: mod_ref[0, pl.ds(row, 1), k * D:(k + 1) * D]
    pick = lambda p_ref, s_ref: jnp.where(is_sample, s_ref[...], p_ref[...])
    merged = gate_ref[:, 0:D].astype(F32) * _dot(pick(op_ref, os_ref), wo_ref[...])
    merged = merged + gate_ref[:, D:2 * D].astype(F32) * _dot(u_ref[...], wpw_ref[...])
    merged = merged + gate_ref[:, 2 * D:3 * D].astype(F32) * _dot(s_ref[...], wsco_ref[...])
    merged = merged + gate_ref[:, 3 * D:4 * D].astype(F32) * _dot(pick(fp_ref, fs_ref), wfn_ref[...])
    x1 = _read_x(x_refs, is_sample) + mod(2) * _dot(merged.astype(BF), wout_ref[...])

    h2 = (_rms(x1, g2_ref[0]) * (1.0 + mod(4)) + mod(3)).astype(BF)
    bounds = list(range(0, FF_HIDDEN, chunk)) + [FF_HIDDEN]
    spans = list(zip(bounds[:-1], bounds[1:]))

    def gate_up(span):
        c0, c1 = span
        return _dot(h2, wg_ref[:, c0:c1]), _dot(h2, wu_ref[:, c0:c1])

    acc = None
    ab_next = gate_up(spans[0])
    for n, (c0, c1) in enumerate(spans):
        a, b = ab_next
        if n + 1 < len(spans):
            ab_next = gate_up(spans[n + 1])
        t = _dot((a * _sigmoid(a) * b).astype(BF), wd_ref[c0:c1, :])
        acc = t if acc is None else acc + t
    x2 = x1 + mod(5) * acc
    if not final:
        out_refs[0][...] = x2
        return
    y = _rms(x2, gfin_ref[...])

    @pl.when(is_sample)
    def _():
        out_refs[1][...] = y

    @pl.when(jnp.logical_not(is_sample))
    def _():
        out_refs[0][...] = y


def _stage3_call(l, x, o_p, o_s, u, s, f_p, f_s, gates, mod_l, g2, gfin, w3, w_in_t_f32, tm, final):
    n_tiles = N_TOK // tm
    row_blk = lambda w: pl.BlockSpec((tm, w), lambda i: (i, 0))
    body = functools.partial(_stage3_body, n_prompt_tiles=N_PROMPT // tm, tiles_per_seq=DEC_SEQ // tm,
                             n_x=len(x), final=final, chunk=256)
    if final:
        extra_in, extra_args = [], []
        out_specs = list(_split_specs(tm, D))
        out_shape = [jax.ShapeDtypeStruct((N_PROMPT, D), F32), jax.ShapeDtypeStruct((N_SAMPLE, D), F32)]
    else:
        slab = -(-IN_COLS // (n_tiles * BF16_ROWS)) * BF16_ROWS
        extra_in = [pl.BlockSpec((1, slab, D), lambda i: (l + 1, i, 0))]
        extra_args = [w_in_t_f32]
        out_specs = [row_blk(D), pl.BlockSpec((1, slab, D), lambda i: (0, i, 0))]
        out_shape = [jax.ShapeDtypeStruct((N_TOK, D), F32), jax.ShapeDtypeStruct((1, IN_COLS, D), BF)]
    return pl.pallas_call(
        body,
        grid=(n_tiles,),
        in_specs=[
            *_x_specs(x, tm), *_split_specs(tm, N_HEADS * V_HEAD), row_blk(W_CONF), row_blk(W_SC),
            *_split_specs(tm, W_FN),
            row_blk(N_BRANCH * D), _layer(l, (8, 6 * D)), _layer(l, (1, D)), _full((1, D)),
            _full((N_HEADS * V_HEAD, D)), _full((W_CONF, D)), _full((W_SC, D)), _full((W_FN, D)),
            _full((D, D)), _full((D, FF_HIDDEN)), _full((D, FF_HIDDEN)), _full((FF_HIDDEN, D)),
            *extra_in,
        ],
        out_specs=out_specs,
        out_shape=out_shape,
        compiler_params=_params(1),
        name="stage3",
    )(*x, o_p, o_s, u, s, f_p, f_s, gates, mod_l, g2, gfin, *w3, *extra_args)


def _rope_table(tm):
    rows = DEC_SEQ // GRID_W
    row_pos = np.repeat(np.arange(rows, dtype=np.float64), GRID_W)
    col_pos = np.tile(np.arange(GRID_W, dtype=np.float64), rows)
    inv = ROPE_THETA ** (-np.arange(0, AXIS_ROPE, 2, dtype=np.float64) / AXIS_ROPE)
    ang = np.concatenate([row_pos[:, None] * inv, col_pos[:, None] * inv], axis=1)
    half = QK_ROPE // 2
    tab = np.zeros((tm + DEC_SEQ, 3 * LANE), np.float64)
    tab[:, 0:LANE] = 1.0
    tab[tm:, ROPE_LANE0:ROPE_LANE0 + half] = np.cos(ang)
    tab[tm:, ROPE_LANE0 + half:ROPE_LANE0 + QK_ROPE] = np.cos(ang)
    tab[tm:, LANE + ROPE_LANE0:LANE + ROPE_LANE0 + half] = -np.sin(ang)
    tab[tm:, 2 * LANE + ROPE_LANE0 + half:2 * LANE + ROPE_LANE0 + QK_ROPE] = np.sin(ang)
    return jnp.asarray(tab.astype(np.float32))


def _head_layout_body(wqb_ref, wkvb_ref, pq_ref, pkv_ref, qb_ref, kvb_ref):
    qb_ref[0] = _dot(wqb_ref[0].astype(BF), pq_ref[...]).astype(BF)
    kvb_ref[0] = _dot(wkvb_ref[0].astype(BF), pkv_ref[...]).astype(BF)


def _head_layout_call(w_qb, w_kvb):
    dqk = QK_NOPE + QK_ROPE
    pq = np.zeros((N_HEADS * dqk, N_HEADS * LANE), np.float32)
    pkv = np.zeros((N_HEADS * LANE, 2 * N_HEADS * LANE), np.float32)
    for h in range(N_HEADS):
        src = h * dqk + np.concatenate([np.arange(QK_NOPE), QK_NOPE + ROPE_PERM])
        pq[src, h * LANE + np.arange(dqk)] = 1.0
        nope = np.arange(QK_NOPE)
        pkv[h * LANE + nope, h * LANE + nope] = 1.0
        val = np.arange(V_HEAD)
        pkv[h * LANE + QK_NOPE + val, (N_HEADS + h) * LANE + (h % 2) * V_HEAD + val] = 1.0
    return pl.pallas_call(
        _head_layout_body,
        grid=(DEPTH,),
        in_specs=[pl.BlockSpec((1, Q_LORA, N_HEADS * dqk), lambda l: (l, 0, 0)),
                  pl.BlockSpec((1, KV_LORA, N_HEADS * LANE), lambda l: (l, 0, 0)),
                  _full(pq.shape), _full(pkv.shape)],
        out_specs=[pl.BlockSpec((1, Q_LORA, N_HEADS * LANE), lambda l: (l, 0, 0)),
                   pl.BlockSpec((1, KV_LORA, 2 * N_HEADS * LANE), lambda l: (l, 0, 0))],
        out_shape=[jax.ShapeDtypeStruct((DEPTH, Q_LORA, N_HEADS * LANE), BF),
                   jax.ShapeDtypeStruct((DEPTH, KV_LORA, 2 * N_HEADS * LANE), BF)],
        compiler_params=_params(1),
        name="head_layout",
    )(w_qb, w_kvb, jnp.asarray(pq, BF), jnp.asarray(pkv, BF))


def kernel(x_prompt, x_sample, cache_ckv, cache_krope, c, c_ctx, w_ada, b_ada, g_norm1, g_norm2, w_in, g_qa, w_qb, g_kva, w_kvb, w_o_mla, w_conf_dw, b_conf_dw, g_conf_ln, b_conf_ln, w_conf_pw, w_sc_conv, w_sc_out, w_fn, w_out, w_ffn_gate, w_ffn_up, w_ffn_down, g_final):
    tm = 512
    xp = x_prompt.reshape(N_PROMPT, D)
    xs = x_sample.reshape(N_SAMPLE, D)
    cvec = jnp.concatenate([c_ctx[None, :], c, jnp.zeros((8 - 1 - DEC_BATCH, D), F32)], axis=0)
    mod = _ada_call(cvec, w_ada, b_ada)

    rope_tab = _rope_table(tm)
    place = np.zeros((QK_ROPE, LANE), np.float32)
    place[ROPE_PERM, ROPE_LANE0 + np.arange(QK_ROPE)] = 1.0
    wqb, wkvb_all = _head_layout_call(w_qb, w_kvb)
    kc, vc = _ctx_call(cache_ckv, cache_krope, wkvb_all, jnp.asarray(place, BF))

    cg, sg = _dft_cos_sin(FN_GROUP_W)
    eye = np.eye(FN_GROUPS, dtype=np.float32)
    gd = jnp.asarray(np.concatenate([np.kron(eye, cg), np.kron(eye, sg)], axis=1)).astype(BF)
    cs_p = jnp.asarray(np.concatenate(_dft_cos_sin(SEQ), axis=1)).astype(BF)
    cs_s = jnp.asarray(np.concatenate(_dft_cos_sin(DEC_SEQ), axis=1)).astype(BF)

    w_in_t_f32 = jnp.swapaxes(w_in, 1, 2)
    w_in_t = w_in_t_f32[0:1].astype(BF)
    w3_f32 = (w_o_mla, w_conf_pw, w_sc_out, w_fn, w_out, w_ffn_gate, w_ffn_up, w_ffn_down)

    rows = lambda a: a[:, None, :]
    conv_w = (w_conf_dw, rows(b_conf_dw), rows(g_conf_ln), rows(b_conf_ln), w_sc_conv)
    x = (xp, xs)
    new_ckv, new_krope = [], []
    for l in range(DEPTH):
        (q, k, v, ckv, kr, u0, gcx, gb, fn, gates), w3 = _stage1_call(
            l, x, mod, rows(g_norm1), rows(g_qa), rows(g_kva), w_in_t, wqb, wkvb_all, rope_tab,
            w3_f32, tm)
        new_ckv.append(ckv[:N_PROMPT].reshape(BATCH, SEQ, KV_LORA))
        new_krope.append(kr[:N_PROMPT].reshape(BATCH, SEQ, QK_ROPE))

        o_p = _attn_prompt_call(q, k, v)
        o_s = _attn_sample_call(l, q, k, v, kc, vc, 512)
        f_p = _fourier_call(fn, gd, cs_p, SEQ, BATCH, 0, SEQ, 4)
        f_s, u, s = _fourier_call(fn, gd, cs_s, DEC_SEQ, DEC_BATCH, N_PROMPT, 512, DEC_BATCH,
                                  conv=(l, u0, gcx, gb, conv_w))
        out = _stage3_call(l, x, o_p, o_s, u, s, f_p, f_s, gates, mod, rows(g_norm2), g_final[None, :], w3,
                           w_in_t_f32, tm, l == DEPTH - 1)
        if l < DEPTH - 1:
            x, w_in_t = (out[0],), out[1]

    y_prompt, y_sample = out
    return (y_prompt.reshape(BATCH, SEQ, D), y_sample.reshape(DEC_BATCH, DEC_SEQ, D),
            jnp.stack(new_ckv, axis=1), jnp.stack(new_krope, axis=1))
```

```python
import functools

import numpy as np
import jax
import jax.numpy as jnp
from jax import lax
from jax.experimental import pallas as pl
from jax.experimental.pallas import tpu as pltpu

BF = jnp.bfloat16
F32 = jnp.float32

D = 1024
BATCH = 16
SEQ = 256
DEPTH = 2
DEC_BATCH = 2
DEC_SEQ = 2048
PAST_LEN = 256
GRID_W = 64
N_HEADS = 8
QK_NOPE = 64
QK_ROPE = 32
V_HEAD = 64
Q_LORA = 384
KV_LORA = 256
AXIS_ROPE = QK_ROPE // 2
ROPE_THETA = 10000.0
W_CONF = D // 4
CONF_K = 31
W_SC = D // 4
SC_K = 3
W_FN = D // 4
FN_GROUPS = 4
FN_GROUP_W = W_FN // FN_GROUPS
N_BRANCH = 4
FF_HIDDEN = ((8 * D // 3 + 255) // 256) * 256
EPS = 1e-6
OFF_QA = 0
OFF_KVA = OFF_QA + Q_LORA
OFF_CONF = OFF_KVA + KV_LORA + QK_ROPE
OFF_SC = OFF_CONF + 2 * W_CONF
OFF_FN = OFF_SC + 3 * W_SC
OFF_GATE = OFF_FN + W_FN
IN_COLS = OFF_GATE + N_BRANCH * D

N_PROMPT = BATCH * SEQ
N_SAMPLE = DEC_BATCH * DEC_SEQ
N_TOK = N_PROMPT + N_SAMPLE
LANE = 128
SUBLANES = 8
BF16_ROWS = 2 * SUBLANES
HALO = 16
SM_SCALE = float((QK_NOPE + QK_ROPE) ** -0.5)
Q_SCALE = SM_SCALE * float(np.log2(np.e))
VMEM_LIMIT = 56 * 1024 * 1024

QK_END = OFF_KVA + KV_LORA + LANE

ROPE_PERM = np.array(list(range(0, 8)) + list(range(16, 24)) + list(range(8, 16)) + list(range(24, 32)))
ROPE_LANE0 = QK_NOPE


def _dot(a, b):
    return jnp.dot(a, b, preferred_element_type=F32)


def _dot_nt(a, b):
    return lax.dot_general(a, b, (((1,), (1,)), ((), ())), preferred_element_type=F32)


def _sigmoid(x):
    return jax.nn.sigmoid(x)


def _rms(x, g):
    return x * lax.rsqrt(jnp.mean(x * x, axis=-1, keepdims=True) + EPS) * g


def _full(shape):
    return pl.BlockSpec(shape, lambda *_: (0,) * len(shape))


def _layer(l, shape):
    return pl.BlockSpec((1, *shape), lambda *_: (l,) + (0,) * len(shape), pipeline_mode=pl.Buffered(1))


def _params(n_axes):
    return pltpu.CompilerParams(
        dimension_semantics=("arbitrary",) * n_axes, vmem_limit_bytes=VMEM_LIMIT)


def _mod_row(i, n_prompt_tiles, tiles_per_seq):
    return jnp.where(i >= n_prompt_tiles, 1 + (i - n_prompt_tiles) // tiles_per_seq, 0)


def _ada_body(c_ref, w_ref, b_ref, out_ref):
    cv = c_ref[...]
    sc = (cv * _sigmoid(cv)).astype(BF)
    out_ref[0] = _dot(sc, w_ref[0].astype(BF)) + b_ref[0]


def _ada_call(cvec, w_ada, b_ada):
    wc = 2 * D
    return pl.pallas_call(
        _ada_body,
        grid=(DEPTH, 6 * D // wc),
        in_specs=[
            _full((8, D)),
            pl.BlockSpec((1, D, wc), lambda l, j: (l, 0, j)),
            pl.BlockSpec((1, 1, wc), lambda l, j: (l, 0, j)),
        ],
        out_specs=pl.BlockSpec((1, 8, wc), lambda l, j: (l, 0, j)),
        out_shape=jax.ShapeDtypeStruct((DEPTH, 8, 6 * D), F32),
        compiler_params=_params(2),
        name="ada_mod",
    )(cvec, w_ada, b_ada.reshape(DEPTH, 1, 6 * D))


def _ctx_body(ckv_ref, kr_ref, wkvb_ref, place_ref, k_ref, v_ref):
    kv = _dot(ckv_ref[0, 0].astype(BF), wkvb_ref[0])
    krp = _dot(kr_ref[0, 0].astype(BF), place_ref[...])
    for h in range(N_HEADS):
        sl = slice(h * LANE, (h + 1) * LANE)
        k_ref[0, 0, :, sl] = (kv[:, sl] + krp).astype(BF)
    v_ref[0, 0] = (kv[:, N_HEADS * LANE:] + _value_ones_row()).astype(BF)


def _ctx_call(cache_ckv, cache_krope, wkvb_all, place):
    return pl.pallas_call(
        _ctx_body,
        grid=(DEPTH, DEC_BATCH),
        in_specs=[
            pl.BlockSpec((1, 1, PAST_LEN, KV_LORA), lambda l, b: (b, l, 0, 0)),
            pl.BlockSpec((1, 1, PAST_LEN, QK_ROPE), lambda l, b: (b, l, 0, 0)),
            pl.BlockSpec((1, KV_LORA, 2 * N_HEADS * LANE), lambda l, b: (l, 0, 0)),
            _full((QK_ROPE, LANE)),
        ],
        out_specs=[
            pl.BlockSpec((1, 1, PAST_LEN, N_HEADS * LANE), lambda l, b: (l, b, 0, 0)),
            pl.BlockSpec((1, 1, PAST_LEN, N_HEADS * LANE), lambda l, b: (l, b, 0, 0)),
        ],
        out_shape=[
            jax.ShapeDtypeStruct((DEPTH, DEC_BATCH, PAST_LEN, N_HEADS * LANE), BF),
            jax.ShapeDtypeStruct((DEPTH, DEC_BATCH, PAST_LEN, N_HEADS * LANE), BF),
        ],
        compiler_params=_params(2),
        name="ctx_keys",
    )(cache_ckv, cache_krope, wkvb_all, place)


def _take(refs, n):
    return refs[:n], refs[n:]


def _read_x(x_refs, is_sample):
    if len(x_refs) == 1:
        return x_refs[0][...]
    return jnp.where(is_sample, x_refs[1][...], x_refs[0][...])


def _stage1_body(*refs, n_prompt_tiles, tiles_per_seq, n_x, n_cast):
    x_refs, refs = _take(refs, n_x)
    (mod_ref, g1_ref, gqa_ref, gkva_ref, wt_ref, wqb_ref, wkvb_ref, rope_ref), refs = _take(refs, 8)
    cast_in, refs = _take(refs, n_cast)
    (q_ref, k_ref, v_ref, ckv_ref, kr_ref, u0_ref, gcx_ref, gb_ref, fn_ref, gate_ref), cast_out = _take(refs, 10)

    for src, dst in zip(cast_in, cast_out, strict=True):
        dst[...] = src[0].astype(BF)

    i = pl.program_id(0)
    is_sample = i >= n_prompt_tiles
    row = _mod_row(i, n_prompt_tiles, tiles_per_seq)
    sh1 = mod_ref[0, pl.ds(row, 1), 0:D]
    sc1 = mod_ref[0, pl.ds(row, 1), D:2 * D]
    hb = (_rms(_read_x(x_refs, is_sample), g1_ref[0]) * (1.0 + sc1) + sh1).astype(BF)

    def proj(c0, c1):
        return _dot_nt(hb, wt_ref[0, c0:c1, :])

    pqk = proj(OFF_QA, QK_END)
    q = _dot(_rms(pqk[:, 0:Q_LORA], gqa_ref[0]).astype(BF), wqb_ref[0])

    ckv = _rms(pqk[:, OFF_KVA:OFF_KVA + KV_LORA], gkva_ref[0])
    ckv_ref[...] = ckv
    kr3 = pqk[:, OFF_KVA + KV_LORA:QK_END]
    kr_ref[...] = kr3[:, 0:QK_ROPE]
    lane = lax.broadcasted_iota(jnp.int32, (1, LANE), 1)
    group = lambda k: (lane >= ROPE_LANE0 + 8 * k) & (lane < ROPE_LANE0 + 8 * (k + 1))
    krm = jnp.where(group(0) | group(3), pltpu.roll(kr3, ROPE_LANE0, 1),
                    jnp.where(group(1), pltpu.roll(kr3, ROPE_LANE0 - 8, 1),
                              jnp.where(group(2), pltpu.roll(kr3, ROPE_LANE0 + 8, 1), 0.0)))
    kv = _dot(ckv.astype(BF), wkvb_ref[0])
    v_ref[...] = (kv[:, N_HEADS * LANE:] + _value_ones_row()).astype(BF)

    def rope(t):
        return (t * rope_ref[:, 0:LANE]
                + pltpu.roll(t, LANE - 16, 1) * rope_ref[:, LANE:2 * LANE]
                + pltpu.roll(t, 16, 1) * rope_ref[:, 2 * LANE:3 * LANE])

    krr = rope(krm)
    for h in range(N_HEADS):
        sl = slice(h * LANE, (h + 1) * LANE)
        q_ref[:, sl] = (rope(q[:, sl]) * Q_SCALE).astype(BF)
        k_ref[:, sl] = (kv[:, sl] + krr).astype(BF)

    for j in range(N_BRANCH):
        gate_ref[:, j * D:(j + 1) * D] = _sigmoid(
            proj(OFF_GATE + j * D, OFF_GATE + (j + 1) * D)).astype(BF)

    pc = proj(OFF_CONF, OFF_SC)
    u0_ref[...] = pc[:, 0:W_CONF] * _sigmoid(pc[:, W_CONF:])
    ps = proj(OFF_SC, OFF_FN)
    gb_ref[...] = ps[:, 0:W_SC]
    gcx_ref[...] = ps[:, W_SC:2 * W_SC] * ps[:, 2 * W_SC:]
    fn_ref[...] = proj(OFF_FN, OFF_GATE).astype(BF)


def _split_specs(tm, w):
    n_p = N_PROMPT // tm
    return (pl.BlockSpec((tm, w), lambda i: (jnp.minimum(i, n_p - 1), 0)),
            pl.BlockSpec((tm, w), lambda i: (jnp.maximum(i - n_p, 0), 0)))


def _x_specs(x, tm):
    return list(_split_specs(tm, D)) if len(x) == 2 else [pl.BlockSpec((tm, D), lambda i: (i, 0))]


def _stage1_call(l, x, mod_l, g1, gqa, gkva, w_in_t, wqb, wkvb, rope_tab, to_cast, tm):
    n_prompt_tiles = N_PROMPT // tm
    tiles_per_seq = DEC_SEQ // tm
    row_blk = lambda w: pl.BlockSpec((tm, w), lambda i: (i, 0))
    n_steps = N_TOK // tm
    body = functools.partial(_stage1_body, n_prompt_tiles=n_prompt_tiles, tiles_per_seq=tiles_per_seq,
                             n_x=len(x), n_cast=len(to_cast))
    slab = lambda w: w.shape[1] // n_steps
    assert all(w.shape[1] % (BF16_ROWS * n_steps) == 0 for w in to_cast)
    outs = pl.pallas_call(
        body,
        grid=(n_steps,),
        in_specs=[
            *_x_specs(x, tm),
            _layer(l, (8, 6 * D)),
            _layer(l, (1, D)),
            _layer(l, (1, Q_LORA)),
            _layer(l, (1, KV_LORA)),
            _layer(0, (IN_COLS, D)),
            _layer(l, (Q_LORA, N_HEADS * LANE)),
            _layer(l, (KV_LORA, 2 * N_HEADS * LANE)),
            pl.BlockSpec((tm, 3 * LANE),
                         lambda i: (jnp.where(i < n_prompt_tiles, 0,
                                              1 + (i - n_prompt_tiles) % tiles_per_seq), 0)),
            *[pl.BlockSpec((1, slab(w), w.shape[2]), lambda i: (l, i, 0)) for w in to_cast],
        ],
        out_specs=[
            row_blk(N_HEADS * LANE), row_blk(N_HEADS * LANE), row_blk(N_HEADS * LANE),
            row_blk(KV_LORA), row_blk(QK_ROPE), row_blk(W_CONF), row_blk(W_SC), row_blk(W_SC),
            row_blk(W_FN), row_blk(N_BRANCH * D),
            *[pl.BlockSpec((slab(w), w.shape[2]), lambda i: (i, 0)) for w in to_cast],
        ],
        out_shape=[
            jax.ShapeDtypeStruct((N_TOK, N_HEADS * LANE), BF),
            jax.ShapeDtypeStruct((N_TOK, N_HEADS * LANE), BF),
            jax.ShapeDtypeStruct((N_TOK, N_HEADS * LANE), BF),
            jax.ShapeDtypeStruct((N_TOK, KV_LORA), F32),
            jax.ShapeDtypeStruct((N_TOK, QK_ROPE), F32),
            jax.ShapeDtypeStruct((N_TOK, W_CONF), F32),
            jax.ShapeDtypeStruct((N_TOK, W_SC), F32),
            jax.ShapeDtypeStruct((N_TOK, W_SC), F32),
            jax.ShapeDtypeStruct((N_TOK, W_FN), BF),
            jax.ShapeDtypeStruct((N_TOK, N_BRANCH * D), BF),
            *[jax.ShapeDtypeStruct(w.shape[1:], BF) for w in to_cast],
        ],
        compiler_params=_params(1),
        name="stage1",
    )(*x, mod_l, g1, gqa, gkva, w_in_t, wqb, wkvb, rope_tab, *to_cast)
    return outs[:10], outs[10:]


def _ones_lane(parity):
    return (1 - parity) * V_HEAD


def _value_ones_row():
    col = lax.broadcasted_iota(jnp.int32, (1, N_HEADS * LANE), 1)
    odd = (col // LANE) % 2
    return (col % LANE == jnp.where(odd == 1, _ones_lane(1), _ones_lane(0))).astype(F32)


def _attend_heads(q_ref, kv_refs, o_ref, ahead):
    lane = lax.broadcasted_iota(jnp.int32, (1, LANE), 1)

    def scores(h):
        sl = slice(h * LANE, (h + 1) * LANE)
        return [_dot_nt(q_ref[:, sl], k_ref[:, sl]) for k_ref, _ in kv_refs]

    pending = [scores(h) for h in range(ahead)]
    for hp in range(N_HEADS // 2):
        vsl = slice(hp * LANE, (hp + 1) * LANE)
        outs = []
        for e in range(2):
            h = 2 * hp + e
            sl = slice(h * LANE, (h + 1) * LANE)
            ss = pending.pop(0)
            if h + ahead < N_HEADS:
                pending.append(scores(h + ahead))
            m = functools.reduce(jnp.maximum, [jnp.max(s, axis=-1, keepdims=True) for s in ss])
            o = None
            for s, (_, v_ref) in zip(ss, kv_refs):
                t = _dot(jnp.exp2(s - m).astype(BF), v_ref[:, sl])
                o = t if o is None else o + t
            ones_lane = _ones_lane(e)
            outs.append(o * (1.0 / o[:, ones_lane:ones_lane + 1]))
        o_ref[:, vsl] = jnp.where(lane < V_HEAD, outs[0], outs[1]).astype(BF)


def _prompt_mix_body(q_ref, k_ref, v_ref, fn_ref, gd_ref, cs_ref, o_ref, f_ref, rhs, *, per_step, scale):
    _fourier_rhs(fn_ref, gd_ref, rhs, SEQ, per_step)
    for b, rows in enumerate(_fourier_rows(cs_ref, rhs, scale, per_step)):
        f_ref[b * SEQ:(b + 1) * SEQ, :] = rows
    for b in range(per_step):
        rows = pl.ds(b * SEQ, SEQ)
        _attend_heads(q_ref.at[rows], [(k_ref.at[rows], v_ref.at[rows])], o_ref.at[rows], ahead=N_HEADS)


def _prompt_mix_call(q, k, v, fn, gd, cs):
    per_step = 4
    blk = lambda w: pl.BlockSpec((per_step * SEQ, w), lambda b: (b, 0))
    body = functools.partial(_prompt_mix_body, per_step=per_step, scale=float((SEQ * FN_GROUP_W) ** -0.5))
    return pl.pallas_call(
        body,
        grid=(BATCH // per_step,),
        in_specs=[blk(N_HEADS * LANE), blk(N_HEADS * LANE), blk(N_HEADS * LANE), blk(W_FN),
                  _full((W_FN, 2 * W_FN)), _full((SEQ, 2 * SEQ))],
        out_specs=[blk(N_HEADS * V_HEAD), blk(W_FN)],
        out_shape=[jax.ShapeDtypeStruct((N_PROMPT, N_HEADS * V_HEAD), BF),
                   jax.ShapeDtypeStruct((N_PROMPT, W_FN), BF)],
        scratch_shapes=[pltpu.VMEM((2 * SEQ, per_step * W_FN), BF)],
        compiler_params=_params(1),
        name="prompt_mix",
    )(q, k, v, fn, gd, cs)


def _attn_sample_body(q_ref, kc_ref, vc_ref, k_ref, v_ref, o_ref):
    _attend_heads(q_ref, [(kc_ref.at[0, 0], vc_ref.at[0, 0]), (k_ref, v_ref)], o_ref, ahead=2)


def _attn_sample_call(l, q, k, v, kc, vc, tq):
    n_q = DEC_SEQ // tq
    q0 = N_PROMPT // tq
    s0 = N_PROMPT // DEC_SEQ
    return pl.pallas_call(
        _attn_sample_body,
        grid=(DEC_BATCH, n_q),
        in_specs=[
            pl.BlockSpec((tq, N_HEADS * LANE), lambda b, j: (q0 + b * n_q + j, 0)),
            pl.BlockSpec((1, 1, PAST_LEN, N_HEADS * LANE), lambda b, j: (l, b, 0, 0)),
            pl.BlockSpec((1, 1, PAST_LEN, N_HEADS * LANE), lambda b, j: (l, b, 0, 0)),
            pl.BlockSpec((DEC_SEQ, N_HEADS * LANE), lambda b, j: (s0 + b, 0)),
            pl.BlockSpec((DEC_SEQ, N_HEADS * LANE), lambda b, j: (s0 + b, 0)),
        ],
        out_specs=pl.BlockSpec((tq, N_HEADS * V_HEAD), lambda b, j: (b * n_q + j, 0)),
        out_shape=jax.ShapeDtypeStruct((N_SAMPLE, N_HEADS * V_HEAD), BF),
        compiler_params=_params(2),
        name="attn_sample",
    )(q, kc, vc, k, v)


CONV_ROWS = 256


def _conv_pass(ubuf, cbuf, gb_ref, wdw_ref, bdw_ref, gln_ref, bln_ref, wsc_ref, uo_ref, so_ref,
               shifted, cshifted):
    tl = CONV_ROWS
    span = shifted.shape[1]
    p0 = HALO - CONF_K // 2
    q0 = HALO - SC_K // 2
    for b in range(1, SUBLANES):
        shifted[b] = ubuf[pl.ds(b, span), :]
    for t in range(SC_K):
        if (q0 + t) % SUBLANES:
            cshifted[t] = cbuf[pl.ds((q0 + t) % SUBLANES, span), :]

    def staged(p, r, n):
        a, b = divmod(p, SUBLANES)
        if b == 0:
            return ubuf[pl.ds(r + SUBLANES * a, n), :]
        return shifted[b, pl.ds(r + SUBLANES * a, n), :]

    def staged_c(t, r, n):
        a, b = divmod(q0 + t, SUBLANES)
        if b == 0:
            return cbuf[pl.ds(r + SUBLANES * a, n), :]
        return cshifted[t, pl.ds(r + SUBLANES * a, n), :]

    rc = 32
    for r in range(0, tl, rc):
        acc = staged(p0, r, rc) * wdw_ref[0, 0:1, :]
        for t in range(1, CONF_K):
            acc = acc + staged(p0 + t, r, rc) * wdw_ref[0, t:t + 1, :]
        acc = acc + bdw_ref[0]
        mu = jnp.mean(acc, axis=-1, keepdims=True)
        cen = acc - mu
        var = jnp.mean(cen * cen, axis=-1, keepdims=True)
        y = cen * lax.rsqrt(var + EPS) * gln_ref[0] + bln_ref[0]
        uo_ref[pl.ds(r, rc), :] = (y * _sigmoid(y)).astype(BF)

        cv = staged_c(0, r, rc) * wsc_ref[0, 0:1, :]
        for t in range(1, SC_K):
            cv = cv + staged_c(t, r, rc) * wsc_ref[0, t:t + 1, :]
        so_ref[pl.ds(r, rc), :] = (gb_ref[pl.ds(r, rc), :] * cv).astype(BF)


def _conv_tile(tm, u_ref, c_ref, gb_ref, halos, is_latent, has_prev, has_next, conv_w, uo_ref, so_ref, scratch):
    ubuf, cbuf, shifted, cshifted = scratch
    n_pass = tm // CONV_ROWS
    zeros = jnp.zeros((HALO, W_CONF), F32)
    for p in range(n_pass):
        r0 = p * CONV_ROWS
        for src, buf, k in ((u_ref, ubuf, 0), (c_ref, cbuf, 2)):
            if p > 0:
                head = jnp.where(is_latent, src[r0 - HALO:r0, :], 0.0)
            else:
                head = zeros if halos is None else jnp.where(has_prev, halos[k][...], 0.0)
            if p < n_pass - 1:
                tail = jnp.where(is_latent, src[r0 + CONV_ROWS:r0 + CONV_ROWS + HALO, :], 0.0)
            else:
                tail = zeros if halos is None else jnp.where(has_next, halos[k + 1][...], 0.0)
            buf[0:HALO, :] = head
            buf[HALO:HALO + CONV_ROWS, :] = src[r0:r0 + CONV_ROWS, :]
            buf[HALO + CONV_ROWS:2 * HALO + CONV_ROWS, :] = tail
        rows = pl.ds(r0, CONV_ROWS)
        _conv_pass(ubuf, cbuf, gb_ref.at[rows], *conv_w, uo_ref.at[rows], so_ref.at[rows], shifted, cshifted)


def _conv_scratch():
    staged = CONV_ROWS + 2 * HALO
    return [pltpu.VMEM((staged, W_CONF), F32), pltpu.VMEM((staged, W_SC), F32),
            pltpu.VMEM((SUBLANES, staged - SUBLANES, W_CONF), F32),
            pltpu.VMEM((SC_K, staged - SUBLANES, W_SC), F32)]


def _conv_body(u_ref, c_ref, gb_ref, up_ref, un_ref, cp_ref, cn_ref, *rest, tl, n_prompt_tiles, tiles_per_seq):
    conv_w, (uo_ref, so_ref, *scratch) = rest[:5], rest[5:]
    i = pl.program_id(0)
    is_latent = i >= n_prompt_tiles
    j = jnp.maximum(i - n_prompt_tiles, 0) % tiles_per_seq
    _conv_tile(tl, u_ref, c_ref, gb_ref, (up_ref, un_ref, cp_ref, cn_ref), is_latent,
               is_latent & (j > 0), is_latent & (j < tiles_per_seq - 1), conv_w, uo_ref, so_ref, scratch)


def _conv_call(l, u0, gcx, gb, conv_w, tl):
    n_halo = N_TOK // HALO
    per_tile = tl // HALO
    cur = pl.BlockSpec((tl, W_CONF), lambda i: (i, 0))
    prv = pl.BlockSpec((HALO, W_CONF), lambda i: (jnp.maximum(i * per_tile - 1, 0), 0))
    nxt = pl.BlockSpec((HALO, W_CONF), lambda i: (jnp.minimum((i + 1) * per_tile, n_halo - 1), 0))
    body = functools.partial(_conv_body, tl=tl, n_prompt_tiles=N_PROMPT // tl, tiles_per_seq=DEC_SEQ // tl)
    return pl.pallas_call(
        body,
        grid=(N_TOK // tl,),
        in_specs=[cur, cur, cur, prv, nxt, prv, nxt,
                  _layer(l, (CONF_K, W_CONF)), _layer(l, (1, W_CONF)), _layer(l, (1, W_CONF)),
                  _layer(l, (1, W_CONF)), _layer(l, (SC_K, W_SC))],
        out_specs=[cur, cur],
        out_shape=[jax.ShapeDtypeStruct((N_TOK, W_CONF), BF), jax.ShapeDtypeStruct((N_TOK, W_SC), BF)],
        scratch_shapes=_conv_scratch(),
        compiler_params=_params(1),
        name="dwconv",
    )(u0, gcx, gb, u0, u0, gcx, gcx, *conv_w)


def _fourier_rhs(fn_ref, gd_ref, rhs, seq, group):
    for b in range(group):
        v = _dot(fn_ref[b * seq:(b + 1) * seq, :], gd_ref[...])
        rhs[0:seq, b * W_FN:(b + 1) * W_FN] = v[:, 0:W_FN].astype(BF)
        rhs[seq:2 * seq, b * W_FN:(b + 1) * W_FN] = (-v[:, W_FN:]).astype(BF)


def _fourier_rows(cs_ref, rhs, scale, group):
    res = _dot(cs_ref[...], rhs[...]) * scale
    return [res[:, b * W_FN:(b + 1) * W_FN].astype(BF) for b in range(group)]


def _fourier_body(fn_ref, gd_ref, cs_ref, *rest, seq, group, scale, conv_rows):
    if conv_rows:
        (u_ref, c_ref, gb_ref), conv_w, (out_ref, uo_ref, so_ref, rhs, *conv_scratch) = (
            rest[:3], rest[3:8], rest[8:])
    else:
        out_ref, rhs = rest

    @pl.when(pl.program_id(1) == 0)
    def _():
        _fourier_rhs(fn_ref, gd_ref, rhs, seq, group)

    for b, rows in enumerate(_fourier_rows(cs_ref, rhs, scale, group)):
        out_ref[b] = rows

    if conv_rows:
        is_latent = pl.program_id(1) * conv_rows >= N_PROMPT
        _conv_tile(conv_rows, u_ref, c_ref, gb_ref, None, is_latent, False, False, conv_w, uo_ref, so_ref,
                   conv_scratch)


def _fourier_call(fn, gd, cs, seq, n_seq, row0, tl, group, conv=None):
    n_t = seq // tl
    g0 = row0 // (group * seq)
    n_steps = (n_seq // group) * n_t
    conv_rows = 0
    extra_in, extra_out, extra_shape, extra_scratch, extra_args = [], [], [], [], []
    if conv is not None:
        l, u0, gcx, gb, conv_w = conv
        conv_rows = N_TOK // n_steps
        assert n_seq == group and conv_rows % DEC_SEQ == 0
        blk = pl.BlockSpec((conv_rows, W_CONF), lambda g, j: (j, 0))
        extra_in = [blk, blk, blk, _layer(l, (CONF_K, W_CONF)), _layer(l, (1, W_CONF)),
                    _layer(l, (1, W_CONF)), _layer(l, (1, W_CONF)), _layer(l, (SC_K, W_SC))]
        extra_out = [blk, blk]
        extra_shape = [jax.ShapeDtypeStruct((N_TOK, W_CONF), BF), jax.ShapeDtypeStruct((N_TOK, W_SC), BF)]
        extra_scratch = _conv_scratch()
        extra_args = [u0, gcx, gb, *conv_w]
    body = functools.partial(_fourier_body, seq=seq, group=group, scale=float((seq * FN_GROUP_W) ** -0.5),
                             conv_rows=conv_rows)
    outs = pl.pallas_call(
        body,
        grid=(n_seq // group, n_t),
        in_specs=[
            pl.BlockSpec((group * seq, W_FN), lambda g, j: (g0 + g, 0)),
            _full((W_FN, 2 * W_FN)),
            pl.BlockSpec((tl, 2 * seq), lambda g, j: (j, 0)),
            *extra_in,
        ],
        out_specs=[pl.BlockSpec((group, tl, W_FN), lambda g, j: (g, j, 0)), *extra_out],
        out_shape=[jax.ShapeDtypeStruct((n_seq, seq, W_FN), BF), *extra_shape],
        scratch_shapes=[pltpu.VMEM((2 * seq, group * W_FN), BF), *extra_scratch],
        compiler_params=_params(2),
        name=f"fourier_{seq}",
    )(fn, gd, cs, *extra_args)
    f = outs[0].reshape(n_seq * seq, W_FN)
    return f if conv is None else (f, outs[1], outs[2])


def _dft_cos_sin(n):
    r = np.arange(n, dtype=np.int64)
    ang = ((r[:, None] * r[None, :]) % n).astype(np.float64) * (2.0 * np.pi / n)
    return np.cos(ang).astype(np.float32), np.sin(ang).astype(np.float32)


def _stage3_body(*refs, n_prompt_tiles, tiles_per_seq, n_x, final, chunk):
    x_refs, refs = _take(refs, n_x)
    (op_ref, os_ref, u_ref, s_ref, fp_ref, fs_ref, gate_ref, mod_ref, g2_ref, gfin_ref), refs = _take(refs, 10)
    (wo_ref, wpw_ref, wsco_ref, wfn_ref, wout_ref, wg_ref, wu_ref, wd_ref), refs = _take(refs, 8)
    if final:
        out_refs = refs
    else:
        (next_w_ref,), (x_out_ref, next_w_out_ref) = _take(refs, 1)
        next_w_out_ref[...] = next_w_ref[...].astype(BF)
        out_refs = (x_out_ref,)

    i = pl.program_id(0)
    is_sample = i >= n_prompt_tiles
    row = _mod_row(i, n_prompt_tiles, tiles_per_seq)
    mod = lambda k: mod_ref[0, pl.ds(row, 1), k * D:(k + 1) * D]
    pick = lambda p_ref, s_ref: jnp.where(is_sample, s_ref[...], p_ref[...])
    merged = gate_ref[:, 0:D].astype(F32) * _dot(pick(op_ref, os_ref), wo_ref[...])
    merged = merged + gate_ref[:, D:2 * D].astype(F32) * _dot(u_ref[...], wpw_ref[...])
    merged = merged + gate_ref[:, 2 * D:3 * D].astype(F32) * _dot(s_ref[...], wsco_ref[...])
    merged = merged + gate_ref[:, 3 * D:4 * D].astype(F32) * _dot(pick(fp_ref, fs_ref), wfn_ref[...])
    x1 = _read_x(x_refs, is_sample) + mod(2) * _dot(merged.astype(BF), wout_ref[...])

    h2 = (_rms(x1, g2_ref[0]) * (1.0 + mod(4)) + mod(3)).astype(BF)
    bounds = list(range(0, FF_HIDDEN, chunk)) + [FF_HIDDEN]
    spans = list(zip(bounds[:-1], bounds[1:]))

    def gate_up(span):
        c0, c1 = span
        return _dot(h2, wg_ref[:, c0:c1]), _dot(h2, wu_ref[:, c0:c1])

    acc = None
    ab_next = gate_up(spans[0])
    for n, (c0, c1) in enumerate(spans):
        a, b = ab_next
        if n + 1 < len(spans):
            ab_next = gate_up(spans[n + 1])
        t = _dot((a * _sigmoid(a) * b).astype(BF), wd_ref[c0:c1, :])
        acc = t if acc is None else acc + t
    x2 = x1 + mod(5) * acc
    if not final:
        out_refs[0][...] = x2
        return
    y = _rms(x2, gfin_ref[...])

    @pl.when(is_sample)
    def _():
        out_refs[1][...] = y

    @pl.when(jnp.logical_not(is_sample))
    def _():
        out_refs[0][...] = y


def _stage3_call(l, x, o_p, o_s, u, s, f_p, f_s, gates, mod_l, g2, gfin, w3, w_in_t_f32, tm, final):
    n_tiles = N_TOK // tm
    row_blk = lambda w: pl.BlockSpec((tm, w), lambda i: (i, 0))
    body = functools.partial(_stage3_body, n_prompt_tiles=N_PROMPT // tm, tiles_per_seq=DEC_SEQ // tm,
                             n_x=len(x), final=final, chunk=256)
    if final:
        extra_in, extra_args = [], []
        out_specs = list(_split_specs(tm, D))
        out_shape = [jax.ShapeDtypeStruct((N_PROMPT, D), F32), jax.ShapeDtypeStruct((N_SAMPLE, D), F32)]
    else:
        slab = -(-IN_COLS // (n_tiles * BF16_ROWS)) * BF16_ROWS
        extra_in = [pl.BlockSpec((1, slab, D), lambda i: (l + 1, i, 0))]
        extra_args = [w_in_t_f32]
        out_specs = [row_blk(D), pl.BlockSpec((1, slab, D), lambda i: (0, i, 0))]
        out_shape = [jax.ShapeDtypeStruct((N_TOK, D), F32), jax.ShapeDtypeStruct((1, IN_COLS, D), BF)]
    return pl.pallas_call(
        body,
        grid=(n_tiles,),
        in_specs=[
            *_x_specs(x, tm), *_split_specs(tm, N_HEADS * V_HEAD), row_blk(W_CONF), row_blk(W_SC),
            *_split_specs(tm, W_FN),
            row_blk(N_BRANCH * D), _layer(l, (8, 6 * D)), _layer(l, (1, D)), _full((1, D)),
            _full((N_HEADS * V_HEAD, D)), _full((W_CONF, D)), _full((W_SC, D)), _full((W_FN, D)),
            _full((D, D)), _full((D, FF_HIDDEN)), _full((D, FF_HIDDEN)), _full((FF_HIDDEN, D)),
            *extra_in,
        ],
        out_specs=out_specs,
        out_shape=out_shape,
        compiler_params=_params(1),
        name="stage3",
    )(*x, o_p, o_s, u, s, f_p, f_s, gates, mod_l, g2, gfin, *w3, *extra_args)


def _rope_table(tm):
    rows = DEC_SEQ // GRID_W
    row_pos = np.repeat(np.arange(rows, dtype=np.float64), GRID_W)
    col_pos = np.tile(np.arange(GRID_W, dtype=np.float64), rows)
    inv = ROPE_THETA ** (-np.arange(0, AXIS_ROPE, 2, dtype=np.float64) / AXIS_ROPE)
    ang = np.concatenate([row_pos[:, None] * inv, col_pos[:, None] * inv], axis=1)
    half = QK_ROPE // 2
    tab = np.zeros((tm + DEC_SEQ, 3 * LANE), np.float64)
    tab[:, 0:LANE] = 1.0
    tab[tm:, ROPE_LANE0:ROPE_LANE0 + half] = np.cos(ang)
    tab[tm:, ROPE_LANE0 + half:ROPE_LANE0 + QK_ROPE] = np.cos(ang)
    tab[tm:, LANE + ROPE_LANE0:LANE + ROPE_LANE0 + half] = -np.sin(ang)
    tab[tm:, 2 * LANE + ROPE_LANE0 + half:2 * LANE + ROPE_LANE0 + QK_ROPE] = np.sin(ang)
    return jnp.asarray(tab.astype(np.float32))


def _head_layout_body(wqb_ref, wkvb_ref, pq_ref, pkv_ref, qb_ref, kvb_ref):
    qb_ref[0] = _dot(wqb_ref[0].astype(BF), pq_ref[...]).astype(BF)
    kvb_ref[0] = _dot(wkvb_ref[0].astype(BF), pkv_ref[...]).astype(BF)


def _head_layout_call(w_qb, w_kvb):
    dqk = QK_NOPE + QK_ROPE
    pq = np.zeros((N_HEADS * dqk, N_HEADS * LANE), np.float32)
    pkv = np.zeros((N_HEADS * LANE, 2 * N_HEADS * LANE), np.float32)
    for h in range(N_HEADS):
        src = h * dqk + np.concatenate([np.arange(QK_NOPE), QK_NOPE + ROPE_PERM])
        pq[src, h * LANE + np.arange(dqk)] = 1.0
        nope = np.arange(QK_NOPE)
        pkv[h * LANE + nope, h * LANE + nope] = 1.0
        val = np.arange(V_HEAD)
        pkv[h * LANE + QK_NOPE + val, (N_HEADS + h) * LANE + (h % 2) * V_HEAD + val] = 1.0
    return pl.pallas_call(
        _head_layout_body,
        grid=(DEPTH,),
        in_specs=[pl.BlockSpec((1, Q_LORA, N_HEADS * dqk), lambda l: (l, 0, 0)),
                  pl.BlockSpec((1, KV_LORA, N_HEADS * LANE), lambda l: (l, 0, 0)),
                  _full(pq.shape), _full(pkv.shape)],
        out_specs=[pl.BlockSpec((1, Q_LORA, N_HEADS * LANE), lambda l: (l, 0, 0)),
                   pl.BlockSpec((1, KV_LORA, 2 * N_HEADS * LANE), lambda l: (l, 0, 0))],
        out_shape=[jax.ShapeDtypeStruct((DEPTH, Q_LORA, N_HEADS * LANE), BF),
                   jax.ShapeDtypeStruct((DEPTH, KV_LORA, 2 * N_HEADS * LANE), BF)],
        compiler_params=_params(1),
        name="head_layout",
    )(w_qb, w_kvb, jnp.asarray(pq, BF), jnp.asarray(pkv, BF))


def kernel(x_prompt, x_sample, cache_ckv, cache_krope, c, c_ctx, w_ada, b_ada, g_norm1, g_norm2, w_in, g_qa, w_qb, g_kva, w_kvb, w_o_mla, w_conf_dw, b_conf_dw, g_conf_ln, b_conf_ln, w_conf_pw, w_sc_conv, w_sc_out, w_fn, w_out, w_ffn_gate, w_ffn_up, w_ffn_down, g_final):
    tm = 512
    xp = x_prompt.reshape(N_PROMPT, D)
    xs = x_sample.reshape(N_SAMPLE, D)
    cvec = jnp.concatenate([c_ctx[None, :], c, jnp.zeros((8 - 1 - DEC_BATCH, D), F32)], axis=0)
    mod = _ada_call(cvec, w_ada, b_ada)

    rope_tab = _rope_table(tm)
    place = np.zeros((QK_ROPE, LANE), np.float32)
    place[ROPE_PERM, ROPE_LANE0 + np.arange(QK_ROPE)] = 1.0
    wqb, wkvb_all = _head_layout_call(w_qb, w_kvb)
    kc, vc = _ctx_call(cache_ckv, cache_krope, wkvb_all, jnp.asarray(place, BF))

    cg, sg = _dft_cos_sin(FN_GROUP_W)
    eye = np.eye(FN_GROUPS, dtype=np.float32)
    gd = jnp.asarray(np.concatenate([np.kron(eye, cg), np.kron(eye, sg)], axis=1)).astype(BF)
    cs_p = jnp.asarray(np.concatenate(_dft_cos_sin(SEQ), axis=1)).astype(BF)
    cs_s = jnp.asarray(np.concatenate(_dft_cos_sin(DEC_SEQ), axis=1)).astype(BF)

    w_in_t_f32 = jnp.swapaxes(w_in, 1, 2)
    w_in_t = w_in_t_f32[0:1].astype(BF)
    w3_f32 = (w_o_mla, w_conf_pw, w_sc_out, w_fn, w_out, w_ffn_gate, w_ffn_up, w_ffn_down)

    rows = lambda a: a[:, None, :]
    conv_w = (w_conf_dw, rows(b_conf_dw), rows(g_conf_ln), rows(b_conf_ln), w_sc_conv)
    x = (xp, xs)
    new_ckv, new_krope = [], []
    for l in range(DEPTH):
        (q, k, v, ckv, kr, u0, gcx, gb, fn, gates), w3 = _stage1_call(
            l, x, mod, rows(g_norm1), rows(g_qa), rows(g_kva), w_in_t, wqb, wkvb_all, rope_tab,
            w3_f32, tm)
        new_ckv.append(ckv[:N_PROMPT].reshape(BATCH, SEQ, KV_LORA))
        new_krope.append(kr[:N_PROMPT].reshape(BATCH, SEQ, QK_ROPE))

        o_p, f_p = _prompt_mix_call(q, k, v, fn, gd, cs_p)
        o_s = _attn_sample_call(l, q, k, v, kc, vc, 512)
        f_s, u, s = _fourier_call(fn, gd, cs_s, DEC_SEQ, DEC_BATCH, N_PROMPT, 512, DEC_BATCH,
                                  conv=(l, u0, gcx, gb, conv_w))
        out = _stage3_call(l, x, o_p, o_s, u, s, f_p, f_s, gates, mod, rows(g_norm2), g_final[None, :], w3,
                           w_in_t_f32, tm, l == DEPTH - 1)
        if l < DEPTH - 1:
            x, w_in_t = (out[0],), out[1]

    y_prompt, y_sample = out
    return (y_prompt.reshape(BATCH, SEQ, D), y_sample.reshape(DEC_BATCH, DEC_SEQ, D),
            jnp.stack(new_ckv, axis=1), jnp.stack(new_krope, axis=1))
```

```python
import functools

import numpy as np
import jax
import jax.numpy as jnp
from jax import lax
from jax.experimental import pallas as pl
from jax.experimental.pallas import tpu as pltpu

BF = jnp.bfloat16
F32 = jnp.float32

D = 1024
BATCH = 16
SEQ = 256
DEPTH = 2
DEC_BATCH = 2
DEC_SEQ = 2048
PAST_LEN = 256
GRID_W = 64
N_HEADS = 8
QK_NOPE = 64
QK_ROPE = 32
V_HEAD = 64
Q_LORA = 384
KV_LORA = 256
AXIS_ROPE = QK_ROPE // 2
ROPE_THETA = 10000.0
W_CONF = D // 4
CONF_K = 31
W_SC = D // 4
SC_K = 3
W_FN = D // 4
FN_GROUPS = 4
FN_GROUP_W = W_FN // FN_GROUPS
N_BRANCH = 4
FF_HIDDEN = ((8 * D // 3 + 255) // 256) * 256
EPS = 1e-6
OFF_QA = 0
OFF_KVA = OFF_QA + Q_LORA
OFF_CONF = OFF_KVA + KV_LORA + QK_ROPE
OFF_SC = OFF_CONF + 2 * W_CONF
OFF_FN = OFF_SC + 3 * W_SC
OFF_GATE = OFF_FN + W_FN
IN_COLS = OFF_GATE + N_BRANCH * D

N_PROMPT = BATCH * SEQ
N_SAMPLE = DEC_BATCH * DEC_SEQ
N_TOK = N_PROMPT + N_SAMPLE
LANE = 128
SUBLANES = 8
BF16_ROWS = 2 * SUBLANES
HALO = 16
SM_SCALE = float((QK_NOPE + QK_ROPE) ** -0.5)
Q_SCALE = SM_SCALE * float(np.log2(np.e))
VMEM_LIMIT = 56 * 1024 * 1024

QK_END = OFF_KVA + KV_LORA + LANE

ROPE_PERM = np.array(list(range(0, 8)) + list(range(16, 24)) + list(range(8, 16)) + list(range(24, 32)))
ROPE_LANE0 = QK_NOPE


def _dot(a, b):
    return jnp.dot(a, b, preferred_element_type=F32)


def _dot_nt(a, b):
    return lax.dot_general(a, b, (((1,), (1,)), ((), ())), preferred_element_type=F32)


def _sigmoid(x):
    return jax.nn.sigmoid(x)


def _rms(x, g):
    return x * lax.rsqrt(jnp.mean(x * x, axis=-1, keepdims=True) + EPS) * g


def _full(shape):
    return pl.BlockSpec(shape, lambda *_: (0,) * len(shape))


def _layer(l, shape):
    return pl.BlockSpec((1, *shape), lambda *_: (l,) + (0,) * len(shape), pipeline_mode=pl.Buffered(1))


def _params(n_axes):
    return pltpu.CompilerParams(
        dimension_semantics=("arbitrary",) * n_axes, vmem_limit_bytes=VMEM_LIMIT)


def _mod_row(i, n_prompt_tiles, tiles_per_seq):
    return jnp.where(i >= n_prompt_tiles, 1 + (i - n_prompt_tiles) // tiles_per_seq, 0)


def _ada_body(c_ref, w_ref, b_ref, out_ref):
    cv = c_ref[...]
    sc = (cv * _sigmoid(cv)).astype(BF)
    out_ref[0] = _dot(sc, w_ref[0].astype(BF)) + b_ref[0]


def _ada_call(cvec, w_ada, b_ada):
    wc = 2 * D
    return pl.pallas_call(
        _ada_body,
        grid=(DEPTH, 6 * D // wc),
        in_specs=[
            _full((8, D)),
            pl.BlockSpec((1, D, wc), lambda l, j: (l, 0, j)),
            pl.BlockSpec((1, 1, wc), lambda l, j: (l, 0, j)),
        ],
        out_specs=pl.BlockSpec((1, 8, wc), lambda l, j: (l, 0, j)),
        out_shape=jax.ShapeDtypeStruct((DEPTH, 8, 6 * D), F32),
        compiler_params=_params(2),
        name="ada_mod",
    )(cvec, w_ada, b_ada.reshape(DEPTH, 1, 6 * D))


def _ctx_body(ckv_ref, kr_ref, wkvb_ref, place_ref, k_ref, v_ref):
    kv = _dot(ckv_ref[0, 0].astype(BF), wkvb_ref[0])
    krp = _dot(kr_ref[0, 0].astype(BF), place_ref[...])
    for h in range(N_HEADS):
        sl = slice(h * LANE, (h + 1) * LANE)
        k_ref[0, 0, :, sl] = (kv[:, sl] + krp).astype(BF)
    v_ref[0, 0] = (kv[:, N_HEADS * LANE:] + _value_ones_row()).astype(BF)


def _ctx_call(cache_ckv, cache_krope, wkvb_all, place):
    return pl.pallas_call(
        _ctx_body,
        grid=(DEPTH, DEC_BATCH),
        in_specs=[
            pl.BlockSpec((1, 1, PAST_LEN, KV_LORA), lambda l, b: (b, l, 0, 0)),
            pl.BlockSpec((1, 1, PAST_LEN, QK_ROPE), lambda l, b: (b, l, 0, 0)),
            pl.BlockSpec((1, KV_LORA, 2 * N_HEADS * LANE), lambda l, b: (l, 0, 0)),
            _full((QK_ROPE, LANE)),
        ],
        out_specs=[
            pl.BlockSpec((1, 1, PAST_LEN, N_HEADS * LANE), lambda l, b: (l, b, 0, 0)),
            pl.BlockSpec((1, 1, PAST_LEN, N_HEADS * LANE), lambda l, b: (l, b, 0, 0)),
        ],
        out_shape=[
            jax.ShapeDtypeStruct((DEPTH, DEC_BATCH, PAST_LEN, N_HEADS * LANE), BF),
            jax.ShapeDtypeStruct((DEPTH, DEC_BATCH, PAST_LEN, N_HEADS * LANE), BF),
        ],
        compiler_params=_params(2),
        name="ctx_keys",
    )(cache_ckv, cache_krope, wkvb_all, place)


def _take(refs, n):
    return refs[:n], refs[n:]


def _read_x(x_refs, is_sample):
    if len(x_refs) == 1:
        return x_refs[0][...]
    return jnp.where(is_sample, x_refs[1][...], x_refs[0][...])


def _stage1_body(*refs, layer, n_prompt_tiles, tiles_per_seq, n_x, n_cast):
    x_refs, refs = _take(refs, n_x)
    (mod_ref, g1_ref, gqa_ref, gkva_ref, wt_ref, wqb_ref, wkvb_ref, rope_ref), refs = _take(refs, 8)
    cast_in, refs = _take(refs, n_cast)
    (q_ref, k_ref, v_ref, ckv_ref, kr_ref, u0_ref, gcx_ref, gb_ref, fn_ref, gate_ref), cast_out = _take(refs, 10)

    for src, dst in zip(cast_in, cast_out, strict=True):
        dst[...] = src[0].astype(BF)

    i = pl.program_id(0)
    is_sample = i >= n_prompt_tiles
    row = _mod_row(i, n_prompt_tiles, tiles_per_seq)
    sh1 = mod_ref[0, pl.ds(row, 1), 0:D]
    sc1 = mod_ref[0, pl.ds(row, 1), D:2 * D]
    hb = (_rms(_read_x(x_refs, is_sample), _layer_row(g1_ref, layer)[...]) * (1.0 + sc1) + sh1).astype(BF)

    def proj(c0, c1):
        return _dot_nt(hb, wt_ref[0, c0:c1, :])

    pqk = proj(OFF_QA, QK_END)
    q = _dot(_rms(pqk[:, 0:Q_LORA], _layer_row(gqa_ref, layer)[...]).astype(BF), wqb_ref[0])

    ckv = _rms(pqk[:, OFF_KVA:OFF_KVA + KV_LORA], _layer_row(gkva_ref, layer)[...])
    ckv_ref[...] = ckv
    kr3 = pqk[:, OFF_KVA + KV_LORA:QK_END]
    kr_ref[...] = kr3[:, 0:QK_ROPE]
    lane = lax.broadcasted_iota(jnp.int32, (1, LANE), 1)
    group = lambda k: (lane >= ROPE_LANE0 + 8 * k) & (lane < ROPE_LANE0 + 8 * (k + 1))
    krm = jnp.where(group(0) | group(3), pltpu.roll(kr3, ROPE_LANE0, 1),
                    jnp.where(group(1), pltpu.roll(kr3, ROPE_LANE0 - 8, 1),
                              jnp.where(group(2), pltpu.roll(kr3, ROPE_LANE0 + 8, 1), 0.0)))
    kv = _dot(ckv.astype(BF), wkvb_ref[0])
    v_ref[...] = (kv[:, N_HEADS * LANE:] + _value_ones_row()).astype(BF)

    def rope(t):
        return (t * rope_ref[:, 0:LANE]
                + pltpu.roll(t, LANE - 16, 1) * rope_ref[:, LANE:2 * LANE]
                + pltpu.roll(t, 16, 1) * rope_ref[:, 2 * LANE:3 * LANE])

    krr = rope(krm)
    for h in range(N_HEADS):
        sl = slice(h * LANE, (h + 1) * LANE)
        q_ref[:, sl] = (rope(q[:, sl]) * Q_SCALE).astype(BF)
        k_ref[:, sl] = (kv[:, sl] + krr).astype(BF)

    for j in range(N_BRANCH):
        gate_ref[:, j * D:(j + 1) * D] = _sigmoid(
            proj(OFF_GATE + j * D, OFF_GATE + (j + 1) * D)).astype(BF)

    pc = proj(OFF_CONF, OFF_SC)
    u0_ref[...] = pc[:, 0:W_CONF] * _sigmoid(pc[:, W_CONF:])
    ps = proj(OFF_SC, OFF_FN)
    gb_ref[...] = ps[:, 0:W_SC]
    gcx_ref[...] = ps[:, W_SC:2 * W_SC] * ps[:, 2 * W_SC:]
    fn_ref[...] = proj(OFF_FN, OFF_GATE).astype(BF)


def _split_specs(tm, w):
    n_p = N_PROMPT // tm
    return (pl.BlockSpec((tm, w), lambda i: (jnp.minimum(i, n_p - 1), 0)),
            pl.BlockSpec((tm, w), lambda i: (jnp.maximum(i - n_p, 0), 0)))


def _x_specs(x, tm):
    return list(_split_specs(tm, D)) if len(x) == 2 else [pl.BlockSpec((tm, D), lambda i: (i, 0))]


def _stage1_call(l, x, mod_l, g1, gqa, gkva, w_in_t, wqb, wkvb, rope_tab, to_cast, tm):
    n_prompt_tiles = N_PROMPT // tm
    tiles_per_seq = DEC_SEQ // tm
    row_blk = lambda w: pl.BlockSpec((tm, w), lambda i: (i, 0))
    n_steps = N_TOK // tm
    body = functools.partial(_stage1_body, layer=l, n_prompt_tiles=n_prompt_tiles, tiles_per_seq=tiles_per_seq,
                             n_x=len(x), n_cast=len(to_cast))
    slab = lambda w: w.shape[1] // n_steps
    assert all(w.shape[1] % (BF16_ROWS * n_steps) == 0 for w in to_cast)
    outs = pl.pallas_call(
        body,
        grid=(n_steps,),
        in_specs=[
            *_x_specs(x, tm),
            _layer(l, (8, 6 * D)),
            _full((DEPTH, D)),
            _full((DEPTH, Q_LORA)),
            _full((DEPTH, KV_LORA)),
            _layer(0, (IN_COLS, D)),
            _layer(l, (Q_LORA, N_HEADS * LANE)),
            _layer(l, (KV_LORA, 2 * N_HEADS * LANE)),
            pl.BlockSpec((tm, 3 * LANE),
                         lambda i: (jnp.where(i < n_prompt_tiles, 0,
                                              1 + (i - n_prompt_tiles) % tiles_per_seq), 0)),
            *[pl.BlockSpec((1, slab(w), w.shape[2]), lambda i: (l, i, 0)) for w in to_cast],
        ],
        out_specs=[
            row_blk(N_HEADS * LANE), row_blk(N_HEADS * LANE), row_blk(N_HEADS * LANE),
            row_blk(KV_LORA), row_blk(QK_ROPE), row_blk(W_CONF), row_blk(W_SC), row_blk(W_SC),
            row_blk(W_FN), row_blk(N_BRANCH * D),
            *[pl.BlockSpec((slab(w), w.shape[2]), lambda i: (i, 0)) for w in to_cast],
        ],
        out_shape=[
            jax.ShapeDtypeStruct((N_TOK, N_HEADS * LANE), BF),
            jax.ShapeDtypeStruct((N_TOK, N_HEADS * LANE), BF),
            jax.ShapeDtypeStruct((N_TOK, N_HEADS * LANE), BF),
            jax.ShapeDtypeStruct((N_TOK, KV_LORA), F32),
            jax.ShapeDtypeStruct((N_TOK, QK_ROPE), F32),
            jax.ShapeDtypeStruct((N_TOK, W_CONF), F32),
            jax.ShapeDtypeStruct((N_TOK, W_SC), F32),
            jax.ShapeDtypeStruct((N_TOK, W_SC), F32),
            jax.ShapeDtypeStruct((N_TOK, W_FN), BF),
            jax.ShapeDtypeStruct((N_TOK, N_BRANCH * D), BF),
            *[jax.ShapeDtypeStruct(w.shape[1:], BF) for w in to_cast],
        ],
        compiler_params=_params(1),
        name="stage1",
    )(*x, mod_l, g1, gqa, gkva, w_in_t, wqb, wkvb, rope_tab, *to_cast)
    return outs[:10], outs[10:]


def _ones_lane(parity):
    return (1 - parity) * V_HEAD


def _value_ones_row():
    col = lax.broadcasted_iota(jnp.int32, (1, N_HEADS * LANE), 1)
    odd = (col // LANE) % 2
    return (col % LANE == jnp.where(odd == 1, _ones_lane(1), _ones_lane(0))).astype(F32)


def _attend_heads(q_ref, kv_refs, o_ref, ahead):
    lane = lax.broadcasted_iota(jnp.int32, (1, LANE), 1)

    def scores(h):
        sl = slice(h * LANE, (h + 1) * LANE)
        return [_dot_nt(q_ref[:, sl], k_ref[:, sl]) for k_ref, _ in kv_refs]

    pending = [scores(h) for h in range(ahead)]
    for hp in range(N_HEADS // 2):
        vsl = slice(hp * LANE, (hp + 1) * LANE)
        outs = []
        for e in range(2):
            h = 2 * hp + e
            sl = slice(h * LANE, (h + 1) * LANE)
            ss = pending.pop(0)
            if h + ahead < N_HEADS:
                pending.append(scores(h + ahead))
            m = functools.reduce(jnp.maximum, [jnp.max(s, axis=-1, keepdims=True) for s in ss])
            o = None
            for s, (_, v_ref) in zip(ss, kv_refs):
                t = _dot(jnp.exp2(s - m).astype(BF), v_ref[:, sl])
                o = t if o is None else o + t
            ones_lane = _ones_lane(e)
            outs.append(o * (1.0 / o[:, ones_lane:ones_lane + 1]))
        o_ref[:, vsl] = jnp.where(lane < V_HEAD, outs[0], outs[1]).astype(BF)


def _prompt_mix_body(q_ref, k_ref, v_ref, fn_ref, gd_ref, cs_ref, o_ref, f_ref, rhs, *, per_step, scale):
    _fourier_rhs(fn_ref, gd_ref, rhs, SEQ, per_step)
    for b, rows in enumerate(_fourier_rows(cs_ref, rhs, scale, per_step)):
        f_ref[b * SEQ:(b + 1) * SEQ, :] = rows
    for b in range(per_step):
        rows = pl.ds(b * SEQ, SEQ)
        _attend_heads(q_ref.at[rows], [(k_ref.at[rows], v_ref.at[rows])], o_ref.at[rows], ahead=N_HEADS)


def _prompt_mix_call(q, k, v, fn, gd, cs):
    per_step = 4
    blk = lambda w: pl.BlockSpec((per_step * SEQ, w), lambda b: (b, 0))
    body = functools.partial(_prompt_mix_body, per_step=per_step, scale=float((SEQ * FN_GROUP_W) ** -0.5))
    return pl.pallas_call(
        body,
        grid=(BATCH // per_step,),
        in_specs=[blk(N_HEADS * LANE), blk(N_HEADS * LANE), blk(N_HEADS * LANE), blk(W_FN),
                  _full((W_FN, 2 * W_FN)), _full((SEQ, 2 * SEQ))],
        out_specs=[blk(N_HEADS * V_HEAD), blk(W_FN)],
        out_shape=[jax.ShapeDtypeStruct((N_PROMPT, N_HEADS * V_HEAD), BF),
                   jax.ShapeDtypeStruct((N_PROMPT, W_FN), BF)],
        scratch_shapes=[pltpu.VMEM((2 * SEQ, per_step * W_FN), BF)],
        compiler_params=_params(1),
        name="prompt_mix",
    )(q, k, v, fn, gd, cs)


def _attn_sample_body(q_ref, kc_ref, vc_ref, k_ref, v_ref, o_ref):
    _attend_heads(q_ref, [(kc_ref.at[0, 0], vc_ref.at[0, 0]), (k_ref, v_ref)], o_ref, ahead=2)


def _attn_sample_call(l, q, k, v, kc, vc, tq):
    n_q = DEC_SEQ // tq
    q0 = N_PROMPT // tq
    s0 = N_PROMPT // DEC_SEQ
    return pl.pallas_call(
        _attn_sample_body,
        grid=(DEC_BATCH, n_q),
        in_specs=[
            pl.BlockSpec((tq, N_HEADS * LANE), lambda b, j: (q0 + b * n_q + j, 0)),
            pl.BlockSpec((1, 1, PAST_LEN, N_HEADS * LANE), lambda b, j: (l, b, 0, 0)),
            pl.BlockSpec((1, 1, PAST_LEN, N_HEADS * LANE), lambda b, j: (l, b, 0, 0)),
            pl.BlockSpec((DEC_SEQ, N_HEADS * LANE), lambda b, j: (s0 + b, 0)),
            pl.BlockSpec((DEC_SEQ, N_HEADS * LANE), lambda b, j: (s0 + b, 0)),
        ],
        out_specs=pl.BlockSpec((tq, N_HEADS * V_HEAD), lambda b, j: (b * n_q + j, 0)),
        out_shape=jax.ShapeDtypeStruct((N_SAMPLE, N_HEADS * V_HEAD), BF),
        compiler_params=_params(2),
        name="attn_sample",
    )(q, kc, vc, k, v)


CONV_ROWS = 256


def _conv_pass(ubuf, cbuf, gb_ref, wdw_ref, bdw_ref, gln_ref, bln_ref, wsc_ref, uo_ref, so_ref,
               shifted, cshifted):
    tl = CONV_ROWS
    span = shifted.shape[1]
    p0 = HALO - CONF_K // 2
    q0 = HALO - SC_K // 2
    for b in range(1, SUBLANES):
        shifted[b] = ubuf[pl.ds(b, span), :]
    for t in range(SC_K):
        if (q0 + t) % SUBLANES:
            cshifted[t] = cbuf[pl.ds((q0 + t) % SUBLANES, span), :]

    def staged(p, r, n):
        a, b = divmod(p, SUBLANES)
        if b == 0:
            return ubuf[pl.ds(r + SUBLANES * a, n), :]
        return shifted[b, pl.ds(r + SUBLANES * a, n), :]

    def staged_c(t, r, n):
        a, b = divmod(q0 + t, SUBLANES)
        if b == 0:
            return cbuf[pl.ds(r + SUBLANES * a, n), :]
        return cshifted[t, pl.ds(r + SUBLANES * a, n), :]

    rc = 32
    for r in range(0, tl, rc):
        acc = staged(p0, r, rc) * wdw_ref[0, 0:1, :]
        for t in range(1, CONF_K):
            acc = acc + staged(p0 + t, r, rc) * wdw_ref[0, t:t + 1, :]
        acc = acc + bdw_ref[...]
        mu = jnp.mean(acc, axis=-1, keepdims=True)
        cen = acc - mu
        var = jnp.mean(cen * cen, axis=-1, keepdims=True)
        y = cen * lax.rsqrt(var + EPS) * gln_ref[...] + bln_ref[...]
        uo_ref[pl.ds(r, rc), :] = (y * _sigmoid(y)).astype(BF)

        cv = staged_c(0, r, rc) * wsc_ref[0, 0:1, :]
        for t in range(1, SC_K):
            cv = cv + staged_c(t, r, rc) * wsc_ref[0, t:t + 1, :]
        so_ref[pl.ds(r, rc), :] = (gb_ref[pl.ds(r, rc), :] * cv).astype(BF)


def _conv_tile(tm, u_ref, c_ref, gb_ref, halos, is_latent, has_prev, has_next, conv_w, uo_ref, so_ref, scratch):
    ubuf, cbuf, shifted, cshifted = scratch
    n_pass = tm // CONV_ROWS
    zeros = jnp.zeros((HALO, W_CONF), F32)
    for p in range(n_pass):
        r0 = p * CONV_ROWS
        for src, buf, k in ((u_ref, ubuf, 0), (c_ref, cbuf, 2)):
            if p > 0:
                head = jnp.where(is_latent, src[r0 - HALO:r0, :], 0.0)
            else:
                head = zeros if halos is None else jnp.where(has_prev, halos[k][...], 0.0)
            if p < n_pass - 1:
                tail = jnp.where(is_latent, src[r0 + CONV_ROWS:r0 + CONV_ROWS + HALO, :], 0.0)
            else:
                tail = zeros if halos is None else jnp.where(has_next, halos[k + 1][...], 0.0)
            buf[0:HALO, :] = head
            buf[HALO:HALO + CONV_ROWS, :] = src[r0:r0 + CONV_ROWS, :]
            buf[HALO + CONV_ROWS:2 * HALO + CONV_ROWS, :] = tail
        rows = pl.ds(r0, CONV_ROWS)
        _conv_pass(ubuf, cbuf, gb_ref.at[rows], *conv_w, uo_ref.at[rows], so_ref.at[rows], shifted, cshifted)


def _conv_scratch():
    staged = CONV_ROWS + 2 * HALO
    return [pltpu.VMEM((staged, W_CONF), F32), pltpu.VMEM((staged, W_SC), F32),
            pltpu.VMEM((SUBLANES, staged - SUBLANES, W_CONF), F32),
            pltpu.VMEM((SC_K, staged - SUBLANES, W_SC), F32)]


def _fourier_rhs(fn_ref, gd_ref, rhs, seq, group):
    for b in range(group):
        v = _dot(fn_ref[b * seq:(b + 1) * seq, :], gd_ref[...])
        rhs[0:seq, b * W_FN:(b + 1) * W_FN] = v[:, 0:W_FN].astype(BF)
        rhs[seq:2 * seq, b * W_FN:(b + 1) * W_FN] = (-v[:, W_FN:]).astype(BF)


def _fourier_rows(cs_ref, rhs, scale, group):
    res = _dot(cs_ref[...], rhs[...]) * scale
    return [res[:, b * W_FN:(b + 1) * W_FN].astype(BF) for b in range(group)]


def _layer_row(ref, layer):
    return ref.at[pl.ds(layer, 1)]


def _fourier_body(fn_ref, gd_ref, cs_ref, *rest, seq, group, scale, conv_rows, layer):
    if conv_rows:
        (u_ref, c_ref, gb_ref), (wdw_ref, bdw_ref, gln_ref, bln_ref, wsc_ref) = rest[:3], rest[3:8]
        conv_w = (wdw_ref, _layer_row(bdw_ref, layer), _layer_row(gln_ref, layer), _layer_row(bln_ref, layer),
                  wsc_ref)
        out_ref, uo_ref, so_ref, rhs, *conv_scratch = rest[8:]
    else:
        out_ref, rhs = rest

    @pl.when(pl.program_id(1) == 0)
    def _():
        _fourier_rhs(fn_ref, gd_ref, rhs, seq, group)

    for b, rows in enumerate(_fourier_rows(cs_ref, rhs, scale, group)):
        out_ref[b] = rows

    if conv_rows:
        is_latent = pl.program_id(1) * conv_rows >= N_PROMPT
        _conv_tile(conv_rows, u_ref, c_ref, gb_ref, None, is_latent, False, False, conv_w, uo_ref, so_ref,
                   conv_scratch)


def _fourier_call(fn, gd, cs, seq, n_seq, row0, tl, group, conv=None):
    n_t = seq // tl
    g0 = row0 // (group * seq)
    n_steps = (n_seq // group) * n_t
    conv_rows, l = 0, 0
    extra_in, extra_out, extra_shape, extra_scratch, extra_args = [], [], [], [], []
    if conv is not None:
        l, u0, gcx, gb, conv_w = conv
        conv_rows = N_TOK // n_steps
        assert n_seq == group and conv_rows % DEC_SEQ == 0
        blk = pl.BlockSpec((conv_rows, W_CONF), lambda g, j: (j, 0))
        extra_in = [blk, blk, blk, _layer(l, (CONF_K, W_CONF)), _full((DEPTH, W_CONF)),
                    _full((DEPTH, W_CONF)), _full((DEPTH, W_CONF)), _layer(l, (SC_K, W_SC))]
        extra_out = [blk, blk]
        extra_shape = [jax.ShapeDtypeStruct((N_TOK, W_CONF), BF), jax.ShapeDtypeStruct((N_TOK, W_SC), BF)]
        extra_scratch = _conv_scratch()
        extra_args = [u0, gcx, gb, *conv_w]
    body = functools.partial(_fourier_body, seq=seq, group=group, scale=float((seq * FN_GROUP_W) ** -0.5),
                             conv_rows=conv_rows, layer=l)
    outs = pl.pallas_call(
        body,
        grid=(n_seq // group, n_t),
        in_specs=[
            pl.BlockSpec((group * seq, W_FN), lambda g, j: (g0 + g, 0)),
            _full((W_FN, 2 * W_FN)),
            pl.BlockSpec((tl, 2 * seq), lambda g, j: (j, 0)),
            *extra_in,
        ],
        out_specs=[pl.BlockSpec((group, tl, W_FN), lambda g, j: (g, j, 0)), *extra_out],
        out_shape=[jax.ShapeDtypeStruct((n_seq, seq, W_FN), BF), *extra_shape],
        scratch_shapes=[pltpu.VMEM((2 * seq, group * W_FN), BF), *extra_scratch],
        compiler_params=_params(2),
        name=f"fourier_{seq}",
    )(fn, gd, cs, *extra_args)
    f = outs[0].reshape(n_seq * seq, W_FN)
    return f if conv is None else (f, outs[1], outs[2])


def _dft_cos_sin(n):
    r = np.arange(n, dtype=np.int64)
    ang = ((r[:, None] * r[None, :]) % n).astype(np.float64) * (2.0 * np.pi / n)
    return np.cos(ang).astype(np.float32), np.sin(ang).astype(np.float32)


def _stage3_body(*refs, layer, n_prompt_tiles, tiles_per_seq, n_x, final, chunk):
    x_refs, refs = _take(refs, n_x)
    (op_ref, os_ref, u_ref, s_ref, fp_ref, fs_ref, gate_ref, mod_ref, g2_ref, gfin_ref), refs = _take(refs, 10)
    (wo_ref, wpw_ref, wsco_ref, wfn_ref, wout_ref, wg_ref, wu_ref, wd_ref), refs = _take(refs, 8)
    if final:
        out_refs = refs
    else:
        (next_w_ref,), (x_out_ref, next_w_out_ref) = _take(refs, 1)
        next_w_out_ref[...] = next_w_ref[...].astype(BF)
        out_refs = (x_out_ref,)

    i = pl.program_id(0)
    is_sample = i >= n_prompt_tiles
    row = _mod_row(i, n_prompt_tiles, tiles_per_seq)
    mod = lambda k: mod_ref[0, pl.ds(row, 1), k * D:(k + 1) * D]
    pick = lambda p_ref, s_ref: jnp.where(is_sample, s_ref[...], p_ref[...])
    merged = gate_ref[:, 0:D].astype(F32) * _dot(pick(op_ref, os_ref), wo_ref[...])
    merged = merged + gate_ref[:, D:2 * D].astype(F32) * _dot(u_ref[...], wpw_ref[...])
    merged = merged + gate_ref[:, 2 * D:3 * D].astype(F32) * _dot(s_ref[...], wsco_ref[...])
    merged = merged + gate_ref[:, 3 * D:4 * D].astype(F32) * _dot(pick(fp_ref, fs_ref), wfn_ref[...])
    x1 = _read_x(x_refs, is_sample) + mod(2) * _dot(merged.astype(BF), wout_ref[...])

    h2 = (_rms(x1, _layer_row(g2_ref, layer)[...]) * (1.0 + mod(4)) + mod(3)).astype(BF)
    bounds = list(range(0, FF_HIDDEN, chunk)) + [FF_HIDDEN]
    spans = list(zip(bounds[:-1], bounds[1:]))

    def gate_up(span):
        c0, c1 = span
        return _dot(h2, wg_ref[:, c0:c1]), _dot(h2, wu_ref[:, c0:c1])

    acc = None
    ab_next = gate_up(spans[0])
    for n, (c0, c1) in enumerate(spans):
        a, b = ab_next
        if n + 1 < len(spans):
            ab_next = gate_up(spans[n + 1])
        t = _dot((a * _sigmoid(a) * b).astype(BF), wd_ref[c0:c1, :])
        acc = t if acc is None else acc + t
    x2 = x1 + mod(5) * acc
    if not final:
        out_refs[0][...] = x2
        return
    y = _rms(x2, gfin_ref[...])

    @pl.when(is_sample)
    def _():
        out_refs[1][...] = y

    @pl.when(jnp.logical_not(is_sample))
    def _():
        out_refs[0][...] = y


def _stage3_call(l, x, o_p, o_s, u, s, f_p, f_s, gates, mod_l, g2, gfin, w3, w_in_t_f32, tm, final):
    n_tiles = N_TOK // tm
    row_blk = lambda w: pl.BlockSpec((tm, w), lambda i: (i, 0))
    body = functools.partial(_stage3_body, layer=l, n_prompt_tiles=N_PROMPT // tm, tiles_per_seq=DEC_SEQ // tm,
                             n_x=len(x), final=final, chunk=256)
    if final:
        extra_in, extra_args = [], []
        out_specs = list(_split_specs(tm, D))
        out_shape = [jax.ShapeDtypeStruct((N_PROMPT, D), F32), jax.ShapeDtypeStruct((N_SAMPLE, D), F32)]
    else:
        slab = -(-IN_COLS // (n_tiles * BF16_ROWS)) * BF16_ROWS
        extra_in = [pl.BlockSpec((1, slab, D), lambda i: (l + 1, i, 0))]
        extra_args = [w_in_t_f32]
        out_specs = [row_blk(D), pl.BlockSpec((1, slab, D), lambda i: (0, i, 0))]
        out_shape = [jax.ShapeDtypeStruct((N_TOK, D), F32), jax.ShapeDtypeStruct((1, IN_COLS, D), BF)]
    return pl.pallas_call(
        body,
        grid=(n_tiles,),
        in_specs=[
            *_x_specs(x, tm), *_split_specs(tm, N_HEADS * V_HEAD), row_blk(W_CONF), row_blk(W_SC),
            *_split_specs(tm, W_FN),
            row_blk(N_BRANCH * D), _layer(l, (8, 6 * D)), _full((DEPTH, D)), _full((1, D)),
            _full((N_HEADS * V_HEAD, D)), _full((W_CONF, D)), _full((W_SC, D)), _full((W_FN, D)),
            _full((D, D)), _full((D, FF_HIDDEN)), _full((D, FF_HIDDEN)), _full((FF_HIDDEN, D)),
            *extra_in,
        ],
        out_specs=out_specs,
        out_shape=out_shape,
        compiler_params=_params(1),
        name="stage3",
    )(*x, o_p, o_s, u, s, f_p, f_s, gates, mod_l, g2, gfin, *w3, *extra_args)


def _rope_table(tm):
    rows = DEC_SEQ // GRID_W
    row_pos = np.repeat(np.arange(rows, dtype=np.float64), GRID_W)
    col_pos = np.tile(np.arange(GRID_W, dtype=np.float64), rows)
    inv = ROPE_THETA ** (-np.arange(0, AXIS_ROPE, 2, dtype=np.float64) / AXIS_ROPE)
    ang = np.concatenate([row_pos[:, None] * inv, col_pos[:, None] * inv], axis=1)
    half = QK_ROPE // 2
    tab = np.zeros((tm + DEC_SEQ, 3 * LANE), np.float64)
    tab[:, 0:LANE] = 1.0
    tab[tm:, ROPE_LANE0:ROPE_LANE0 + half] = np.cos(ang)
    tab[tm:, ROPE_LANE0 + half:ROPE_LANE0 + QK_ROPE] = np.cos(ang)
    tab[tm:, LANE + ROPE_LANE0:LANE + ROPE_LANE0 + half] = -np.sin(ang)
    tab[tm:, 2 * LANE + ROPE_LANE0 + half:2 * LANE + ROPE_LANE0 + QK_ROPE] = np.sin(ang)
    return jnp.asarray(tab.astype(np.float32))


def _head_layout_body(wqb_ref, wkvb_ref, pq_ref, pkv_ref, qb_ref, kvb_ref):
    qb_ref[0] = _dot(wqb_ref[0].astype(BF), pq_ref[...]).astype(BF)
    kvb_ref[0] = _dot(wkvb_ref[0].astype(BF), pkv_ref[...]).astype(BF)


def _head_layout_call(w_qb, w_kvb):
    dqk = QK_NOPE + QK_ROPE
    pq = np.zeros((N_HEADS * dqk, N_HEADS * LANE), np.float32)
    pkv = np.zeros((N_HEADS * LANE, 2 * N_HEADS * LANE), np.float32)
    for h in range(N_HEADS):
        src = h * dqk + np.concatenate([np.arange(QK_NOPE), QK_NOPE + ROPE_PERM])
        pq[src, h * LANE + np.arange(dqk)] = 1.0
        nope = np.arange(QK_NOPE)
        pkv[h * LANE + nope, h * LANE + nope] = 1.0
        val = np.arange(V_HEAD)
        pkv[h * LANE + QK_NOPE + val, (N_HEADS + h) * LANE + (h % 2) * V_HEAD + val] = 1.0
    return pl.pallas_call(
        _head_layout_body,
        grid=(DEPTH,),
        in_specs=[pl.BlockSpec((1, Q_LORA, N_HEADS * dqk), lambda l: (l, 0, 0)),
                  pl.BlockSpec((1, KV_LORA, N_HEADS * LANE), lambda l: (l, 0, 0)),
                  _full(pq.shape), _full(pkv.shape)],
        out_specs=[pl.BlockSpec((1, Q_LORA, N_HEADS * LANE), lambda l: (l, 0, 0)),
                   pl.BlockSpec((1, KV_LORA, 2 * N_HEADS * LANE), lambda l: (l, 0, 0))],
        out_shape=[jax.ShapeDtypeStruct((DEPTH, Q_LORA, N_HEADS * LANE), BF),
                   jax.ShapeDtypeStruct((DEPTH, KV_LORA, 2 * N_HEADS * LANE), BF)],
        compiler_params=_params(1),
        name="head_layout",
    )(w_qb, w_kvb, jnp.asarray(pq, BF), jnp.asarray(pkv, BF))


def kernel(x_prompt, x_sample, cache_ckv, cache_krope, c, c_ctx, w_ada, b_ada, g_norm1, g_norm2, w_in, g_qa, w_qb, g_kva, w_kvb, w_o_mla, w_conf_dw, b_conf_dw, g_conf_ln, b_conf_ln, w_conf_pw, w_sc_conv, w_sc_out, w_fn, w_out, w_ffn_gate, w_ffn_up, w_ffn_down, g_final):
    tm = 512
    xp = x_prompt.reshape(N_PROMPT, D)
    xs = x_sample.reshape(N_SAMPLE, D)
    cvec = jnp.concatenate([c_ctx[None, :], c, jnp.zeros((8 - 1 - DEC_BATCH, D), F32)], axis=0)
    mod = _ada_call(cvec, w_ada, b_ada)

    rope_tab = _rope_table(tm)
    place = np.zeros((QK_ROPE, LANE), np.float32)
    place[ROPE_PERM, ROPE_LANE0 + np.arange(QK_ROPE)] = 1.0
    wqb, wkvb_all = _head_layout_call(w_qb, w_kvb)
    kc, vc = _ctx_call(cache_ckv, cache_krope, wkvb_all, jnp.asarray(place, BF))

    cg, sg = _dft_cos_sin(FN_GROUP_W)
    eye = np.eye(FN_GROUPS, dtype=np.float32)
    gd = jnp.asarray(np.concatenate([np.kron(eye, cg), np.kron(eye, sg)], axis=1)).astype(BF)
    cs_p = jnp.asarray(np.concatenate(_dft_cos_sin(SEQ), axis=1)).astype(BF)
    cs_s = jnp.asarray(np.concatenate(_dft_cos_sin(DEC_SEQ), axis=1)).astype(BF)

    w_in_t_f32 = jnp.swapaxes(w_in, 1, 2)
    w_in_t = w_in_t_f32[0:1].astype(BF)
    w3_f32 = (w_o_mla, w_conf_pw, w_sc_out, w_fn, w_out, w_ffn_gate, w_ffn_up, w_ffn_down)

    conv_w = (w_conf_dw, b_conf_dw, g_conf_ln, b_conf_ln, w_sc_conv)
    x = (xp, xs)
    new_ckv, new_krope = [], []
    for l in range(DEPTH):
        (q, k, v, ckv, kr, u0, gcx, gb, fn, gates), w3 = _stage1_call(
            l, x, mod, g_norm1, g_qa, g_kva, w_in_t, wqb, wkvb_all, rope_tab, w3_f32, tm)
        new_ckv.append(ckv[:N_PROMPT].reshape(BATCH, SEQ, KV_LORA))
        new_krope.append(kr[:N_PROMPT].reshape(BATCH, SEQ, QK_ROPE))

        o_p, f_p = _prompt_mix_call(q, k, v, fn, gd, cs_p)
        o_s = _attn_sample_call(l, q, k, v, kc, vc, 512)
        f_s, u, s = _fourier_call(fn, gd, cs_s, DEC_SEQ, DEC_BATCH, N_PROMPT, 512, DEC_BATCH,
                                  conv=(l, u0, gcx, gb, conv_w))
        out = _stage3_call(l, x, o_p, o_s, u, s, f_p, f_s, gates, mod, g_norm2, g_final[None, :], w3,
                           w_in_t_f32, tm, l == DEPTH - 1)
        if l < DEPTH - 1:
            x, w_in_t = (out[0],), out[1]

    y_prompt, y_sample = out
    return (y_prompt.reshape(BATCH, SEQ, D), y_sample.reshape(DEC_BATCH, DEC_SEQ, D),
            jnp.stack(new_ckv, axis=1), jnp.stack(new_krope, axis=1))
```

```python
import functools

import numpy as np
import jax
import jax.numpy as jnp
from jax import lax
from jax.experimental import pallas as pl
from jax.experimental.pallas import tpu as pltpu

BF = jnp.bfloat16
F32 = jnp.float32

D = 1024
BATCH = 16
SEQ = 256
DEPTH = 2
DEC_BATCH = 2
DEC_SEQ = 2048
PAST_LEN = 256
GRID_W = 64
N_HEADS = 8
QK_NOPE = 64
QK_ROPE = 32
V_HEAD = 64
Q_LORA = 384
KV_LORA = 256
AXIS_ROPE = QK_ROPE // 2
ROPE_THETA = 10000.0
W_CONF = D // 4
CONF_K = 31
W_SC = D // 4
SC_K = 3
W_FN = D // 4
FN_GROUPS = 4
FN_GROUP_W = W_FN // FN_GROUPS
N_BRANCH = 4
FF_HIDDEN = ((8 * D // 3 + 255) // 256) * 256
EPS = 1e-6
OFF_QA = 0
OFF_KVA = OFF_QA + Q_LORA
OFF_CONF = OFF_KVA + KV_LORA + QK_ROPE
OFF_SC = OFF_CONF + 2 * W_CONF
OFF_FN = OFF_SC + 3 * W_SC
OFF_GATE = OFF_FN + W_FN
IN_COLS = OFF_GATE + N_BRANCH * D

N_PROMPT = BATCH * SEQ
N_SAMPLE = DEC_BATCH * DEC_SEQ
N_TOK = N_PROMPT + N_SAMPLE
LANE = 128
SUBLANES = 8
BF16_ROWS = 2 * SUBLANES
HALO = 16
SM_SCALE = float((QK_NOPE + QK_ROPE) ** -0.5)
Q_SCALE = SM_SCALE * float(np.log2(np.e))
VMEM_LIMIT = 56 * 1024 * 1024

QK_END = OFF_KVA + KV_LORA + LANE

ROPE_PERM = np.array(list(range(0, 8)) + list(range(16, 24)) + list(range(8, 16)) + list(range(24, 32)))
ROPE_LANE0 = QK_NOPE


def _dot(a, b):
    return jnp.dot(a, b, preferred_element_type=F32)


def _dot_nt(a, b):
    return lax.dot_general(a, b, (((1,), (1,)), ((), ())), preferred_element_type=F32)


def _sigmoid(x):
    return jax.nn.sigmoid(x)


def _rms(x, g):
    return x * lax.rsqrt(jnp.mean(x * x, axis=-1, keepdims=True) + EPS) * g


def _full(shape):
    return pl.BlockSpec(shape, lambda *_: (0,) * len(shape))


def _layer(l, shape):
    return pl.BlockSpec((1, *shape), lambda *_: (l,) + (0,) * len(shape), pipeline_mode=pl.Buffered(1))


def _params(n_axes):
    return pltpu.CompilerParams(
        dimension_semantics=("arbitrary",) * n_axes, vmem_limit_bytes=VMEM_LIMIT)


def _mod_row(i, n_prompt_tiles, tiles_per_seq):
    return jnp.where(i >= n_prompt_tiles, 1 + (i - n_prompt_tiles) // tiles_per_seq, 0)


def _ada_body(c_ref, w_ref, b_ref, out_ref):
    cv = c_ref[...]
    sc = (cv * _sigmoid(cv)).astype(BF)
    out_ref[0] = _dot(sc, w_ref[0].astype(BF)) + b_ref[0]


def _ada_call(cvec, w_ada, b_ada):
    wc = 2 * D
    return pl.pallas_call(
        _ada_body,
        grid=(DEPTH, 6 * D // wc),
        in_specs=[
            _full((8, D)),
            pl.BlockSpec((1, D, wc), lambda l, j: (l, 0, j)),
            pl.BlockSpec((1, 1, wc), lambda l, j: (l, 0, j)),
        ],
        out_specs=pl.BlockSpec((1, 8, wc), lambda l, j: (l, 0, j)),
        out_shape=jax.ShapeDtypeStruct((DEPTH, 8, 6 * D), F32),
        compiler_params=_params(2),
        name="ada_mod",
    )(cvec, w_ada, b_ada.reshape(DEPTH, 1, 6 * D))


def _ctx_body(ckv_ref, kr_ref, wkvb_ref, place_ref, k_ref, v_ref):
    kv = _dot(ckv_ref[0, 0].astype(BF), wkvb_ref[0])
    krp = _dot(kr_ref[0, 0].astype(BF), place_ref[...])
    for h in range(N_HEADS):
        sl = slice(h * LANE, (h + 1) * LANE)
        k_ref[0, 0, :, sl] = (kv[:, sl] + krp).astype(BF)
    v_ref[0, 0] = (kv[:, N_HEADS * LANE:] + _value_ones_row()).astype(BF)


def _ctx_call(cache_ckv, cache_krope, wkvb_all, place):
    return pl.pallas_call(
        _ctx_body,
        grid=(DEPTH, DEC_BATCH),
        in_specs=[
            pl.BlockSpec((1, 1, PAST_LEN, KV_LORA), lambda l, b: (b, l, 0, 0)),
            pl.BlockSpec((1, 1, PAST_LEN, QK_ROPE), lambda l, b: (b, l, 0, 0)),
            pl.BlockSpec((1, KV_LORA, 2 * N_HEADS * LANE), lambda l, b: (l, 0, 0)),
            _full((QK_ROPE, LANE)),
        ],
        out_specs=[
            pl.BlockSpec((1, 1, PAST_LEN, N_HEADS * LANE), lambda l, b: (l, b, 0, 0)),
            pl.BlockSpec((1, 1, PAST_LEN, N_HEADS * LANE), lambda l, b: (l, b, 0, 0)),
        ],
        out_shape=[
            jax.ShapeDtypeStruct((DEPTH, DEC_BATCH, PAST_LEN, N_HEADS * LANE), BF),
            jax.ShapeDtypeStruct((DEPTH, DEC_BATCH, PAST_LEN, N_HEADS * LANE), BF),
        ],
        compiler_params=_params(2),
        name="ctx_keys",
    )(cache_ckv, cache_krope, wkvb_all, place)


def _take(refs, n):
    return refs[:n], refs[n:]


def _read_x(x_refs, is_sample):
    if len(x_refs) == 1:
        return x_refs[0][...]
    return jnp.where(is_sample, x_refs[1][...], x_refs[0][...])


def _stage1_body(*refs, layer, n_prompt_tiles, tiles_per_seq, n_x, n_cast):
    x_refs, refs = _take(refs, n_x)
    (mod_ref, g1_ref, gqa_ref, gkva_ref, wt_ref, wqb_ref, wkvb_ref, rope_ref), refs = _take(refs, 8)
    cast_in, refs = _take(refs, n_cast)
    (q_ref, k_ref, v_ref, ckv_ref, kr_ref, u0_ref, gcx_ref, gb_ref, fn_ref, gate_ref), cast_out = _take(refs, 10)

    for src, dst in zip(cast_in, cast_out, strict=True):
        dst[...] = src[0].astype(BF)

    i = pl.program_id(0)
    is_sample = i >= n_prompt_tiles
    row = _mod_row(i, n_prompt_tiles, tiles_per_seq)
    sh1 = mod_ref[0, pl.ds(row, 1), 0:D]
    sc1 = mod_ref[0, pl.ds(row, 1), D:2 * D]
    hb = (_rms(_read_x(x_refs, is_sample), _layer_row(g1_ref, layer)[...]) * (1.0 + sc1) + sh1).astype(BF)

    def proj(c0, c1):
        return _dot_nt(hb, wt_ref[0, c0:c1, :])

    pqk = proj(OFF_QA, QK_END)
    q = _dot(_rms(pqk[:, 0:Q_LORA], _layer_row(gqa_ref, layer)[...]).astype(BF), wqb_ref[0])

    ckv = _rms(pqk[:, OFF_KVA:OFF_KVA + KV_LORA], _layer_row(gkva_ref, layer)[...])
    ckv_ref[...] = ckv
    kr3 = pqk[:, OFF_KVA + KV_LORA:QK_END]
    kr_ref[...] = kr3[:, 0:QK_ROPE]
    lane = lax.broadcasted_iota(jnp.int32, (1, LANE), 1)
    group = lambda k: (lane >= ROPE_LANE0 + 8 * k) & (lane < ROPE_LANE0 + 8 * (k + 1))
    krm = jnp.where(group(0) | group(3), pltpu.roll(kr3, ROPE_LANE0, 1),
                    jnp.where(group(1), pltpu.roll(kr3, ROPE_LANE0 - 8, 1),
                              jnp.where(group(2), pltpu.roll(kr3, ROPE_LANE0 + 8, 1), 0.0)))
    kv = _dot(ckv.astype(BF), wkvb_ref[0])
    v_ref[...] = (kv[:, N_HEADS * LANE:] + _value_ones_row()).astype(BF)

    def rope(t):
        return (t * rope_ref[:, 0:LANE]
                + pltpu.roll(t, LANE - 16, 1) * rope_ref[:, LANE:2 * LANE]
                + pltpu.roll(t, 16, 1) * rope_ref[:, 2 * LANE:3 * LANE])

    krr = rope(krm)
    for h in range(N_HEADS):
        sl = slice(h * LANE, (h + 1) * LANE)
        q_ref[:, sl] = (rope(q[:, sl]) * Q_SCALE).astype(BF)
        k_ref[:, sl] = (kv[:, sl] + krr).astype(BF)

    for j in range(N_BRANCH):
        gate_ref[:, j * D:(j + 1) * D] = _sigmoid(
            proj(OFF_GATE + j * D, OFF_GATE + (j + 1) * D)).astype(BF)

    pc = proj(OFF_CONF, OFF_SC)
    u0_ref[...] = pc[:, 0:W_CONF] * _sigmoid(pc[:, W_CONF:])
    ps = proj(OFF_SC, OFF_FN)
    gb_ref[...] = ps[:, 0:W_SC]
    gcx_ref[...] = ps[:, W_SC:2 * W_SC] * ps[:, 2 * W_SC:]
    fn_ref[...] = proj(OFF_FN, OFF_GATE).astype(BF)


def _split_specs(tm, w):
    n_p = N_PROMPT // tm
    return (pl.BlockSpec((tm, w), lambda i: (jnp.minimum(i, n_p - 1), 0)),
            pl.BlockSpec((tm, w), lambda i: (jnp.maximum(i - n_p, 0), 0)))


def _x_specs(x, tm):
    return list(_split_specs(tm, D)) if len(x) == 2 else [pl.BlockSpec((tm, D), lambda i: (i, 0))]


def _stage1_call(l, x, mod_l, g1, gqa, gkva, w_in_t, wqb, wkvb, rope_tab, to_cast, tm):
    n_prompt_tiles = N_PROMPT // tm
    tiles_per_seq = DEC_SEQ // tm
    row_blk = lambda w: pl.BlockSpec((tm, w), lambda i: (i, 0))
    n_steps = N_TOK // tm
    body = functools.partial(_stage1_body, layer=l, n_prompt_tiles=n_prompt_tiles, tiles_per_seq=tiles_per_seq,
                             n_x=len(x), n_cast=len(to_cast))
    slab = lambda w: w.shape[1] // n_steps
    assert all(w.shape[1] % (BF16_ROWS * n_steps) == 0 for w in to_cast)
    outs = pl.pallas_call(
        body,
        grid=(n_steps,),
        in_specs=[
            *_x_specs(x, tm),
            _layer(l, (8, 6 * D)),
            _full((DEPTH, D)),
            _full((DEPTH, Q_LORA)),
            _full((DEPTH, KV_LORA)),
            _layer(0, (IN_COLS, D)),
            _layer(l, (Q_LORA, N_HEADS * LANE)),
            _layer(l, (KV_LORA, 2 * N_HEADS * LANE)),
            pl.BlockSpec((tm, 3 * LANE),
                         lambda i: (jnp.where(i < n_prompt_tiles, 0,
                                              1 + (i - n_prompt_tiles) % tiles_per_seq), 0)),
            *[pl.BlockSpec((1, slab(w), w.shape[2]), lambda i: (l, i, 0)) for w in to_cast],
        ],
        out_specs=[
            row_blk(N_HEADS * LANE), row_blk(N_HEADS * LANE), row_blk(N_HEADS * LANE),
            row_blk(KV_LORA), row_blk(QK_ROPE), row_blk(W_CONF), row_blk(W_SC), row_blk(W_SC),
            row_blk(W_FN), row_blk(N_BRANCH * D),
            *[pl.BlockSpec((slab(w), w.shape[2]), lambda i: (i, 0)) for w in to_cast],
        ],
        out_shape=[
            jax.ShapeDtypeStruct((N_TOK, N_HEADS * LANE), BF),
            jax.ShapeDtypeStruct((N_TOK, N_HEADS * LANE), BF),
            jax.ShapeDtypeStruct((N_TOK, N_HEADS * LANE), BF),
            jax.ShapeDtypeStruct((N_TOK, KV_LORA), F32),
            jax.ShapeDtypeStruct((N_TOK, QK_ROPE), F32),
            jax.ShapeDtypeStruct((N_TOK, W_CONF), F32),
            jax.ShapeDtypeStruct((N_TOK, W_SC), F32),
            jax.ShapeDtypeStruct((N_TOK, W_SC), F32),
            jax.ShapeDtypeStruct((N_TOK, W_FN), BF),
            jax.ShapeDtypeStruct((N_TOK, N_BRANCH * D), BF),
            *[jax.ShapeDtypeStruct(w.shape[1:], BF) for w in to_cast],
        ],
        compiler_params=_params(1),
        name="stage1",
    )(*x, mod_l, g1, gqa, gkva, w_in_t, wqb, wkvb, rope_tab, *to_cast)
    return outs[:10], outs[10:]


def _ones_lane(parity):
    return (1 - parity) * V_HEAD


def _value_ones_row():
    col = lax.broadcasted_iota(jnp.int32, (1, N_HEADS * LANE), 1)
    odd = (col // LANE) % 2
    return (col % LANE == jnp.where(odd == 1, _ones_lane(1), _ones_lane(0))).astype(F32)


def _attend_heads(q_ref, kv_refs, o_ref, ahead):
    lane = lax.broadcasted_iota(jnp.int32, (1, LANE), 1)

    def scores(h):
        sl = slice(h * LANE, (h + 1) * LANE)
        return [_dot_nt(q_ref[:, sl], k_ref[:, sl]) for k_ref, _ in kv_refs]

    pending = [scores(h) for h in range(ahead)]
    for hp in range(N_HEADS // 2):
        vsl = slice(hp * LANE, (hp + 1) * LANE)
        outs = []
        for e in range(2):
            h = 2 * hp + e
            sl = slice(h * LANE, (h + 1) * LANE)
            ss = pending.pop(0)
            if h + ahead < N_HEADS:
                pending.append(scores(h + ahead))
            m = functools.reduce(jnp.maximum, [jnp.max(s, axis=-1, keepdims=True) for s in ss])
            o = None
            for s, (_, v_ref) in zip(ss, kv_refs):
                t = _dot(jnp.exp2(s - m).astype(BF), v_ref[:, sl])
                o = t if o is None else o + t
            ones_lane = _ones_lane(e)
            outs.append(o * (1.0 / o[:, ones_lane:ones_lane + 1]))
        o_ref[:, vsl] = jnp.where(lane < V_HEAD, outs[0], outs[1]).astype(BF)


def _prompt_mix_body(q_ref, k_ref, v_ref, fn_ref, gd_ref, cs_ref, o_ref, f_ref, rhs, *, per_step, scale):
    _fourier_rhs(fn_ref, gd_ref, rhs, SEQ, per_step)
    for b, rows in enumerate(_fourier_rows(cs_ref, rhs, scale, per_step)):
        f_ref[b * SEQ:(b + 1) * SEQ, :] = rows
    for b in range(per_step):
        rows = pl.ds(b * SEQ, SEQ)
        _attend_heads(q_ref.at[rows], [(k_ref.at[rows], v_ref.at[rows])], o_ref.at[rows], ahead=N_HEADS)


def _prompt_mix_call(q, k, v, fn, gd, cs):
    per_step = 4
    blk = lambda w: pl.BlockSpec((per_step * SEQ, w), lambda b: (b, 0))
    body = functools.partial(_prompt_mix_body, per_step=per_step, scale=float((SEQ * FN_GROUP_W) ** -0.5))
    return pl.pallas_call(
        body,
        grid=(BATCH // per_step,),
        in_specs=[blk(N_HEADS * LANE), blk(N_HEADS * LANE), blk(N_HEADS * LANE), blk(W_FN),
                  _full((W_FN, 2 * W_FN)), _full((SEQ, 2 * SEQ))],
        out_specs=[blk(N_HEADS * V_HEAD), blk(W_FN)],
        out_shape=[jax.ShapeDtypeStruct((N_PROMPT, N_HEADS * V_HEAD), BF),
                   jax.ShapeDtypeStruct((N_PROMPT, W_FN), BF)],
        scratch_shapes=[pltpu.VMEM((2 * SEQ, per_step * W_FN), BF)],
        compiler_params=_params(1),
        name="prompt_mix",
    )(q, k, v, fn, gd, cs)


def _attn_sample_body(q_ref, kc_ref, vc_ref, k_ref, v_ref, o_ref):
    _attend_heads(q_ref, [(kc_ref.at[0, 0], vc_ref.at[0, 0]), (k_ref, v_ref)], o_ref, ahead=2)


def _attn_sample_call(l, q, k, v, kc, vc, tq):
    n_q = DEC_SEQ // tq
    q0 = N_PROMPT // tq
    s0 = N_PROMPT // DEC_SEQ
    return pl.pallas_call(
        _attn_sample_body,
        grid=(DEC_BATCH, n_q),
        in_specs=[
            pl.BlockSpec((tq, N_HEADS * LANE), lambda b, j: (q0 + b * n_q + j, 0)),
            pl.BlockSpec((1, 1, PAST_LEN, N_HEADS * LANE), lambda b, j: (l, b, 0, 0)),
            pl.BlockSpec((1, 1, PAST_LEN, N_HEADS * LANE), lambda b, j: (l, b, 0, 0)),
            pl.BlockSpec((DEC_SEQ, N_HEADS * LANE), lambda b, j: (s0 + b, 0)),
            pl.BlockSpec((DEC_SEQ, N_HEADS * LANE), lambda b, j: (s0 + b, 0)),
        ],
        out_specs=pl.BlockSpec((tq, N_HEADS * V_HEAD), lambda b, j: (b * n_q + j, 0)),
        out_shape=jax.ShapeDtypeStruct((N_SAMPLE, N_HEADS * V_HEAD), BF),
        compiler_params=_params(2),
        name="attn_sample",
    )(q, kc, vc, k, v)


CONV_ROWS = 256


def _conv_pass(ubuf, cbuf, gb_ref, wdw_ref, bdw_ref, gln_ref, bln_ref, wsc_ref, uo_ref, so_ref,
               shifted, cshifted):
    tl = CONV_ROWS
    span = shifted.shape[1]
    p0 = HALO - CONF_K // 2
    q0 = HALO - SC_K // 2
    for b in range(1, SUBLANES):
        shifted[b] = ubuf[pl.ds(b, span), :]
    for t in range(SC_K):
        if (q0 + t) % SUBLANES:
            cshifted[t] = cbuf[pl.ds((q0 + t) % SUBLANES, span), :]

    def staged(p, r, n):
        a, b = divmod(p, SUBLANES)
        if b == 0:
            return ubuf[pl.ds(r + SUBLANES * a, n), :]
        return shifted[b, pl.ds(r + SUBLANES * a, n), :]

    def staged_c(t, r, n):
        a, b = divmod(q0 + t, SUBLANES)
        if b == 0:
            return cbuf[pl.ds(r + SUBLANES * a, n), :]
        return cshifted[t, pl.ds(r + SUBLANES * a, n), :]

    rc = 32
    for r in range(0, tl, rc):
        acc = staged(p0, r, rc) * wdw_ref[0, 0:1, :]
        for t in range(1, CONF_K):
            acc = acc + staged(p0 + t, r, rc) * wdw_ref[0, t:t + 1, :]
        acc = acc + bdw_ref[...]
        mu = jnp.mean(acc, axis=-1, keepdims=True)
        cen = acc - mu
        var = jnp.mean(cen * cen, axis=-1, keepdims=True)
        y = cen * lax.rsqrt(var + EPS) * gln_ref[...] + bln_ref[...]
        uo_ref[pl.ds(r, rc), :] = (y * _sigmoid(y)).astype(BF)

        cv = staged_c(0, r, rc) * wsc_ref[0, 0:1, :]
        for t in range(1, SC_K):
            cv = cv + staged_c(t, r, rc) * wsc_ref[0, t:t + 1, :]
        so_ref[pl.ds(r, rc), :] = (gb_ref[pl.ds(r, rc), :] * cv).astype(BF)


def _conv_tile(tm, u_ref, c_ref, gb_ref, halos, is_latent, has_prev, has_next, conv_w, uo_ref, so_ref, scratch,
               hooks=None):
    ubuf, cbuf, shifted, cshifted = scratch
    n_pass = tm // CONV_ROWS
    zeros = jnp.zeros((HALO, W_CONF), F32)
    for p in range(n_pass):
        if hooks and p in hooks:
            hooks[p]()
        r0 = p * CONV_ROWS
        for src, buf, k in ((u_ref, ubuf, 0), (c_ref, cbuf, 2)):
            if p > 0:
                head = jnp.where(is_latent, src[r0 - HALO:r0, :], 0.0)
            else:
                head = zeros if halos is None else jnp.where(has_prev, halos[k][...], 0.0)
            if p < n_pass - 1:
                tail = jnp.where(is_latent, src[r0 + CONV_ROWS:r0 + CONV_ROWS + HALO, :], 0.0)
            else:
                tail = zeros if halos is None else jnp.where(has_next, halos[k + 1][...], 0.0)
            buf[0:HALO, :] = head
            buf[HALO:HALO + CONV_ROWS, :] = src[r0:r0 + CONV_ROWS, :]
            buf[HALO + CONV_ROWS:2 * HALO + CONV_ROWS, :] = tail
        rows = pl.ds(r0, CONV_ROWS)
        _conv_pass(ubuf, cbuf, gb_ref.at[rows], *conv_w, uo_ref.at[rows], so_ref.at[rows], shifted, cshifted)


def _conv_scratch():
    staged = CONV_ROWS + 2 * HALO
    return [pltpu.VMEM((staged, W_CONF), F32), pltpu.VMEM((staged, W_SC), F32),
            pltpu.VMEM((SUBLANES, staged - SUBLANES, W_CONF), F32),
            pltpu.VMEM((SC_K, staged - SUBLANES, W_SC), F32)]


def _fourier_rhs(fn_ref, gd_ref, rhs, seq, group):
    for b in range(group):
        v = _dot(fn_ref[b * seq:(b + 1) * seq, :], gd_ref[...])
        rhs[0:seq, b * W_FN:(b + 1) * W_FN] = v[:, 0:W_FN].astype(BF)
        rhs[seq:2 * seq, b * W_FN:(b + 1) * W_FN] = (-v[:, W_FN:]).astype(BF)


def _fourier_rows(cs_ref, rhs, scale, group):
    res = _dot(cs_ref[...], rhs[...]) * scale
    return [res[:, b * W_FN:(b + 1) * W_FN].astype(BF) for b in range(group)]


def _layer_row(ref, layer):
    return ref.at[pl.ds(layer, 1)]


def _fourier_body(fn_ref, gd_ref, cs_ref, *rest, seq, group, scale, conv_rows, layer):
    if conv_rows:
        (u_ref, c_ref, gb_ref), (wdw_ref, bdw_ref, gln_ref, bln_ref, wsc_ref) = rest[:3], rest[3:8]
        conv_w = (wdw_ref, _layer_row(bdw_ref, layer), _layer_row(gln_ref, layer), _layer_row(bln_ref, layer),
                  wsc_ref)
        out_ref, uo_ref, so_ref, rhs, *conv_scratch = rest[8:]
    else:
        out_ref, rhs = rest

    @pl.when(pl.program_id(1) == 0)
    def _():
        _fourier_rhs(fn_ref, gd_ref, rhs, seq, group)

    def dft_rows(b):
        out_ref[b] = (_dot(cs_ref[...], rhs[:, b * W_FN:(b + 1) * W_FN]) * scale).astype(BF)

    if not conv_rows:
        for b in range(group):
            dft_rows(b)
        return

    is_latent = pl.program_id(1) * conv_rows >= N_PROMPT
    n_pass = conv_rows // CONV_ROWS
    hooks = {(b * n_pass) // group: functools.partial(dft_rows, b) for b in range(group)}
    _conv_tile(conv_rows, u_ref, c_ref, gb_ref, None, is_latent, False, False, conv_w, uo_ref, so_ref,
               conv_scratch, hooks)


def _fourier_call(fn, gd, cs, seq, n_seq, row0, tl, group, conv=None):
    n_t = seq // tl
    g0 = row0 // (group * seq)
    n_steps = (n_seq // group) * n_t
    conv_rows, l = 0, 0
    extra_in, extra_out, extra_shape, extra_scratch, extra_args = [], [], [], [], []
    if conv is not None:
        l, u0, gcx, gb, conv_w = conv
        conv_rows = N_TOK // n_steps
        assert n_seq == group and conv_rows % DEC_SEQ == 0
        blk = pl.BlockSpec((conv_rows, W_CONF), lambda g, j: (j, 0))
        extra_in = [blk, blk, blk, _layer(l, (CONF_K, W_CONF)), _full((DEPTH, W_CONF)),
                    _full((DEPTH, W_CONF)), _full((DEPTH, W_CONF)), _layer(l, (SC_K, W_SC))]
        extra_out = [blk, blk]
        extra_shape = [jax.ShapeDtypeStruct((N_TOK, W_CONF), BF), jax.ShapeDtypeStruct((N_TOK, W_SC), BF)]
        extra_scratch = _conv_scratch()
        extra_args = [u0, gcx, gb, *conv_w]
    body = functools.partial(_fourier_body, seq=seq, group=group, scale=float((seq * FN_GROUP_W) ** -0.5),
                             conv_rows=conv_rows, layer=l)
    outs = pl.pallas_call(
        body,
        grid=(n_seq // group, n_t),
        in_specs=[
            pl.BlockSpec((group * seq, W_FN), lambda g, j: (g0 + g, 0)),
            _full((W_FN, 2 * W_FN)),
            pl.BlockSpec((tl, 2 * seq), lambda g, j: (j, 0)),
            *extra_in,
        ],
        out_specs=[pl.BlockSpec((group, tl, W_FN), lambda g, j: (g, j, 0)), *extra_out],
        out_shape=[jax.ShapeDtypeStruct((n_seq, seq, W_FN), BF), *extra_shape],
        scratch_shapes=[pltpu.VMEM((2 * seq, group * W_FN), BF), *extra_scratch],
        compiler_params=_params(2),
        name=f"fourier_{seq}",
    )(fn, gd, cs, *extra_args)
    f = outs[0].reshape(n_seq * seq, W_FN)
    return f if conv is None else (f, outs[1], outs[2])


def _dft_cos_sin(n):
    r = np.arange(n, dtype=np.int64)
    ang = ((r[:, None] * r[None, :]) % n).astype(np.float64) * (2.0 * np.pi / n)
    return np.cos(ang).astype(np.float32), np.sin(ang).astype(np.float32)


def _stage3_body(*refs, layer, n_prompt_tiles, tiles_per_seq, n_x, final, chunk):
    x_refs, refs = _take(refs, n_x)
    (op_ref, os_ref, u_ref, s_ref, fp_ref, fs_ref, gate_ref, mod_ref, g2_ref, gfin_ref), refs = _take(refs, 10)
    (wo_ref, wpw_ref, wsco_ref, wfn_ref, wout_ref, wg_ref, wu_ref, wd_ref), refs = _take(refs, 8)
    if final:
        out_refs = refs
    else:
        (next_w_ref,), (x_out_ref, next_w_out_ref) = _take(refs, 1)
        next_w_out_ref[...] = next_w_ref[...].astype(BF)
        out_refs = (x_out_ref,)

    i = pl.program_id(0)
    is_sample = i >= n_prompt_tiles
    row = _mod_row(i, n_prompt_tiles, tiles_per_seq)
    mod = lambda k: mod_ref[0, pl.ds(row, 1), k * D:(k + 1) * D]
    pick = lambda p_ref, s_ref: jnp.where(is_sample, s_ref[...], p_ref[...])
    merged = gate_ref[:, 0:D].astype(F32) * _dot(pick(op_ref, os_ref), wo_ref[...])
    merged = merged + gate_ref[:, D:2 * D].astype(F32) * _dot(u_ref[...], wpw_ref[...])
    merged = merged + gate_ref[:, 2 * D:3 * D].astype(F32) * _dot(s_ref[...], wsco_ref[...])
    merged = merged + gate_ref[:, 3 * D:4 * D].astype(F32) * _dot(pick(fp_ref, fs_ref), wfn_ref[...])
    x1 = _read_x(x_refs, is_sample) + mod(2) * _dot(merged.astype(BF), wout_ref[...])

    h2 = (_rms(x1, _layer_row(g2_ref, layer)[...]) * (1.0 + mod(4)) + mod(3)).astype(BF)
    bounds = list(range(0, FF_HIDDEN, chunk)) + [FF_HIDDEN]
    spans = list(zip(bounds[:-1], bounds[1:]))

    def gate_up(span):
        c0, c1 = span
        return _dot(h2, wg_ref[:, c0:c1]), _dot(h2, wu_ref[:, c0:c1])

    acc = None
    ab_next = gate_up(spans[0])
    for n, (c0, c1) in enumerate(spans):
        a, b = ab_next
        if n + 1 < len(spans):
            ab_next = gate_up(spans[n + 1])
        t = _dot((a * _sigmoid(a) * b).astype(BF), wd_ref[c0:c1, :])
        acc = t if acc is None else acc + t
    x2 = x1 + mod(5) * acc
    if not final:
        out_refs[0][...] = x2
        return
    y = _rms(x2, gfin_ref[...])

    @pl.when(is_sample)
    def _():
        out_refs[1][...] = y

    @pl.when(jnp.logical_not(is_sample))
    def _():
        out_refs[0][...] = y


def _stage3_call(l, x, o_p, o_s, u, s, f_p, f_s, gates, mod_l, g2, gfin, w3, w_in_t_f32, tm, final):
    n_tiles = N_TOK // tm
    row_blk = lambda w: pl.BlockSpec((tm, w), lambda i: (i, 0))
    body = functools.partial(_stage3_body, layer=l, n_prompt_tiles=N_PROMPT // tm, tiles_per_seq=DEC_SEQ // tm,
                             n_x=len(x), final=final, chunk=256)
    if final:
        extra_in, extra_args = [], []
        out_specs = list(_split_specs(tm, D))
        out_shape = [jax.ShapeDtypeStruct((N_PROMPT, D), F32), jax.ShapeDtypeStruct((N_SAMPLE, D), F32)]
    else:
        slab = -(-IN_COLS // (n_tiles * BF16_ROWS)) * BF16_ROWS
        extra_in = [pl.BlockSpec((1, slab, D), lambda i: (l + 1, i, 0))]
        extra_args = [w_in_t_f32]
        out_specs = [row_blk(D), pl.BlockSpec((1, slab, D), lambda i: (0, i, 0))]
        out_shape = [jax.ShapeDtypeStruct((N_TOK, D), F32), jax.ShapeDtypeStruct((1, IN_COLS, D), BF)]
    return pl.pallas_call(
        body,
        grid=(n_tiles,),
        in_specs=[
            *_x_specs(x, tm), *_split_specs(tm, N_HEADS * V_HEAD), row_blk(W_CONF), row_blk(W_SC),
            *_split_specs(tm, W_FN),
            row_blk(N_BRANCH * D), _layer(l, (8, 6 * D)), _full((DEPTH, D)), _full((1, D)),
            _full((N_HEADS * V_HEAD, D)), _full((W_CONF, D)), _full((W_SC, D)), _full((W_FN, D)),
            _full((D, D)), _full((D, FF_HIDDEN)), _full((D, FF_HIDDEN)), _full((FF_HIDDEN, D)),
            *extra_in,
        ],
        out_specs=out_specs,
        out_shape=out_shape,
        compiler_params=_params(1),
        name="stage3",
    )(*x, o_p, o_s, u, s, f_p, f_s, gates, mod_l, g2, gfin, *w3, *extra_args)


def _rope_table(tm):
    rows = DEC_SEQ // GRID_W
    row_pos = np.repeat(np.arange(rows, dtype=np.float64), GRID_W)
    col_pos = np.tile(np.arange(GRID_W, dtype=np.float64), rows)
    inv = ROPE_THETA ** (-np.arange(0, AXIS_ROPE, 2, dtype=np.float64) / AXIS_ROPE)
    ang = np.concatenate([row_pos[:, None] * inv, col_pos[:, None] * inv], axis=1)
    half = QK_ROPE // 2
    tab = np.zeros((tm + DEC_SEQ, 3 * LANE), np.float64)
    tab[:, 0:LANE] = 1.0
    tab[tm:, ROPE_LANE0:ROPE_LANE0 + half] = np.cos(ang)
    tab[tm:, ROPE_LANE0 + half:ROPE_LANE0 + QK_ROPE] = np.cos(ang)
    tab[tm:, LANE + ROPE_LANE0:LANE + ROPE_LANE0 + half] = -np.sin(ang)
    tab[tm:, 2 * LANE + ROPE_LANE0 + half:2 * LANE + ROPE_LANE0 + QK_ROPE] = np.sin(ang)
    return jnp.asarray(tab.astype(np.float32))


def _head_layout_body(wqb_ref, wkvb_ref, pq_ref, pkv_ref, qb_ref, kvb_ref):
    qb_ref[0] = _dot(wqb_ref[0].astype(BF), pq_ref[...]).astype(BF)
    kvb_ref[0] = _dot(wkvb_ref[0].astype(BF), pkv_ref[...]).astype(BF)


def _head_layout_call(w_qb, w_kvb):
    dqk = QK_NOPE + QK_ROPE
    pq = np.zeros((N_HEADS * dqk, N_HEADS * LANE), np.float32)
    pkv = np.zeros((N_HEADS * LANE, 2 * N_HEADS * LANE), np.float32)
    for h in range(N_HEADS):
        src = h * dqk + np.concatenate([np.arange(QK_NOPE), QK_NOPE + ROPE_PERM])
        pq[src, h * LANE + np.arange(dqk)] = 1.0
        nope = np.arange(QK_NOPE)
        pkv[h * LANE + nope, h * LANE + nope] = 1.0
        val = np.arange(V_HEAD)
        pkv[h * LANE + QK_NOPE + val, (N_HEADS + h) * LANE + (h % 2) * V_HEAD + val] = 1.0
    return pl.pallas_call(
        _head_layout_body,
        grid=(DEPTH,),
        in_specs=[pl.BlockSpec((1, Q_LORA, N_HEADS * dqk), lambda l: (l, 0, 0)),
                  pl.BlockSpec((1, KV_LORA, N_HEADS * LANE), lambda l: (l, 0, 0)),
                  _full(pq.shape), _full(pkv.shape)],
        out_specs=[pl.BlockSpec((1, Q_LORA, N_HEADS * LANE), lambda l: (l, 0, 0)),
                   pl.BlockSpec((1, KV_LORA, 2 * N_HEADS * LANE), lambda l: (l, 0, 0))],
        out_shape=[jax.ShapeDtypeStruct((DEPTH, Q_LORA, N_HEADS * LANE), BF),
                   jax.ShapeDtypeStruct((DEPTH, KV_LORA, 2 * N_HEADS * LANE), BF)],
        compiler_params=_params(1),
        name="head_layout",
    )(w_qb, w_kvb, jnp.asarray(pq, BF), jnp.asarray(pkv, BF))


def kernel(x_prompt, x_sample, cache_ckv, cache_krope, c, c_ctx, w_ada, b_ada, g_norm1, g_norm2, w_in, g_qa, w_qb, g_kva, w_kvb, w_o_mla, w_conf_dw, b_conf_dw, g_conf_ln, b_conf_ln, w_conf_pw, w_sc_conv, w_sc_out, w_fn, w_out, w_ffn_gate, w_ffn_up, w_ffn_down, g_final):
    tm = 512
    xp = x_prompt.reshape(N_PROMPT, D)
    xs = x_sample.reshape(N_SAMPLE, D)
    cvec = jnp.concatenate([c_ctx[None, :], c, jnp.zeros((8 - 1 - DEC_BATCH, D), F32)], axis=0)
    mod = _ada_call(cvec, w_ada, b_ada)

    rope_tab = _rope_table(tm)
    place = np.zeros((QK_ROPE, LANE), np.float32)
    place[ROPE_PERM, ROPE_LANE0 + np.arange(QK_ROPE)] = 1.0
    wqb, wkvb_all = _head_layout_call(w_qb, w_kvb)
    kc, vc = _ctx_call(cache_ckv, cache_krope, wkvb_all, jnp.asarray(place, BF))

    cg, sg = _dft_cos_sin(FN_GROUP_W)
    eye = np.eye(FN_GROUPS, dtype=np.float32)
    gd = jnp.asarray(np.concatenate([np.kron(eye, cg), np.kron(eye, sg)], axis=1)).astype(BF)
    cs_p = jnp.asarray(np.concatenate(_dft_cos_sin(SEQ), axis=1)).astype(BF)
    cs_s = jnp.asarray(np.concatenate(_dft_cos_sin(DEC_SEQ), axis=1)).astype(BF)

    w_in_t_f32 = jnp.swapaxes(w_in, 1, 2)
    w_in_t = w_in_t_f32[0:1].astype(BF)
    w3_f32 = (w_o_mla, w_conf_pw, w_sc_out, w_fn, w_out, w_ffn_gate, w_ffn_up, w_ffn_down)

    conv_w = (w_conf_dw, b_conf_dw, g_conf_ln, b_conf_ln, w_sc_conv)
    x = (xp, xs)
    new_ckv, new_krope = [], []
    for l in range(DEPTH):
        (q, k, v, ckv, kr, u0, gcx, gb, fn, gates), w3 = _stage1_call(
            l, x, mod, g_norm1, g_qa, g_kva, w_in_t, wqb, wkvb_all, rope_tab, w3_f32, tm)
        new_ckv.append(ckv[:N_PROMPT].reshape(BATCH, SEQ, KV_LORA))
        new_krope.append(kr[:N_PROMPT].reshape(BATCH, SEQ, QK_ROPE))

        o_p, f_p = _prompt_mix_call(q, k, v, fn, gd, cs_p)
        o_s = _attn_sample_call(l, q, k, v, kc, vc, 512)
        f_s, u, s = _fourier_call(fn, gd, cs_s, DEC_SEQ, DEC_BATCH, N_PROMPT, 512, DEC_BATCH,
                                  conv=(l, u0, gcx, gb, conv_w))
        out = _stage3_call(l, x, o_p, o_s, u, s, f_p, f_s, gates, mod, g_norm2, g_final[None, :], w3,
                           w_in_t_f32, tm, l == DEPTH - 1)
        if l < DEPTH - 1:
            x, w_in_t = (out[0],), out[1]

    y_prompt, y_sample = out
    return (y_prompt.reshape(BATCH, SEQ, D), y_sample.reshape(DEC_BATCH, DEC_SEQ, D),
            jnp.stack(new_ckv, axis=1), jnp.stack(new_krope, axis=1))
```

```python
import functools

import numpy as np
import jax
import jax.numpy as jnp
from jax import lax
from jax.experimental import pallas as pl
from jax.experimental.pallas import tpu as pltpu

BF = jnp.bfloat16
F32 = jnp.float32

D = 1024
BATCH = 16
SEQ = 256
DEPTH = 2
DEC_BATCH = 2
DEC_SEQ = 2048
PAST_LEN = 256
GRID_W = 64
N_HEADS = 8
QK_NOPE = 64
QK_ROPE = 32
V_HEAD = 64
Q_LORA = 384
KV_LORA = 256
AXIS_ROPE = QK_ROPE // 2
ROPE_THETA = 10000.0
W_CONF = D // 4
CONF_K = 31
W_SC = D // 4
SC_K = 3
W_FN = D // 4
FN_GROUPS = 4
FN_GROUP_W = W_FN // FN_GROUPS
N_BRANCH = 4
FF_HIDDEN = ((8 * D // 3 + 255) // 256) * 256
EPS = 1e-6
OFF_QA = 0
OFF_KVA = OFF_QA + Q_LORA
OFF_CONF = OFF_KVA + KV_LORA + QK_ROPE
OFF_SC = OFF_CONF + 2 * W_CONF
OFF_FN = OFF_SC + 3 * W_SC
OFF_GATE = OFF_FN + W_FN
IN_COLS = OFF_GATE + N_BRANCH * D

N_PROMPT = BATCH * SEQ
N_SAMPLE = DEC_BATCH * DEC_SEQ
N_TOK = N_PROMPT + N_SAMPLE
LANE = 128
SUBLANES = 8
BF16_ROWS = 2 * SUBLANES
HALO = 16
SM_SCALE = float((QK_NOPE + QK_ROPE) ** -0.5)
Q_SCALE = SM_SCALE * float(np.log2(np.e))
VMEM_LIMIT = 56 * 1024 * 1024

QK_END = OFF_KVA + KV_LORA + LANE

ROPE_PERM = np.array(list(range(0, 8)) + list(range(16, 24)) + list(range(8, 16)) + list(range(24, 32)))
ROPE_LANE0 = QK_NOPE


def _dot(a, b):
    return jnp.dot(a, b, preferred_element_type=F32)


def _dot_nt(a, b):
    return lax.dot_general(a, b, (((1,), (1,)), ((), ())), preferred_element_type=F32)


def _sigmoid(x):
    return jax.nn.sigmoid(x)


def _rms(x, g):
    return x * lax.rsqrt(jnp.mean(x * x, axis=-1, keepdims=True) + EPS) * g


def _full(shape):
    return pl.BlockSpec(shape, lambda *_: (0,) * len(shape))


def _layer(l, shape):
    return pl.BlockSpec((1, *shape), lambda *_: (l,) + (0,) * len(shape), pipeline_mode=pl.Buffered(1))


def _params(n_axes):
    return pltpu.CompilerParams(
        dimension_semantics=("arbitrary",) * n_axes, vmem_limit_bytes=VMEM_LIMIT)


def _mod_row(i, n_prompt_tiles, tiles_per_seq):
    return jnp.where(i >= n_prompt_tiles, 1 + (i - n_prompt_tiles) // tiles_per_seq, 0)


def _ada_body(c_ref, w_ref, b_ref, out_ref):
    cv = c_ref[...]
    sc = (cv * _sigmoid(cv)).astype(BF)
    out_ref[0] = _dot(sc, w_ref[0].astype(BF)) + b_ref[0]


def _ada_call(cvec, w_ada, b_ada):
    wc = 2 * D
    return pl.pallas_call(
        _ada_body,
        grid=(DEPTH, 6 * D // wc),
        in_specs=[
            _full((8, D)),
            pl.BlockSpec((1, D, wc), lambda l, j: (l, 0, j)),
            pl.BlockSpec((1, 1, wc), lambda l, j: (l, 0, j)),
        ],
        out_specs=pl.BlockSpec((1, 8, wc), lambda l, j: (l, 0, j)),
        out_shape=jax.ShapeDtypeStruct((DEPTH, 8, 6 * D), F32),
        compiler_params=_params(2),
        name="ada_mod",
    )(cvec, w_ada, b_ada.reshape(DEPTH, 1, 6 * D))


def _ctx_body(ckv_ref, kr_ref, wkvb_ref, place_ref, k_ref, v_ref):
    kv = _dot(ckv_ref[0, 0].astype(BF), wkvb_ref[0])
    krp = _dot(kr_ref[0, 0].astype(BF), place_ref[...])
    for h in range(N_HEADS):
        sl = slice(h * LANE, (h + 1) * LANE)
        k_ref[0, 0, :, sl] = (kv[:, sl] + krp).astype(BF)
    v_ref[0, 0] = (kv[:, N_HEADS * LANE:] + _value_ones_row()).astype(BF)


def _ctx_call(cache_ckv, cache_krope, wkvb_all, place):
    return pl.pallas_call(
        _ctx_body,
        grid=(DEPTH, DEC_BATCH),
        in_specs=[
            pl.BlockSpec((1, 1, PAST_LEN, KV_LORA), lambda l, b: (b, l, 0, 0)),
            pl.BlockSpec((1, 1, PAST_LEN, QK_ROPE), lambda l, b: (b, l, 0, 0)),
            pl.BlockSpec((1, KV_LORA, 2 * N_HEADS * LANE), lambda l, b: (l, 0, 0)),
            _full((QK_ROPE, LANE)),
        ],
        out_specs=[
            pl.BlockSpec((1, 1, PAST_LEN, N_HEADS * LANE), lambda l, b: (l, b, 0, 0)),
            pl.BlockSpec((1, 1, PAST_LEN, N_HEADS * LANE), lambda l, b: (l, b, 0, 0)),
        ],
        out_shape=[
            jax.ShapeDtypeStruct((DEPTH, DEC_BATCH, PAST_LEN, N_HEADS * LANE), BF),
            jax.ShapeDtypeStruct((DEPTH, DEC_BATCH, PAST_LEN, N_HEADS * LANE), BF),
        ],
        compiler_params=_params(2),
        name="ctx_keys",
    )(cache_ckv, cache_krope, wkvb_all, place)


def _take(refs, n):
    return refs[:n], refs[n:]


def _read_x(x_refs, is_sample):
    if len(x_refs) == 1:
        return x_refs[0][...]
    return jnp.where(is_sample, x_refs[1][...], x_refs[0][...])


def _stage1_body(*refs, layer, n_prompt_tiles, tiles_per_seq, n_x, n_cast):
    x_refs, refs = _take(refs, n_x)
    (mod_ref, g1_ref, gqa_ref, gkva_ref, wt_ref, wqb_ref, wkvb_ref, rope_ref), refs = _take(refs, 8)
    cast_in, refs = _take(refs, n_cast)
    (q_ref, k_ref, v_ref, ckv_ref, kr_ref, u0_ref, gcx_ref, gb_ref, fn_ref, gate_ref), cast_out = _take(refs, 10)

    for src, dst in zip(cast_in, cast_out, strict=True):
        dst[...] = src[0].astype(BF)

    i = pl.program_id(0)
    is_sample = i >= n_prompt_tiles
    row = _mod_row(i, n_prompt_tiles, tiles_per_seq)
    sh1 = mod_ref[0, pl.ds(row, 1), 0:D]
    sc1 = mod_ref[0, pl.ds(row, 1), D:2 * D]
    hb = (_rms(_read_x(x_refs, is_sample), _layer_row(g1_ref, layer)[...]) * (1.0 + sc1) + sh1).astype(BF)

    def proj(c0, c1):
        return _dot_nt(hb, wt_ref[0, c0:c1, :])

    pqk = proj(OFF_QA, QK_END)
    q = _dot(_rms(pqk[:, 0:Q_LORA], _layer_row(gqa_ref, layer)[...]).astype(BF), wqb_ref[0])

    ckv = _rms(pqk[:, OFF_KVA:OFF_KVA + KV_LORA], _layer_row(gkva_ref, layer)[...])
    ckv_ref[...] = ckv
    kr3 = pqk[:, OFF_KVA + KV_LORA:QK_END]
    kr_ref[...] = kr3[:, 0:QK_ROPE]
    lane = lax.broadcasted_iota(jnp.int32, (1, LANE), 1)
    group = lambda k: (lane >= ROPE_LANE0 + 8 * k) & (lane < ROPE_LANE0 + 8 * (k + 1))
    krm = jnp.where(group(0) | group(3), pltpu.roll(kr3, ROPE_LANE0, 1),
                    jnp.where(group(1), pltpu.roll(kr3, ROPE_LANE0 - 8, 1),
                              jnp.where(group(2), pltpu.roll(kr3, ROPE_LANE0 + 8, 1), 0.0)))
    kv = _dot(ckv.astype(BF), wkvb_ref[0])
    v_ref[...] = (kv[:, N_HEADS * LANE:] + _value_ones_row()).astype(BF)

    def rope(t):
        return (t * rope_ref[:, 0:LANE]
                + pltpu.roll(t, LANE - 16, 1) * rope_ref[:, LANE:2 * LANE]
                + pltpu.roll(t, 16, 1) * rope_ref[:, 2 * LANE:3 * LANE])

    krr = rope(krm)
    for h in range(N_HEADS):
        sl = slice(h * LANE, (h + 1) * LANE)
        q_ref[:, sl] = (rope(q[:, sl]) * Q_SCALE).astype(BF)
        k_ref[:, sl] = (kv[:, sl] + krr).astype(BF)

    for j in range(N_BRANCH):
        gate_ref[:, j * D:(j + 1) * D] = _sigmoid(
            proj(OFF_GATE + j * D, OFF_GATE + (j + 1) * D)).astype(BF)

    pc = proj(OFF_CONF, OFF_SC)
    u0_ref[...] = pc[:, 0:W_CONF] * _sigmoid(pc[:, W_CONF:])
    ps = proj(OFF_SC, OFF_FN)
    gb_ref[...] = ps[:, 0:W_SC]
    gcx_ref[...] = ps[:, W_SC:2 * W_SC] * ps[:, 2 * W_SC:]
    fn_ref[...] = proj(OFF_FN, OFF_GATE).astype(BF)


def _split_specs(tm, w):
    n_p = N_PROMPT // tm
    return (pl.BlockSpec((tm, w), lambda i: (jnp.minimum(i, n_p - 1), 0)),
            pl.BlockSpec((tm, w), lambda i: (jnp.maximum(i - n_p, 0), 0)))


def _x_specs(x, tm):
    return list(_split_specs(tm, D)) if len(x) == 2 else [pl.BlockSpec((tm, D), lambda i: (i, 0))]


def _stage1_call(l, x, mod_l, g1, gqa, gkva, w_in_t, wqb, wkvb, rope_tab, to_cast, tm):
    n_prompt_tiles = N_PROMPT // tm
    tiles_per_seq = DEC_SEQ // tm
    row_blk = lambda w: pl.BlockSpec((tm, w), lambda i: (i, 0))
    n_steps = N_TOK // tm
    body = functools.partial(_stage1_body, layer=l, n_prompt_tiles=n_prompt_tiles, tiles_per_seq=tiles_per_seq,
                             n_x=len(x), n_cast=len(to_cast))
    slab = lambda w: w.shape[1] // n_steps
    assert all(w.shape[1] % (BF16_ROWS * n_steps) == 0 for w in to_cast)
    outs = pl.pallas_call(
        body,
        grid=(n_steps,),
        in_specs=[
            *_x_specs(x, tm),
            _layer(l, (8, 6 * D)),
            _full((DEPTH, D)),
            _full((DEPTH, Q_LORA)),
            _full((DEPTH, KV_LORA)),
            _layer(0, (IN_COLS, D)),
            _layer(l, (Q_LORA, N_HEADS * LANE)),
            _layer(l, (KV_LORA, 2 * N_HEADS * LANE)),
            pl.BlockSpec((tm, 3 * LANE),
                         lambda i: (jnp.where(i < n_prompt_tiles, 0,
                                              1 + (i - n_prompt_tiles) % tiles_per_seq), 0)),
            *[pl.BlockSpec((1, slab(w), w.shape[2]), lambda i: (l, i, 0)) for w in to_cast],
        ],
        out_specs=[
            row_blk(N_HEADS * LANE), row_blk(N_HEADS * LANE), row_blk(N_HEADS * LANE),
            row_blk(KV_LORA), row_blk(QK_ROPE), row_blk(W_CONF), row_blk(W_SC), row_blk(W_SC),
            row_blk(W_FN), row_blk(N_BRANCH * D),
            *[pl.BlockSpec((slab(w), w.shape[2]), lambda i: (i, 0)) for w in to_cast],
        ],
        out_shape=[
            jax.ShapeDtypeStruct((N_TOK, N_HEADS * LANE), BF),
            jax.ShapeDtypeStruct((N_TOK, N_HEADS * LANE), BF),
            jax.ShapeDtypeStruct((N_TOK, N_HEADS * LANE), BF),
            jax.ShapeDtypeStruct((N_TOK, KV_LORA), F32),
            jax.ShapeDtypeStruct((N_TOK, QK_ROPE), F32),
            jax.ShapeDtypeStruct((N_TOK, W_CONF), F32),
            jax.ShapeDtypeStruct((N_TOK, W_SC), F32),
            jax.ShapeDtypeStruct((N_TOK, W_SC), F32),
            jax.ShapeDtypeStruct((N_TOK, W_FN), BF),
            jax.ShapeDtypeStruct((N_TOK, N_BRANCH * D), BF),
            *[jax.ShapeDtypeStruct(w.shape[1:], BF) for w in to_cast],
        ],
        compiler_params=_params(1),
        name="stage1",
    )(*x, mod_l, g1, gqa, gkva, w_in_t, wqb, wkvb, rope_tab, *to_cast)
    return outs[:10], outs[10:]


def _ones_lane(parity):
    return (1 - parity) * V_HEAD


def _value_ones_row():
    col = lax.broadcasted_iota(jnp.int32, (1, N_HEADS * LANE), 1)
    odd = (col // LANE) % 2
    return (col % LANE == jnp.where(odd == 1, _ones_lane(1), _ones_lane(0))).astype(F32)


def _attend_heads(q_ref, kv_refs, o_ref, ahead):
    lane = lax.broadcasted_iota(jnp.int32, (1, LANE), 1)

    def scores(h):
        sl = slice(h * LANE, (h + 1) * LANE)
        return [_dot_nt(q_ref[:, sl], k_ref[:, sl]) for k_ref, _ in kv_refs]

    pending = [scores(h) for h in range(ahead)]
    for hp in range(N_HEADS // 2):
        vsl = slice(hp * LANE, (hp + 1) * LANE)
        outs = []
        for e in range(2):
            h = 2 * hp + e
            sl = slice(h * LANE, (h + 1) * LANE)
            ss = pending.pop(0)
            if h + ahead < N_HEADS:
                pending.append(scores(h + ahead))
            m = functools.reduce(jnp.maximum, [jnp.max(s, axis=-1, keepdims=True) for s in ss])
            o = None
            for s, (_, v_ref) in zip(ss, kv_refs):
                t = _dot(jnp.exp2(s - m).astype(BF), v_ref[:, sl])
                o = t if o is None else o + t
            ones_lane = _ones_lane(e)
            outs.append(o * (1.0 / o[:, ones_lane:ones_lane + 1]))
        o_ref[:, vsl] = jnp.where(lane < V_HEAD, outs[0], outs[1]).astype(BF)


def _conv_weights(refs, layer):
    wdw_ref, bdw_ref, gln_ref, bln_ref, wsc_ref = refs
    return (wdw_ref, _layer_row(bdw_ref, layer), _layer_row(gln_ref, layer), _layer_row(bln_ref, layer), wsc_ref)


def _conv_weight_specs(l):
    return [_layer(l, (CONF_K, W_CONF)), _full((DEPTH, W_CONF)), _full((DEPTH, W_CONF)),
            _full((DEPTH, W_CONF)), _layer(l, (SC_K, W_SC))]


def _prompt_mix_body(q_ref, k_ref, v_ref, fn_ref, gd_ref, cs_ref, u_ref, c_ref, gb_ref, *rest,
                     per_step, scale, layer):
    conv_w, (o_ref, f_ref, uo_ref, so_ref, rhs, *conv_scratch) = _conv_weights(rest[:5], layer), rest[5:]
    _fourier_rhs(fn_ref, gd_ref, rhs, SEQ, per_step)
    for b, rows in enumerate(_fourier_rows(cs_ref, rhs, scale, per_step)):
        f_ref[b * SEQ:(b + 1) * SEQ, :] = rows
    for b in range(per_step):
        rows = pl.ds(b * SEQ, SEQ)
        _attend_heads(q_ref.at[rows], [(k_ref.at[rows], v_ref.at[rows])], o_ref.at[rows], ahead=N_HEADS)
    _conv_tile(per_step * SEQ, u_ref, c_ref, gb_ref, None, False, False, False, conv_w, uo_ref, so_ref,
               conv_scratch)


def _prompt_mix_call(l, q, k, v, fn, gd, cs, u0, gcx, gb, conv_w):
    per_step = 4
    assert SEQ == CONV_ROWS
    blk = lambda w: pl.BlockSpec((per_step * SEQ, w), lambda b: (b, 0))
    body = functools.partial(_prompt_mix_body, per_step=per_step, scale=float((SEQ * FN_GROUP_W) ** -0.5),
                             layer=l)
    return pl.pallas_call(
        body,
        grid=(BATCH // per_step,),
        in_specs=[blk(N_HEADS * LANE), blk(N_HEADS * LANE), blk(N_HEADS * LANE), blk(W_FN),
                  _full((W_FN, 2 * W_FN)), _full((SEQ, 2 * SEQ)),
                  blk(W_CONF), blk(W_SC), blk(W_SC), *_conv_weight_specs(l)],
        out_specs=[blk(N_HEADS * V_HEAD), blk(W_FN), blk(W_CONF), blk(W_SC)],
        out_shape=[jax.ShapeDtypeStruct((N_PROMPT, N_HEADS * V_HEAD), BF),
                   jax.ShapeDtypeStruct((N_PROMPT, W_FN), BF),
                   jax.ShapeDtypeStruct((N_PROMPT, W_CONF), BF),
                   jax.ShapeDtypeStruct((N_PROMPT, W_SC), BF)],
        scratch_shapes=[pltpu.VMEM((2 * SEQ, per_step * W_FN), BF), *_conv_scratch()],
        compiler_params=_params(1),
        name="prompt_mix",
    )(q, k, v, fn, gd, cs, u0, gcx, gb, *conv_w)


def _attn_sample_body(q_ref, kc_ref, vc_ref, k_ref, v_ref, o_ref):
    _attend_heads(q_ref, [(kc_ref.at[0, 0], vc_ref.at[0, 0]), (k_ref, v_ref)], o_ref, ahead=2)


def _attn_sample_call(l, q, k, v, kc, vc, tq):
    n_q = DEC_SEQ // tq
    q0 = N_PROMPT // tq
    s0 = N_PROMPT // DEC_SEQ
    return pl.pallas_call(
        _attn_sample_body,
        grid=(DEC_BATCH, n_q),
        in_specs=[
            pl.BlockSpec((tq, N_HEADS * LANE), lambda b, j: (q0 + b * n_q + j, 0)),
            pl.BlockSpec((1, 1, PAST_LEN, N_HEADS * LANE), lambda b, j: (l, b, 0, 0)),
            pl.BlockSpec((1, 1, PAST_LEN, N_HEADS * LANE), lambda b, j: (l, b, 0, 0)),
            pl.BlockSpec((DEC_SEQ, N_HEADS * LANE), lambda b, j: (s0 + b, 0)),
            pl.BlockSpec((DEC_SEQ, N_HEADS * LANE), lambda b, j: (s0 + b, 0)),
        ],
        out_specs=pl.BlockSpec((tq, N_HEADS * V_HEAD), lambda b, j: (b * n_q + j, 0)),
        out_shape=jax.ShapeDtypeStruct((N_SAMPLE, N_HEADS * V_HEAD), BF),
        compiler_params=_params(2),
        name="attn_sample",
    )(q, kc, vc, k, v)


CONV_ROWS = 256


def _conv_pass(ubuf, cbuf, gb_ref, wdw_ref, bdw_ref, gln_ref, bln_ref, wsc_ref, uo_ref, so_ref,
               shifted, cshifted):
    tl = CONV_ROWS
    span = shifted.shape[1]
    p0 = HALO - CONF_K // 2
    q0 = HALO - SC_K // 2
    for b in range(1, SUBLANES):
        shifted[b] = ubuf[pl.ds(b, span), :]
    for t in range(SC_K):
        if (q0 + t) % SUBLANES:
            cshifted[t] = cbuf[pl.ds((q0 + t) % SUBLANES, span), :]

    def staged(p, r, n):
        a, b = divmod(p, SUBLANES)
        if b == 0:
            return ubuf[pl.ds(r + SUBLANES * a, n), :]
        return shifted[b, pl.ds(r + SUBLANES * a, n), :]

    def staged_c(t, r, n):
        a, b = divmod(q0 + t, SUBLANES)
        if b == 0:
            return cbuf[pl.ds(r + SUBLANES * a, n), :]
        return cshifted[t, pl.ds(r + SUBLANES * a, n), :]

    rc = 32
    for r in range(0, tl, rc):
        acc = staged(p0, r, rc) * wdw_ref[0, 0:1, :]
        for t in range(1, CONF_K):
            acc = acc + staged(p0 + t, r, rc) * wdw_ref[0, t:t + 1, :]
        acc = acc + bdw_ref[...]
        mu = jnp.mean(acc, axis=-1, keepdims=True)
        cen = acc - mu
        var = jnp.mean(cen * cen, axis=-1, keepdims=True)
        y = cen * lax.rsqrt(var + EPS) * gln_ref[...] + bln_ref[...]
        uo_ref[pl.ds(r, rc), :] = (y * _sigmoid(y)).astype(BF)

        cv = staged_c(0, r, rc) * wsc_ref[0, 0:1, :]
        for t in range(1, SC_K):
            cv = cv + staged_c(t, r, rc) * wsc_ref[0, t:t + 1, :]
        so_ref[pl.ds(r, rc), :] = (gb_ref[pl.ds(r, rc), :] * cv).astype(BF)


def _conv_tile(tm, u_ref, c_ref, gb_ref, halos, is_latent, has_prev, has_next, conv_w, uo_ref, so_ref, scratch,
               hooks=None):
    ubuf, cbuf, shifted, cshifted = scratch
    n_pass = tm // CONV_ROWS
    zeros = jnp.zeros((HALO, W_CONF), F32)
    for p in range(n_pass):
        if hooks and p in hooks:
            hooks[p]()
        r0 = p * CONV_ROWS
        for src, buf, k in ((u_ref, ubuf, 0), (c_ref, cbuf, 2)):
            if p > 0:
                head = jnp.where(is_latent, src[r0 - HALO:r0, :], 0.0)
            else:
                head = zeros if halos is None else jnp.where(has_prev, halos[k][...], 0.0)
            if p < n_pass - 1:
                tail = jnp.where(is_latent, src[r0 + CONV_ROWS:r0 + CONV_ROWS + HALO, :], 0.0)
            else:
                tail = zeros if halos is None else jnp.where(has_next, halos[k + 1][...], 0.0)
            buf[0:HALO, :] = head
            buf[HALO:HALO + CONV_ROWS, :] = src[r0:r0 + CONV_ROWS, :]
            buf[HALO + CONV_ROWS:2 * HALO + CONV_ROWS, :] = tail
        rows = pl.ds(r0, CONV_ROWS)
        _conv_pass(ubuf, cbuf, gb_ref.at[rows], *conv_w, uo_ref.at[rows], so_ref.at[rows], shifted, cshifted)


def _conv_scratch():
    staged = CONV_ROWS + 2 * HALO
    return [pltpu.VMEM((staged, W_CONF), F32), pltpu.VMEM((staged, W_SC), F32),
            pltpu.VMEM((SUBLANES, staged - SUBLANES, W_CONF), F32),
            pltpu.VMEM((SC_K, staged - SUBLANES, W_SC), F32)]


def _fourier_rhs(fn_ref, gd_ref, rhs, seq, group):
    for b in range(group):
        v = _dot(fn_ref[b * seq:(b + 1) * seq, :], gd_ref[...])
        rhs[0:seq, b * W_FN:(b + 1) * W_FN] = v[:, 0:W_FN].astype(BF)
        rhs[seq:2 * seq, b * W_FN:(b + 1) * W_FN] = (-v[:, W_FN:]).astype(BF)


def _fourier_rows(cs_ref, rhs, scale, group):
    res = _dot(cs_ref[...], rhs[...]) * scale
    return [res[:, b * W_FN:(b + 1) * W_FN].astype(BF) for b in range(group)]


def _layer_row(ref, layer):
    return ref.at[pl.ds(layer, 1)]


def _fourier_body(fn_ref, gd_ref, cs_ref, *rest, seq, group, scale, conv_rows, layer):
    if conv_rows:
        (u_ref, c_ref, gb_ref), conv_w = rest[:3], _conv_weights(rest[3:8], layer)
        out_ref, uo_ref, so_ref, rhs, *conv_scratch = rest[8:]
    else:
        out_ref, rhs = rest

    @pl.when(pl.program_id(1) == 0)
    def _():
        _fourier_rhs(fn_ref, gd_ref, rhs, seq, group)

    def dft_rows(b):
        out_ref[b] = (_dot(cs_ref[...], rhs[:, b * W_FN:(b + 1) * W_FN]) * scale).astype(BF)

    if not conv_rows:
        for b in range(group):
            dft_rows(b)
        return

    n_pass = conv_rows // CONV_ROWS
    hooks = {(b * n_pass) // group: functools.partial(dft_rows, b) for b in range(group)}
    _conv_tile(conv_rows, u_ref, c_ref, gb_ref, None, True, False, False, conv_w, uo_ref, so_ref,
               conv_scratch, hooks)


def _fourier_call(fn, gd, cs, seq, n_seq, row0, tl, group, conv=None):
    n_t = seq // tl
    g0 = row0 // (group * seq)
    n_steps = (n_seq // group) * n_t
    conv_rows, l = 0, 0
    extra_in, extra_out, extra_shape, extra_scratch, extra_args = [], [], [], [], []
    if conv is not None:
        l, u0, gcx, gb, conv_w = conv
        conv_rows = seq
        assert n_seq == group == n_steps
        c0 = row0 // conv_rows
        blk_in = pl.BlockSpec((conv_rows, W_CONF), lambda g, j: (c0 + j, 0))
        blk_out = pl.BlockSpec((conv_rows, W_CONF), lambda g, j: (j, 0))
        extra_in = [blk_in, blk_in, blk_in, *_conv_weight_specs(l)]
        extra_out = [blk_out, blk_out]
        extra_shape = [jax.ShapeDtypeStruct((n_seq * seq, W_CONF), BF),
                       jax.ShapeDtypeStruct((n_seq * seq, W_SC), BF)]
        extra_scratch = _conv_scratch()
        extra_args = [u0, gcx, gb, *conv_w]
    body = functools.partial(_fourier_body, seq=seq, group=group, scale=float((seq * FN_GROUP_W) ** -0.5),
                             conv_rows=conv_rows, layer=l)
    outs = pl.pallas_call(
        body,
        grid=(n_seq // group, n_t),
        in_specs=[
            pl.BlockSpec((group * seq, W_FN), lambda g, j: (g0 + g, 0)),
            _full((W_FN, 2 * W_FN)),
            pl.BlockSpec((tl, 2 * seq), lambda g, j: (j, 0)),
            *extra_in,
        ],
        out_specs=[pl.BlockSpec((group, tl, W_FN), lambda g, j: (g, j, 0)), *extra_out],
        out_shape=[jax.ShapeDtypeStruct((n_seq, seq, W_FN), BF), *extra_shape],
        scratch_shapes=[pltpu.VMEM((2 * seq, group * W_FN), BF), *extra_scratch],
        compiler_params=_params(2),
        name=f"fourier_{seq}",
    )(fn, gd, cs, *extra_args)
    f = outs[0].reshape(n_seq * seq, W_FN)
    return f if conv is None else (f, outs[1], outs[2])


def _dft_cos_sin(n):
    r = np.arange(n, dtype=np.int64)
    ang = ((r[:, None] * r[None, :]) % n).astype(np.float64) * (2.0 * np.pi / n)
    return np.cos(ang).astype(np.float32), np.sin(ang).astype(np.float32)


def _stage3_body(*refs, layer, n_prompt_tiles, tiles_per_seq, n_x, final, chunk):
    x_refs, refs = _take(refs, n_x)
    (op_ref, os_ref, up_ref, us_ref, sp_ref, ss_ref, fp_ref, fs_ref, gate_ref, mod_ref, g2_ref,
     gfin_ref), refs = _take(refs, 12)
    (wo_ref, wpw_ref, wsco_ref, wfn_ref, wout_ref, wg_ref, wu_ref, wd_ref), refs = _take(refs, 8)
    if final:
        out_refs = refs
    else:
        (next_w_ref,), (x_out_ref, next_w_out_ref) = _take(refs, 1)
        next_w_out_ref[...] = next_w_ref[...].astype(BF)
        out_refs = (x_out_ref,)

    i = pl.program_id(0)
    is_sample = i >= n_prompt_tiles
    row = _mod_row(i, n_prompt_tiles, tiles_per_seq)
    mod = lambda k: mod_ref[0, pl.ds(row, 1), k * D:(k + 1) * D]
    pick = lambda p_ref, s_ref: jnp.where(is_sample, s_ref[...], p_ref[...])
    merged = gate_ref[:, 0:D].astype(F32) * _dot(pick(op_ref, os_ref), wo_ref[...])
    merged = merged + gate_ref[:, D:2 * D].astype(F32) * _dot(pick(up_ref, us_ref), wpw_ref[...])
    merged = merged + gate_ref[:, 2 * D:3 * D].astype(F32) * _dot(pick(sp_ref, ss_ref), wsco_ref[...])
    merged = merged + gate_ref[:, 3 * D:4 * D].astype(F32) * _dot(pick(fp_ref, fs_ref), wfn_ref[...])
    x1 = _read_x(x_refs, is_sample) + mod(2) * _dot(merged.astype(BF), wout_ref[...])

    h2 = (_rms(x1, _layer_row(g2_ref, layer)[...]) * (1.0 + mod(4)) + mod(3)).astype(BF)
    bounds = list(range(0, FF_HIDDEN, chunk)) + [FF_HIDDEN]
    spans = list(zip(bounds[:-1], bounds[1:]))

    def gate_up(span):
        c0, c1 = span
        return _dot(h2, wg_ref[:, c0:c1]), _dot(h2, wu_ref[:, c0:c1])

    acc = None
    ab_next = gate_up(spans[0])
    for n, (c0, c1) in enumerate(spans):
        a, b = ab_next
        if n + 1 < len(spans):
            ab_next = gate_up(spans[n + 1])
        t = _dot((a * _sigmoid(a) * b).astype(BF), wd_ref[c0:c1, :])
        acc = t if acc is None else acc + t
    x2 = x1 + mod(5) * acc
    if not final:
        out_refs[0][...] = x2
        return
    y = _rms(x2, gfin_ref[...])

    @pl.when(is_sample)
    def _():
        out_refs[1][...] = y

    @pl.when(jnp.logical_not(is_sample))
    def _():
        out_refs[0][...] = y


def _stage3_call(l, x, o_p, o_s, u, s, f_p, f_s, gates, mod_l, g2, gfin, w3, w_in_t_f32, tm, final):
    n_tiles = N_TOK // tm
    row_blk = lambda w: pl.BlockSpec((tm, w), lambda i: (i, 0))
    body = functools.partial(_stage3_body, layer=l, n_prompt_tiles=N_PROMPT // tm, tiles_per_seq=DEC_SEQ // tm,
                             n_x=len(x), final=final, chunk=256)
    if final:
        extra_in, extra_args = [], []
        out_specs = list(_split_specs(tm, D))
        out_shape = [jax.ShapeDtypeStruct((N_PROMPT, D), F32), jax.ShapeDtypeStruct((N_SAMPLE, D), F32)]
    else:
        slab = -(-IN_COLS // (n_tiles * BF16_ROWS)) * BF16_ROWS
        extra_in = [pl.BlockSpec((1, slab, D), lambda i: (l + 1, i, 0))]
        extra_args = [w_in_t_f32]
        out_specs = [row_blk(D), pl.BlockSpec((1, slab, D), lambda i: (0, i, 0))]
        out_shape = [jax.ShapeDtypeStruct((N_TOK, D), F32), jax.ShapeDtypeStruct((1, IN_COLS, D), BF)]
    return pl.pallas_call(
        body,
        grid=(n_tiles,),
        in_specs=[
            *_x_specs(x, tm), *_split_specs(tm, N_HEADS * V_HEAD), *_split_specs(tm, W_CONF),
            *_split_specs(tm, W_SC),
            *_split_specs(tm, W_FN),
            row_blk(N_BRANCH * D), _layer(l, (8, 6 * D)), _full((DEPTH, D)), _full((1, D)),
            _full((N_HEADS * V_HEAD, D)), _full((W_CONF, D)), _full((W_SC, D)), _full((W_FN, D)),
            _full((D, D)), _full((D, FF_HIDDEN)), _full((D, FF_HIDDEN)), _full((FF_HIDDEN, D)),
            *extra_in,
        ],
        out_specs=out_specs,
        out_shape=out_shape,
        compiler_params=_params(1),
        name="stage3",
    )(*x, o_p, o_s, *u, *s, f_p, f_s, gates, mod_l, g2, gfin, *w3, *extra_args)


def _rope_table(tm):
    rows = DEC_SEQ // GRID_W
    row_pos = np.repeat(np.arange(rows, dtype=np.float64), GRID_W)
    col_pos = np.tile(np.arange(GRID_W, dtype=np.float64), rows)
    inv = ROPE_THETA ** (-np.arange(0, AXIS_ROPE, 2, dtype=np.float64) / AXIS_ROPE)
    ang = np.concatenate([row_pos[:, None] * inv, col_pos[:, None] * inv], axis=1)
    half = QK_ROPE // 2
    tab = np.zeros((tm + DEC_SEQ, 3 * LANE), np.float64)
    tab[:, 0:LANE] = 1.0
    tab[tm:, ROPE_LANE0:ROPE_LANE0 + half] = np.cos(ang)
    tab[tm:, ROPE_LANE0 + half:ROPE_LANE0 + QK_ROPE] = np.cos(ang)
    tab[tm:, LANE + ROPE_LANE0:LANE + ROPE_LANE0 + half] = -np.sin(ang)
    tab[tm:, 2 * LANE + ROPE_LANE0 + half:2 * LANE + ROPE_LANE0 + QK_ROPE] = np.sin(ang)
    return jnp.asarray(tab.astype(np.float32))


def _head_layout_body(wqb_ref, wkvb_ref, pq_ref, pkv_ref, qb_ref, kvb_ref):
    qb_ref[0] = _dot(wqb_ref[0].astype(BF), pq_ref[...]).astype(BF)
    kvb_ref[0] = _dot(wkvb_ref[0].astype(BF), pkv_ref[...]).astype(BF)


def _head_layout_call(w_qb, w_kvb):
    dqk = QK_NOPE + QK_ROPE
    pq = np.zeros((N_HEADS * dqk, N_HEADS * LANE), np.float32)
    pkv = np.zeros((N_HEADS * LANE, 2 * N_HEADS * LANE), np.float32)
    for h in range(N_HEADS):
        src = h * dqk + np.concatenate([np.arange(QK_NOPE), QK_NOPE + ROPE_PERM])
        pq[src, h * LANE + np.arange(dqk)] = 1.0
        nope = np.arange(QK_NOPE)
        pkv[h * LANE + nope, h * LANE + nope] = 1.0
        val = np.arange(V_HEAD)
        pkv[h * LANE + QK_NOPE + val, (N_HEADS + h) * LANE + (h % 2) * V_HEAD + val] = 1.0
    return pl.pallas_call(
        _head_layout_body,
        grid=(DEPTH,),
        in_specs=[pl.BlockSpec((1, Q_LORA, N_HEADS * dqk), lambda l: (l, 0, 0)),
                  pl.BlockSpec((1, KV_LORA, N_HEADS * LANE), lambda l: (l, 0, 0)),
                  _full(pq.shape), _full(pkv.shape)],
        out_specs=[pl.BlockSpec((1, Q_LORA, N_HEADS * LANE), lambda l: (l, 0, 0)),
                   pl.BlockSpec((1, KV_LORA, 2 * N_HEADS * LANE), lambda l: (l, 0, 0))],
        out_shape=[jax.ShapeDtypeStruct((DEPTH, Q_LORA, N_HEADS * LANE), BF),
                   jax.ShapeDtypeStruct((DEPTH, KV_LORA, 2 * N_HEADS * LANE), BF)],
        compiler_params=_params(1),
        name="head_layout",
    )(w_qb, w_kvb, jnp.asarray(pq, BF), jnp.asarray(pkv, BF))


def kernel(x_prompt, x_sample, cache_ckv, cache_krope, c, c_ctx, w_ada, b_ada, g_norm1, g_norm2, w_in, g_qa, w_qb, g_kva, w_kvb, w_o_mla, w_conf_dw, b_conf_dw, g_conf_ln, b_conf_ln, w_conf_pw, w_sc_conv, w_sc_out, w_fn, w_out, w_ffn_gate, w_ffn_up, w_ffn_down, g_final):
    tm = 512
    xp = x_prompt.reshape(N_PROMPT, D)
    xs = x_sample.reshape(N_SAMPLE, D)
    cvec = jnp.concatenate([c_ctx[None, :], c, jnp.zeros((8 - 1 - DEC_BATCH, D), F32)], axis=0)
    mod = _ada_call(cvec, w_ada, b_ada)

    rope_tab = _rope_table(tm)
    place = np.zeros((QK_ROPE, LANE), np.float32)
    place[ROPE_PERM, ROPE_LANE0 + np.arange(QK_ROPE)] = 1.0
    wqb, wkvb_all = _head_layout_call(w_qb, w_kvb)
    kc, vc = _ctx_call(cache_ckv, cache_krope, wkvb_all, jnp.asarray(place, BF))

    cg, sg = _dft_cos_sin(FN_GROUP_W)
    eye = np.eye(FN_GROUPS, dtype=np.float32)
    gd = jnp.asarray(np.concatenate([np.kron(eye, cg), np.kron(eye, sg)], axis=1)).astype(BF)
    cs_p = jnp.asarray(np.concatenate(_dft_cos_sin(SEQ), axis=1)).astype(BF)
    cs_s = jnp.asarray(np.concatenate(_dft_cos_sin(DEC_SEQ), axis=1)).astype(BF)

    w_in_t_f32 = jnp.swapaxes(w_in, 1, 2)
    w_in_t = w_in_t_f32[0:1].astype(BF)
    w3_f32 = (w_o_mla, w_conf_pw, w_sc_out, w_fn, w_out, w_ffn_gate, w_ffn_up, w_ffn_down)

    conv_w = (w_conf_dw, b_conf_dw, g_conf_ln, b_conf_ln, w_sc_conv)
    x = (xp, xs)
    new_ckv, new_krope = [], []
    for l in range(DEPTH):
        (q, k, v, ckv, kr, u0, gcx, gb, fn, gates), w3 = _stage1_call(
            l, x, mod, g_norm1, g_qa, g_kva, w_in_t, wqb, wkvb_all, rope_tab, w3_f32, tm)
        new_ckv.append(ckv[:N_PROMPT].reshape(BATCH, SEQ, KV_LORA))
        new_krope.append(kr[:N_PROMPT].reshape(BATCH, SEQ, QK_ROPE))

        o_p, f_p, u_p, s_p = _prompt_mix_call(l, q, k, v, fn, gd, cs_p, u0, gcx, gb, conv_w)
        o_s = _attn_sample_call(l, q, k, v, kc, vc, 512)
        f_s, u_s, s_s = _fourier_call(fn, gd, cs_s, DEC_SEQ, DEC_BATCH, N_PROMPT, DEC_SEQ // DEC_BATCH,
                                      DEC_BATCH, conv=(l, u0, gcx, gb, conv_w))
        u, s = (u_p, u_s), (s_p, s_s)
        out = _stage3_call(l, x, o_p, o_s, u, s, f_p, f_s, gates, mod, g_norm2, g_final[None, :], w3,
                           w_in_t_f32, tm, l == DEPTH - 1)
        if l < DEPTH - 1:
            x, w_in_t = (out[0],), out[1]

    y_prompt, y_sample = out
    return (y_prompt.reshape(BATCH, SEQ, D), y_sample.reshape(DEC_BATCH, DEC_SEQ, D),
            jnp.stack(new_ckv, axis=1), jnp.stack(new_krope, axis=1))
```

```python
import functools

import numpy as np
import jax
import jax.numpy as jnp
from jax import lax
from jax.experimental import pallas as pl
from jax.experimental.pallas import tpu as pltpu

BF = jnp.bfloat16
F32 = jnp.float32

D = 1024
BATCH = 16
SEQ = 256
DEPTH = 2
DEC_BATCH = 2
DEC_SEQ = 2048
PAST_LEN = 256
GRID_W = 64
N_HEADS = 8
QK_NOPE = 64
QK_ROPE = 32
V_HEAD = 64
Q_LORA = 384
KV_LORA = 256
AXIS_ROPE = QK_ROPE // 2
ROPE_THETA = 10000.0
W_CONF = D // 4
CONF_K = 31
W_SC = D // 4
SC_K = 3
W_FN = D // 4
FN_GROUPS = 4
FN_GROUP_W = W_FN // FN_GROUPS
N_BRANCH = 4
FF_HIDDEN = ((8 * D // 3 + 255) // 256) * 256
EPS = 1e-6
OFF_QA = 0
OFF_KVA = OFF_QA + Q_LORA
OFF_CONF = OFF_KVA + KV_LORA + QK_ROPE
OFF_SC = OFF_CONF + 2 * W_CONF
OFF_FN = OFF_SC + 3 * W_SC
OFF_GATE = OFF_FN + W_FN
IN_COLS = OFF_GATE + N_BRANCH * D

N_PROMPT = BATCH * SEQ
N_SAMPLE = DEC_BATCH * DEC_SEQ
N_TOK = N_PROMPT + N_SAMPLE
LANE = 128
SUBLANES = 8
BF16_ROWS = 2 * SUBLANES
HALO = 16
SM_SCALE = float((QK_NOPE + QK_ROPE) ** -0.5)
Q_SCALE = SM_SCALE * float(np.log2(np.e))
VMEM_LIMIT = 56 * 1024 * 1024

QK_END = OFF_KVA + KV_LORA + LANE

ROPE_PERM = np.array(list(range(0, 8)) + list(range(16, 24)) + list(range(8, 16)) + list(range(24, 32)))
ROPE_LANE0 = QK_NOPE


def _dot(a, b):
    return jnp.dot(a, b, preferred_element_type=F32)


def _dot_nt(a, b):
    return lax.dot_general(a, b, (((1,), (1,)), ((), ())), preferred_element_type=F32)


def _sigmoid(x):
    return jax.nn.sigmoid(x)


def _rms(x, g):
    return x * lax.rsqrt(jnp.mean(x * x, axis=-1, keepdims=True) + EPS) * g


def _full(shape):
    return pl.BlockSpec(shape, lambda *_: (0,) * len(shape))


def _layer(l, shape):
    return pl.BlockSpec((1, *shape), lambda *_: (l,) + (0,) * len(shape), pipeline_mode=pl.Buffered(1))


def _params(n_axes):
    return pltpu.CompilerParams(
        dimension_semantics=("arbitrary",) * n_axes, vmem_limit_bytes=VMEM_LIMIT)


def _mod_row(i, n_prompt_tiles, tiles_per_seq):
    return jnp.where(i >= n_prompt_tiles, 1 + (i - n_prompt_tiles) // tiles_per_seq, 0)


def _ada_body(c_ref, w_ref, b_ref, out_ref):
    cv = c_ref[...]
    sc = (cv * _sigmoid(cv)).astype(BF)
    out_ref[0] = _dot(sc, w_ref[0].astype(BF)) + b_ref[0]


def _ada_call(cvec, w_ada, b_ada):
    wc = 2 * D
    return pl.pallas_call(
        _ada_body,
        grid=(DEPTH, 6 * D // wc),
        in_specs=[
            _full((8, D)),
            pl.BlockSpec((1, D, wc), lambda l, j: (l, 0, j)),
            pl.BlockSpec((1, 1, wc), lambda l, j: (l, 0, j)),
        ],
        out_specs=pl.BlockSpec((1, 8, wc), lambda l, j: (l, 0, j)),
        out_shape=jax.ShapeDtypeStruct((DEPTH, 8, 6 * D), F32),
        compiler_params=_params(2),
        name="ada_mod",
    )(cvec, w_ada, b_ada.reshape(DEPTH, 1, 6 * D))


def _ctx_body(ckv_ref, kr_ref, wkvb_ref, place_ref, k_ref, v_ref):
    kv = _dot(ckv_ref[0, 0].astype(BF), wkvb_ref[0])
    krp = _dot(kr_ref[0, 0].astype(BF), place_ref[...])
    for h in range(N_HEADS):
        sl = slice(h * LANE, (h + 1) * LANE)
        k_ref[0, 0, :, sl] = (kv[:, sl] + krp).astype(BF)
    v_ref[0, 0] = (kv[:, N_HEADS * LANE:] + _value_ones_row()).astype(BF)


def _ctx_call(cache_ckv, cache_krope, wkvb_all, place):
    return pl.pallas_call(
        _ctx_body,
        grid=(DEPTH, DEC_BATCH),
        in_specs=[
            pl.BlockSpec((1, 1, PAST_LEN, KV_LORA), lambda l, b: (b, l, 0, 0)),
            pl.BlockSpec((1, 1, PAST_LEN, QK_ROPE), lambda l, b: (b, l, 0, 0)),
            pl.BlockSpec((1, KV_LORA, 2 * N_HEADS * LANE), lambda l, b: (l, 0, 0)),
            _full((QK_ROPE, LANE)),
        ],
        out_specs=[
            pl.BlockSpec((1, 1, PAST_LEN, N_HEADS * LANE), lambda l, b: (l, b, 0, 0)),
            pl.BlockSpec((1, 1, PAST_LEN, N_HEADS * LANE), lambda l, b: (l, b, 0, 0)),
        ],
        out_shape=[
            jax.ShapeDtypeStruct((DEPTH, DEC_BATCH, PAST_LEN, N_HEADS * LANE), BF),
            jax.ShapeDtypeStruct((DEPTH, DEC_BATCH, PAST_LEN, N_HEADS * LANE), BF),
        ],
        compiler_params=_params(2),
        name="ctx_keys",
    )(cache_ckv, cache_krope, wkvb_all, place)


def _take(refs, n):
    return refs[:n], refs[n:]


def _read_x(x_refs, is_sample):
    if len(x_refs) == 1:
        return x_refs[0][...]
    return jnp.where(is_sample, x_refs[1][...], x_refs[0][...])


def _stage1_body(*refs, layer, n_prompt_tiles, tiles_per_seq, n_x, n_cast):
    x_refs, refs = _take(refs, n_x)
    (mod_ref, g1_ref, gqa_ref, gkva_ref, wt_ref, wqb_ref, wkvb_ref, rope_ref), refs = _take(refs, 8)
    cast_in, refs = _take(refs, n_cast)
    (q_ref, k_ref, v_ref, ckv_ref, kr_ref, u0_ref, gcx_ref, gb_ref, fn_ref, gate_ref), cast_out = _take(refs, 10)

    for src, dst in zip(cast_in, cast_out, strict=True):
        dst[...] = src[0].astype(BF)

    i = pl.program_id(0)
    is_sample = i >= n_prompt_tiles
    row = _mod_row(i, n_prompt_tiles, tiles_per_seq)
    sh1 = mod_ref[0, pl.ds(row, 1), 0:D]
    sc1 = mod_ref[0, pl.ds(row, 1), D:2 * D]
    hb = (_rms(_read_x(x_refs, is_sample), _layer_row(g1_ref, layer)[...]) * (1.0 + sc1) + sh1).astype(BF)

    def proj(c0, c1):
        return _dot_nt(hb, wt_ref[0, c0:c1, :])

    pqk = proj(OFF_QA, QK_END)
    q = _dot(_rms(pqk[:, 0:Q_LORA], _layer_row(gqa_ref, layer)[...]).astype(BF), wqb_ref[0])

    ckv = _rms(pqk[:, OFF_KVA:OFF_KVA + KV_LORA], _layer_row(gkva_ref, layer)[...])
    ckv_ref[...] = ckv
    kr3 = pqk[:, OFF_KVA + KV_LORA:QK_END]
    kr_ref[...] = kr3[:, 0:QK_ROPE]
    lane = lax.broadcasted_iota(jnp.int32, (1, LANE), 1)
    group = lambda k: (lane >= ROPE_LANE0 + 8 * k) & (lane < ROPE_LANE0 + 8 * (k + 1))
    krm = jnp.where(group(0) | group(3), pltpu.roll(kr3, ROPE_LANE0, 1),
                    jnp.where(group(1), pltpu.roll(kr3, ROPE_LANE0 - 8, 1),
                              jnp.where(group(2), pltpu.roll(kr3, ROPE_LANE0 + 8, 1), 0.0)))
    kv = _dot(ckv.astype(BF), wkvb_ref[0])
    v_ref[...] = (kv[:, N_HEADS * LANE:] + _value_ones_row()).astype(BF)

    def rope(t):
        return (t * rope_ref[:, 0:LANE]
                + pltpu.roll(t, LANE - 16, 1) * rope_ref[:, LANE:2 * LANE]
                + pltpu.roll(t, 16, 1) * rope_ref[:, 2 * LANE:3 * LANE])

    krr = rope(krm)
    for h in range(N_HEADS):
        sl = slice(h * LANE, (h + 1) * LANE)
        q_ref[:, sl] = (rope(q[:, sl]) * Q_SCALE).astype(BF)
        k_ref[:, sl] = (kv[:, sl] + krr).astype(BF)

    for j in range(N_BRANCH):
        gate_ref[:, j * D:(j + 1) * D] = _sigmoid(
            proj(OFF_GATE + j * D, OFF_GATE + (j + 1) * D)).astype(BF)

    pc = proj(OFF_CONF, OFF_SC)
    u0_ref[...] = pc[:, 0:W_CONF] * _sigmoid(pc[:, W_CONF:])
    ps = proj(OFF_SC, OFF_FN)
    gb_ref[...] = ps[:, 0:W_SC]
    gcx_ref[...] = ps[:, W_SC:2 * W_SC] * ps[:, 2 * W_SC:]
    fn_ref[...] = proj(OFF_FN, OFF_GATE).astype(BF)


def _split_specs(tm, w):
    n_p = N_PROMPT // tm
    return (pl.BlockSpec((tm, w), lambda i: (jnp.minimum(i, n_p - 1), 0)),
            pl.BlockSpec((tm, w), lambda i: (jnp.maximum(i - n_p, 0), 0)))


def _x_specs(x, tm):
    return list(_split_specs(tm, D)) if len(x) == 2 else [pl.BlockSpec((tm, D), lambda i: (i, 0))]


def _stage1_call(l, x, mod_l, g1, gqa, gkva, w_in_t, wqb, wkvb, rope_tab, to_cast, tm):
    n_prompt_tiles = N_PROMPT // tm
    tiles_per_seq = DEC_SEQ // tm
    row_blk = lambda w: pl.BlockSpec((tm, w), lambda i: (i, 0))
    n_steps = N_TOK // tm
    body = functools.partial(_stage1_body, layer=l, n_prompt_tiles=n_prompt_tiles, tiles_per_seq=tiles_per_seq,
                             n_x=len(x), n_cast=len(to_cast))
    slab = lambda w: w.shape[1] // n_steps
    assert all(w.shape[1] % (BF16_ROWS * n_steps) == 0 for w in to_cast)
    outs = pl.pallas_call(
        body,
        grid=(n_steps,),
        in_specs=[
            *_x_specs(x, tm),
            _layer(l, (8, 6 * D)),
            _full((DEPTH, D)),
            _full((DEPTH, Q_LORA)),
            _full((DEPTH, KV_LORA)),
            _layer(0, (IN_COLS, D)),
            _layer(l, (Q_LORA, N_HEADS * LANE)),
            _layer(l, (KV_LORA, 2 * N_HEADS * LANE)),
            pl.BlockSpec((tm, 3 * LANE),
                         lambda i: (jnp.where(i < n_prompt_tiles, 0,
                                              1 + (i - n_prompt_tiles) % tiles_per_seq), 0)),
            *[pl.BlockSpec((1, slab(w), w.shape[2]), lambda i: (l, i, 0)) for w in to_cast],
        ],
        out_specs=[
            row_blk(N_HEADS * LANE), row_blk(N_HEADS * LANE), row_blk(N_HEADS * LANE),
            row_blk(KV_LORA), row_blk(QK_ROPE), row_blk(W_CONF), row_blk(W_SC), row_blk(W_SC),
            row_blk(W_FN), row_blk(N_BRANCH * D),
            *[pl.BlockSpec((slab(w), w.shape[2]), lambda i: (i, 0)) for w in to_cast],
        ],
        out_shape=[
            jax.ShapeDtypeStruct((N_TOK, N_HEADS * LANE), BF),
            jax.ShapeDtypeStruct((N_TOK, N_HEADS * LANE), BF),
            jax.ShapeDtypeStruct((N_TOK, N_HEADS * LANE), BF),
            jax.ShapeDtypeStruct((N_TOK, KV_LORA), F32),
            jax.ShapeDtypeStruct((N_TOK, QK_ROPE), F32),
            jax.ShapeDtypeStruct((N_TOK, W_CONF), F32),
            jax.ShapeDtypeStruct((N_TOK, W_SC), F32),
            jax.ShapeDtypeStruct((N_TOK, W_SC), F32),
            jax.ShapeDtypeStruct((N_TOK, W_FN), BF),
            jax.ShapeDtypeStruct((N_TOK, N_BRANCH * D), BF),
            *[jax.ShapeDtypeStruct(w.shape[1:], BF) for w in to_cast],
        ],
        compiler_params=_params(1),
        name="stage1",
    )(*x, mod_l, g1, gqa, gkva, w_in_t, wqb, wkvb, rope_tab, *to_cast)
    return outs[:10], outs[10:]


def _ones_lane(parity):
    return (1 - parity) * V_HEAD


def _value_ones_row():
    col = lax.broadcasted_iota(jnp.int32, (1, N_HEADS * LANE), 1)
    odd = (col // LANE) % 2
    return (col % LANE == jnp.where(odd == 1, _ones_lane(1), _ones_lane(0))).astype(F32)


def _attend_heads(q_ref, kv_refs, o_ref, ahead):
    lane = lax.broadcasted_iota(jnp.int32, (1, LANE), 1)

    def scores(h):
        sl = slice(h * LANE, (h + 1) * LANE)
        return [_dot_nt(q_ref[:, sl], k_ref[:, sl]) for k_ref, _ in kv_refs]

    pending = [scores(h) for h in range(ahead)]
    for hp in range(N_HEADS // 2):
        vsl = slice(hp * LANE, (hp + 1) * LANE)
        outs = []
        for e in range(2):
            h = 2 * hp + e
            sl = slice(h * LANE, (h + 1) * LANE)
            ss = pending.pop(0)
            if h + ahead < N_HEADS:
                pending.append(scores(h + ahead))
            m = functools.reduce(jnp.maximum, [jnp.max(s, axis=-1, keepdims=True) for s in ss])
            o = None
            for s, (_, v_ref) in zip(ss, kv_refs):
                t = _dot(jnp.exp2(s - m).astype(BF), v_ref[:, sl])
                o = t if o is None else o + t
            ones_lane = _ones_lane(e)
            outs.append(o * (1.0 / o[:, ones_lane:ones_lane + 1]))
        o_ref[:, vsl] = jnp.where(lane < V_HEAD, outs[0], outs[1]).astype(BF)


def _prompt_mix_body(q_ref, k_ref, v_ref, fn_ref, gd_ref, cs_ref, o_ref, f_ref, rhs, *, per_step, scale):
    _fourier_rhs(fn_ref, gd_ref, rhs, SEQ, per_step)
    for b, rows in enumerate(_fourier_rows(cs_ref, rhs, scale, per_step)):
        f_ref[b * SEQ:(b + 1) * SEQ, :] = rows
    for b in range(per_step):
        rows = pl.ds(b * SEQ, SEQ)
        _attend_heads(q_ref.at[rows], [(k_ref.at[rows], v_ref.at[rows])], o_ref.at[rows], ahead=N_HEADS)


def _prompt_mix_call(q, k, v, fn, gd, cs):
    per_step = 4
    blk = lambda w: pl.BlockSpec((per_step * SEQ, w), lambda b: (b, 0))
    body = functools.partial(_prompt_mix_body, per_step=per_step, scale=float((SEQ * FN_GROUP_W) ** -0.5))
    return pl.pallas_call(
        body,
        grid=(BATCH // per_step,),
        in_specs=[blk(N_HEADS * LANE), blk(N_HEADS * LANE), blk(N_HEADS * LANE), blk(W_FN),
                  _full((W_FN, 2 * W_FN)), _full((SEQ, 2 * SEQ))],
        out_specs=[blk(N_HEADS * V_HEAD), blk(W_FN)],
        out_shape=[jax.ShapeDtypeStruct((N_PROMPT, N_HEADS * V_HEAD), BF),
                   jax.ShapeDtypeStruct((N_PROMPT, W_FN), BF)],
        scratch_shapes=[pltpu.VMEM((2 * SEQ, per_step * W_FN), BF)],
        compiler_params=_params(1),
        name="prompt_mix",
    )(q, k, v, fn, gd, cs)


def _attn_sample_body(q_ref, kc_ref, vc_ref, k_ref, v_ref, o_ref):
    _attend_heads(q_ref, [(kc_ref.at[0, 0], vc_ref.at[0, 0]), (k_ref, v_ref)], o_ref, ahead=2)


def _attn_sample_call(l, q, k, v, kc, vc, tq):
    n_q = DEC_SEQ // tq
    q0 = N_PROMPT // tq
    s0 = N_PROMPT // DEC_SEQ
    return pl.pallas_call(
        _attn_sample_body,
        grid=(DEC_BATCH, n_q),
        in_specs=[
            pl.BlockSpec((tq, N_HEADS * LANE), lambda b, j: (q0 + b * n_q + j, 0)),
            pl.BlockSpec((1, 1, PAST_LEN, N_HEADS * LANE), lambda b, j: (l, b, 0, 0)),
            pl.BlockSpec((1, 1, PAST_LEN, N_HEADS * LANE), lambda b, j: (l, b, 0, 0)),
            pl.BlockSpec((DEC_SEQ, N_HEADS * LANE), lambda b, j: (s0 + b, 0)),
            pl.BlockSpec((DEC_SEQ, N_HEADS * LANE), lambda b, j: (s0 + b, 0)),
        ],
        out_specs=pl.BlockSpec((tq, N_HEADS * V_HEAD), lambda b, j: (b * n_q + j, 0)),
        out_shape=jax.ShapeDtypeStruct((N_SAMPLE, N_HEADS * V_HEAD), BF),
        compiler_params=_params(2),
        name="attn_sample",
    )(q, kc, vc, k, v)


CONV_ROWS = 256


def _conv_pass(ubuf, cbuf, gb_ref, wdw_ref, bdw_ref, gln_ref, bln_ref, wsc_ref, uo_ref, so_ref,
               shifted, cshifted):
    tl = CONV_ROWS
    span = shifted.shape[1]
    p0 = HALO - CONF_K // 2
    q0 = HALO - SC_K // 2
    for b in range(1, SUBLANES):
        shifted[b] = ubuf[pl.ds(b, span), :]
    for t in range(SC_K):
        if (q0 + t) % SUBLANES:
            cshifted[t] = cbuf[pl.ds((q0 + t) % SUBLANES, span), :]

    def staged(p, r, n):
        a, b = divmod(p, SUBLANES)
        if b == 0:
            return ubuf[pl.ds(r + SUBLANES * a, n), :]
        return shifted[b, pl.ds(r + SUBLANES * a, n), :]

    def staged_c(t, r, n):
        a, b = divmod(q0 + t, SUBLANES)
        if b == 0:
            return cbuf[pl.ds(r + SUBLANES * a, n), :]
        return cshifted[t, pl.ds(r + SUBLANES * a, n), :]

    rc = 64
    for r in range(0, tl, rc):
        acc = staged(p0, r, rc) * wdw_ref[0, 0:1, :]
        for t in range(1, CONF_K):
            acc = acc + staged(p0 + t, r, rc) * wdw_ref[0, t:t + 1, :]
        acc = acc + bdw_ref[...]
        mu = jnp.mean(acc, axis=-1, keepdims=True)
        cen = acc - mu
        var = jnp.mean(cen * cen, axis=-1, keepdims=True)
        y = cen * lax.rsqrt(var + EPS) * gln_ref[...] + bln_ref[...]
        uo_ref[pl.ds(r, rc), :] = (y * _sigmoid(y)).astype(BF)

        cv = staged_c(0, r, rc) * wsc_ref[0, 0:1, :]
        for t in range(1, SC_K):
            cv = cv + staged_c(t, r, rc) * wsc_ref[0, t:t + 1, :]
        so_ref[pl.ds(r, rc), :] = (gb_ref[pl.ds(r, rc), :] * cv).astype(BF)


def _conv_tile(tm, u_ref, c_ref, gb_ref, halos, is_latent, has_prev, has_next, conv_w, uo_ref, so_ref, scratch,
               hooks=None):
    ubuf, cbuf, shifted, cshifted = scratch
    n_pass = tm // CONV_ROWS
    zeros = jnp.zeros((HALO, W_CONF), F32)
    for p in range(n_pass):
        if hooks and p in hooks:
            hooks[p]()
        r0 = p * CONV_ROWS
        for src, buf, k in ((u_ref, ubuf, 0), (c_ref, cbuf, 2)):
            if p > 0:
                head = jnp.where(is_latent, src[r0 - HALO:r0, :], 0.0)
            else:
                head = zeros if halos is None else jnp.where(has_prev, halos[k][...], 0.0)
            if p < n_pass - 1:
                tail = jnp.where(is_latent, src[r0 + CONV_ROWS:r0 + CONV_ROWS + HALO, :], 0.0)
            else:
                tail = zeros if halos is None else jnp.where(has_next, halos[k + 1][...], 0.0)
            buf[0:HALO, :] = head
            buf[HALO:HALO + CONV_ROWS, :] = src[r0:r0 + CONV_ROWS, :]
            buf[HALO + CONV_ROWS:2 * HALO + CONV_ROWS, :] = tail
        rows = pl.ds(r0, CONV_ROWS)
        _conv_pass(ubuf, cbuf, gb_ref.at[rows], *conv_w, uo_ref.at[rows], so_ref.at[rows], shifted, cshifted)


def _conv_scratch():
    staged = CONV_ROWS + 2 * HALO
    return [pltpu.VMEM((staged, W_CONF), F32), pltpu.VMEM((staged, W_SC), F32),
            pltpu.VMEM((SUBLANES, staged - SUBLANES, W_CONF), F32),
            pltpu.VMEM((SC_K, staged - SUBLANES, W_SC), F32)]


def _fourier_rhs(fn_ref, gd_ref, rhs, seq, group):
    for b in range(group):
        v = _dot(fn_ref[b * seq:(b + 1) * seq, :], gd_ref[...])
        rhs[0:seq, b * W_FN:(b + 1) * W_FN] = v[:, 0:W_FN].astype(BF)
        rhs[seq:2 * seq, b * W_FN:(b + 1) * W_FN] = (-v[:, W_FN:]).astype(BF)


def _fourier_rows(cs_ref, rhs, scale, group):
    res = _dot(cs_ref[...], rhs[...]) * scale
    return [res[:, b * W_FN:(b + 1) * W_FN].astype(BF) for b in range(group)]


def _layer_row(ref, layer):
    return ref.at[pl.ds(layer, 1)]


def _fourier_body(fn_ref, gd_ref, cs_ref, *rest, seq, group, scale, conv_rows, layer):
    if conv_rows:
        (u_ref, c_ref, gb_ref), (wdw_ref, bdw_ref, gln_ref, bln_ref, wsc_ref) = rest[:3], rest[3:8]
        conv_w = (wdw_ref, _layer_row(bdw_ref, layer), _layer_row(gln_ref, layer), _layer_row(bln_ref, layer),
                  wsc_ref)
        out_ref, uo_ref, so_ref, rhs, *conv_scratch = rest[8:]
    else:
        out_ref, rhs = rest

    @pl.when(pl.program_id(1) == 0)
    def _():
        _fourier_rhs(fn_ref, gd_ref, rhs, seq, group)

    def dft_rows(b):
        out_ref[b] = (_dot(cs_ref[...], rhs[:, b * W_FN:(b + 1) * W_FN]) * scale).astype(BF)

    if not conv_rows:
        for b in range(group):
            dft_rows(b)
        return

    is_latent = pl.program_id(1) * conv_rows >= N_PROMPT
    n_pass = conv_rows // CONV_ROWS
    hooks = {(b * n_pass) // group: functools.partial(dft_rows, b) for b in range(group)}
    _conv_tile(conv_rows, u_ref, c_ref, gb_ref, None, is_latent, False, False, conv_w, uo_ref, so_ref,
               conv_scratch, hooks)


def _fourier_call(fn, gd, cs, seq, n_seq, row0, tl, group, conv=None):
    n_t = seq // tl
    g0 = row0 // (group * seq)
    n_steps = (n_seq // group) * n_t
    conv_rows, l = 0, 0
    extra_in, extra_out, extra_shape, extra_scratch, extra_args = [], [], [], [], []
    if conv is not None:
        l, u0, gcx, gb, conv_w = conv
        conv_rows = N_TOK // n_steps
        assert n_seq == group and conv_rows % DEC_SEQ == 0
        blk = pl.BlockSpec((conv_rows, W_CONF), lambda g, j: (j, 0))
        extra_in = [blk, blk, blk, _layer(l, (CONF_K, W_CONF)), _full((DEPTH, W_CONF)),
                    _full((DEPTH, W_CONF)), _full((DEPTH, W_CONF)), _layer(l, (SC_K, W_SC))]
        extra_out = [blk, blk]
        extra_shape = [jax.ShapeDtypeStruct((N_TOK, W_CONF), BF), jax.ShapeDtypeStruct((N_TOK, W_SC), BF)]
        extra_scratch = _conv_scratch()
        extra_args = [u0, gcx, gb, *conv_w]
    body = functools.partial(_fourier_body, seq=seq, group=group, scale=float((seq * FN_GROUP_W) ** -0.5),
                             conv_rows=conv_rows, layer=l)
    outs = pl.pallas_call(
        body,
        grid=(n_seq // group, n_t),
        in_specs=[
            pl.BlockSpec((group * seq, W_FN), lambda g, j: (g0 + g, 0)),
            _full((W_FN, 2 * W_FN)),
            pl.BlockSpec((tl, 2 * seq), lambda g, j: (j, 0)),
            *extra_in,
        ],
        out_specs=[pl.BlockSpec((group, tl, W_FN), lambda g, j: (g, j, 0)), *extra_out],
        out_shape=[jax.ShapeDtypeStruct((n_seq, seq, W_FN), BF), *extra_shape],
        scratch_shapes=[pltpu.VMEM((2 * seq, group * W_FN), BF), *extra_scratch],
        compiler_params=_params(2),
        name=f"fourier_{seq}",
    )(fn, gd, cs, *extra_args)
    f = outs[0].reshape(n_seq * seq, W_FN)
    return f if conv is None else (f, outs[1], outs[2])


def _dft_cos_sin(n):
    r = np.arange(n, dtype=np.int64)
    ang = ((r[:, None] * r[None, :]) % n).astype(np.float64) * (2.0 * np.pi / n)
    return np.cos(ang).astype(np.float32), np.sin(ang).astype(np.float32)


def _stage3_body(*refs, layer, n_prompt_tiles, tiles_per_seq, n_x, final, chunk):
    x_refs, refs = _take(refs, n_x)
    (op_ref, os_ref, u_ref, s_ref, fp_ref, fs_ref, gate_ref, mod_ref, g2_ref, gfin_ref), refs = _take(refs, 10)
    (wo_ref, wpw_ref, wsco_ref, wfn_ref, wout_ref, wg_ref, wu_ref, wd_ref), refs = _take(refs, 8)
    if final:
        out_refs = refs
    else:
        (next_w_ref,), (x_out_ref, next_w_out_ref) = _take(refs, 1)
        next_w_out_ref[...] = next_w_ref[...].astype(BF)
        out_refs = (x_out_ref,)

    i = pl.program_id(0)
    is_sample = i >= n_prompt_tiles
    row = _mod_row(i, n_prompt_tiles, tiles_per_seq)
    mod = lambda k: mod_ref[0, pl.ds(row, 1), k * D:(k + 1) * D]
    pick = lambda p_ref, s_ref: jnp.where(is_sample, s_ref[...], p_ref[...])
    merged = gate_ref[:, 0:D].astype(F32) * _dot(pick(op_ref, os_ref), wo_ref[...])
    merged = merged + gate_ref[:, D:2 * D].astype(F32) * _dot(u_ref[...], wpw_ref[...])
    merged = merged + gate_ref[:, 2 * D:3 * D].astype(F32) * _dot(s_ref[...], wsco_ref[...])
    merged = merged + gate_ref[:, 3 * D:4 * D].astype(F32) * _dot(pick(fp_ref, fs_ref), wfn_ref[...])
    x1 = _read_x(x_refs, is_sample) + mod(2) * _dot(merged.astype(BF), wout_ref[...])

    h2 = (_rms(x1, _layer_row(g2_ref, layer)[...]) * (1.0 + mod(4)) + mod(3)).astype(BF)
    bounds = list(range(0, FF_HIDDEN, chunk)) + [FF_HIDDEN]
    spans = list(zip(bounds[:-1], bounds[1:]))

    def gate_up(span):
        c0, c1 = span
        return _dot(h2, wg_ref[:, c0:c1]), _dot(h2, wu_ref[:, c0:c1])

    acc = None
    ab_next = gate_up(spans[0])
    for n, (c0, c1) in enumerate(spans):
        a, b = ab_next
        if n + 1 < len(spans):
            ab_next = gate_up(spans[n + 1])
        t = _dot((a * _sigmoid(a) * b).astype(BF), wd_ref[c0:c1, :])
        acc = t if acc is None else acc + t
    x2 = x1 + mod(5) * acc
    if not final:
        out_refs[0][...] = x2
        return
    y = _rms(x2, gfin_ref[...])

    @pl.when(is_sample)
    def _():
        out_refs[1][...] = y

    @pl.when(jnp.logical_not(is_sample))
    def _():
        out_refs[0][...] = y


def _stage3_call(l, x, o_p, o_s, u, s, f_p, f_s, gates, mod_l, g2, gfin, w3, w_in_t_f32, tm, final):
    n_tiles = N_TOK // tm
    row_blk = lambda w: pl.BlockSpec((tm, w), lambda i: (i, 0))
    body = functools.partial(_stage3_body, layer=l, n_prompt_tiles=N_PROMPT // tm, tiles_per_seq=DEC_SEQ // tm,
                             n_x=len(x), final=final, chunk=256)
    if final:
        extra_in, extra_args = [], []
        out_specs = list(_split_specs(tm, D))
        out_shape = [jax.ShapeDtypeStruct((N_PROMPT, D), F32), jax.ShapeDtypeStruct((N_SAMPLE, D), F32)]
    else:
        slab = -(-IN_COLS // (n_tiles * BF16_ROWS)) * BF16_ROWS
        extra_in = [pl.BlockSpec((1, slab, D), lambda i: (l + 1, i, 0))]
        extra_args = [w_in_t_f32]
        out_specs = [row_blk(D), pl.BlockSpec((1, slab, D), lambda i: (0, i, 0))]
        out_shape = [jax.ShapeDtypeStruct((N_TOK, D), F32), jax.ShapeDtypeStruct((1, IN_COLS, D), BF)]
    return pl.pallas_call(
        body,
        grid=(n_tiles,),
        in_specs=[
            *_x_specs(x, tm), *_split_specs(tm, N_HEADS * V_HEAD), row_blk(W_CONF), row_blk(W_SC),
            *_split_specs(tm, W_FN),
            row_blk(N_BRANCH * D), _layer(l, (8, 6 * D)), _full((DEPTH, D)), _full((1, D)),
            _full((N_HEADS * V_HEAD, D)), _full((W_CONF, D)), _full((W_SC, D)), _full((W_FN, D)),
            _full((D, D)), _full((D, FF_HIDDEN)), _full((D, FF_HIDDEN)), _full((FF_HIDDEN, D)),
            *extra_in,
        ],
        out_specs=out_specs,
        out_shape=out_shape,
        compiler_params=_params(1),
        name="stage3",
    )(*x, o_p, o_s, u, s, f_p, f_s, gates, mod_l, g2, gfin, *w3, *extra_args)


def _rope_table(tm):
    rows = DEC_SEQ // GRID_W
    row_pos = np.repeat(np.arange(rows, dtype=np.float64), GRID_W)
    col_pos = np.tile(np.arange(GRID_W, dtype=np.float64), rows)
    inv = ROPE_THETA ** (-np.arange(0, AXIS_ROPE, 2, dtype=np.float64) / AXIS_ROPE)
    ang = np.concatenate([row_pos[:, None] * inv, col_pos[:, None] * inv], axis=1)
    half = QK_ROPE // 2
    tab = np.zeros((tm + DEC_SEQ, 3 * LANE), np.float64)
    tab[:, 0:LANE] = 1.0
    tab[tm:, ROPE_LANE0:ROPE_LANE0 + half] = np.cos(ang)
    tab[tm:, ROPE_LANE0 + half:ROPE_LANE0 + QK_ROPE] = np.cos(ang)
    tab[tm:, LANE + ROPE_LANE0:LANE + ROPE_LANE0 + half] = -np.sin(ang)
    tab[tm:, 2 * LANE + ROPE_LANE0 + half:2 * LANE + ROPE_LANE0 + QK_ROPE] = np.sin(ang)
    return jnp.asarray(tab.astype(np.float32))


def _head_layout_body(wqb_ref, wkvb_ref, pq_ref, pkv_ref, qb_ref, kvb_ref):
    qb_ref[0] = _dot(wqb_ref[0].astype(BF), pq_ref[...]).astype(BF)
    kvb_ref[0] = _dot(wkvb_ref[0].astype(BF), pkv_ref[...]).astype(BF)


def _head_layout_call(w_qb, w_kvb):
    dqk = QK_NOPE + QK_ROPE
    pq = np.zeros((N_HEADS * dqk, N_HEADS * LANE), np.float32)
    pkv = np.zeros((N_HEADS * LANE, 2 * N_HEADS * LANE), np.float32)
    for h in range(N_HEADS):
        src = h * dqk + np.concatenate([np.arange(QK_NOPE), QK_NOPE + ROPE_PERM])
        pq[src, h * LANE + np.arange(dqk)] = 1.0
        nope = np.arange(QK_NOPE)
        pkv[h * LANE + nope, h * LANE + nope] = 1.0
        val = np.arange(V_HEAD)
        pkv[h * LANE + QK_NOPE + val, (N_HEADS + h) * LANE + (h % 2) * V_HEAD + val] = 1.0
    return pl.pallas_call(
        _head_layout_body,
        grid=(DEPTH,),
        in_specs=[pl.BlockSpec((1, Q_LORA, N_HEADS * dqk), lambda l: (l, 0, 0)),
                  pl.BlockSpec((1, KV_LORA, N_HEADS * LANE), lambda l: (l, 0, 0)),
                  _full(pq.shape), _full(pkv.shape)],
        out_specs=[pl.BlockSpec((1, Q_LORA, N_HEADS * LANE), lambda l: (l, 0, 0)),
                   pl.BlockSpec((1, KV_LORA, 2 * N_HEADS * LANE), lambda l: (l, 0, 0))],
        out_shape=[jax.ShapeDtypeStruct((DEPTH, Q_LORA, N_HEADS * LANE), BF),
                   jax.ShapeDtypeStruct((DEPTH, KV_LORA, 2 * N_HEADS * LANE), BF)],
        compiler_params=_params(1),
        name="head_layout",
    )(w_qb, w_kvb, jnp.asarray(pq, BF), jnp.asarray(pkv, BF))


def kernel(x_prompt, x_sample, cache_ckv, cache_krope, c, c_ctx, w_ada, b_ada, g_norm1, g_norm2, w_in, g_qa, w_qb, g_kva, w_kvb, w_o_mla, w_conf_dw, b_conf_dw, g_conf_ln, b_conf_ln, w_conf_pw, w_sc_conv, w_sc_out, w_fn, w_out, w_ffn_gate, w_ffn_up, w_ffn_down, g_final):
    tm = 512
    xp = x_prompt.reshape(N_PROMPT, D)
    xs = x_sample.reshape(N_SAMPLE, D)
    cvec = jnp.concatenate([c_ctx[None, :], c, jnp.zeros((8 - 1 - DEC_BATCH, D), F32)], axis=0)
    mod = _ada_call(cvec, w_ada, b_ada)

    rope_tab = _rope_table(tm)
    place = np.zeros((QK_ROPE, LANE), np.float32)
    place[ROPE_PERM, ROPE_LANE0 + np.arange(QK_ROPE)] = 1.0
    wqb, wkvb_all = _head_layout_call(w_qb, w_kvb)
    kc, vc = _ctx_call(cache_ckv, cache_krope, wkvb_all, jnp.asarray(place, BF))

    cg, sg = _dft_cos_sin(FN_GROUP_W)
    eye = np.eye(FN_GROUPS, dtype=np.float32)
    gd = jnp.asarray(np.concatenate([np.kron(eye, cg), np.kron(eye, sg)], axis=1)).astype(BF)
    cs_p = jnp.asarray(np.concatenate(_dft_cos_sin(SEQ), axis=1)).astype(BF)
    cs_s = jnp.asarray(np.concatenate(_dft_cos_sin(DEC_SEQ), axis=1)).astype(BF)

    w_in_t_f32 = jnp.swapaxes(w_in, 1, 2)
    w_in_t = w_in_t_f32[0:1].astype(BF)
    w3_f32 = (w_o_mla, w_conf_pw, w_sc_out, w_fn, w_out, w_ffn_gate, w_ffn_up, w_ffn_down)

    conv_w = (w_conf_dw, b_conf_dw, g_conf_ln, b_conf_ln, w_sc_conv)
    x = (xp, xs)
    new_ckv, new_krope = [], []
    for l in range(DEPTH):
        (q, k, v, ckv, kr, u0, gcx, gb, fn, gates), w3 = _stage1_call(
            l, x, mod, g_norm1, g_qa, g_kva, w_in_t, wqb, wkvb_all, rope_tab, w3_f32, tm)
        new_ckv.append(ckv[:N_PROMPT].reshape(BATCH, SEQ, KV_LORA))
        new_krope.append(kr[:N_PROMPT].reshape(BATCH, SEQ, QK_ROPE))

        o_p, f_p = _prompt_mix_call(q, k, v, fn, gd, cs_p)
        o_s = _attn_sample_call(l, q, k, v, kc, vc, 512)
        f_s, u, s = _fourier_call(fn, gd, cs_s, DEC_SEQ, DEC_BATCH, N_PROMPT, 512, DEC_BATCH,
                                  conv=(l, u0, gcx, gb, conv_w))
        out = _stage3_call(l, x, o_p, o_s, u, s, f_p, f_s, gates, mod, g_norm2, g_final[None, :], w3,
                           w_in_t_f32, tm, l == DEPTH - 1)
        if l < DEPTH - 1:
            x, w_in_t = (out[0],), out[1]

    y_prompt, y_sample = out
    return (y_prompt.reshape(BATCH, SEQ, D), y_sample.reshape(DEC_BATCH, DEC_SEQ, D),
            jnp.stack(new_ckv, axis=1), jnp.stack(new_krope, axis=1))
```

```python
import functools

import numpy as np
import jax
import jax.numpy as jnp
from jax import lax
from jax.experimental import pallas as pl
from jax.experimental.pallas import tpu as pltpu

BF = jnp.bfloat16
F32 = jnp.float32

D = 1024
BATCH = 16
SEQ = 256
DEPTH = 2
DEC_BATCH = 2
DEC_SEQ = 2048
PAST_LEN = 256
GRID_W = 64
N_HEADS = 8
QK_NOPE = 64
QK_ROPE = 32
V_HEAD = 64
Q_LORA = 384
KV_LORA = 256
AXIS_ROPE = QK_ROPE // 2
ROPE_THETA = 10000.0
W_CONF = D // 4
CONF_K = 31
W_SC = D // 4
SC_K = 3
W_FN = D // 4
FN_GROUPS = 4
FN_GROUP_W = W_FN // FN_GROUPS
N_BRANCH = 4
FF_HIDDEN = ((8 * D // 3 + 255) // 256) * 256
EPS = 1e-6
OFF_QA = 0
OFF_KVA = OFF_QA + Q_LORA
OFF_CONF = OFF_KVA + KV_LORA + QK_ROPE
OFF_SC = OFF_CONF + 2 * W_CONF
OFF_FN = OFF_SC + 3 * W_SC
OFF_GATE = OFF_FN + W_FN
IN_COLS = OFF_GATE + N_BRANCH * D

N_PROMPT = BATCH * SEQ
N_SAMPLE = DEC_BATCH * DEC_SEQ
N_TOK = N_PROMPT + N_SAMPLE
LANE = 128
SUBLANES = 8
BF16_ROWS = 2 * SUBLANES
HALO = 16
SM_SCALE = float((QK_NOPE + QK_ROPE) ** -0.5)
Q_SCALE = SM_SCALE * float(np.log2(np.e))
VMEM_LIMIT = 56 * 1024 * 1024

QK_END = OFF_KVA + KV_LORA + LANE

ROPE_PERM = np.array(list(range(0, 8)) + list(range(16, 24)) + list(range(8, 16)) + list(range(24, 32)))
ROPE_LANE0 = QK_NOPE


def _dot(a, b):
    return jnp.dot(a, b, preferred_element_type=F32)


def _dot_nt(a, b):
    return lax.dot_general(a, b, (((1,), (1,)), ((), ())), preferred_element_type=F32)


def _sigmoid(x):
    return jax.nn.sigmoid(x)


def _rms(x, g):
    return x * lax.rsqrt(jnp.mean(x * x, axis=-1, keepdims=True) + EPS) * g


def _full(shape):
    return pl.BlockSpec(shape, lambda *_: (0,) * len(shape))


def _layer(l, shape):
    return pl.BlockSpec((1, *shape), lambda *_: (l,) + (0,) * len(shape), pipeline_mode=pl.Buffered(1))


def _params(n_axes):
    return pltpu.CompilerParams(
        dimension_semantics=("arbitrary",) * n_axes, vmem_limit_bytes=VMEM_LIMIT)


def _mod_row(i, n_prompt_tiles, tiles_per_seq):
    return jnp.where(i >= n_prompt_tiles, 1 + (i - n_prompt_tiles) // tiles_per_seq, 0)


def _ada_body(c_ref, w_ref, b_ref, out_ref):
    cv = c_ref[...]
    sc = (cv * _sigmoid(cv)).astype(BF)
    out_ref[0] = _dot(sc, w_ref[0].astype(BF)) + b_ref[0]


def _ada_call(cvec, w_ada, b_ada):
    wc = 2 * D
    return pl.pallas_call(
        _ada_body,
        grid=(DEPTH, 6 * D // wc),
        in_specs=[
            _full((8, D)),
            pl.BlockSpec((1, D, wc), lambda l, j: (l, 0, j)),
            pl.BlockSpec((1, 1, wc), lambda l, j: (l, 0, j)),
        ],
        out_specs=pl.BlockSpec((1, 8, wc), lambda l, j: (l, 0, j)),
        out_shape=jax.ShapeDtypeStruct((DEPTH, 8, 6 * D), F32),
        compiler_params=_params(2),
        name="ada_mod",
    )(cvec, w_ada, b_ada.reshape(DEPTH, 1, 6 * D))


def _ctx_body(ckv_ref, kr_ref, wkvb_ref, place_ref, k_ref, v_ref):
    kv = _dot(ckv_ref[0, 0].astype(BF), wkvb_ref[0])
    krp = _dot(kr_ref[0, 0].astype(BF), place_ref[...])
    for h in range(N_HEADS):
        sl = slice(h * LANE, (h + 1) * LANE)
        k_ref[0, 0, :, sl] = (kv[:, sl] + krp).astype(BF)
    v_ref[0, 0] = (kv[:, N_HEADS * LANE:] + _value_ones_row()).astype(BF)


def _ctx_call(cache_ckv, cache_krope, wkvb_all, place):
    return pl.pallas_call(
        _ctx_body,
        grid=(DEPTH, DEC_BATCH),
        in_specs=[
            pl.BlockSpec((1, 1, PAST_LEN, KV_LORA), lambda l, b: (b, l, 0, 0)),
            pl.BlockSpec((1, 1, PAST_LEN, QK_ROPE), lambda l, b: (b, l, 0, 0)),
            pl.BlockSpec((1, KV_LORA, 2 * N_HEADS * LANE), lambda l, b: (l, 0, 0)),
            _full((QK_ROPE, LANE)),
        ],
        out_specs=[
            pl.BlockSpec((1, 1, PAST_LEN, N_HEADS * LANE), lambda l, b: (l, b, 0, 0)),
            pl.BlockSpec((1, 1, PAST_LEN, N_HEADS * LANE), lambda l, b: (l, b, 0, 0)),
        ],
        out_shape=[
            jax.ShapeDtypeStruct((DEPTH, DEC_BATCH, PAST_LEN, N_HEADS * LANE), BF),
            jax.ShapeDtypeStruct((DEPTH, DEC_BATCH, PAST_LEN, N_HEADS * LANE), BF),
        ],
        compiler_params=_params(2),
        name="ctx_keys",
    )(cache_ckv, cache_krope, wkvb_all, place)


def _take(refs, n):
    return refs[:n], refs[n:]


def _read_x(x_refs, is_sample):
    if len(x_refs) == 1:
        return x_refs[0][...]
    return jnp.where(is_sample, x_refs[1][...], x_refs[0][...])


def _stage1_body(*refs, layer, n_prompt_tiles, tiles_per_seq, n_x, n_cast):
    x_refs, refs = _take(refs, n_x)
    (mod_ref, g1_ref, gqa_ref, gkva_ref, wt_ref, wqb_ref, wkvb_ref, rope_ref), refs = _take(refs, 8)
    cast_in, refs = _take(refs, n_cast)
    (q_ref, k_ref, v_ref, ckv_ref, kr_ref, u0_ref, gcx_ref, gb_ref, fn_ref, gate_ref), cast_out = _take(refs, 10)

    for src, dst in zip(cast_in, cast_out, strict=True):
        dst[...] = src[0].astype(BF)

    i = pl.program_id(0)
    is_sample = i >= n_prompt_tiles
    row = _mod_row(i, n_prompt_tiles, tiles_per_seq)
    sh1 = mod_ref[0, pl.ds(row, 1), 0:D]
    sc1 = mod_ref[0, pl.ds(row, 1), D:2 * D]
    hb = (_rms(_read_x(x_refs, is_sample), _layer_row(g1_ref, layer)[...]) * (1.0 + sc1) + sh1).astype(BF)

    def proj(c0, c1):
        return _dot_nt(hb, wt_ref[0, c0:c1, :])

    pqk = proj(OFF_QA, QK_END)
    q = _dot(_rms(pqk[:, 0:Q_LORA], _layer_row(gqa_ref, layer)[...]).astype(BF), wqb_ref[0])

    ckv = _rms(pqk[:, OFF_KVA:OFF_KVA + KV_LORA], _layer_row(gkva_ref, layer)[...])
    ckv_ref[...] = ckv
    kr3 = pqk[:, OFF_KVA + KV_LORA:QK_END]
    kr_ref[...] = kr3[:, 0:QK_ROPE]
    lane = lax.broadcasted_iota(jnp.int32, (1, LANE), 1)
    group = lambda k: (lane >= ROPE_LANE0 + 8 * k) & (lane < ROPE_LANE0 + 8 * (k + 1))
    krm = jnp.where(group(0) | group(3), pltpu.roll(kr3, ROPE_LANE0, 1),
                    jnp.where(group(1), pltpu.roll(kr3, ROPE_LANE0 - 8, 1),
                              jnp.where(group(2), pltpu.roll(kr3, ROPE_LANE0 + 8, 1), 0.0)))
    kv = _dot(ckv.astype(BF), wkvb_ref[0])
    v_ref[...] = (kv[:, N_HEADS * LANE:] + _value_ones_row()).astype(BF)

    def rope(t):
        return (t * rope_ref[:, 0:LANE]
                + pltpu.roll(t, LANE - 16, 1) * rope_ref[:, LANE:2 * LANE]
                + pltpu.roll(t, 16, 1) * rope_ref[:, 2 * LANE:3 * LANE])

    krr = rope(krm)
    for h in range(N_HEADS):
        sl = slice(h * LANE, (h + 1) * LANE)
        q_ref[:, sl] = (rope(q[:, sl]) * Q_SCALE).astype(BF)
        k_ref[:, sl] = (kv[:, sl] + krr).astype(BF)

    for j in range(N_BRANCH):
        gate_ref[:, j * D:(j + 1) * D] = _sigmoid(
            proj(OFF_GATE + j * D, OFF_GATE + (j + 1) * D)).astype(BF)

    pc = proj(OFF_CONF, OFF_SC)
    u0_ref[...] = pc[:, 0:W_CONF] * _sigmoid(pc[:, W_CONF:])
    ps = proj(OFF_SC, OFF_FN)
    gb_ref[...] = ps[:, 0:W_SC]
    gcx_ref[...] = ps[:, W_SC:2 * W_SC] * ps[:, 2 * W_SC:]
    fn_ref[...] = proj(OFF_FN, OFF_GATE).astype(BF)


def _split_specs(tm, w):
    n_p = N_PROMPT // tm
    return (pl.BlockSpec((tm, w), lambda i: (jnp.minimum(i, n_p - 1), 0)),
            pl.BlockSpec((tm, w), lambda i: (jnp.maximum(i - n_p, 0), 0)))


def _x_specs(x, tm):
    return list(_split_specs(tm, D)) if len(x) == 2 else [pl.BlockSpec((tm, D), lambda i: (i, 0))]


def _stage1_call(l, x, mod_l, g1, gqa, gkva, w_in_t, wqb, wkvb, rope_tab, to_cast, tm):
    n_prompt_tiles = N_PROMPT // tm
    tiles_per_seq = DEC_SEQ // tm
    row_blk = lambda w: pl.BlockSpec((tm, w), lambda i: (i, 0))
    n_steps = N_TOK // tm
    body = functools.partial(_stage1_body, layer=l, n_prompt_tiles=n_prompt_tiles, tiles_per_seq=tiles_per_seq,
                             n_x=len(x), n_cast=len(to_cast))
    slab = lambda w: w.shape[1] // n_steps
    assert all(w.shape[1] % (BF16_ROWS * n_steps) == 0 for w in to_cast)
    outs = pl.pallas_call(
        body,
        grid=(n_steps,),
        in_specs=[
            *_x_specs(x, tm),
            _layer(l, (8, 6 * D)),
            _full((DEPTH, D)),
            _full((DEPTH, Q_LORA)),
            _full((DEPTH, KV_LORA)),
            _layer(0, (IN_COLS, D)),
            _layer(l, (Q_LORA, N_HEADS * LANE)),
            _layer(l, (KV_LORA, 2 * N_HEADS * LANE)),
            pl.BlockSpec((tm, 3 * LANE),
                         lambda i: (jnp.where(i < n_prompt_tiles, 0,
                                              1 + (i - n_prompt_tiles) % tiles_per_seq), 0)),
            *[pl.BlockSpec((1, slab(w), w.shape[2]), lambda i: (l, i, 0)) for w in to_cast],
        ],
        out_specs=[
            row_blk(N_HEADS * LANE), row_blk(N_HEADS * LANE), row_blk(N_HEADS * LANE),
            row_blk(KV_LORA), row_blk(QK_ROPE), row_blk(W_CONF), row_blk(W_SC), row_blk(W_SC),
            row_blk(W_FN), row_blk(N_BRANCH * D),
            *[pl.BlockSpec((slab(w), w.shape[2]), lambda i: (i, 0)) for w in to_cast],
        ],
        out_shape=[
            jax.ShapeDtypeStruct((N_TOK, N_HEADS * LANE), BF),
            jax.ShapeDtypeStruct((N_TOK, N_HEADS * LANE), BF),
            jax.ShapeDtypeStruct((N_TOK, N_HEADS * LANE), BF),
            jax.ShapeDtypeStruct((N_TOK, KV_LORA), F32),
            jax.ShapeDtypeStruct((N_TOK, QK_ROPE), F32),
            jax.ShapeDtypeStruct((N_TOK, W_CONF), F32),
            jax.ShapeDtypeStruct((N_TOK, W_SC), F32),
            jax.ShapeDtypeStruct((N_TOK, W_SC), F32),
            jax.ShapeDtypeStruct((N_TOK, W_FN), BF),
            jax.ShapeDtypeStruct((N_TOK, N_BRANCH * D), BF),
            *[jax.ShapeDtypeStruct(w.shape[1:], BF) for w in to_cast],
        ],
        compiler_params=_params(1),
        name="stage1",
    )(*x, mod_l, g1, gqa, gkva, w_in_t, wqb, wkvb, rope_tab, *to_cast)
    return outs[:10], outs[10:]


def _ones_lane(parity):
    return (1 - parity) * V_HEAD


def _value_ones_row():
    col = lax.broadcasted_iota(jnp.int32, (1, N_HEADS * LANE), 1)
    odd = (col // LANE) % 2
    return (col % LANE == jnp.where(odd == 1, _ones_lane(1), _ones_lane(0))).astype(F32)


def _attend_heads(q_ref, kv_refs, o_ref, ahead):
    lane = lax.broadcasted_iota(jnp.int32, (1, LANE), 1)

    def scores(h):
        sl = slice(h * LANE, (h + 1) * LANE)
        return [_dot_nt(q_ref[:, sl], k_ref[:, sl]) for k_ref, _ in kv_refs]

    pending = [scores(h) for h in range(ahead)]
    for hp in range(N_HEADS // 2):
        vsl = slice(hp * LANE, (hp + 1) * LANE)
        outs = []
        for e in range(2):
            h = 2 * hp + e
            sl = slice(h * LANE, (h + 1) * LANE)
            ss = pending.pop(0)
            if h + ahead < N_HEADS:
                pending.append(scores(h + ahead))
            m = functools.reduce(jnp.maximum, [jnp.max(s, axis=-1, keepdims=True) for s in ss])
            o = None
            for s, (_, v_ref) in zip(ss, kv_refs):
                t = _dot(jnp.exp2(s - m).astype(BF), v_ref[:, sl])
                o = t if o is None else o + t
            ones_lane = _ones_lane(e)
            outs.append(o * (1.0 / o[:, ones_lane:ones_lane + 1]))
        o_ref[:, vsl] = jnp.where(lane < V_HEAD, outs[0], outs[1]).astype(BF)


def _prompt_mix_body(q_ref, k_ref, v_ref, fn_ref, gd_ref, cs_ref, o_ref, f_ref, rhs, *, per_step, scale):
    _fourier_rhs(fn_ref, gd_ref, rhs, SEQ, per_step)
    for b, rows in enumerate(_fourier_rows(cs_ref, rhs, scale, per_step)):
        f_ref[b * SEQ:(b + 1) * SEQ, :] = rows
    for b in range(per_step):
        rows = pl.ds(b * SEQ, SEQ)
        _attend_heads(q_ref.at[rows], [(k_ref.at[rows], v_ref.at[rows])], o_ref.at[rows], ahead=N_HEADS)


def _prompt_mix_call(q, k, v, fn, gd, cs):
    per_step = 4
    blk = lambda w: pl.BlockSpec((per_step * SEQ, w), lambda b: (b, 0))
    body = functools.partial(_prompt_mix_body, per_step=per_step, scale=float((SEQ * FN_GROUP_W) ** -0.5))
    return pl.pallas_call(
        body,
        grid=(BATCH // per_step,),
        in_specs=[blk(N_HEADS * LANE), blk(N_HEADS * LANE), blk(N_HEADS * LANE), blk(W_FN),
                  _full((W_FN, 2 * W_FN)), _full((SEQ, 2 * SEQ))],
        out_specs=[blk(N_HEADS * V_HEAD), blk(W_FN)],
        out_shape=[jax.ShapeDtypeStruct((N_PROMPT, N_HEADS * V_HEAD), BF),
                   jax.ShapeDtypeStruct((N_PROMPT, W_FN), BF)],
        scratch_shapes=[pltpu.VMEM((2 * SEQ, per_step * W_FN), BF)],
        compiler_params=_params(1),
        name="prompt_mix",
    )(q, k, v, fn, gd, cs)


def _attn_sample_body(q_ref, kc_ref, vc_ref, k_ref, v_ref, o_ref):
    _attend_heads(q_ref, [(kc_ref.at[0, 0], vc_ref.at[0, 0]), (k_ref, v_ref)], o_ref, ahead=2)


def _attn_sample_call(l, q, k, v, kc, vc, tq):
    n_q = DEC_SEQ // tq
    q0 = N_PROMPT // tq
    s0 = N_PROMPT // DEC_SEQ
    return pl.pallas_call(
        _attn_sample_body,
        grid=(DEC_BATCH, n_q),
        in_specs=[
            pl.BlockSpec((tq, N_HEADS * LANE), lambda b, j: (q0 + b * n_q + j, 0)),
            pl.BlockSpec((1, 1, PAST_LEN, N_HEADS * LANE), lambda b, j: (l, b, 0, 0)),
            pl.BlockSpec((1, 1, PAST_LEN, N_HEADS * LANE), lambda b, j: (l, b, 0, 0)),
            pl.BlockSpec((DEC_SEQ, N_HEADS * LANE), lambda b, j: (s0 + b, 0)),
            pl.BlockSpec((DEC_SEQ, N_HEADS * LANE), lambda b, j: (s0 + b, 0)),
        ],
        out_specs=pl.BlockSpec((tq, N_HEADS * V_HEAD), lambda b, j: (b * n_q + j, 0)),
        out_shape=jax.ShapeDtypeStruct((N_SAMPLE, N_HEADS * V_HEAD), BF),
        compiler_params=_params(2),
        name="attn_sample",
    )(q, kc, vc, k, v)


CONV_ROWS = 256


def _conv_pass(ubuf, cbuf, gb_ref, wdw_ref, bdw_ref, gln_ref, bln_ref, wsc_ref, uo_ref, so_ref,
               shifted, cshifted):
    tl = CONV_ROWS
    span = shifted.shape[1]
    p0 = HALO - CONF_K // 2
    q0 = HALO - SC_K // 2
    for b in range(1, SUBLANES):
        shifted[b] = ubuf[pl.ds(b, span), :]
    for t in range(SC_K):
        if (q0 + t) % SUBLANES:
            cshifted[t] = cbuf[pl.ds((q0 + t) % SUBLANES, span), :]

    def staged(p, r, n):
        a, b = divmod(p, SUBLANES)
        if b == 0:
            return ubuf[pl.ds(r + SUBLANES * a, n), :]
        return shifted[b, pl.ds(r + SUBLANES * a, n), :]

    def staged_c(t, r, n):
        a, b = divmod(q0 + t, SUBLANES)
        if b == 0:
            return cbuf[pl.ds(r + SUBLANES * a, n), :]
        return cshifted[t, pl.ds(r + SUBLANES * a, n), :]

    rc = 32
    for r in range(0, tl, rc):
        acc = staged(p0, r, rc) * wdw_ref[0, 0:1, :]
        for t in range(1, CONF_K):
            acc = acc + staged(p0 + t, r, rc) * wdw_ref[0, t:t + 1, :]
        acc = acc + bdw_ref[...]
        mu = jnp.mean(acc, axis=-1, keepdims=True)
        cen = acc - mu
        var = jnp.mean(cen * cen, axis=-1, keepdims=True)
        y = cen * lax.rsqrt(var + EPS) * gln_ref[...] + bln_ref[...]
        uo_ref[pl.ds(r, rc), :] = (y * _sigmoid(y)).astype(BF)

        cv = staged_c(0, r, rc) * wsc_ref[0, 0:1, :]
        for t in range(1, SC_K):
            cv = cv + staged_c(t, r, rc) * wsc_ref[0, t:t + 1, :]
        so_ref[pl.ds(r, rc), :] = (gb_ref[pl.ds(r, rc), :] * cv).astype(BF)


def _conv_tile(tm, u_ref, c_ref, gb_ref, halos, is_latent, has_prev, has_next, conv_w, uo_ref, so_ref, scratch,
               hooks=None):
    ubuf, cbuf, shifted, cshifted = scratch
    n_pass = tm // CONV_ROWS
    zeros = jnp.zeros((HALO, W_CONF), F32)
    for p in range(n_pass):
        if hooks and p in hooks:
            hooks[p]()
        r0 = p * CONV_ROWS
        for src, buf, k in ((u_ref, ubuf, 0), (c_ref, cbuf, 2)):
            if p > 0:
                head = jnp.where(is_latent, src[r0 - HALO:r0, :], 0.0)
            else:
                head = zeros if halos is None else jnp.where(has_prev, halos[k][...], 0.0)
            if p < n_pass - 1:
                tail = jnp.where(is_latent, src[r0 + CONV_ROWS:r0 + CONV_ROWS + HALO, :], 0.0)
            else:
                tail = zeros if halos is None else jnp.where(has_next, halos[k + 1][...], 0.0)
            buf[0:HALO, :] = head
            buf[HALO:HALO + CONV_ROWS, :] = src[r0:r0 + CONV_ROWS, :]
            buf[HALO + CONV_ROWS:2 * HALO + CONV_ROWS, :] = tail
        rows = pl.ds(r0, CONV_ROWS)
        _conv_pass(ubuf, cbuf, gb_ref.at[rows], *conv_w, uo_ref.at[rows], so_ref.at[rows], shifted, cshifted)


def _conv_scratch():
    staged = CONV_ROWS + 2 * HALO
    return [pltpu.VMEM((staged, W_CONF), F32), pltpu.VMEM((staged, W_SC), F32),
            pltpu.VMEM((SUBLANES, staged - SUBLANES, W_CONF), F32),
            pltpu.VMEM((SC_K, staged - SUBLANES, W_SC), F32)]


def _fourier_rhs(fn_ref, gd_ref, rhs, seq, group):
    for b in range(group):
        v = _dot(fn_ref[b * seq:(b + 1) * seq, :], gd_ref[...])
        rhs[0:seq, b * W_FN:(b + 1) * W_FN] = v[:, 0:W_FN].astype(BF)
        rhs[seq:2 * seq, b * W_FN:(b + 1) * W_FN] = (-v[:, W_FN:]).astype(BF)


def _fourier_rows(cs_ref, rhs, scale, group):
    res = _dot(cs_ref[...], rhs[...]) * scale
    return [res[:, b * W_FN:(b + 1) * W_FN].astype(BF) for b in range(group)]


def _layer_row(ref, layer):
    return ref.at[pl.ds(layer, 1)]


def _fourier_body(fn_ref, gd_ref, cs_ref, *rest, seq, group, scale, conv_rows, layer):
    if conv_rows:
        (u_ref, c_ref, gb_ref), (wdw_ref, bdw_ref, gln_ref, bln_ref, wsc_ref) = rest[:3], rest[3:8]
        conv_w = (wdw_ref, _layer_row(bdw_ref, layer), _layer_row(gln_ref, layer), _layer_row(bln_ref, layer),
                  wsc_ref)
        out_ref, uo_ref, so_ref, rhs, *conv_scratch = rest[8:]
    else:
        out_ref, rhs = rest

    @pl.when(pl.program_id(1) == 0)
    def _():
        _fourier_rhs(fn_ref, gd_ref, rhs, seq, group)

    def dft_rows(b):
        out_ref[b] = (_dot(cs_ref[...], rhs[:, b * W_FN:(b + 1) * W_FN]) * scale).astype(BF)

    if not conv_rows:
        for b in range(group):
            dft_rows(b)
        return

    is_latent = pl.program_id(1) * conv_rows >= N_PROMPT
    n_pass = conv_rows // CONV_ROWS
    hooks = {(b * n_pass) // group: functools.partial(dft_rows, b) for b in range(group)}
    _conv_tile(conv_rows, u_ref, c_ref, gb_ref, None, is_latent, False, False, conv_w, uo_ref, so_ref,
               conv_scratch, hooks)


def _fourier_call(fn, gd, cs, seq, n_seq, row0, tl, group, conv=None):
    n_t = seq // tl
    g0 = row0 // (group * seq)
    n_steps = (n_seq // group) * n_t
    conv_rows, l = 0, 0
    extra_in, extra_out, extra_shape, extra_scratch, extra_args = [], [], [], [], []
    if conv is not None:
        l, u0, gcx, gb, conv_w = conv
        conv_rows = N_TOK // n_steps
        assert n_seq == group and conv_rows % DEC_SEQ == 0
        blk = pl.BlockSpec((conv_rows, W_CONF), lambda g, j: (j, 0))
        extra_in = [blk, blk, blk, _layer(l, (CONF_K, W_CONF)), _full((DEPTH, W_CONF)),
                    _full((DEPTH, W_CONF)), _full((DEPTH, W_CONF)), _layer(l, (SC_K, W_SC))]
        extra_out = [blk, blk]
        extra_shape = [jax.ShapeDtypeStruct((N_TOK, W_CONF), BF), jax.ShapeDtypeStruct((N_TOK, W_SC), BF)]
        extra_scratch = _conv_scratch()
        extra_args = [u0, gcx, gb, *conv_w]
    body = functools.partial(_fourier_body, seq=seq, group=group, scale=float((seq * FN_GROUP_W) ** -0.5),
                             conv_rows=conv_rows, layer=l)
    outs = pl.pallas_call(
        body,
        grid=(n_seq // group, n_t),
        in_specs=[
            pl.BlockSpec((group * seq, W_FN), lambda g, j: (g0 + g, 0)),
            _full((W_FN, 2 * W_FN)),
            pl.BlockSpec((tl, 2 * seq), lambda g, j: (j, 0)),
            *extra_in,
        ],
        out_specs=[pl.BlockSpec((group, tl, W_FN), lambda g, j: (g, j, 0)), *extra_out],
        out_shape=[jax.ShapeDtypeStruct((n_seq, seq, W_FN), BF), *extra_shape],
        scratch_shapes=[pltpu.VMEM((2 * seq, group * W_FN), BF), *extra_scratch],
        compiler_params=_params(2),
        name=f"fourier_{seq}",
    )(fn, gd, cs, *extra_args)
    f = outs[0].reshape(n_seq * seq, W_FN)
    return f if conv is None else (f, outs[1], outs[2])


def _dft_cos_sin(n):
    r = np.arange(n, dtype=np.int64)
    ang = ((r[:, None] * r[None, :]) % n).astype(np.float64) * (2.0 * np.pi / n)
    return np.cos(ang).astype(np.float32), np.sin(ang).astype(np.float32)


def _stage3_body(*refs, layer, n_prompt_tiles, tiles_per_seq, n_x, final, chunk):
    x_refs, refs = _take(refs, n_x)
    (op_ref, os_ref, u_ref, s_ref, fp_ref, fs_ref, gate_ref, mod_ref, g2_ref, gfin_ref), refs = _take(refs, 10)
    (wo_ref, wpw_ref, wsco_ref, wfn_ref, wout_ref, wg_hbm, wu_hbm, wd_hbm), refs = _take(refs, 8)
    if final:
        out_refs, refs = _take(refs, 2)
    else:
        (next_w_ref,), refs = _take(refs, 1)
        (x_out_ref, next_w_out_ref), refs = _take(refs, 2)
        next_w_out_ref[...] = next_w_ref[...].astype(BF)
        out_refs = (x_out_ref,)
    wg_ref, wu_ref, wd_ref, ffn_sem = refs

    ffn_copies = [pltpu.make_async_copy(src, dst, ffn_sem.at[n])
                  for n, (src, dst) in enumerate(((wg_hbm, wg_ref), (wu_hbm, wu_ref), (wd_hbm, wd_ref)))]

    i = pl.program_id(0)

    @pl.when(i == 0)
    def _():
        for cp in ffn_copies:
            cp.start()

    is_sample = i >= n_prompt_tiles
    row = _mod_row(i, n_prompt_tiles, tiles_per_seq)
    mod = lambda k: mod_ref[0, pl.ds(row, 1), k * D:(k + 1) * D]
    pick = lambda p_ref, s_ref: jnp.where(is_sample, s_ref[...], p_ref[...])
    merged = gate_ref[:, 0:D].astype(F32) * _dot(pick(op_ref, os_ref), wo_ref[...])
    merged = merged + gate_ref[:, D:2 * D].astype(F32) * _dot(u_ref[...], wpw_ref[...])
    merged = merged + gate_ref[:, 2 * D:3 * D].astype(F32) * _dot(s_ref[...], wsco_ref[...])
    merged = merged + gate_ref[:, 3 * D:4 * D].astype(F32) * _dot(pick(fp_ref, fs_ref), wfn_ref[...])
    x1 = _read_x(x_refs, is_sample) + mod(2) * _dot(merged.astype(BF), wout_ref[...])

    @pl.when(i == 0)
    def _():
        for cp in ffn_copies:
            cp.wait()

    h2 = (_rms(x1, _layer_row(g2_ref, layer)[...]) * (1.0 + mod(4)) + mod(3)).astype(BF)
    bounds = list(range(0, FF_HIDDEN, chunk)) + [FF_HIDDEN]
    spans = list(zip(bounds[:-1], bounds[1:]))

    def gate_up(span):
        c0, c1 = span
        return _dot(h2, wg_ref[:, c0:c1]), _dot(h2, wu_ref[:, c0:c1])

    acc = None
    ab_next = gate_up(spans[0])
    for n, (c0, c1) in enumerate(spans):
        a, b = ab_next
        if n + 1 < len(spans):
            ab_next = gate_up(spans[n + 1])
        t = _dot((a * _sigmoid(a) * b).astype(BF), wd_ref[c0:c1, :])
        acc = t if acc is None else acc + t
    x2 = x1 + mod(5) * acc
    if not final:
        out_refs[0][...] = x2
        return
    y = _rms(x2, gfin_ref[...])

    @pl.when(is_sample)
    def _():
        out_refs[1][...] = y

    @pl.when(jnp.logical_not(is_sample))
    def _():
        out_refs[0][...] = y


def _stage3_call(l, x, o_p, o_s, u, s, f_p, f_s, gates, mod_l, g2, gfin, w3, w_in_t_f32, tm, final):
    n_tiles = N_TOK // tm
    row_blk = lambda w: pl.BlockSpec((tm, w), lambda i: (i, 0))
    body = functools.partial(_stage3_body, layer=l, n_prompt_tiles=N_PROMPT // tm, tiles_per_seq=DEC_SEQ // tm,
                             n_x=len(x), final=final, chunk=256)
    if final:
        extra_in, extra_args = [], []
        out_specs = list(_split_specs(tm, D))
        out_shape = [jax.ShapeDtypeStruct((N_PROMPT, D), F32), jax.ShapeDtypeStruct((N_SAMPLE, D), F32)]
    else:
        slab = -(-IN_COLS // (n_tiles * BF16_ROWS)) * BF16_ROWS
        extra_in = [pl.BlockSpec((1, slab, D), lambda i: (l + 1, i, 0))]
        extra_args = [w_in_t_f32]
        out_specs = [row_blk(D), pl.BlockSpec((1, slab, D), lambda i: (0, i, 0))]
        out_shape = [jax.ShapeDtypeStruct((N_TOK, D), F32), jax.ShapeDtypeStruct((1, IN_COLS, D), BF)]
    return pl.pallas_call(
        body,
        grid=(n_tiles,),
        in_specs=[
            *_x_specs(x, tm), *_split_specs(tm, N_HEADS * V_HEAD), row_blk(W_CONF), row_blk(W_SC),
            *_split_specs(tm, W_FN),
            row_blk(N_BRANCH * D), _layer(l, (8, 6 * D)), _full((DEPTH, D)), _full((1, D)),
            _full((N_HEADS * V_HEAD, D)), _full((W_CONF, D)), _full((W_SC, D)), _full((W_FN, D)),
            _full((D, D)), pl.BlockSpec(memory_space=pl.ANY), pl.BlockSpec(memory_space=pl.ANY),
            pl.BlockSpec(memory_space=pl.ANY),
            *extra_in,
        ],
        out_specs=out_specs,
        out_shape=out_shape,
        scratch_shapes=[pltpu.VMEM((D, FF_HIDDEN), BF), pltpu.VMEM((D, FF_HIDDEN), BF),
                        pltpu.VMEM((FF_HIDDEN, D), BF), pltpu.SemaphoreType.DMA((3,))],
        compiler_params=_params(1),
        name="stage3",
    )(*x, o_p, o_s, u, s, f_p, f_s, gates, mod_l, g2, gfin, *w3, *extra_args)


def _rope_table(tm):
    rows = DEC_SEQ // GRID_W
    row_pos = np.repeat(np.arange(rows, dtype=np.float64), GRID_W)
    col_pos = np.tile(np.arange(GRID_W, dtype=np.float64), rows)
    inv = ROPE_THETA ** (-np.arange(0, AXIS_ROPE, 2, dtype=np.float64) / AXIS_ROPE)
    ang = np.concatenate([row_pos[:, None] * inv, col_pos[:, None] * inv], axis=1)
    half = QK_ROPE // 2
    tab = np.zeros((tm + DEC_SEQ, 3 * LANE), np.float64)
    tab[:, 0:LANE] = 1.0
    tab[tm:, ROPE_LANE0:ROPE_LANE0 + half] = np.cos(ang)
    tab[tm:, ROPE_LANE0 + half:ROPE_LANE0 + QK_ROPE] = np.cos(ang)
    tab[tm:, LANE + ROPE_LANE0:LANE + ROPE_LANE0 + half] = -np.sin(ang)
    tab[tm:, 2 * LANE + ROPE_LANE0 + half:2 * LANE + ROPE_LANE0 + QK_ROPE] = np.sin(ang)
    return jnp.asarray(tab.astype(np.float32))


def _head_layout_body(wqb_ref, wkvb_ref, pq_ref, pkv_ref, qb_ref, kvb_ref):
    qb_ref[0] = _dot(wqb_ref[0].astype(BF), pq_ref[...]).astype(BF)
    kvb_ref[0] = _dot(wkvb_ref[0].astype(BF), pkv_ref[...]).astype(BF)


def _head_layout_call(w_qb, w_kvb):
    dqk = QK_NOPE + QK_ROPE
    pq = np.zeros((N_HEADS * dqk, N_HEADS * LANE), np.float32)
    pkv = np.zeros((N_HEADS * LANE, 2 * N_HEADS * LANE), np.float32)
    for h in range(N_HEADS):
        src = h * dqk + np.concatenate([np.arange(QK_NOPE), QK_NOPE + ROPE_PERM])
        pq[src, h * LANE + np.arange(dqk)] = 1.0
        nope = np.arange(QK_NOPE)
        pkv[h * LANE + nope, h * LANE + nope] = 1.0
        val = np.arange(V_HEAD)
        pkv[h * LANE + QK_NOPE + val, (N_HEADS + h) * LANE + (h % 2) * V_HEAD + val] = 1.0
    return pl.pallas_call(
        _head_layout_body,
        grid=(DEPTH,),
        in_specs=[pl.BlockSpec((1, Q_LORA, N_HEADS * dqk), lambda l: (l, 0, 0)),
                  pl.BlockSpec((1, KV_LORA, N_HEADS * LANE), lambda l: (l, 0, 0)),
                  _full(pq.shape), _full(pkv.shape)],
        out_specs=[pl.BlockSpec((1, Q_LORA, N_HEADS * LANE), lambda l: (l, 0, 0)),
                   pl.BlockSpec((1, KV_LORA, 2 * N_HEADS * LANE), lambda l: (l, 0, 0))],
        out_shape=[jax.ShapeDtypeStruct((DEPTH, Q_LORA, N_HEADS * LANE), BF),
                   jax.ShapeDtypeStruct((DEPTH, KV_LORA, 2 * N_HEADS * LANE), BF)],
        compiler_params=_params(1),
        name="head_layout",
    )(w_qb, w_kvb, jnp.asarray(pq, BF), jnp.asarray(pkv, BF))


def kernel(x_prompt, x_sample, cache_ckv, cache_krope, c, c_ctx, w_ada, b_ada, g_norm1, g_norm2, w_in, g_qa, w_qb, g_kva, w_kvb, w_o_mla, w_conf_dw, b_conf_dw, g_conf_ln, b_conf_ln, w_conf_pw, w_sc_conv, w_sc_out, w_fn, w_out, w_ffn_gate, w_ffn_up, w_ffn_down, g_final):
    tm = 512
    xp = x_prompt.reshape(N_PROMPT, D)
    xs = x_sample.reshape(N_SAMPLE, D)
    cvec = jnp.concatenate([c_ctx[None, :], c, jnp.zeros((8 - 1 - DEC_BATCH, D), F32)], axis=0)
    mod = _ada_call(cvec, w_ada, b_ada)

    rope_tab = _rope_table(tm)
    place = np.zeros((QK_ROPE, LANE), np.float32)
    place[ROPE_PERM, ROPE_LANE0 + np.arange(QK_ROPE)] = 1.0
    wqb, wkvb_all = _head_layout_call(w_qb, w_kvb)
    kc, vc = _ctx_call(cache_ckv, cache_krope, wkvb_all, jnp.asarray(place, BF))

    cg, sg = _dft_cos_sin(FN_GROUP_W)
    eye = np.eye(FN_GROUPS, dtype=np.float32)
    gd = jnp.asarray(np.concatenate([np.kron(eye, cg), np.kron(eye, sg)], axis=1)).astype(BF)
    cs_p = jnp.asarray(np.concatenate(_dft_cos_sin(SEQ), axis=1)).astype(BF)
    cs_s = jnp.asarray(np.concatenate(_dft_cos_sin(DEC_SEQ), axis=1)).astype(BF)

    w_in_t_f32 = jnp.swapaxes(w_in, 1, 2)
    w_in_t = w_in_t_f32[0:1].astype(BF)
    w3_f32 = (w_o_mla, w_conf_pw, w_sc_out, w_fn, w_out, w_ffn_gate, w_ffn_up, w_ffn_down)

    conv_w = (w_conf_dw, b_conf_dw, g_conf_ln, b_conf_ln, w_sc_conv)
    x = (xp, xs)
    new_ckv, new_krope = [], []
    for l in range(DEPTH):
        (q, k, v, ckv, kr, u0, gcx, gb, fn, gates), w3 = _stage1_call(
            l, x, mod, g_norm1, g_qa, g_kva, w_in_t, wqb, wkvb_all, rope_tab, w3_f32, tm)
        new_ckv.append(ckv[:N_PROMPT].reshape(BATCH, SEQ, KV_LORA))
        new_krope.append(kr[:N_PROMPT].reshape(BATCH, SEQ, QK_ROPE))

        o_p, f_p = _prompt_mix_call(q, k, v, fn, gd, cs_p)
        o_s = _attn_sample_call(l, q, k, v, kc, vc, 512)
        f_s, u, s = _fourier_call(fn, gd, cs_s, DEC_SEQ, DEC_BATCH, N_PROMPT, 512, DEC_BATCH,
                                  conv=(l, u0, gcx, gb, conv_w))
        out = _stage3_call(l, x, o_p, o_s, u, s, f_p, f_s, gates, mod, g_norm2, g_final[None, :], w3,
                           w_in_t_f32, tm, l == DEPTH - 1)
        if l < DEPTH - 1:
            x, w_in_t = (out[0],), out[1]

    y_prompt, y_sample = out
    return (y_prompt.reshape(BATCH, SEQ, D), y_sample.reshape(DEC_BATCH, DEC_SEQ, D),
            jnp.stack(new_ckv, axis=1), jnp.stack(new_krope, axis=1))
```

```python
import functools

import numpy as np
import jax
import jax.numpy as jnp
from jax import lax
from jax.experimental import pallas as pl
from jax.experimental.pallas import tpu as pltpu

BF = jnp.bfloat16
F32 = jnp.float32

D = 1024
BATCH = 16
SEQ = 256
DEPTH = 2
DEC_BATCH = 2
DEC_SEQ = 2048
PAST_LEN = 256
GRID_W = 64
N_HEADS = 8
QK_NOPE = 64
QK_ROPE = 32
V_HEAD = 64
Q_LORA = 384
KV_LORA = 256
AXIS_ROPE = QK_ROPE // 2
ROPE_THETA = 10000.0
W_CONF = D // 4
CONF_K = 31
W_SC = D // 4
SC_K = 3
W_FN = D // 4
FN_GROUPS = 4
FN_GROUP_W = W_FN // FN_GROUPS
N_BRANCH = 4
FF_HIDDEN = ((8 * D // 3 + 255) // 256) * 256
EPS = 1e-6
OFF_QA = 0
OFF_KVA = OFF_QA + Q_LORA
OFF_CONF = OFF_KVA + KV_LORA + QK_ROPE
OFF_SC = OFF_CONF + 2 * W_CONF
OFF_FN = OFF_SC + 3 * W_SC
OFF_GATE = OFF_FN + W_FN
IN_COLS = OFF_GATE + N_BRANCH * D

N_PROMPT = BATCH * SEQ
N_SAMPLE = DEC_BATCH * DEC_SEQ
N_TOK = N_PROMPT + N_SAMPLE
LANE = 128
SUBLANES = 8
BF16_ROWS = 2 * SUBLANES
HALO = 16
SM_SCALE = float((QK_NOPE + QK_ROPE) ** -0.5)
Q_SCALE = SM_SCALE * float(np.log2(np.e))
VMEM_LIMIT = 56 * 1024 * 1024

QK_END = OFF_KVA + KV_LORA + LANE

ROPE_PERM = np.array(list(range(0, 8)) + list(range(16, 24)) + list(range(8, 16)) + list(range(24, 32)))
ROPE_LANE0 = QK_NOPE


def _dot(a, b):
    return jnp.dot(a, b, preferred_element_type=F32)


def _dot_nt(a, b):
    return lax.dot_general(a, b, (((1,), (1,)), ((), ())), preferred_element_type=F32)


def _sigmoid(x):
    return jax.nn.sigmoid(x)


def _rms(x, g):
    return x * lax.rsqrt(jnp.mean(x * x, axis=-1, keepdims=True) + EPS) * g


def _full(shape):
    return pl.BlockSpec(shape, lambda *_: (0,) * len(shape))


def _layer(l, shape):
    return pl.BlockSpec((1, *shape), lambda *_: (l,) + (0,) * len(shape), pipeline_mode=pl.Buffered(1))


def _params(n_axes):
    return pltpu.CompilerParams(
        dimension_semantics=("arbitrary",) * n_axes, vmem_limit_bytes=VMEM_LIMIT)


def _mod_row(i, n_prompt_tiles, tiles_per_seq):
    return jnp.where(i >= n_prompt_tiles, 1 + (i - n_prompt_tiles) // tiles_per_seq, 0)


def _ada_body(c_ref, w_ref, b_ref, win_ref, out_ref, win_out_ref):
    cv = c_ref[...]
    sc = (cv * _sigmoid(cv)).astype(BF)
    out_ref[0] = _dot(sc, w_ref[0].astype(BF)) + b_ref[0]
    win_out_ref[...] = win_ref[...].astype(BF)


def _ada_call(cvec, w_ada, b_ada, w_in_t_f32):
    wc = 2 * D
    n_col = 6 * D // wc
    slab = -(-IN_COLS // (DEPTH * n_col * BF16_ROWS)) * BF16_ROWS
    return pl.pallas_call(
        _ada_body,
        grid=(DEPTH, n_col),
        in_specs=[
            _full((8, D)),
            pl.BlockSpec((1, D, wc), lambda l, j: (l, 0, j)),
            pl.BlockSpec((1, 1, wc), lambda l, j: (l, 0, j)),
            pl.BlockSpec((1, slab, D), lambda l, j: (0, l * n_col + j, 0)),
        ],
        out_specs=[pl.BlockSpec((1, 8, wc), lambda l, j: (l, 0, j)),
                   pl.BlockSpec((1, slab, D), lambda l, j: (0, l * n_col + j, 0))],
        out_shape=[jax.ShapeDtypeStruct((DEPTH, 8, 6 * D), F32),
                   jax.ShapeDtypeStruct((1, IN_COLS, D), BF)],
        compiler_params=_params(2),
        name="ada_mod",
    )(cvec, w_ada, b_ada.reshape(DEPTH, 1, 6 * D), w_in_t_f32)


def _ctx_body(ckv_ref, kr_ref, wkvb_ref, place_ref, k_ref, v_ref):
    kv = _dot(ckv_ref[0, 0].astype(BF), wkvb_ref[0])
    krp = _dot(kr_ref[0, 0].astype(BF), place_ref[...])
    for h in range(N_HEADS):
        sl = slice(h * LANE, (h + 1) * LANE)
        k_ref[0, 0, :, sl] = (kv[:, sl] + krp).astype(BF)
    v_ref[0, 0] = (kv[:, N_HEADS * LANE:] + _value_ones_row()).astype(BF)


def _ctx_call(cache_ckv, cache_krope, wkvb_all, place):
    return pl.pallas_call(
        _ctx_body,
        grid=(DEPTH, DEC_BATCH),
        in_specs=[
            pl.BlockSpec((1, 1, PAST_LEN, KV_LORA), lambda l, b: (b, l, 0, 0)),
            pl.BlockSpec((1, 1, PAST_LEN, QK_ROPE), lambda l, b: (b, l, 0, 0)),
            pl.BlockSpec((1, KV_LORA, 2 * N_HEADS * LANE), lambda l, b: (l, 0, 0)),
            _full((QK_ROPE, LANE)),
        ],
        out_specs=[
            pl.BlockSpec((1, 1, PAST_LEN, N_HEADS * LANE), lambda l, b: (l, b, 0, 0)),
            pl.BlockSpec((1, 1, PAST_LEN, N_HEADS * LANE), lambda l, b: (l, b, 0, 0)),
        ],
        out_shape=[
            jax.ShapeDtypeStruct((DEPTH, DEC_BATCH, PAST_LEN, N_HEADS * LANE), BF),
            jax.ShapeDtypeStruct((DEPTH, DEC_BATCH, PAST_LEN, N_HEADS * LANE), BF),
        ],
        compiler_params=_params(2),
        name="ctx_keys",
    )(cache_ckv, cache_krope, wkvb_all, place)


def _take(refs, n):
    return refs[:n], refs[n:]


def _read_x(x_refs, is_sample):
    if len(x_refs) == 1:
        return x_refs[0][...]
    return jnp.where(is_sample, x_refs[1][...], x_refs[0][...])


def _stage1_body(*refs, layer, n_prompt_tiles, tiles_per_seq, n_x, n_cast):
    x_refs, refs = _take(refs, n_x)
    (mod_ref, g1_ref, gqa_ref, gkva_ref, wt_ref, wqb_ref, wkvb_ref, rope_ref), refs = _take(refs, 8)
    cast_in, refs = _take(refs, n_cast)
    (q_ref, k_ref, v_ref, ckv_ref, kr_ref, u0_ref, gcx_ref, gb_ref, fn_ref, gate_ref), cast_out = _take(refs, 10)

    for src, dst in zip(cast_in, cast_out, strict=True):
        dst[...] = src[0].astype(BF)

    i = pl.program_id(0)
    is_sample = i >= n_prompt_tiles
    row = _mod_row(i, n_prompt_tiles, tiles_per_seq)
    sh1 = mod_ref[0, pl.ds(row, 1), 0:D]
    sc1 = mod_ref[0, pl.ds(row, 1), D:2 * D]
    hb = (_rms(_read_x(x_refs, is_sample), _layer_row(g1_ref, layer)[...]) * (1.0 + sc1) + sh1).astype(BF)

    def proj(c0, c1):
        return _dot_nt(hb, wt_ref[0, c0:c1, :])

    pqk = proj(OFF_QA, QK_END)
    q = _dot(_rms(pqk[:, 0:Q_LORA], _layer_row(gqa_ref, layer)[...]).astype(BF), wqb_ref[0])

    ckv = _rms(pqk[:, OFF_KVA:OFF_KVA + KV_LORA], _layer_row(gkva_ref, layer)[...])
    ckv_ref[...] = ckv
    kr3 = pqk[:, OFF_KVA + KV_LORA:QK_END]
    kr_ref[...] = kr3[:, 0:QK_ROPE]
    lane = lax.broadcasted_iota(jnp.int32, (1, LANE), 1)
    group = lambda k: (lane >= ROPE_LANE0 + 8 * k) & (lane < ROPE_LANE0 + 8 * (k + 1))
    krm = jnp.where(group(0) | group(3), pltpu.roll(kr3, ROPE_LANE0, 1),
                    jnp.where(group(1), pltpu.roll(kr3, ROPE_LANE0 - 8, 1),
                              jnp.where(group(2), pltpu.roll(kr3, ROPE_LANE0 + 8, 1), 0.0)))
    kv = _dot(ckv.astype(BF), wkvb_ref[0])
    v_ref[...] = (kv[:, N_HEADS * LANE:] + _value_ones_row()).astype(BF)

    def rope(t):
        return (t * rope_ref[:, 0:LANE]
                + pltpu.roll(t, LANE - 16, 1) * rope_ref[:, LANE:2 * LANE]
                + pltpu.roll(t, 16, 1) * rope_ref[:, 2 * LANE:3 * LANE])

    krr = rope(krm)
    for h in range(N_HEADS):
        sl = slice(h * LANE, (h + 1) * LANE)
        q_ref[:, sl] = (rope(q[:, sl]) * Q_SCALE).astype(BF)
        k_ref[:, sl] = (kv[:, sl] + krr).astype(BF)

    for j in range(N_BRANCH):
        gate_ref[:, j * D:(j + 1) * D] = _sigmoid(
            proj(OFF_GATE + j * D, OFF_GATE + (j + 1) * D)).astype(BF)

    pc = proj(OFF_CONF, OFF_SC)
    u0_ref[...] = pc[:, 0:W_CONF] * _sigmoid(pc[:, W_CONF:])
    ps = proj(OFF_SC, OFF_FN)
    gb_ref[...] = ps[:, 0:W_SC]
    gcx_ref[...] = ps[:, W_SC:2 * W_SC] * ps[:, 2 * W_SC:]
    fn_ref[...] = proj(OFF_FN, OFF_GATE).astype(BF)


def _split_specs(tm, w):
    n_p = N_PROMPT // tm
    return (pl.BlockSpec((tm, w), lambda i: (jnp.minimum(i, n_p - 1), 0)),
            pl.BlockSpec((tm, w), lambda i: (jnp.maximum(i - n_p, 0), 0)))


def _x_specs(x, tm):
    return list(_split_specs(tm, D)) if len(x) == 2 else [pl.BlockSpec((tm, D), lambda i: (i, 0))]


def _stage1_call(l, x, mod_l, g1, gqa, gkva, w_in_t, wqb, wkvb, rope_tab, to_cast, tm):
    n_prompt_tiles = N_PROMPT // tm
    tiles_per_seq = DEC_SEQ // tm
    row_blk = lambda w: pl.BlockSpec((tm, w), lambda i: (i, 0))
    n_steps = N_TOK // tm
    body = functools.partial(_stage1_body, layer=l, n_prompt_tiles=n_prompt_tiles, tiles_per_seq=tiles_per_seq,
                             n_x=len(x), n_cast=len(to_cast))
    slab = lambda w: w.shape[1] // n_steps
    assert all(w.shape[1] % (BF16_ROWS * n_steps) == 0 for w in to_cast)
    outs = pl.pallas_call(
        body,
        grid=(n_steps,),
        in_specs=[
            *_x_specs(x, tm),
            _layer(l, (8, 6 * D)),
            _full((DEPTH, D)),
            _full((DEPTH, Q_LORA)),
            _full((DEPTH, KV_LORA)),
            _layer(0, (IN_COLS, D)),
            _layer(l, (Q_LORA, N_HEADS * LANE)),
            _layer(l, (KV_LORA, 2 * N_HEADS * LANE)),
            pl.BlockSpec((tm, 3 * LANE),
                         lambda i: (jnp.where(i < n_prompt_tiles, 0,
                                              1 + (i - n_prompt_tiles) % tiles_per_seq), 0)),
            *[pl.BlockSpec((1, slab(w), w.shape[2]), lambda i: (l, i, 0)) for w in to_cast],
        ],
        out_specs=[
            row_blk(N_HEADS * LANE), row_blk(N_HEADS * LANE), row_blk(N_HEADS * LANE),
            row_blk(KV_LORA), row_blk(QK_ROPE), row_blk(W_CONF), row_blk(W_SC), row_blk(W_SC),
            row_blk(W_FN), row_blk(N_BRANCH * D),
            *[pl.BlockSpec((slab(w), w.shape[2]), lambda i: (i, 0)) for w in to_cast],
        ],
        out_shape=[
            jax.ShapeDtypeStruct((N_TOK, N_HEADS * LANE), BF),
            jax.ShapeDtypeStruct((N_TOK, N_HEADS * LANE), BF),
            jax.ShapeDtypeStruct((N_TOK, N_HEADS * LANE), BF),
            jax.ShapeDtypeStruct((N_TOK, KV_LORA), F32),
            jax.ShapeDtypeStruct((N_TOK, QK_ROPE), F32),
            jax.ShapeDtypeStruct((N_TOK, W_CONF), F32),
            jax.ShapeDtypeStruct((N_TOK, W_SC), F32),
            jax.ShapeDtypeStruct((N_TOK, W_SC), F32),
            jax.ShapeDtypeStruct((N_TOK, W_FN), BF),
            jax.ShapeDtypeStruct((N_TOK, N_BRANCH * D), BF),
            *[jax.ShapeDtypeStruct(w.shape[1:], BF) for w in to_cast],
        ],
        compiler_params=_params(1),
        name="stage1",
    )(*x, mod_l, g1, gqa, gkva, w_in_t, wqb, wkvb, rope_tab, *to_cast)
    return outs[:10], outs[10:]


def _ones_lane(parity):
    return (1 - parity) * V_HEAD


def _value_ones_row():
    col = lax.broadcasted_iota(jnp.int32, (1, N_HEADS * LANE), 1)
    odd = (col // LANE) % 2
    return (col % LANE == jnp.where(odd == 1, _ones_lane(1), _ones_lane(0))).astype(F32)


def _attend_heads(q_ref, kv_refs, o_ref, ahead):
    lane = lax.broadcasted_iota(jnp.int32, (1, LANE), 1)

    def scores(h):
        sl = slice(h * LANE, (h + 1) * LANE)
        return [_dot_nt(q_ref[:, sl], k_ref[:, sl]) for k_ref, _ in kv_refs]

    pending = [scores(h) for h in range(ahead)]
    for hp in range(N_HEADS // 2):
        vsl = slice(hp * LANE, (hp + 1) * LANE)
        outs = []
        for e in range(2):
            h = 2 * hp + e
            sl = slice(h * LANE, (h + 1) * LANE)
            ss = pending.pop(0)
            if h + ahead < N_HEADS:
                pending.append(scores(h + ahead))
            m = functools.reduce(jnp.maximum, [jnp.max(s, axis=-1, keepdims=True) for s in ss])
            o = None
            for s, (_, v_ref) in zip(ss, kv_refs):
                t = _dot(jnp.exp2(s - m).astype(BF), v_ref[:, sl])
                o = t if o is None else o + t
            ones_lane = _ones_lane(e)
            outs.append(o * (1.0 / o[:, ones_lane:ones_lane + 1]))
        o_ref[:, vsl] = jnp.where(lane < V_HEAD, outs[0], outs[1]).astype(BF)


def _prompt_mix_body(q_ref, k_ref, v_ref, fn_ref, gd_ref, cs_ref, o_ref, f_ref, rhs, *, per_step, scale):
    _fourier_rhs(fn_ref, gd_ref, rhs, SEQ, per_step)
    for b, rows in enumerate(_fourier_rows(cs_ref, rhs, scale, per_step)):
        f_ref[b * SEQ:(b + 1) * SEQ, :] = rows
    for b in range(per_step):
        rows = pl.ds(b * SEQ, SEQ)
        _attend_heads(q_ref.at[rows], [(k_ref.at[rows], v_ref.at[rows])], o_ref.at[rows], ahead=N_HEADS)


def _prompt_mix_call(q, k, v, fn, gd, cs):
    per_step = 4
    blk = lambda w: pl.BlockSpec((per_step * SEQ, w), lambda b: (b, 0))
    body = functools.partial(_prompt_mix_body, per_step=per_step, scale=float((SEQ * FN_GROUP_W) ** -0.5))
    return pl.pallas_call(
        body,
        grid=(BATCH // per_step,),
        in_specs=[blk(N_HEADS * LANE), blk(N_HEADS * LANE), blk(N_HEADS * LANE), blk(W_FN),
                  _full((W_FN, 2 * W_FN)), _full((SEQ, 2 * SEQ))],
        out_specs=[blk(N_HEADS * V_HEAD), blk(W_FN)],
        out_shape=[jax.ShapeDtypeStruct((N_PROMPT, N_HEADS * V_HEAD), BF),
                   jax.ShapeDtypeStruct((N_PROMPT, W_FN), BF)],
        scratch_shapes=[pltpu.VMEM((2 * SEQ, per_step * W_FN), BF)],
        compiler_params=_params(1),
        name="prompt_mix",
    )(q, k, v, fn, gd, cs)


def _attn_sample_body(q_ref, kc_ref, vc_ref, k_ref, v_ref, o_ref):
    _attend_heads(q_ref, [(kc_ref.at[0, 0], vc_ref.at[0, 0]), (k_ref, v_ref)], o_ref, ahead=2)


def _attn_sample_call(l, q, k, v, kc, vc, tq):
    n_q = DEC_SEQ // tq
    q0 = N_PROMPT // tq
    s0 = N_PROMPT // DEC_SEQ
    return pl.pallas_call(
        _attn_sample_body,
        grid=(DEC_BATCH, n_q),
        in_specs=[
            pl.BlockSpec((tq, N_HEADS * LANE), lambda b, j: (q0 + b * n_q + j, 0)),
            pl.BlockSpec((1, 1, PAST_LEN, N_HEADS * LANE), lambda b, j: (l, b, 0, 0)),
            pl.BlockSpec((1, 1, PAST_LEN, N_HEADS * LANE), lambda b, j: (l, b, 0, 0)),
            pl.BlockSpec((DEC_SEQ, N_HEADS * LANE), lambda b, j: (s0 + b, 0)),
            pl.BlockSpec((DEC_SEQ, N_HEADS * LANE), lambda b, j: (s0 + b, 0)),
        ],
        out_specs=pl.BlockSpec((tq, N_HEADS * V_HEAD), lambda b, j: (b * n_q + j, 0)),
        out_shape=jax.ShapeDtypeStruct((N_SAMPLE, N_HEADS * V_HEAD), BF),
        compiler_params=_params(2),
        name="attn_sample",
    )(q, kc, vc, k, v)


CONV_ROWS = 256


def _conv_pass(ubuf, cbuf, gb_ref, wdw_ref, bdw_ref, gln_ref, bln_ref, wsc_ref, uo_ref, so_ref,
               shifted, cshifted):
    tl = CONV_ROWS
    span = shifted.shape[1]
    p0 = HALO - CONF_K // 2
    q0 = HALO - SC_K // 2
    for b in range(1, SUBLANES):
        shifted[b] = ubuf[pl.ds(b, span), :]
    for t in range(SC_K):
        if (q0 + t) % SUBLANES:
            cshifted[t] = cbuf[pl.ds((q0 + t) % SUBLANES, span), :]

    def staged(p, r, n):
        a, b = divmod(p, SUBLANES)
        if b == 0:
            return ubuf[pl.ds(r + SUBLANES * a, n), :]
        return shifted[b, pl.ds(r + SUBLANES * a, n), :]

    def staged_c(t, r, n):
        a, b = divmod(q0 + t, SUBLANES)
        if b == 0:
            return cbuf[pl.ds(r + SUBLANES * a, n), :]
        return cshifted[t, pl.ds(r + SUBLANES * a, n), :]

    rc = 32
    for r in range(0, tl, rc):
        acc = staged(p0, r, rc) * wdw_ref[0, 0:1, :]
        for t in range(1, CONF_K):
            acc = acc + staged(p0 + t, r, rc) * wdw_ref[0, t:t + 1, :]
        acc = acc + bdw_ref[...]
        mu = jnp.mean(acc, axis=-1, keepdims=True)
        cen = acc - mu
        var = jnp.mean(cen * cen, axis=-1, keepdims=True)
        y = cen * lax.rsqrt(var + EPS) * gln_ref[...] + bln_ref[...]
        uo_ref[pl.ds(r, rc), :] = (y * _sigmoid(y)).astype(BF)

        cv = staged_c(0, r, rc) * wsc_ref[0, 0:1, :]
        for t in range(1, SC_K):
            cv = cv + staged_c(t, r, rc) * wsc_ref[0, t:t + 1, :]
        so_ref[pl.ds(r, rc), :] = (gb_ref[pl.ds(r, rc), :] * cv).astype(BF)


def _conv_tile(tm, u_ref, c_ref, gb_ref, halos, is_latent, has_prev, has_next, conv_w, uo_ref, so_ref, scratch,
               hooks=None):
    ubuf, cbuf, shifted, cshifted = scratch
    n_pass = tm // CONV_ROWS
    zeros = jnp.zeros((HALO, W_CONF), F32)
    for p in range(n_pass):
        if hooks and p in hooks:
            hooks[p]()
        r0 = p * CONV_ROWS
        for src, buf, k in ((u_ref, ubuf, 0), (c_ref, cbuf, 2)):
            if p > 0:
                head = jnp.where(is_latent, src[r0 - HALO:r0, :], 0.0)
            else:
                head = zeros if halos is None else jnp.where(has_prev, halos[k][...], 0.0)
            if p < n_pass - 1:
                tail = jnp.where(is_latent, src[r0 + CONV_ROWS:r0 + CONV_ROWS + HALO, :], 0.0)
            else:
                tail = zeros if halos is None else jnp.where(has_next, halos[k + 1][...], 0.0)
            buf[0:HALO, :] = head
            buf[HALO:HALO + CONV_ROWS, :] = src[r0:r0 + CONV_ROWS, :]
            buf[HALO + CONV_ROWS:2 * HALO + CONV_ROWS, :] = tail
        rows = pl.ds(r0, CONV_ROWS)
        _conv_pass(ubuf, cbuf, gb_ref.at[rows], *conv_w, uo_ref.at[rows], so_ref.at[rows], shifted, cshifted)


def _conv_scratch():
    staged = CONV_ROWS + 2 * HALO
    return [pltpu.VMEM((staged, W_CONF), F32), pltpu.VMEM((staged, W_SC), F32),
            pltpu.VMEM((SUBLANES, staged - SUBLANES, W_CONF), F32),
            pltpu.VMEM((SC_K, staged - SUBLANES, W_SC), F32)]


def _fourier_rhs(fn_ref, gd_ref, rhs, seq, group):
    for b in range(group):
        v = _dot(fn_ref[b * seq:(b + 1) * seq, :], gd_ref[...])
        rhs[0:seq, b * W_FN:(b + 1) * W_FN] = v[:, 0:W_FN].astype(BF)
        rhs[seq:2 * seq, b * W_FN:(b + 1) * W_FN] = (-v[:, W_FN:]).astype(BF)


def _fourier_rows(cs_ref, rhs, scale, group):
    res = _dot(cs_ref[...], rhs[...]) * scale
    return [res[:, b * W_FN:(b + 1) * W_FN].astype(BF) for b in range(group)]


def _layer_row(ref, layer):
    return ref.at[pl.ds(layer, 1)]


def _fourier_body(fn_ref, gd_ref, cs_ref, *rest, seq, group, scale, conv_rows, layer):
    if conv_rows:
        (u_ref, c_ref, gb_ref), (wdw_ref, bdw_ref, gln_ref, bln_ref, wsc_ref) = rest[:3], rest[3:8]
        conv_w = (wdw_ref, _layer_row(bdw_ref, layer), _layer_row(gln_ref, layer), _layer_row(bln_ref, layer),
                  wsc_ref)
        out_ref, uo_ref, so_ref, rhs, *conv_scratch = rest[8:]
    else:
        out_ref, rhs = rest

    @pl.when(pl.program_id(1) == 0)
    def _():
        _fourier_rhs(fn_ref, gd_ref, rhs, seq, group)

    def dft_rows(b):
        out_ref[b] = (_dot(cs_ref[...], rhs[:, b * W_FN:(b + 1) * W_FN]) * scale).astype(BF)

    if not conv_rows:
        for b in range(group):
            dft_rows(b)
        return

    is_latent = pl.program_id(1) * conv_rows >= N_PROMPT
    n_pass = conv_rows // CONV_ROWS
    hooks = {(b * n_pass) // group: functools.partial(dft_rows, b) for b in range(group)}
    _conv_tile(conv_rows, u_ref, c_ref, gb_ref, None, is_latent, False, False, conv_w, uo_ref, so_ref,
               conv_scratch, hooks)


def _fourier_call(fn, gd, cs, seq, n_seq, row0, tl, group, conv=None):
    n_t = seq // tl
    g0 = row0 // (group * seq)
    n_steps = (n_seq // group) * n_t
    conv_rows, l = 0, 0
    extra_in, extra_out, extra_shape, extra_scratch, extra_args = [], [], [], [], []
    if conv is not None:
        l, u0, gcx, gb, conv_w = conv
        conv_rows = N_TOK // n_steps
        assert n_seq == group and conv_rows % DEC_SEQ == 0
        blk = pl.BlockSpec((conv_rows, W_CONF), lambda g, j: (j, 0))
        extra_in = [blk, blk, blk, _layer(l, (CONF_K, W_CONF)), _full((DEPTH, W_CONF)),
                    _full((DEPTH, W_CONF)), _full((DEPTH, W_CONF)), _layer(l, (SC_K, W_SC))]
        extra_out = [blk, blk]
        extra_shape = [jax.ShapeDtypeStruct((N_TOK, W_CONF), BF), jax.ShapeDtypeStruct((N_TOK, W_SC), BF)]
        extra_scratch = _conv_scratch()
        extra_args = [u0, gcx, gb, *conv_w]
    body = functools.partial(_fourier_body, seq=seq, group=group, scale=float((seq * FN_GROUP_W) ** -0.5),
                             conv_rows=conv_rows, layer=l)
    outs = pl.pallas_call(
        body,
        grid=(n_seq // group, n_t),
        in_specs=[
            pl.BlockSpec((group * seq, W_FN), lambda g, j: (g0 + g, 0)),
            _full((W_FN, 2 * W_FN)),
            pl.BlockSpec((tl, 2 * seq), lambda g, j: (j, 0)),
            *extra_in,
        ],
        out_specs=[pl.BlockSpec((group, tl, W_FN), lambda g, j: (g, j, 0)), *extra_out],
        out_shape=[jax.ShapeDtypeStruct((n_seq, seq, W_FN), BF), *extra_shape],
        scratch_shapes=[pltpu.VMEM((2 * seq, group * W_FN), BF), *extra_scratch],
        compiler_params=_params(2),
        name=f"fourier_{seq}",
    )(fn, gd, cs, *extra_args)
    f = outs[0].reshape(n_seq * seq, W_FN)
    return f if conv is None else (f, outs[1], outs[2])


def _dft_cos_sin(n):
    r = np.arange(n, dtype=np.int64)
    ang = ((r[:, None] * r[None, :]) % n).astype(np.float64) * (2.0 * np.pi / n)
    return np.cos(ang).astype(np.float32), np.sin(ang).astype(np.float32)


def _stage3_body(*refs, layer, n_prompt_tiles, tiles_per_seq, n_x, final, chunk):
    x_refs, refs = _take(refs, n_x)
    (op_ref, os_ref, u_ref, s_ref, fp_ref, fs_ref, gate_ref, mod_ref, g2_ref, gfin_ref), refs = _take(refs, 10)
    (wo_ref, wpw_ref, wsco_ref, wfn_ref, wout_ref, wg_ref, wu_ref, wd_ref), refs = _take(refs, 8)
    if final:
        out_refs = refs
    else:
        (next_w_ref,), (x_out_ref, next_w_out_ref) = _take(refs, 1)
        next_w_out_ref[...] = next_w_ref[...].astype(BF)
        out_refs = (x_out_ref,)

    i = pl.program_id(0)
    is_sample = i >= n_prompt_tiles
    row = _mod_row(i, n_prompt_tiles, tiles_per_seq)
    mod = lambda k: mod_ref[0, pl.ds(row, 1), k * D:(k + 1) * D]
    pick = lambda p_ref, s_ref: jnp.where(is_sample, s_ref[...], p_ref[...])
    merged = gate_ref[:, 0:D].astype(F32) * _dot(pick(op_ref, os_ref), wo_ref[...])
    merged = merged + gate_ref[:, D:2 * D].astype(F32) * _dot(u_ref[...], wpw_ref[...])
    merged = merged + gate_ref[:, 2 * D:3 * D].astype(F32) * _dot(s_ref[...], wsco_ref[...])
    merged = merged + gate_ref[:, 3 * D:4 * D].astype(F32) * _dot(pick(fp_ref, fs_ref), wfn_ref[...])
    x1 = _read_x(x_refs, is_sample) + mod(2) * _dot(merged.astype(BF), wout_ref[...])

    h2 = (_rms(x1, _layer_row(g2_ref, layer)[...]) * (1.0 + mod(4)) + mod(3)).astype(BF)
    bounds = list(range(0, FF_HIDDEN, chunk)) + [FF_HIDDEN]
    spans = list(zip(bounds[:-1], bounds[1:]))

    def gate_up(span):
        c0, c1 = span
        return _dot(h2, wg_ref[:, c0:c1]), _dot(h2, wu_ref[:, c0:c1])

    acc = None
    ab_next = gate_up(spans[0])
    for n, (c0, c1) in enumerate(spans):
        a, b = ab_next
        if n + 1 < len(spans):
            ab_next = gate_up(spans[n + 1])
        t = _dot((a * _sigmoid(a) * b).astype(BF), wd_ref[c0:c1, :])
        acc = t if acc is None else acc + t
    x2 = x1 + mod(5) * acc
    if not final:
        out_refs[0][...] = x2
        return
    y = _rms(x2, gfin_ref[...])

    @pl.when(is_sample)
    def _():
        out_refs[1][...] = y

    @pl.when(jnp.logical_not(is_sample))
    def _():
        out_refs[0][...] = y


def _stage3_call(l, x, o_p, o_s, u, s, f_p, f_s, gates, mod_l, g2, gfin, w3, w_in_t_f32, tm, final):
    n_tiles = N_TOK // tm
    row_blk = lambda w: pl.BlockSpec((tm, w), lambda i: (i, 0))
    body = functools.partial(_stage3_body, layer=l, n_prompt_tiles=N_PROMPT // tm, tiles_per_seq=DEC_SEQ // tm,
                             n_x=len(x), final=final, chunk=256)
    if final:
        extra_in, extra_args = [], []
        out_specs = list(_split_specs(tm, D))
        out_shape = [jax.ShapeDtypeStruct((N_PROMPT, D), F32), jax.ShapeDtypeStruct((N_SAMPLE, D), F32)]
    else:
        slab = -(-IN_COLS // (n_tiles * BF16_ROWS)) * BF16_ROWS
        extra_in = [pl.BlockSpec((1, slab, D), lambda i: (l + 1, i, 0))]
        extra_args = [w_in_t_f32]
        out_specs = [row_blk(D), pl.BlockSpec((1, slab, D), lambda i: (0, i, 0))]
        out_shape = [jax.ShapeDtypeStruct((N_TOK, D), F32), jax.ShapeDtypeStruct((1, IN_COLS, D), BF)]
    return pl.pallas_call(
        body,
        grid=(n_tiles,),
        in_specs=[
            *_x_specs(x, tm), *_split_specs(tm, N_HEADS * V_HEAD), row_blk(W_CONF), row_blk(W_SC),
            *_split_specs(tm, W_FN),
            row_blk(N_BRANCH * D), _layer(l, (8, 6 * D)), _full((DEPTH, D)), _full((1, D)),
            _full((N_HEADS * V_HEAD, D)), _full((W_CONF, D)), _full((W_SC, D)), _full((W_FN, D)),
            _full((D, D)), _full((D, FF_HIDDEN)), _full((D, FF_HIDDEN)), _full((FF_HIDDEN, D)),
            *extra_in,
        ],
        out_specs=out_specs,
        out_shape=out_shape,
        compiler_params=_params(1),
        name="stage3",
    )(*x, o_p, o_s, u, s, f_p, f_s, gates, mod_l, g2, gfin, *w3, *extra_args)


def _rope_table(tm):
    rows = DEC_SEQ // GRID_W
    row_pos = np.repeat(np.arange(rows, dtype=np.float64), GRID_W)
    col_pos = np.tile(np.arange(GRID_W, dtype=np.float64), rows)
    inv = ROPE_THETA ** (-np.arange(0, AXIS_ROPE, 2, dtype=np.float64) / AXIS_ROPE)
    ang = np.concatenate([row_pos[:, None] * inv, col_pos[:, None] * inv], axis=1)
    half = QK_ROPE // 2
    tab = np.zeros((tm + DEC_SEQ, 3 * LANE), np.float64)
    tab[:, 0:LANE] = 1.0
    tab[tm:, ROPE_LANE0:ROPE_LANE0 + half] = np.cos(ang)
    tab[tm:, ROPE_LANE0 + half:ROPE_LANE0 + QK_ROPE] = np.cos(ang)
    tab[tm:, LANE + ROPE_LANE0:LANE + ROPE_LANE0 + half] = -np.sin(ang)
    tab[tm:, 2 * LANE + ROPE_LANE0 + half:2 * LANE + ROPE_LANE0 + QK_ROPE] = np.sin(ang)
    return jnp.asarray(tab.astype(np.float32))


def _head_layout_body(wqb_ref, wkvb_ref, pq_ref, pkv_ref, qb_ref, kvb_ref):
    qb_ref[0] = _dot(wqb_ref[0].astype(BF), pq_ref[...]).astype(BF)
    kvb_ref[0] = _dot(wkvb_ref[0].astype(BF), pkv_ref[...]).astype(BF)


def _head_layout_call(w_qb, w_kvb):
    dqk = QK_NOPE + QK_ROPE
    pq = np.zeros((N_HEADS * dqk, N_HEADS * LANE), np.float32)
    pkv = np.zeros((N_HEADS * LANE, 2 * N_HEADS * LANE), np.float32)
    for h in range(N_HEADS):
        src = h * dqk + np.concatenate([np.arange(QK_NOPE), QK_NOPE + ROPE_PERM])
        pq[src, h * LANE + np.arange(dqk)] = 1.0
        nope = np.arange(QK_NOPE)
        pkv[h * LANE + nope, h * LANE + nope] = 1.0
        val = np.arange(V_HEAD)
        pkv[h * LANE + QK_NOPE + val, (N_HEADS + h) * LANE + (h % 2) * V_HEAD + val] = 1.0
    return pl.pallas_call(
        _head_layout_body,
        grid=(DEPTH,),
        in_specs=[pl.BlockSpec((1, Q_LORA, N_HEADS * dqk), lambda l: (l, 0, 0)),
                  pl.BlockSpec((1, KV_LORA, N_HEADS * LANE), lambda l: (l, 0, 0)),
                  _full(pq.shape), _full(pkv.shape)],
        out_specs=[pl.BlockSpec((1, Q_LORA, N_HEADS * LANE), lambda l: (l, 0, 0)),
                   pl.BlockSpec((1, KV_LORA, 2 * N_HEADS * LANE), lambda l: (l, 0, 0))],
        out_shape=[jax.ShapeDtypeStruct((DEPTH, Q_LORA, N_HEADS * LANE), BF),
                   jax.ShapeDtypeStruct((DEPTH, KV_LORA, 2 * N_HEADS * LANE), BF)],
        compiler_params=_params(1),
        name="head_layout",
    )(w_qb, w_kvb, jnp.asarray(pq, BF), jnp.asarray(pkv, BF))


def kernel(x_prompt, x_sample, cache_ckv, cache_krope, c, c_ctx, w_ada, b_ada, g_norm1, g_norm2, w_in, g_qa, w_qb, g_kva, w_kvb, w_o_mla, w_conf_dw, b_conf_dw, g_conf_ln, b_conf_ln, w_conf_pw, w_sc_conv, w_sc_out, w_fn, w_out, w_ffn_gate, w_ffn_up, w_ffn_down, g_final):
    tm = 512
    xp = x_prompt.reshape(N_PROMPT, D)
    xs = x_sample.reshape(N_SAMPLE, D)
    cvec = jnp.concatenate([c_ctx[None, :], c, jnp.zeros((8 - 1 - DEC_BATCH, D), F32)], axis=0)
    w_in_t_f32 = jnp.swapaxes(w_in, 1, 2)
    mod, w_in_t = _ada_call(cvec, w_ada, b_ada, w_in_t_f32)

    rope_tab = _rope_table(tm)
    place = np.zeros((QK_ROPE, LANE), np.float32)
    place[ROPE_PERM, ROPE_LANE0 + np.arange(QK_ROPE)] = 1.0
    wqb, wkvb_all = _head_layout_call(w_qb, w_kvb)
    kc, vc = _ctx_call(cache_ckv, cache_krope, wkvb_all, jnp.asarray(place, BF))

    cg, sg = _dft_cos_sin(FN_GROUP_W)
    eye = np.eye(FN_GROUPS, dtype=np.float32)
    gd = jnp.asarray(np.concatenate([np.kron(eye, cg), np.kron(eye, sg)], axis=1)).astype(BF)
    cs_p = jnp.asarray(np.concatenate(_dft_cos_sin(SEQ), axis=1)).astype(BF)
    cs_s = jnp.asarray(np.concatenate(_dft_cos_sin(DEC_SEQ), axis=1)).astype(BF)

    w3_f32 = (w_o_mla, w_conf_pw, w_sc_out, w_fn, w_out, w_ffn_gate, w_ffn_up, w_ffn_down)

    conv_w = (w_conf_dw, b_conf_dw, g_conf_ln, b_conf_ln, w_sc_conv)
    x = (xp, xs)
    new_ckv, new_krope = [], []
    for l in range(DEPTH):
        (q, k, v, ckv, kr, u0, gcx, gb, fn, gates), w3 = _stage1_call(
            l, x, mod, g_norm1, g_qa, g_kva, w_in_t, wqb, wkvb_all, rope_tab, w3_f32, tm)
        new_ckv.append(ckv[:N_PROMPT].reshape(BATCH, SEQ, KV_LORA))
        new_krope.append(kr[:N_PROMPT].reshape(BATCH, SEQ, QK_ROPE))

        o_p, f_p = _prompt_mix_call(q, k, v, fn, gd, cs_p)
        o_s = _attn_sample_call(l, q, k, v, kc, vc, 512)
        f_s, u, s = _fourier_call(fn, gd, cs_s, DEC_SEQ, DEC_BATCH, N_PROMPT, 512, DEC_BATCH,
                                  conv=(l, u0, gcx, gb, conv_w))
        out = _stage3_call(l, x, o_p, o_s, u, s, f_p, f_s, gates, mod, g_norm2, g_final[None, :], w3,
                           w_in_t_f32, tm, l == DEPTH - 1)
        if l < DEPTH - 1:
            x, w_in_t = (out[0],), out[1]

    y_prompt, y_sample = out
    return (y_prompt.reshape(BATCH, SEQ, D), y_sample.reshape(DEC_BATCH, DEC_SEQ, D),
            jnp.stack(new_ckv, axis=1), jnp.stack(new_krope, axis=1))
```

```python
import functools

import numpy as np
import jax
import jax.numpy as jnp
from jax import lax
from jax.experimental import pallas as pl
from jax.experimental.pallas import tpu as pltpu

BF = jnp.bfloat16
F32 = jnp.float32

D = 1024
BATCH = 16
SEQ = 256
DEPTH = 2
DEC_BATCH = 2
DEC_SEQ = 2048
PAST_LEN = 256
GRID_W = 64
N_HEADS = 8
QK_NOPE = 64
QK_ROPE = 32
V_HEAD = 64
Q_LORA = 384
KV_LORA = 256
AXIS_ROPE = QK_ROPE // 2
ROPE_THETA = 10000.0
W_CONF = D // 4
CONF_K = 31
W_SC = D // 4
SC_K = 3
W_FN = D // 4
FN_GROUPS = 4
FN_GROUP_W = W_FN // FN_GROUPS
N_BRANCH = 4
FF_HIDDEN = ((8 * D // 3 + 255) // 256) * 256
EPS = 1e-6
OFF_QA = 0
OFF_KVA = OFF_QA + Q_LORA
OFF_CONF = OFF_KVA + KV_LORA + QK_ROPE
OFF_SC = OFF_CONF + 2 * W_CONF
OFF_FN = OFF_SC + 3 * W_SC
OFF_GATE = OFF_FN + W_FN
IN_COLS = OFF_GATE + N_BRANCH * D

N_PROMPT = BATCH * SEQ
N_SAMPLE = DEC_BATCH * DEC_SEQ
N_TOK = N_PROMPT + N_SAMPLE
LANE = 128
SUBLANES = 8
BF16_ROWS = 2 * SUBLANES
HALO = 16
SM_SCALE = float((QK_NOPE + QK_ROPE) ** -0.5)
Q_SCALE = SM_SCALE * float(np.log2(np.e))
VMEM_LIMIT = 56 * 1024 * 1024

QK_END = OFF_KVA + KV_LORA + LANE

ROPE_PERM = np.array(list(range(0, 8)) + list(range(16, 24)) + list(range(8, 16)) + list(range(24, 32)))
ROPE_LANE0 = QK_NOPE


def _dot(a, b):
    return jnp.dot(a, b, preferred_element_type=F32)


def _dot_nt(a, b):
    return lax.dot_general(a, b, (((1,), (1,)), ((), ())), preferred_element_type=F32)


def _sigmoid(x):
    return jax.nn.sigmoid(x)


def _rms(x, g):
    return x * lax.rsqrt(jnp.mean(x * x, axis=-1, keepdims=True) + EPS) * g


def _full(shape):
    return pl.BlockSpec(shape, lambda *_: (0,) * len(shape))


def _layer(l, shape):
    return pl.BlockSpec((1, *shape), lambda *_: (l,) + (0,) * len(shape), pipeline_mode=pl.Buffered(1))


def _params(n_axes):
    return pltpu.CompilerParams(
        dimension_semantics=("arbitrary",) * n_axes, vmem_limit_bytes=VMEM_LIMIT)


def _mod_row(i, n_prompt_tiles, tiles_per_seq):
    return jnp.where(i >= n_prompt_tiles, 1 + (i - n_prompt_tiles) // tiles_per_seq, 0)


def _ada_body(c_ref, w_ref, b_ref, win_ref, out_ref, win_out_ref):
    cv = c_ref[...]
    sc = (cv * _sigmoid(cv)).astype(BF)
    out_ref[0] = _dot(sc, w_ref[0].astype(BF)) + b_ref[0]
    win_out_ref[...] = win_ref[...].astype(BF)


def _ada_call(cvec, w_ada, b_ada, w_in_t_f32):
    wc = 2 * D
    n_col = 6 * D // wc
    slab = -(-IN_COLS // (DEPTH * n_col * BF16_ROWS)) * BF16_ROWS
    return pl.pallas_call(
        _ada_body,
        grid=(DEPTH, n_col),
        in_specs=[
            _full((8, D)),
            pl.BlockSpec((1, D, wc), lambda l, j: (l, 0, j)),
            pl.BlockSpec((1, 1, wc), lambda l, j: (l, 0, j)),
            pl.BlockSpec((1, slab, D), lambda l, j: (0, l * n_col + j, 0)),
        ],
        out_specs=[pl.BlockSpec((1, 8, wc), lambda l, j: (l, 0, j)),
                   pl.BlockSpec((1, slab, D), lambda l, j: (0, l * n_col + j, 0))],
        out_shape=[jax.ShapeDtypeStruct((DEPTH, 8, 6 * D), F32),
                   jax.ShapeDtypeStruct((1, IN_COLS, D), BF)],
        compiler_params=_params(2),
        name="ada_mod",
    )(cvec, w_ada, b_ada.reshape(DEPTH, 1, 6 * D), w_in_t_f32)


def _ctx_body(ckv_ref, kr_ref, wkvb_ref, place_ref, k_ref, v_ref):
    kv = _dot(ckv_ref[0, 0].astype(BF), wkvb_ref[0])
    krp = _dot(kr_ref[0, 0].astype(BF), place_ref[...])
    for h in range(N_HEADS):
        sl = slice(h * LANE, (h + 1) * LANE)
        k_ref[0, 0, :, sl] = (kv[:, sl] + krp).astype(BF)
    v_ref[0, 0] = (kv[:, N_HEADS * LANE:] + _value_ones_row()).astype(BF)


def _ctx_call(cache_ckv, cache_krope, wkvb_all, place):
    return pl.pallas_call(
        _ctx_body,
        grid=(DEPTH, DEC_BATCH),
        in_specs=[
            pl.BlockSpec((1, 1, PAST_LEN, KV_LORA), lambda l, b: (b, l, 0, 0)),
            pl.BlockSpec((1, 1, PAST_LEN, QK_ROPE), lambda l, b: (b, l, 0, 0)),
            pl.BlockSpec((1, KV_LORA, 2 * N_HEADS * LANE), lambda l, b: (l, 0, 0)),
            _full((QK_ROPE, LANE)),
        ],
        out_specs=[
            pl.BlockSpec((1, 1, PAST_LEN, N_HEADS * LANE), lambda l, b: (l, b, 0, 0)),
            pl.BlockSpec((1, 1, PAST_LEN, N_HEADS * LANE), lambda l, b: (l, b, 0, 0)),
        ],
        out_shape=[
            jax.ShapeDtypeStruct((DEPTH, DEC_BATCH, PAST_LEN, N_HEADS * LANE), BF),
            jax.ShapeDtypeStruct((DEPTH, DEC_BATCH, PAST_LEN, N_HEADS * LANE), BF),
        ],
        compiler_params=_params(2),
        name="ctx_keys",
    )(cache_ckv, cache_krope, wkvb_all, place)


def _take(refs, n):
    return refs[:n], refs[n:]


def _read_x(x_refs, is_sample):
    if len(x_refs) == 1:
        return x_refs[0][...]
    return jnp.where(is_sample, x_refs[1][...], x_refs[0][...])


def _stage1_body(*refs, layer, n_prompt_tiles, tiles_per_seq, n_x, n_cast):
    x_refs, refs = _take(refs, n_x)
    (mod_ref, g1_ref, gqa_ref, gkva_ref, wt_ref, wqb_ref, wkvb_ref, rope_ref), refs = _take(refs, 8)
    cast_in, refs = _take(refs, n_cast)
    (q_ref, k_ref, v_ref, ckv_ref, kr_ref, u0_ref, gcx_ref, gb_ref, fn_ref, gate_ref), cast_out = _take(refs, 10)

    for src, dst in zip(cast_in, cast_out, strict=True):
        dst[...] = src[0].astype(BF)

    i = pl.program_id(0)
    is_sample = i >= n_prompt_tiles
    row = _mod_row(i, n_prompt_tiles, tiles_per_seq)
    sh1 = mod_ref[0, pl.ds(row, 1), 0:D]
    sc1 = mod_ref[0, pl.ds(row, 1), D:2 * D]
    hb = (_rms(_read_x(x_refs, is_sample), _layer_row(g1_ref, layer)[...]) * (1.0 + sc1) + sh1).astype(BF)

    def proj(c0, c1):
        return _dot_nt(hb, wt_ref[0, c0:c1, :])

    pqk = proj(OFF_QA, QK_END)
    q = _dot(_rms(pqk[:, 0:Q_LORA], _layer_row(gqa_ref, layer)[...]).astype(BF), wqb_ref[0])

    ckv = _rms(pqk[:, OFF_KVA:OFF_KVA + KV_LORA], _layer_row(gkva_ref, layer)[...])
    ckv_ref[...] = ckv
    kr3 = pqk[:, OFF_KVA + KV_LORA:QK_END]
    kr_ref[...] = kr3[:, 0:QK_ROPE]
    lane = lax.broadcasted_iota(jnp.int32, (1, LANE), 1)
    group = lambda k: (lane >= ROPE_LANE0 + 8 * k) & (lane < ROPE_LANE0 + 8 * (k + 1))
    krm = jnp.where(group(0) | group(3), pltpu.roll(kr3, ROPE_LANE0, 1),
                    jnp.where(group(1), pltpu.roll(kr3, ROPE_LANE0 - 8, 1),
                              jnp.where(group(2), pltpu.roll(kr3, ROPE_LANE0 + 8, 1), 0.0)))
    kv = _dot(ckv.astype(BF), wkvb_ref[0])
    v_ref[...] = (kv[:, N_HEADS * LANE:] + _value_ones_row()).astype(BF)

    def rope(t):
        return (t * rope_ref[:, 0:LANE]
                + pltpu.roll(t, LANE - 16, 1) * rope_ref[:, LANE:2 * LANE]
                + pltpu.roll(t, 16, 1) * rope_ref[:, 2 * LANE:3 * LANE])

    krr = rope(krm)
    for h in range(N_HEADS):
        sl = slice(h * LANE, (h + 1) * LANE)
        q_ref[:, sl] = (rope(q[:, sl]) * Q_SCALE).astype(BF)
        k_ref[:, sl] = (kv[:, sl] + krr).astype(BF)

    for j in range(N_BRANCH):
        gate_ref[:, j * D:(j + 1) * D] = _sigmoid(
            proj(OFF_GATE + j * D, OFF_GATE + (j + 1) * D)).astype(BF)

    pc = proj(OFF_CONF, OFF_SC)
    u0_ref[...] = pc[:, 0:W_CONF] * _sigmoid(pc[:, W_CONF:])
    ps = proj(OFF_SC, OFF_FN)
    gb_ref[...] = ps[:, 0:W_SC]
    gcx_ref[...] = ps[:, W_SC:2 * W_SC] * ps[:, 2 * W_SC:]
    fn_ref[...] = proj(OFF_FN, OFF_GATE).astype(BF)


def _split_specs(tm, w):
    n_p = N_PROMPT // tm
    return (pl.BlockSpec((tm, w), lambda i: (jnp.minimum(i, n_p - 1), 0)),
            pl.BlockSpec((tm, w), lambda i: (jnp.maximum(i - n_p, 0), 0)))


def _x_specs(x, tm):
    return list(_split_specs(tm, D)) if len(x) == 2 else [pl.BlockSpec((tm, D), lambda i: (i, 0))]


def _stage1_call(l, x, mod_l, g1, gqa, gkva, w_in_t, wqb, wkvb, rope_tab, to_cast, tm):
    n_prompt_tiles = N_PROMPT // tm
    tiles_per_seq = DEC_SEQ // tm
    row_blk = lambda w: pl.BlockSpec((tm, w), lambda i: (i, 0))
    n_steps = N_TOK // tm
    body = functools.partial(_stage1_body, layer=l, n_prompt_tiles=n_prompt_tiles, tiles_per_seq=tiles_per_seq,
                             n_x=len(x), n_cast=len(to_cast))
    slab = lambda w: w.shape[1] // n_steps
    assert all(w.shape[1] % (BF16_ROWS * n_steps) == 0 for w in to_cast)
    outs = pl.pallas_call(
        body,
        grid=(n_steps,),
        in_specs=[
            *_x_specs(x, tm),
            _layer(l, (8, 6 * D)),
            _full((DEPTH, D)),
            _full((DEPTH, Q_LORA)),
            _full((DEPTH, KV_LORA)),
            _layer(0, (IN_COLS, D)),
            _layer(l, (Q_LORA, N_HEADS * LANE)),
            _layer(l, (KV_LORA, 2 * N_HEADS * LANE)),
            pl.BlockSpec((tm, 3 * LANE),
                         lambda i: (jnp.where(i < n_prompt_tiles, 0,
                                              1 + (i - n_prompt_tiles) % tiles_per_seq), 0)),
            *[pl.BlockSpec((1, slab(w), w.shape[2]), lambda i: (l, i, 0)) for w in to_cast],
        ],
        out_specs=[
            row_blk(N_HEADS * LANE), row_blk(N_HEADS * LANE), row_blk(N_HEADS * LANE),
            row_blk(KV_LORA), row_blk(QK_ROPE), row_blk(W_CONF), row_blk(W_SC), row_blk(W_SC),
            row_blk(W_FN), row_blk(N_BRANCH * D),
            *[pl.BlockSpec((slab(w), w.shape[2]), lambda i: (i, 0)) for w in to_cast],
        ],
        out_shape=[
            jax.ShapeDtypeStruct((N_TOK, N_HEADS * LANE), BF),
            jax.ShapeDtypeStruct((N_TOK, N_HEADS * LANE), BF),
            jax.ShapeDtypeStruct((N_TOK, N_HEADS * LANE), BF),
            jax.ShapeDtypeStruct((N_TOK, KV_LORA), F32),
            jax.ShapeDtypeStruct((N_TOK, QK_ROPE), F32),
            jax.ShapeDtypeStruct((N_TOK, W_CONF), F32),
            jax.ShapeDtypeStruct((N_TOK, W_SC), F32),
            jax.ShapeDtypeStruct((N_TOK, W_SC), F32),
            jax.ShapeDtypeStruct((N_TOK, W_FN), BF),
            jax.ShapeDtypeStruct((N_TOK, N_BRANCH * D), BF),
            *[jax.ShapeDtypeStruct(w.shape[1:], BF) for w in to_cast],
        ],
        compiler_params=_params(1),
        name="stage1",
    )(*x, mod_l, g1, gqa, gkva, w_in_t, wqb, wkvb, rope_tab, *to_cast)
    return outs[:10], outs[10:]


def _ones_lane(parity):
    return (1 - parity) * V_HEAD


def _value_ones_row():
    col = lax.broadcasted_iota(jnp.int32, (1, N_HEADS * LANE), 1)
    odd = (col // LANE) % 2
    return (col % LANE == jnp.where(odd == 1, _ones_lane(1), _ones_lane(0))).astype(F32)


def _attend_heads(q_ref, kv_refs, o_ref, ahead):
    lane = lax.broadcasted_iota(jnp.int32, (1, LANE), 1)

    def scores(h):
        sl = slice(h * LANE, (h + 1) * LANE)
        return [_dot_nt(q_ref[:, sl], k_ref[:, sl]) for k_ref, _ in kv_refs]

    pending = [scores(h) for h in range(ahead)]
    for hp in range(N_HEADS // 2):
        vsl = slice(hp * LANE, (hp + 1) * LANE)
        outs = []
        for e in range(2):
            h = 2 * hp + e
            sl = slice(h * LANE, (h + 1) * LANE)
            ss = pending.pop(0)
            if h + ahead < N_HEADS:
                pending.append(scores(h + ahead))
            m = functools.reduce(jnp.maximum, [jnp.max(s, axis=-1, keepdims=True) for s in ss])
            o = None
            for s, (_, v_ref) in zip(ss, kv_refs):
                t = _dot(jnp.exp2(s - m).astype(BF), v_ref[:, sl])
                o = t if o is None else o + t
            ones_lane = _ones_lane(e)
            outs.append(o * (1.0 / o[:, ones_lane:ones_lane + 1]))
        o_ref[:, vsl] = jnp.where(lane < V_HEAD, outs[0], outs[1]).astype(BF)


def _prompt_mix_body(q_ref, k_ref, v_ref, fn_ref, gd_ref, cs_ref, o_ref, f_ref, rhs, *, per_step, scale):
    _fourier_rhs(fn_ref, gd_ref, rhs, SEQ, per_step)
    for b, rows in enumerate(_fourier_rows(cs_ref, rhs, scale, per_step)):
        f_ref[b * SEQ:(b + 1) * SEQ, :] = rows
    for b in range(per_step):
        rows = pl.ds(b * SEQ, SEQ)
        _attend_heads(q_ref.at[rows], [(k_ref.at[rows], v_ref.at[rows])], o_ref.at[rows], ahead=N_HEADS)


def _prompt_mix_call(q, k, v, fn, gd, cs):
    per_step = 4
    blk = lambda w: pl.BlockSpec((per_step * SEQ, w), lambda b: (b, 0))
    body = functools.partial(_prompt_mix_body, per_step=per_step, scale=float((SEQ * FN_GROUP_W) ** -0.5))
    return pl.pallas_call(
        body,
        grid=(BATCH // per_step,),
        in_specs=[blk(N_HEADS * LANE), blk(N_HEADS * LANE), blk(N_HEADS * LANE), blk(W_FN),
                  _full((W_FN, 2 * W_FN)), _full((SEQ, 2 * SEQ))],
        out_specs=[blk(N_HEADS * V_HEAD), blk(W_FN)],
        out_shape=[jax.ShapeDtypeStruct((N_PROMPT, N_HEADS * V_HEAD), BF),
                   jax.ShapeDtypeStruct((N_PROMPT, W_FN), BF)],
        scratch_shapes=[pltpu.VMEM((2 * SEQ, per_step * W_FN), BF)],
        compiler_params=_params(1),
        name="prompt_mix",
    )(q, k, v, fn, gd, cs)


def _attn_sample_body(q_ref, kc_ref, vc_ref, k_ref, v_ref, o_ref):
    _attend_heads(q_ref, [(kc_ref.at[0, 0], vc_ref.at[0, 0]), (k_ref, v_ref)], o_ref, ahead=1)


def _attn_sample_call(l, q, k, v, kc, vc, tq):
    n_q = DEC_SEQ // tq
    q0 = N_PROMPT // tq
    s0 = N_PROMPT // DEC_SEQ
    return pl.pallas_call(
        _attn_sample_body,
        grid=(DEC_BATCH, n_q),
        in_specs=[
            pl.BlockSpec((tq, N_HEADS * LANE), lambda b, j: (q0 + b * n_q + j, 0)),
            pl.BlockSpec((1, 1, PAST_LEN, N_HEADS * LANE), lambda b, j: (l, b, 0, 0)),
            pl.BlockSpec((1, 1, PAST_LEN, N_HEADS * LANE), lambda b, j: (l, b, 0, 0)),
            pl.BlockSpec((DEC_SEQ, N_HEADS * LANE), lambda b, j: (s0 + b, 0)),
            pl.BlockSpec((DEC_SEQ, N_HEADS * LANE), lambda b, j: (s0 + b, 0)),
        ],
        out_specs=pl.BlockSpec((tq, N_HEADS * V_HEAD), lambda b, j: (b * n_q + j, 0)),
        out_shape=jax.ShapeDtypeStruct((N_SAMPLE, N_HEADS * V_HEAD), BF),
        compiler_params=_params(2),
        name="attn_sample",
    )(q, kc, vc, k, v)


CONV_ROWS = 256


def _conv_pass(ubuf, cbuf, gb_ref, wdw_ref, bdw_ref, gln_ref, bln_ref, wsc_ref, uo_ref, so_ref,
               shifted, cshifted):
    tl = CONV_ROWS
    span = shifted.shape[1]
    p0 = HALO - CONF_K // 2
    q0 = HALO - SC_K // 2
    for b in range(1, SUBLANES):
        shifted[b] = ubuf[pl.ds(b, span), :]
    for t in range(SC_K):
        if (q0 + t) % SUBLANES:
            cshifted[t] = cbuf[pl.ds((q0 + t) % SUBLANES, span), :]

    def staged(p, r, n):
        a, b = divmod(p, SUBLANES)
        if b == 0:
            return ubuf[pl.ds(r + SUBLANES * a, n), :]
        return shifted[b, pl.ds(r + SUBLANES * a, n), :]

    def staged_c(t, r, n):
        a, b = divmod(q0 + t, SUBLANES)
        if b == 0:
            return cbuf[pl.ds(r + SUBLANES * a, n), :]
        return cshifted[t, pl.ds(r + SUBLANES * a, n), :]

    rc = 32
    for r in range(0, tl, rc):
        acc = staged(p0, r, rc) * wdw_ref[0, 0:1, :]
        for t in range(1, CONF_K):
            acc = acc + staged(p0 + t, r, rc) * wdw_ref[0, t:t + 1, :]
        acc = acc + bdw_ref[...]
        mu = jnp.mean(acc, axis=-1, keepdims=True)
        cen = acc - mu
        var = jnp.mean(cen * cen, axis=-1, keepdims=True)
        y = cen * lax.rsqrt(var + EPS) * gln_ref[...] + bln_ref[...]
        uo_ref[pl.ds(r, rc), :] = (y * _sigmoid(y)).astype(BF)

        cv = staged_c(0, r, rc) * wsc_ref[0, 0:1, :]
        for t in range(1, SC_K):
            cv = cv + staged_c(t, r, rc) * wsc_ref[0, t:t + 1, :]
        so_ref[pl.ds(r, rc), :] = (gb_ref[pl.ds(r, rc), :] * cv).astype(BF)


def _conv_tile(tm, u_ref, c_ref, gb_ref, halos, is_latent, has_prev, has_next, conv_w, uo_ref, so_ref, scratch,
               hooks=None):
    ubuf, cbuf, shifted, cshifted = scratch
    n_pass = tm // CONV_ROWS
    zeros = jnp.zeros((HALO, W_CONF), F32)
    for p in range(n_pass):
        if hooks and p in hooks:
            hooks[p]()
        r0 = p * CONV_ROWS
        for src, buf, k in ((u_ref, ubuf, 0), (c_ref, cbuf, 2)):
            if p > 0:
                head = jnp.where(is_latent, src[r0 - HALO:r0, :], 0.0)
            else:
                head = zeros if halos is None else jnp.where(has_prev, halos[k][...], 0.0)
            if p < n_pass - 1:
                tail = jnp.where(is_latent, src[r0 + CONV_ROWS:r0 + CONV_ROWS + HALO, :], 0.0)
            else:
                tail = zeros if halos is None else jnp.where(has_next, halos[k + 1][...], 0.0)
            buf[0:HALO, :] = head
            buf[HALO:HALO + CONV_ROWS, :] = src[r0:r0 + CONV_ROWS, :]
            buf[HALO + CONV_ROWS:2 * HALO + CONV_ROWS, :] = tail
        rows = pl.ds(r0, CONV_ROWS)
        _conv_pass(ubuf, cbuf, gb_ref.at[rows], *conv_w, uo_ref.at[rows], so_ref.at[rows], shifted, cshifted)


def _conv_scratch():
    staged = CONV_ROWS + 2 * HALO
    return [pltpu.VMEM((staged, W_CONF), F32), pltpu.VMEM((staged, W_SC), F32),
            pltpu.VMEM((SUBLANES, staged - SUBLANES, W_CONF), F32),
            pltpu.VMEM((SC_K, staged - SUBLANES, W_SC), F32)]


def _fourier_rhs(fn_ref, gd_ref, rhs, seq, group):
    for b in range(group):
        v = _dot(fn_ref[b * seq:(b + 1) * seq, :], gd_ref[...])
        rhs[0:seq, b * W_FN:(b + 1) * W_FN] = v[:, 0:W_FN].astype(BF)
        rhs[seq:2 * seq, b * W_FN:(b + 1) * W_FN] = (-v[:, W_FN:]).astype(BF)


def _fourier_rows(cs_ref, rhs, scale, group):
    res = _dot(cs_ref[...], rhs[...]) * scale
    return [res[:, b * W_FN:(b + 1) * W_FN].astype(BF) for b in range(group)]


def _layer_row(ref, layer):
    return ref.at[pl.ds(layer, 1)]


def _fourier_body(fn_ref, gd_ref, cs_ref, *rest, seq, group, scale, conv_rows, layer):
    if conv_rows:
        (u_ref, c_ref, gb_ref), (wdw_ref, bdw_ref, gln_ref, bln_ref, wsc_ref) = rest[:3], rest[3:8]
        conv_w = (wdw_ref, _layer_row(bdw_ref, layer), _layer_row(gln_ref, layer), _layer_row(bln_ref, layer),
                  wsc_ref)
        out_ref, uo_ref, so_ref, rhs, *conv_scratch = rest[8:]
    else:
        out_ref, rhs = rest

    @pl.when(pl.program_id(1) == 0)
    def _():
        _fourier_rhs(fn_ref, gd_ref, rhs, seq, group)

    def dft_rows(b):
        out_ref[b] = (_dot(cs_ref[...], rhs[:, b * W_FN:(b + 1) * W_FN]) * scale).astype(BF)

    if not conv_rows:
        for b in range(group):
            dft_rows(b)
        return

    is_latent = pl.program_id(1) * conv_rows >= N_PROMPT
    n_pass = conv_rows // CONV_ROWS
    hooks = {(b * n_pass) // group: functools.partial(dft_rows, b) for b in range(group)}
    _conv_tile(conv_rows, u_ref, c_ref, gb_ref, None, is_latent, False, False, conv_w, uo_ref, so_ref,
               conv_scratch, hooks)


def _fourier_call(fn, gd, cs, seq, n_seq, row0, tl, group, conv=None):
    n_t = seq // tl
    g0 = row0 // (group * seq)
    n_steps = (n_seq // group) * n_t
    conv_rows, l = 0, 0
    extra_in, extra_out, extra_shape, extra_scratch, extra_args = [], [], [], [], []
    if conv is not None:
        l, u0, gcx, gb, conv_w = conv
        conv_rows = N_TOK // n_steps
        assert n_seq == group and conv_rows % DEC_SEQ == 0
        blk = pl.BlockSpec((conv_rows, W_CONF), lambda g, j: (j, 0))
        extra_in = [blk, blk, blk, _layer(l, (CONF_K, W_CONF)), _full((DEPTH, W_CONF)),
                    _full((DEPTH, W_CONF)), _full((DEPTH, W_CONF)), _layer(l, (SC_K, W_SC))]
        extra_out = [blk, blk]
        extra_shape = [jax.ShapeDtypeStruct((N_TOK, W_CONF), BF), jax.ShapeDtypeStruct((N_TOK, W_SC), BF)]
        extra_scratch = _conv_scratch()
        extra_args = [u0, gcx, gb, *conv_w]
    body = functools.partial(_fourier_body, seq=seq, group=group, scale=float((seq * FN_GROUP_W) ** -0.5),
                             conv_rows=conv_rows, layer=l)
    outs = pl.pallas_call(
        body,
        grid=(n_seq // group, n_t),
        in_specs=[
            pl.BlockSpec((group * seq, W_FN), lambda g, j: (g0 + g, 0)),
            _full((W_FN, 2 * W_FN)),
            pl.BlockSpec((tl, 2 * seq), lambda g, j: (j, 0)),
            *extra_in,
        ],
        out_specs=[pl.BlockSpec((group, tl, W_FN), lambda g, j: (g, j, 0)), *extra_out],
        out_shape=[jax.ShapeDtypeStruct((n_seq, seq, W_FN), BF), *extra_shape],
        scratch_shapes=[pltpu.VMEM((2 * seq, group * W_FN), BF), *extra_scratch],
        compiler_params=_params(2),
        name=f"fourier_{seq}",
    )(fn, gd, cs, *extra_args)
    f = outs[0].reshape(n_seq * seq, W_FN)
    return f if conv is None else (f, outs[1], outs[2])


def _dft_cos_sin(n):
    r = np.arange(n, dtype=np.int64)
    ang = ((r[:, None] * r[None, :]) % n).astype(np.float64) * (2.0 * np.pi / n)
    return np.cos(ang).astype(np.float32), np.sin(ang).astype(np.float32)


def _stage3_body(*refs, layer, n_prompt_tiles, tiles_per_seq, n_x, final, chunk):
    x_refs, refs = _take(refs, n_x)
    (op_ref, os_ref, u_ref, s_ref, fp_ref, fs_ref, gate_ref, mod_ref, g2_ref, gfin_ref), refs = _take(refs, 10)
    (wo_ref, wpw_ref, wsco_ref, wfn_ref, wout_ref, wg_ref, wu_ref, wd_ref), refs = _take(refs, 8)
    if final:
        out_refs = refs
    else:
        (next_w_ref,), (x_out_ref, next_w_out_ref) = _take(refs, 1)
        next_w_out_ref[...] = next_w_ref[...].astype(BF)
        out_refs = (x_out_ref,)

    i = pl.program_id(0)
    is_sample = i >= n_prompt_tiles
    row = _mod_row(i, n_prompt_tiles, tiles_per_seq)
    mod = lambda k: mod_ref[0, pl.ds(row, 1), k * D:(k + 1) * D]
    pick = lambda p_ref, s_ref: jnp.where(is_sample, s_ref[...], p_ref[...])
    merged = gate_ref[:, 0:D].astype(F32) * _dot(pick(op_ref, os_ref), wo_ref[...])
    merged = merged + gate_ref[:, D:2 * D].astype(F32) * _dot(u_ref[...], wpw_ref[...])
    merged = merged + gate_ref[:, 2 * D:3 * D].astype(F32) * _dot(s_ref[...], wsco_ref[...])
    merged = merged + gate_ref[:, 3 * D:4 * D].astype(F32) * _dot(pick(fp_ref, fs_ref), wfn_ref[...])
    x1 = _read_x(x_refs, is_sample) + mod(2) * _dot(merged.astype(BF), wout_ref[...])

    h2 = (_rms(x1, _layer_row(g2_ref, layer)[...]) * (1.0 + mod(4)) + mod(3)).astype(BF)
    bounds = list(range(0, FF_HIDDEN, chunk)) + [FF_HIDDEN]
    spans = list(zip(bounds[:-1], bounds[1:]))

    def gate_up(span):
        c0, c1 = span
        return _dot(h2, wg_ref[:, c0:c1]), _dot(h2, wu_ref[:, c0:c1])

    acc = None
    ab_next = gate_up(spans[0])
    for n, (c0, c1) in enumerate(spans):
        a, b = ab_next
        if n + 1 < len(spans):
            ab_next = gate_up(spans[n + 1])
        t = _dot((a * _sigmoid(a) * b).astype(BF), wd_ref[c0:c1, :])
        acc = t if acc is None else acc + t
    x2 = x1 + mod(5) * acc
    if not final:
        out_refs[0][...] = x2
        return
    y = _rms(x2, gfin_ref[...])

    @pl.when(is_sample)
    def _():
        out_refs[1][...] = y

    @pl.when(jnp.logical_not(is_sample))
    def _():
        out_refs[0][...] = y


def _stage3_call(l, x, o_p, o_s, u, s, f_p, f_s, gates, mod_l, g2, gfin, w3, w_in_t_f32, tm, final):
    n_tiles = N_TOK // tm
    row_blk = lambda w: pl.BlockSpec((tm, w), lambda i: (i, 0))
    body = functools.partial(_stage3_body, layer=l, n_prompt_tiles=N_PROMPT // tm, tiles_per_seq=DEC_SEQ // tm,
                             n_x=len(x), final=final, chunk=256)
    if final:
        extra_in, extra_args = [], []
        out_specs = list(_split_specs(tm, D))
        out_shape = [jax.ShapeDtypeStruct((N_PROMPT, D), F32), jax.ShapeDtypeStruct((N_SAMPLE, D), F32)]
    else:
        slab = -(-IN_COLS // (n_tiles * BF16_ROWS)) * BF16_ROWS
        extra_in = [pl.BlockSpec((1, slab, D), lambda i: (l + 1, i, 0))]
        extra_args = [w_in_t_f32]
        out_specs = [row_blk(D), pl.BlockSpec((1, slab, D), lambda i: (0, i, 0))]
        out_shape = [jax.ShapeDtypeStruct((N_TOK, D), F32), jax.ShapeDtypeStruct((1, IN_COLS, D), BF)]
    return pl.pallas_call(
        body,
        grid=(n_tiles,),
        in_specs=[
            *_x_specs(x, tm), *_split_specs(tm, N_HEADS * V_HEAD), row_blk(W_CONF), row_blk(W_SC),
            *_split_specs(tm, W_FN),
            row_blk(N_BRANCH * D), _layer(l, (8, 6 * D)), _full((DEPTH, D)), _full((1, D)),
            _full((N_HEADS * V_HEAD, D)), _full((W_CONF, D)), _full((W_SC, D)), _full((W_FN, D)),
            _full((D, D)), _full((D, FF_HIDDEN)), _full((D, FF_HIDDEN)), _full((FF_HIDDEN, D)),
            *extra_in,
        ],
        out_specs=out_specs,
        out_shape=out_shape,
        compiler_params=_params(1),
        name="stage3",
    )(*x, o_p, o_s, u, s, f_p, f_s, gates, mod_l, g2, gfin, *w3, *extra_args)


def _rope_table(tm):
    rows = DEC_SEQ // GRID_W
    row_pos = np.repeat(np.arange(rows, dtype=np.float64), GRID_W)
    col_pos = np.tile(np.arange(GRID_W, dtype=np.float64), rows)
    inv = ROPE_THETA ** (-np.arange(0, AXIS_ROPE, 2, dtype=np.float64) / AXIS_ROPE)
    ang = np.concatenate([row_pos[:, None] * inv, col_pos[:, None] * inv], axis=1)
    half = QK_ROPE // 2
    tab = np.zeros((tm + DEC_SEQ, 3 * LANE), np.float64)
    tab[:, 0:LANE] = 1.0
    tab[tm:, ROPE_LANE0:ROPE_LANE0 + half] = np.cos(ang)
    tab[tm:, ROPE_LANE0 + half:ROPE_LANE0 + QK_ROPE] = np.cos(ang)
    tab[tm:, LANE + ROPE_LANE0:LANE + ROPE_LANE0 + half] = -np.sin(ang)
    tab[tm:, 2 * LANE + ROPE_LANE0 + half:2 * LANE + ROPE_LANE0 + QK_ROPE] = np.sin(ang)
    return jnp.asarray(tab.astype(np.float32))


def _head_layout_body(wqb_ref, wkvb_ref, pq_ref, pkv_ref, qb_ref, kvb_ref):
    qb_ref[0] = _dot(wqb_ref[0].astype(BF), pq_ref[...]).astype(BF)
    kvb_ref[0] = _dot(wkvb_ref[0].astype(BF), pkv_ref[...]).astype(BF)


def _head_layout_call(w_qb, w_kvb):
    dqk = QK_NOPE + QK_ROPE
    pq = np.zeros((N_HEADS * dqk, N_HEADS * LANE), np.float32)
    pkv = np.zeros((N_HEADS * LANE, 2 * N_HEADS * LANE), np.float32)
    for h in range(N_HEADS):
        src = h * dqk + np.concatenate([np.arange(QK_NOPE), QK_NOPE + ROPE_PERM])
        pq[src, h * LANE + np.arange(dqk)] = 1.0
        nope = np.arange(QK_NOPE)
        pkv[h * LANE + nope, h * LANE + nope] = 1.0
        val = np.arange(V_HEAD)
        pkv[h * LANE + QK_NOPE + val, (N_HEADS + h) * LANE + (h % 2) * V_HEAD + val] = 1.0
    return pl.pallas_call(
        _head_layout_body,
        grid=(DEPTH,),
        in_specs=[pl.BlockSpec((1, Q_LORA, N_HEADS * dqk), lambda l: (l, 0, 0)),
                  pl.BlockSpec((1, KV_LORA, N_HEADS * LANE), lambda l: (l, 0, 0)),
                  _full(pq.shape), _full(pkv.shape)],
        out_specs=[pl.BlockSpec((1, Q_LORA, N_HEADS * LANE), lambda l: (l, 0, 0)),
                   pl.BlockSpec((1, KV_LORA, 2 * N_HEADS * LANE), lambda l: (l, 0, 0))],
        out_shape=[jax.ShapeDtypeStruct((DEPTH, Q_LORA, N_HEADS * LANE), BF),
                   jax.ShapeDtypeStruct((DEPTH, KV_LORA, 2 * N_HEADS * LANE), BF)],
        compiler_params=_params(1),
        name="head_layout",
    )(w_qb, w_kvb, jnp.asarray(pq, BF), jnp.asarray(pkv, BF))


def kernel(x_prompt, x_sample, cache_ckv, cache_krope, c, c_ctx, w_ada, b_ada, g_norm1, g_norm2, w_in, g_qa, w_qb, g_kva, w_kvb, w_o_mla, w_conf_dw, b_conf_dw, g_conf_ln, b_conf_ln, w_conf_pw, w_sc_conv, w_sc_out, w_fn, w_out, w_ffn_gate, w_ffn_up, w_ffn_down, g_final):
    tm = 512
    xp = x_prompt.reshape(N_PROMPT, D)
    xs = x_sample.reshape(N_SAMPLE, D)
    cvec = jnp.concatenate([c_ctx[None, :], c, jnp.zeros((8 - 1 - DEC_BATCH, D), F32)], axis=0)
    w_in_t_f32 = jnp.swapaxes(w_in, 1, 2)
    mod, w_in_t = _ada_call(cvec, w_ada, b_ada, w_in_t_f32)

    rope_tab = _rope_table(tm)
    place = np.zeros((QK_ROPE, LANE), np.float32)
    place[ROPE_PERM, ROPE_LANE0 + np.arange(QK_ROPE)] = 1.0
    wqb, wkvb_all = _head_layout_call(w_qb, w_kvb)
    kc, vc = _ctx_call(cache_ckv, cache_krope, wkvb_all, jnp.asarray(place, BF))

    cg, sg = _dft_cos_sin(FN_GROUP_W)
    eye = np.eye(FN_GROUPS, dtype=np.float32)
    gd = jnp.asarray(np.concatenate([np.kron(eye, cg), np.kron(eye, sg)], axis=1)).astype(BF)
    cs_p = jnp.asarray(np.concatenate(_dft_cos_sin(SEQ), axis=1)).astype(BF)
    cs_s = jnp.asarray(np.concatenate(_dft_cos_sin(DEC_SEQ), axis=1)).astype(BF)

    w3_f32 = (w_o_mla, w_conf_pw, w_sc_out, w_fn, w_out, w_ffn_gate, w_ffn_up, w_ffn_down)

    conv_w = (w_conf_dw, b_conf_dw, g_conf_ln, b_conf_ln, w_sc_conv)
    x = (xp, xs)
    new_ckv, new_krope = [], []
    for l in range(DEPTH):
        (q, k, v, ckv, kr, u0, gcx, gb, fn, gates), w3 = _stage1_call(
            l, x, mod, g_norm1, g_qa, g_kva, w_in_t, wqb, wkvb_all, rope_tab, w3_f32, tm)
        new_ckv.append(ckv[:N_PROMPT].reshape(BATCH, SEQ, KV_LORA))
        new_krope.append(kr[:N_PROMPT].reshape(BATCH, SEQ, QK_ROPE))

        o_p, f_p = _prompt_mix_call(q, k, v, fn, gd, cs_p)
        o_s = _attn_sample_call(l, q, k, v, kc, vc, 512)
        f_s, u, s = _fourier_call(fn, gd, cs_s, DEC_SEQ, DEC_BATCH, N_PROMPT, 512, DEC_BATCH,
                                  conv=(l, u0, gcx, gb, conv_w))
        out = _stage3_call(l, x, o_p, o_s, u, s, f_p, f_s, gates, mod, g_norm2, g_final[None, :], w3,
                           w_in_t_f32, tm, l == DEPTH - 1)
        if l < DEPTH - 1:
            x, w_in_t = (out[0],), out[1]

    y_prompt, y_sample = out
    return (y_prompt.reshape(BATCH, SEQ, D), y_sample.reshape(DEC_BATCH, DEC_SEQ, D),
            jnp.stack(new_ckv, axis=1), jnp.stack(new_krope, axis=1))
```
